```python
import numpy as np
import jax, jax.numpy as jnp
from jax import lax

D_MODEL = 1024
BATCH = 2
SEQ = 16384
DEPTH = 4

D_FF = 2816
RMS_EPS = 1e-6
ROPE_THETA = 10000.0
CHUNK = 64

GLA_H = 4
GLA_DK = 64
GLA_DV = 128
GLA_RANK = 16
GLA_GATE_NORM = 16.0

NSA_H = 8
NSA_G = 2
NSA_D = 64
CMP_LEN = 32
CMP_STRIDE = 16
CMP_HIDDEN = 256
SLC_LEN = 64
SLC_TOPK = 16
WIN = 512
Q_BLOCK = 128
FORCED_SCORE = 1e4

MLSTM_H = 4
MLSTM_DK = 64
MLSTM_DV = 128
CONV_W = 4

MIX_W = 512
N_BRANCH = 3

IN_SIZES = (
    GLA_H * GLA_DK, GLA_H * GLA_DK, GLA_H * GLA_DV, GLA_H * GLA_DV, GLA_RANK,
    NSA_H * NSA_D,
    NSA_G * NSA_D, NSA_G * NSA_D,
    NSA_G * NSA_D, NSA_G * NSA_D,
    NSA_G * NSA_D, NSA_G * NSA_D,
    NSA_H * 3,
    2 * MLSTM_H * MLSTM_DK, MLSTM_H * MLSTM_DV, MLSTM_H * MLSTM_DV, MLSTM_H, MLSTM_H,
    D_MODEL, D_MODEL, D_MODEL,
)
N_IN = sum(IN_SIZES)

kernel_name = "hybrid_gla_nsa_mlstm_macaron"


def rms_norm(x, g):
    xf = x.astype(jnp.float32)
    y = xf * lax.rsqrt(jnp.mean(xf * xf, axis=-1, keepdims=True) + RMS_EPS)
    return (y * g.astype(jnp.float32)).astype(x.dtype)


def rope(x, pos):
    half = x.shape[-1] // 2
    freqs = ROPE_THETA ** (-jnp.arange(half, dtype=jnp.float32) / half)
    ang = pos.astype(jnp.float32)[:, None] * freqs[None, :]
    cos = jnp.cos(ang)[:, None, :]
    sin = jnp.sin(ang)[:, None, :]
    xf = x.astype(jnp.float32)
    x1, x2 = xf[..., :half], xf[..., half:]
    return jnp.concatenate([x1 * cos - x2 * sin, x1 * sin + x2 * cos], axis=-1).astype(x.dtype)


def swiglu(x, w_gate, w_up, w_down):
    return (jax.nn.silu(x @ w_gate) * (x @ w_up)) @ w_down


def masked_softmax(s, mask):
    s = jnp.where(mask, s.astype(jnp.float32), -jnp.inf)
    m = jnp.max(s, axis=-1, keepdims=True)
    m = jnp.where(jnp.isfinite(m), m, 0.0)
    p = jnp.exp(s - m)
    return p / jnp.maximum(jnp.sum(p, axis=-1, keepdims=True), 1e-30)


def split_columns(z):
    parts, off = [], 0
    for w in IN_SIZES:
        parts.append(z[..., off:off + w])
        off += w
    return parts


def to_chunks(t):
    B, S, H, d = t.shape
    return t.reshape(B, S // CHUNK, CHUNK, H, d).transpose(1, 0, 3, 2, 4)


def from_chunks(t):
    C, B, H, L, d = t.shape
    return t.transpose(1, 0, 3, 2, 4).reshape(B, C * L, H, d)


def causal_conv(x, w, b):
    K, C = w.shape
    y = lax.conv_general_dilated(x, w[:, None, :].astype(x.dtype), window_strides=(1,),
                                 padding=[(K - 1, 0)], dimension_numbers=('NWC', 'WIO', 'NWC'),
                                 feature_group_count=C)
    return y + b


def gla_chunked(q, k, v, log_alpha):
    B, S, H, DK = q.shape
    DV = v.shape[-1]
    f32 = jnp.float32
    qc = to_chunks(q.astype(f32) * DK ** -0.5)
    kc = to_chunks(k.astype(f32))
    vc = to_chunks(v.astype(f32))
    gc = to_chunks(log_alpha.astype(f32))
    causal = jnp.tril(jnp.ones((CHUNK, CHUNK), dtype=bool))

    def step(state, xs):
        q_i, k_i, v_i, g_i = xs
        b = jnp.cumsum(g_i, axis=2)
        o_inter = jnp.einsum('bhld,bhde->bhle', q_i * jnp.exp(b), state)
        diff = jnp.where(causal[:, :, None], b[:, :, :, None, :] - b[:, :, None, :, :], -jnp.inf)
        attn = jnp.einsum('bhid,bhjd,bhijd->bhij', q_i, k_i, jnp.exp(diff))
        b_last = b[:, :, -1:, :]
        k_dec = k_i * jnp.exp(b_last - b)
        state = jnp.exp(b_last[:, :, 0, :, None]) * state + jnp.einsum('bhld,bhle->bhde', k_dec, v_i)
        return state, o_inter + jnp.einsum('bhij,bhje->bhie', attn, v_i)

    state0 = jnp.zeros((B, H, DK, DV), f32)
    _, o = lax.scan(step, state0, (qc, kc, vc, gc))
    return from_chunks(o)


def mlstm_chunked(q, k, v, i_pre, log_f):
    B, S, H, DK = q.shape
    DV = v.shape[-1]
    f32 = jnp.float32
    qc = to_chunks(q.astype(f32))
    kc = to_chunks(k.astype(f32) * DK ** -0.5)
    vc = to_chunks(v.astype(f32))
    ic = to_chunks(i_pre.astype(f32)[..., None])[..., 0]
    fc = to_chunks(log_f.astype(f32)[..., None])[..., 0]
    causal = jnp.tril(jnp.ones((CHUNK, CHUNK), dtype=bool))

    def step(carry, xs):
        c_st, n_st, m_st = carry
        q_i, k_i, v_i, ii, fi = xs
        b = jnp.cumsum(fi, axis=-1)
        dmat = jnp.where(causal, b[..., :, None] - b[..., None, :] + ii[..., None, :], -jnp.inf)
        inter_log = b + m_st[..., None]
        m_row = jnp.maximum(inter_log, jnp.max(dmat, axis=-1))
        w_inter = jnp.exp(inter_log - m_row)
        s_qk = jnp.einsum('bhid,bhjd->bhij', q_i, k_i) * jnp.exp(dmat - m_row[..., None])
        num = w_inter[..., None] * jnp.einsum('bhld,bhde->bhle', q_i, c_st) + jnp.einsum('bhij,bhje->bhie', s_qk, v_i)
        den = w_inter * jnp.einsum('bhld,bhd->bhl', q_i, n_st) + jnp.sum(s_qk, axis=-1)
        h = num / jnp.maximum(jnp.abs(den), jnp.exp(-m_row))[..., None]
        m_new = m_row[..., -1]
        w_k = jnp.exp(dmat[..., -1, :] - m_new[..., None])
        decay = jnp.exp(b[..., -1] + m_st - m_new)
        c_new = decay[..., None, None] * c_st + jnp.einsum('bhld,bhle->bhde', k_i * w_k[..., None], v_i)
        n_new = decay[..., None] * n_st + jnp.einsum('bhl,bhld->bhd', w_k, k_i)
        return (c_new, n_new, m_new), h

    carry0 = (jnp.zeros((B, H, DK, DV), f32), jnp.zeros((B, H, DK), f32), jnp.zeros((B, H), f32))
    _, h = lax.scan(step, carry0, (qc, kc, vc, ic, fc))
    return from_chunks(h)


def nsa_attention(q, k_cmp, v_cmp, k_slc, v_slc, k_win, v_win, gates, qk_norm, cmp_pos, cmp_w1, cmp_w2):
    B, S, H, D = q.shape
    G = NSA_G
    HPG = H // G
    pos = jnp.arange(S)
    scale = D ** -0.5
    q = rope(rms_norm(q, qk_norm[0]), pos)
    k_slc = rope(rms_norm(k_slc, qk_norm[2]), pos)
    k_win = rope(rms_norm(k_win, qk_norm[3]), pos)

    n_cmp = S // CMP_STRIDE - 1

    def compress(t, pe, w1, w2):
        tb = t.reshape(B, S // CMP_STRIDE, CMP_STRIDE, G, D)
        blocks = jnp.concatenate([tb[:, :-1], tb[:, 1:]], axis=2) + pe[None, None, :, None, :]
        flat = blocks.transpose(0, 1, 3, 2, 4).reshape(B, n_cmp, G, CMP_LEN * D)
        return jax.nn.gelu(flat @ w1) @ w2

    cmp_end = jnp.arange(n_cmp) * CMP_STRIDE + CMP_LEN - 1
    kc = rope(rms_norm(compress(k_cmp, cmp_pos[0], cmp_w1[0], cmp_w2[0]), qk_norm[1]), cmp_end)
    vc = compress(v_cmp, cmp_pos[1], cmp_w1[1], cmp_w2[1])

    n_sel = S // SLC_LEN
    topk = min(SLC_TOPK, n_sel)
    ks_blk = k_slc.reshape(B, n_sel, SLC_LEN, G, D).transpose(0, 3, 1, 2, 4)
    vs_blk = v_slc.reshape(B, n_sel, SLC_LEN, G, D).transpose(0, 3, 1, 2, 4)
    ci = np.arange(n_cmp)[:, None] * CMP_STRIDE
    sj = np.arange(n_sel)[None, :] * SLC_LEN
    overlap = jnp.asarray(((ci < sj + SLC_LEN) & (ci + CMP_LEN > sj)).astype(np.float32))
    take = jax.vmap(jax.vmap(lambda kb, ix: kb[ix]))

    kw_pad = jnp.pad(k_win, ((0, 0), (WIN, 0), (0, 0), (0, 0)))
    vw_pad = jnp.pad(v_win, ((0, 0), (WIN, 0), (0, 0), (0, 0)))

    def block(qb):
        s0 = qb * Q_BLOCK
        t = s0 + jnp.arange(Q_BLOCK)
        qq = lax.dynamic_slice_in_dim(q, s0, Q_BLOCK, axis=1).reshape(B, Q_BLOCK, G, HPG, D)
        gg = lax.dynamic_slice_in_dim(gates, s0, Q_BLOCK, axis=1).reshape(B, Q_BLOCK, G, HPG, 3)
        sc = jnp.einsum('bqghd,bcgd->bghqc', qq, kc) * scale
        p_c = masked_softmax(sc, cmp_end[None, :] <= t[:, None])
        o_c = jnp.einsum('bghqc,bcgd->bqghd', p_c, vc)
        imp = jnp.einsum('bghqc,cn->bgqn', p_c, overlap)
        blk = jnp.arange(n_sel)[None, :]
        valid = blk * SLC_LEN <= t[:, None]
        cur = (t // SLC_LEN)[:, None]
        forced = (blk == 0) | (blk == cur) | (blk == cur - 1)
        score = jnp.where(valid, jnp.where(forced, FORCED_SCORE, imp), -jnp.inf)
        top_val, top_idx = lax.top_k(score, topk)
        k_sel = take(ks_blk, top_idx).reshape(B, G, Q_BLOCK, topk * SLC_LEN, D)
        v_sel = take(vs_blk, top_idx).reshape(B, G, Q_BLOCK, topk * SLC_LEN, D)
        tok = (top_idx[..., None] * SLC_LEN + jnp.arange(SLC_LEN)).reshape(B, G, Q_BLOCK, topk * SLC_LEN)
        m_s = jnp.repeat(jnp.isfinite(top_val), SLC_LEN, axis=-1) & (tok <= t[:, None])
        ss = jnp.einsum('bqghd,bgqnd->bghqn', qq, k_sel) * scale
        p_s = masked_softmax(ss, m_s[:, :, None])
        o_s = jnp.einsum('bghqn,bgqnd->bqghd', p_s, v_sel)
        kw = lax.dynamic_slice_in_dim(kw_pad, s0, WIN + Q_BLOCK, axis=1)
        vw = lax.dynamic_slice_in_dim(vw_pad, s0, WIN + Q_BLOCK, axis=1)
        kpos = s0 - WIN + jnp.arange(WIN + Q_BLOCK)
        m_w = (kpos[None, :] <= t[:, None]) & (kpos[None, :] > t[:, None] - WIN) & (kpos[None, :] >= 0)
        sw = jnp.einsum('bqghd,bkgd->bghqk', qq, kw) * scale
        p_w = masked_softmax(sw, m_w)
        o_w = jnp.einsum('bghqk,bkgd->bqghd', p_w, vw)
        o = gg[..., 0:1] * o_c + gg[..., 1:2] * o_s + gg[..., 2:3] * o_w
        return o.reshape(B, Q_BLOCK, H * D)

    out = lax.map(block, jnp.arange(S // Q_BLOCK))
    return out.transpose(1, 0, 2, 3).reshape(B, S, H * D)


def hybrid_mixer(h, w_in, b_in, gla_w_alpha, gla_b_alpha, gla_out_norm, nsa_qk_norm, nsa_cmp_pos,
                 nsa_cmp_w1, nsa_cmp_w2, mlstm_conv_w, mlstm_conv_b, w_branch, w_out):
    B, S, _ = h.shape
    f32 = jnp.float32
    z = h @ w_in + b_in
    (a_q, a_k, a_v, a_r, a_lr, b_q, b_kc, b_vc, b_ks, b_vs, b_kw, b_vw, b_g,
     c_qk, c_v, c_o, c_i, c_f, gate_a, gate_b, gate_c) = split_columns(z)

    def heads(t, n):
        return t.reshape(B, S, n, -1)

    log_alpha = jax.nn.log_sigmoid((a_lr @ gla_w_alpha + gla_b_alpha).astype(f32)) / GLA_GATE_NORM
    o_a = gla_chunked(heads(a_q, GLA_H), heads(a_k, GLA_H), heads(a_v, GLA_H), heads(log_alpha, GLA_H))
    o_a = rms_norm(o_a, gla_out_norm) * jax.nn.silu(heads(a_r, GLA_H).astype(f32))
    o_a = o_a.reshape(B, S, MIX_W).astype(h.dtype)

    o_b = nsa_attention(heads(b_q, NSA_H), heads(b_kc, NSA_G), heads(b_vc, NSA_G), heads(b_ks, NSA_G),
                        heads(b_vs, NSA_G), heads(b_kw, NSA_G), heads(b_vw, NSA_G),
                        heads(jax.nn.sigmoid(b_g), NSA_H), nsa_qk_norm, nsa_cmp_pos, nsa_cmp_w1, nsa_cmp_w2)
    o_b = o_b.astype(h.dtype)

    qk = jax.nn.silu(causal_conv(c_qk, mlstm_conv_w, mlstm_conv_b))
    c_q, c_k = qk[..., :MLSTM_H * MLSTM_DK], qk[..., MLSTM_H * MLSTM_DK:]
    h_c = mlstm_chunked(heads(c_q, MLSTM_H), heads(c_k, MLSTM_H), heads(c_v, MLSTM_H),
                        c_i, jax.nn.log_sigmoid(c_f.astype(f32)))
    o_c = (h_c.reshape(B, S, MIX_W) * jax.nn.sigmoid(c_o.astype(f32))).astype(h.dtype)

    y = (jax.nn.sigmoid(gate_a) * (o_a @ w_branch[0])
         + jax.nn.sigmoid(gate_b) * (o_b @ w_branch[1])
         + jax.nn.sigmoid(gate_c) * (o_c @ w_branch[2]))
    return y @ w_out


def setup_inputs(seed: int = 0) -> dict:
    key = jax.random.key(seed)
    ks = jax.random.split(key, 23)
    L, D, F = DEPTH, D_MODEL, D_FF
    f32 = jnp.float32

    def nrm(k, shape, scale):
        return jax.random.normal(k, shape, f32) * scale

    def gain(k, shape):
        return 1.0 + 0.02 * jax.random.normal(k, shape, f32)

    f_start = N_IN - 3 * D_MODEL - MLSTM_H
    b_in = nrm(ks[7], (L, N_IN), 0.02)
    b_in = b_in.at[:, f_start:f_start + MLSTM_H].add(jnp.linspace(3.0, 6.0, MLSTM_H, dtype=f32))
    return {
        'x': jax.random.normal(ks[0], (BATCH, SEQ, D), f32),
        'ffn1_norm': gain(ks[1], (L, D)),
        'ffn1_w_gate': nrm(ks[2], (L, D, F), D ** -0.5),
        'ffn1_w_up': nrm(ks[3], (L, D, F), D ** -0.5),
        'ffn1_w_down': nrm(ks[4], (L, F, D), F ** -0.5),
        'mix_norm': gain(ks[5], (L, D)),
        'w_in': nrm(ks[6], (L, D, N_IN), D ** -0.5),
        'b_in': b_in,
        'gla_w_alpha': nrm(ks[8], (L, GLA_RANK, GLA_H * GLA_DK), GLA_RANK ** -0.5),
        'gla_b_alpha': nrm(ks[9], (L, GLA_H * GLA_DK), 0.1),
        'gla_out_norm': gain(ks[10], (L, GLA_DV)),
        'nsa_qk_norm': gain(ks[11], (L, 4, NSA_D)),
        'nsa_cmp_pos': nrm(ks[12], (L, 2, CMP_LEN, NSA_D), 0.1),
        'nsa_cmp_w1': nrm(ks[13], (L, 2, CMP_LEN * NSA_D, CMP_HIDDEN), (CMP_LEN * NSA_D) ** -0.5),
        'nsa_cmp_w2': nrm(ks[14], (L, 2, CMP_HIDDEN, NSA_D), CMP_HIDDEN ** -0.5),
        'mlstm_conv_w': nrm(ks[15], (L, CONV_W, 2 * MLSTM_H * MLSTM_DK), CONV_W ** -0.5),
        'mlstm_conv_b': nrm(ks[16], (L, 2 * MLSTM_H * MLSTM_DK), 0.02),
        'w_branch': nrm(ks[17], (L, N_BRANCH, MIX_W, D), MIX_W ** -0.5),
        'w_out': nrm(ks[18], (L, D, D), D ** -0.5),
        'ffn2_norm': gain(ks[19], (L, D)),
        'ffn2_w_gate': nrm(ks[20], (L, D, F), D ** -0.5),
        'ffn2_w_up': nrm(ks[21], (L, D, F), D ** -0.5),
        'ffn2_w_down': nrm(ks[22], (L, F, D), F ** -0.5),
    }


def reference(x, ffn1_norm, ffn1_w_gate, ffn1_w_up, ffn1_w_down, mix_norm, w_in, b_in, gla_w_alpha,
              gla_b_alpha, gla_out_norm, nsa_qk_norm, nsa_cmp_pos, nsa_cmp_w1, nsa_cmp_w2, mlstm_conv_w,
              mlstm_conv_b, w_branch, w_out, ffn2_norm, ffn2_w_gate, ffn2_w_up, ffn2_w_down):
    for l in range(DEPTH):
        x = x + 0.5 * swiglu(rms_norm(x, ffn1_norm[l]), ffn1_w_gate[l], ffn1_w_up[l], ffn1_w_down[l])
        x = x + hybrid_mixer(rms_norm(x, mix_norm[l]), w_in[l], b_in[l], gla_w_alpha[l], gla_b_alpha[l],
                             gla_out_norm[l], nsa_qk_norm[l], nsa_cmp_pos[l], nsa_cmp_w1[l], nsa_cmp_w2[l],
                             mlstm_conv_w[l], mlstm_conv_b[l], w_branch[l], w_out[l])
        x = x + 0.5 * swiglu(rms_norm(x, ffn2_norm[l]), ffn2_w_gate[l], ffn2_w_up[l], ffn2_w_down[l])
    return x
```

```python
import functools

import numpy as np
import jax
import jax.numpy as jnp
from jax import lax
from jax.experimental import pallas as pl
from jax.experimental.pallas import tpu as pltpu

F32 = jnp.float32
BF16 = jnp.bfloat16

RMS_EPS = 1e-6
ROPE_THETA = 10000.0

GLA_H, GLA_DK, GLA_DV, GLA_RANK, GLA_GATE_NORM = 4, 64, 128, 16, 16.0
GLA_SUB = 16
NSA_H, NSA_G, NSA_D = 8, 2, 64
CMP_LEN, CMP_STRIDE, CMP_HIDDEN = 32, 16, 256
SLC_LEN, SLC_TOPK, WIN = 64, 16, 512
FORCED_SCORE = 1e4
MLSTM_H, MLSTM_DK, MLSTM_DV, CONV_W = 4, 64, 128, 4
MLSTM_CHUNK = 64
MIX_W = 512
LANE = 128
NEG = -1e30

_SEGS = (
    ("a_q", 0, 256, 256), ("a_k", 256, 256, 256), ("a_v", 512, 512, 512), ("a_r", 1024, 512, 512),
    ("a_lr", 1536, 16, 128),
    ("b_q", 1552, 512, 512), ("b_kc", 2064, 128, 128), ("b_vc", 2192, 128, 128), ("b_ks", 2320, 128, 128),
    ("b_vs", 2448, 128, 128), ("b_kw", 2576, 128, 128), ("b_vw", 2704, 128, 128), ("b_g", 2832, 24, 128),
    ("c_qk", 2856, 512, 512), ("c_v", 3368, 512, 512), ("c_o", 3880, 512, 512), ("c_if", 4392, 8, 128),
    ("gates", 4400, 3072, 3072),
)
_N_PACK = sum(s[3] for s in _SEGS)


def _dot(a, b):
    return jnp.dot(a, b, preferred_element_type=F32)


def _dot_nt(a, b):
    return lax.dot_general(a, b, (((1,), (1,)), ((), ())), preferred_element_type=F32)


def _split2(a):
    hi = a.astype(BF16)
    lo = (a - hi.astype(F32)).astype(BF16)
    return hi, lo


def _dot_l2(a, b):
    hi, lo = _split2(a)
    return _dot(hi, b) + _dot(lo, b)


def _dot_r2(a, b):
    hi, lo = _split2(b)
    return _dot(a, hi) + _dot(a, lo)


def _log_sigmoid(x):
    return jnp.minimum(x, 0.0) - jnp.log(1.0 + jnp.exp(-jnp.abs(x)))


def _sigmoid(x):
    return 1.0 / (1.0 + jnp.exp(-x))


def _silu(x):
    return x * _sigmoid(x)


def _iota(shape, dim):
    return lax.broadcasted_iota(jnp.int32, shape, dim)


def _const_spec(shape):
    nd = len(shape)
    return pl.BlockSpec(shape, lambda *_: (0,) * nd, pipeline_mode=pl.Buffered(1))


def _params(sem, vmem_mb=56):
    return pltpu.CompilerParams(dimension_semantics=sem, vmem_limit_bytes=vmem_mb * 1024 * 1024)


def _ffn_body(x_ref, g_ref, wg_ref, wu_ref, wd_ref, o_ref, *, n_chunks):
    x = x_ref[...]
    ms = jnp.mean(x * x, axis=-1, keepdims=True)
    h = (x * lax.rsqrt(ms + RMS_EPS) * g_ref[...]).astype(BF16)
    fc = wg_ref.shape[1] // n_chunks
    acc = jnp.zeros(x.shape, F32)
    for c in range(n_chunks):
        a = _dot(h, wg_ref[:, c * fc:(c + 1) * fc])
        u = _dot(h, wu_ref[:, c * fc:(c + 1) * fc])
        t = (_silu(a) * u).astype(BF16)
        acc = acc + _dot(t, wd_ref[c * fc:(c + 1) * fc, :])
    o_ref[...] = x + 0.5 * acc


def _ffn(x, g, wg, wu, wd, tm=512):
    T, D = x.shape
    F = wg.shape[1]
    return pl.pallas_call(
        functools.partial(_ffn_body, n_chunks=2),
        out_shape=jax.ShapeDtypeStruct((T, D), F32),
        grid=(T // tm,),
        in_specs=[pl.BlockSpec((tm, D), lambda i: (i, 0)), _const_spec((1, D)),
                  _const_spec((D, F)), _const_spec((D, F)), _const_spec((F, D))],
        out_specs=pl.BlockSpec((tm, D), lambda i: (i, 0)),
        compiler_params=_params(("parallel",)),
        name="ffn",
    )(x, g, wg, wu, wd)


def _proj_body(x_ref, g_ref, w_ref, b_ref, *o_refs):
    x = x_ref[...]
    ms = jnp.mean(x * x, axis=-1, keepdims=True)
    h = (x * lax.rsqrt(ms + RMS_EPS) * g_ref[...]).astype(BF16)
    off = 0
    for o_ref in o_refs:
        w = o_ref.shape[1]
        o_ref[...] = _dot(h, w_ref[:, off:off + w]) + b_ref[:, off:off + w]
        off += w


def _proj(x, g, w, b, tm=256):
    T, D = x.shape
    return pl.pallas_call(
        _proj_body,
        out_shape=[jax.ShapeDtypeStruct((T, s[3]), F32) for s in _SEGS],
        grid=(T // tm,),
        in_specs=[pl.BlockSpec((tm, D), lambda i: (i, 0)), _const_spec((1, D)),
                  _const_spec((D, _N_PACK)), _const_spec((1, _N_PACK))],
        out_specs=[pl.BlockSpec((tm, s[3]), lambda i: (i, 0)) for s in _SEGS],
        compiler_params=_params(("parallel",)),
        name="in_proj",
    )(x, g, w, b)


def _gla_body(q_ref, k_ref, v_ref, r_ref, lr_ref, wa_ref, ba_ref, gn_ref, tri_ref, bones_ref, eh_ref, bdm_ref,
              o_ref, st_ref, qs_ref, c_ref, tot_ref, kt_ref, vt_ref, oi_ref):
    Lc = q_ref.shape[0]
    n_sub = Lc // GLA_SUB

    @pl.when(pl.program_id(1) == 0)
    def _():
        st_ref[...] = jnp.zeros(st_ref.shape, F32)

    lr_hi, lr_lo = _split2(lr_ref[...])
    wa_hi, wa_lo = _split2(wa_ref[...])
    u = _dot(lr_hi, wa_hi) + _dot(lr_hi, wa_lo) + _dot(lr_lo, wa_hi) + ba_ref[...]
    g = _log_sigmoid(u) * (1.0 / GLA_GATE_NORM)
    c = _dot_r2(tri_ref[...], g)
    tot = _dot_r2(bones_ref[...], g)
    k = k_ref[...]
    qs_ref[...] = q_ref[...] * (GLA_DK ** -0.5)
    c_ref[...] = c
    tot_ref[...] = tot
    kt_ref[...] = (k * jnp.exp(tot - c)).astype(BF16)
    vt_ref[...] = v_ref[...].T.astype(BF16)
    row_i = _iota((GLA_SUB, 1), 0)
    col_l = _iota((1, Lc), 1)

    def sub(s, carry):
        r0 = pl.multiple_of(s * GLA_SUB, GLA_SUB)
        qs = qs_ref[pl.ds(r0, GLA_SUB), :]
        cs = c_ref[pl.ds(r0, GLA_SUB), :]
        st = st_ref[...]
        inter = _dot_nt((qs * jnp.exp(cs)).astype(BF16), st.astype(BF16))
        xs = []
        for j in range(GLA_SUB):
            kj = k_ref[pl.ds(r0 + j, 1), :]
            cj = c_ref[pl.ds(r0 + j, 1), :]
            xs.append(qs * kj * jnp.exp(jnp.minimum(cs - cj, 0.0)))
        r_all = _dot_l2(jnp.concatenate(xs, axis=0), eh_ref[...])
        intra = jnp.zeros((GLA_SUB, r_all.shape[1]), F32)
        for j in range(GLA_SUB):
            vj = v_ref[pl.ds(r0 + j, 1), :]
            intra = intra + jnp.where(row_i >= j, r_all[j * GLA_SUB:(j + 1) * GLA_SUB, :], 0.0) * vj
        oi_ref[pl.ds(r0, GLA_SUB), :] = inter + intra
        dec = jnp.exp(tot_ref[pl.ds(r0, 1), :])
        in_sub = (col_l >= r0) & (col_l < r0 + GLA_SUB)
        vtm = jnp.where(in_sub, vt_ref[...], jnp.zeros((), BF16))
        kv = _dot(vtm, kt_ref[...])
        st_ref[...] = dec * st + kv * bdm_ref[...]
        return carry

    lax.fori_loop(0, n_sub, sub, 0)

    gn = gn_ref[...]
    for h in range(GLA_H):
        sl = slice(h * GLA_DV, (h + 1) * GLA_DV)
        o = oi_ref[:, sl]
        ms = jnp.mean(o * o, axis=-1, keepdims=True)
        o_ref[:, sl] = o * lax.rsqrt(ms + RMS_EPS) * gn * _silu(r_ref[:, sl])


def _gla_consts(Lc):
    r = np.arange(Lc)
    same = (r[:, None] // GLA_SUB) == (r[None, :] // GLA_SUB)
    tri = (same & (r[None, :] <= r[:, None])).astype(np.float32)
    bones = same.astype(np.float32)
    hk = np.arange(GLA_H * GLA_DK) // GLA_DK
    hv = np.arange(GLA_H * GLA_DV) // GLA_DV
    eh = (hk[:, None] == hv[None, :]).astype(np.float32)
    bdm = eh.T.copy()
    return jnp.asarray(tri, BF16), jnp.asarray(bones, BF16), jnp.asarray(eh, BF16), jnp.asarray(bdm, F32)


def _gla(a_q, a_k, a_v, a_r, a_lr, wa, ba, gn, B, S, Lc=256):
    HK, HV = GLA_H * GLA_DK, GLA_H * GLA_DV
    nb = S // Lc
    tri, bones, eh, bdm = _gla_consts(Lc)
    row = lambda w: pl.BlockSpec((Lc, w), lambda b, i: (b * nb + i, 0))
    return pl.pallas_call(
        _gla_body,
        out_shape=jax.ShapeDtypeStruct((B * S, HV), F32),
        grid=(B, nb),
        in_specs=[row(HK), row(HK), row(HV), row(HV), row(LANE),
                  _const_spec((LANE, HK)), _const_spec((1, HK)), _const_spec((1, GLA_DV)),
                  _const_spec((Lc, Lc)), _const_spec((Lc, Lc)), _const_spec((HK, HV)), _const_spec((HV, HK))],
        out_specs=row(HV),
        scratch_shapes=[pltpu.VMEM((HV, HK), F32), pltpu.VMEM((Lc, HK), F32), pltpu.VMEM((Lc, HK), F32),
                        pltpu.VMEM((Lc, HK), F32), pltpu.VMEM((Lc, HK), BF16), pltpu.VMEM((HV, Lc), BF16),
                        pltpu.VMEM((Lc, HV), F32)],
        compiler_params=_params(("parallel", "arbitrary")),
        name="gla",
    )(a_q, a_k, a_v, a_r, a_lr, wa, ba, gn, tri, bones, eh, bdm)


def _mlstm_body(qk_ref, v_ref, og_ref, if_ref, cw_ref, cb_ref, tri_ref, bdm_ref, e4_ref,
                o_ref, xx_ref, ct_ref, n_ref, m_ref, kw_ref, vt_ref, act_ref):
    Lc = v_ref.shape[0]
    L = MLSTM_CHUNK
    HK = MLSTM_H * MLSTM_DK
    tail = 8

    @pl.when(pl.program_id(1) == 0)
    def _():
        ct_ref[...] = jnp.zeros(ct_ref.shape, F32)
        n_ref[...] = jnp.zeros(n_ref.shape, F32)
        m_ref[...] = jnp.zeros(m_ref.shape, F32)
        xx_ref[0:tail, :] = jnp.zeros((tail, xx_ref.shape[1]), F32)

    xx_ref[tail:tail + Lc, :] = qk_ref[...]
    y = jnp.zeros((Lc, 2 * HK), F32) + cb_ref[...]
    for kk in range(CONV_W):
        y = y + cw_ref[kk:kk + 1, :] * xx_ref[pl.ds(tail - (CONV_W - 1) + kk, Lc), :]
    xx_ref[0:tail, :] = xx_ref[Lc:Lc + tail, :]
    act_ref[...] = _silu(y)

    gates = if_ref[...]
    logf = _log_sigmoid(gates)
    bcum = _dot_r2(tri_ref[...], logf)
    kw_ref[...] = jnp.zeros(kw_ref.shape, BF16)
    vt_ref[...] = v_ref[...].T.astype(BF16)
    eye = (_iota((L, L), 0) == _iota((L, L), 1)).astype(F32)
    causal = _iota((L, L), 0) >= _iota((L, L), 1)
    lane_hk = _iota((1, HK), 1) >> (MLSTM_DK.bit_length() - 1)
    col_l = _iota((1, Lc), 1)

    for ci in range(Lc // L):
        rows = slice(ci * L, (ci + 1) * L)
        q_all = act_ref[rows, 0:HK]
        k_all = act_ref[rows, HK:2 * HK] * (MLSTM_DK ** -0.5)
        ct = ct_ref[...]
        n_row = n_ref[...]
        inter_all = _dot_nt(q_all.astype(BF16), ct.astype(BF16))
        qn_all = _dot_l2(q_all * n_row, e4_ref[...])
        k_bf = k_all.astype(BF16)
        wk_full = jnp.zeros((L, HK), F32)
        dec_row = jnp.zeros((1, HK), F32)
        for h in range(MLSTM_H):
            icol = gates[rows, h:h + 1]
            bcol = bcum[rows, MLSTM_H + h:MLSTM_H + h + 1]
            m_st = m_ref[0:1, h:h + 1]
            r_row = jnp.sum((icol - bcol) * eye, axis=0, keepdims=True)
            dmat = jnp.where(causal, bcol + r_row, -jnp.inf)
            inter_log = bcol + m_st
            m_row = jnp.maximum(inter_log, jnp.max(dmat, axis=-1, keepdims=True))
            w_inter = jnp.exp(inter_log - m_row)
            head = lane_hk == h
            qh = jnp.where(head, q_all, 0.0).astype(BF16)
            s_qk = _dot_nt(qh, k_bf) * jnp.exp(dmat - m_row)
            vh = v_ref[rows, h * MLSTM_DV:(h + 1) * MLSTM_DV]
            num = w_inter * inter_all[:, h * MLSTM_DV:(h + 1) * MLSTM_DV] + _dot(s_qk.astype(BF16), vh.astype(BF16))
            den = w_inter * qn_all[:, h:h + 1] + jnp.sum(s_qk, axis=-1, keepdims=True)
            hh = num / jnp.maximum(jnp.abs(den), jnp.exp(-m_row))
            o_ref[rows, h * MLSTM_DV:(h + 1) * MLSTM_DV] = hh * _sigmoid(og_ref[rows, h * MLSTM_DV:(h + 1) * MLSTM_DV])
            m_new = m_row[L - 1:L, :]
            b_last = bcol[L - 1:L, :]
            w_k = jnp.exp(b_last - bcol + icol - m_new)
            decay = jnp.exp(b_last + m_st - m_new)
            wk_full = wk_full + jnp.where(head, w_k, 0.0)
            dec_row = dec_row + jnp.where(head, decay, 0.0)
            m_ref[0:1, h:h + 1] = m_new
        kw = k_all * wk_full
        kw_ref[rows, :] = kw.astype(BF16)
        in_chunk = (col_l >= ci * L) & (col_l < (ci + 1) * L)
        vtm = jnp.where(in_chunk, vt_ref[...], jnp.zeros((), BF16))
        ct_ref[...] = dec_row * ct + _dot(vtm, kw_ref[...]) * bdm_ref[...]
        n_ref[...] = dec_row * n_row + jnp.sum(kw, axis=0, keepdims=True)


def _mlstm_consts(Lc):
    r = np.arange(Lc)
    same = (r[:, None] // MLSTM_CHUNK) == (r[None, :] // MLSTM_CHUNK)
    tri = (same & (r[None, :] <= r[:, None])).astype(np.float32)
    hk = np.arange(MLSTM_H * MLSTM_DK) // MLSTM_DK
    hv = np.arange(MLSTM_H * MLSTM_DV) // MLSTM_DV
    eh = (hk[:, None] == hv[None, :]).astype(np.float32)
    e4 = (hk[:, None] == np.arange(LANE)[None, :]).astype(np.float32)
    return jnp.asarray(tri, BF16), jnp.asarray(eh.T.copy(), F32), jnp.asarray(e4, BF16)


def _mlstm(c_qk, c_v, c_o, c_if, cw, cb, B, S, Lc=256):
    HK, HV = MLSTM_H * MLSTM_DK, MLSTM_H * MLSTM_DV
    nb = S // Lc
    tri, bdm, e4 = _mlstm_consts(Lc)
    row = lambda w: pl.BlockSpec((Lc, w), lambda b, i: (b * nb + i, 0))
    return pl.pallas_call(
        _mlstm_body,
        out_shape=jax.ShapeDtypeStruct((B * S, HV), F32),
        grid=(B, nb),
        in_specs=[row(2 * HK), row(HV), row(HV), row(LANE),
                  _const_spec((CONV_W, 2 * HK)), _const_spec((1, 2 * HK)),
                  _const_spec((Lc, Lc)), _const_spec((HV, HK)), _const_spec((HK, LANE))],
        out_specs=row(HV),
        scratch_shapes=[pltpu.VMEM((Lc + 8, 2 * HK), F32), pltpu.VMEM((HV, HK), F32), pltpu.VMEM((1, HK), F32),
                        pltpu.VMEM((8, LANE), F32), pltpu.VMEM((Lc, HK), BF16), pltpu.VMEM((HV, Lc), BF16),
                        pltpu.VMEM((Lc, 2 * HK), F32)],
        compiler_params=_params(("parallel", "arbitrary")),
        name="mlstm",
    )(c_qk, c_v, c_o, c_if, cw, cb, tri, bdm, e4)


def _group_rms(x, gavg, gain):
    ms = _dot_l2(x * x, gavg)
    return x * lax.rsqrt(ms + RMS_EPS) * gain


def _rope_lanes(x, cos, sin_signed):
    half = NSA_D // 2
    first = (_iota((1, LANE), 1) & (NSA_D - 1)) < half
    swapped = jnp.where(first, pltpu.roll(x, LANE - half, 1), pltpu.roll(x, half, 1))
    return x * cos + swapped * sin_signed


def _nsa_prep_body(q_ref, ks_ref, kw_ref, vs_ref, vw_ref, g_ref, cos_ref, sin_ref, gq_ref, gs_ref, gw_ref, gavg_ref,
                   qt_ref, kso_ref, kwo_ref, vst_ref, vwt_ref, gt_ref):
    cos, sin = cos_ref[...], sin_ref[...]
    gavg = gavg_ref[...]
    for cb in range(NSA_H * NSA_D // LANE):
        sl = slice(cb * LANE, (cb + 1) * LANE)
        qn = _rope_lanes(_group_rms(q_ref[:, sl], gavg, gq_ref[...]), cos, sin) * (NSA_D ** -0.5)
        qt_ref[sl, :] = qn.T.astype(BF16)
    kso_ref[...] = _rope_lanes(_group_rms(ks_ref[...], gavg, gs_ref[...]), cos, sin).astype(BF16)
    kwo_ref[...] = _rope_lanes(_group_rms(kw_ref[...], gavg, gw_ref[...]), cos, sin).astype(BF16)
    vst_ref[...] = vs_ref[...].T.astype(BF16)
    tw = vwt_ref.shape[2]
    vwt = vw_ref[...].T.astype(BF16)
    for j in range(vwt_ref.shape[0]):
        vwt_ref[j] = vwt[:, j * tw:(j + 1) * tw]
    gt_ref[...] = _sigmoid(g_ref[...]).T


def _nsa_prep(b_q, b_ks, b_kw, b_vs, b_vw, b_g, cos, sin, gq, gs, gw, gavg, B, S, tm=256):
    nb = S // tm
    row = lambda w: pl.BlockSpec((tm, w), lambda b, i: (b * nb + i, 0))
    tab = pl.BlockSpec((tm, LANE), lambda b, i: (i, 0))
    HD = NSA_H * NSA_D
    return pl.pallas_call(
        _nsa_prep_body,
        out_shape=[jax.ShapeDtypeStruct((B, HD, S), BF16),
                   jax.ShapeDtypeStruct((B, S, LANE), BF16),
                   jax.ShapeDtypeStruct((B, S, LANE), BF16),
                   jax.ShapeDtypeStruct((B, S // tm, LANE, tm), BF16),
                   jax.ShapeDtypeStruct((B, S // LANE, LANE, LANE), BF16),
                   jax.ShapeDtypeStruct((B, LANE, S), F32)],
        grid=(B, nb),
        in_specs=[row(HD), row(LANE), row(LANE), row(LANE), row(LANE), row(LANE), tab, tab,
                  _const_spec((1, LANE)), _const_spec((1, LANE)), _const_spec((1, LANE)), _const_spec((LANE, LANE))],
        out_specs=[pl.BlockSpec((None, HD, tm), lambda b, i: (b, 0, i)),
                   pl.BlockSpec((None, tm, LANE), lambda b, i: (b, i, 0)),
                   pl.BlockSpec((None, tm, LANE), lambda b, i: (b, i, 0)),
                   pl.BlockSpec((None, None, LANE, tm), lambda b, i: (b, i, 0, 0)),
                   pl.BlockSpec((None, tm // LANE, LANE, LANE), lambda b, i: (b, i, 0, 0)),
                   pl.BlockSpec((None, LANE, tm), lambda b, i: (b, 0, i))],
        compiler_params=_params(("parallel", "parallel")),
        name="nsa_prep",
    )(b_q, b_ks, b_kw, b_vs, b_vw, b_g, cos, sin, gq, gs, gw, gavg)


def _gelu_tanh(x):
    return 0.5 * x * (1.0 + jnp.tanh(0.7978845608028654 * (x + 0.044715 * x * x * x)))


def _nsa_cmp_body(xk_ref, xv_ref, pe_ref, wt_ref, wb_ref, w2_ref, cos_ref, sin_ref, gk_ref, gavg_ref,
                  kc_ref, vct_ref):
    n = xk_ref.shape[0]

    def compress(x, which):
        u = _dot((x + pe_ref[which, 0:1, :]).astype(BF16), wt_ref[which])
        v = _dot((x + pe_ref[which, 1:2, :]).astype(BF16), wb_ref[which])
        hid = u + pltpu.roll(v, n - 1, 0)
        return _dot(_gelu_tanh(hid).astype(BF16), w2_ref[which])

    ck = compress(xk_ref[...], 0)
    kc_ref[...] = _rope_lanes(_group_rms(ck, gavg_ref[...], gk_ref[...]), cos_ref[...], sin_ref[...]).astype(BF16)
    vct_ref[...] = compress(xv_ref[...], 1).T.astype(BF16)


def _nsa_cmp(xk, xv, pe, wt, wb, w2, cos, sin, gk, gavg):
    B, n, W = xk.shape
    return pl.pallas_call(
        _nsa_cmp_body,
        out_shape=[jax.ShapeDtypeStruct((B, n, LANE), BF16), jax.ShapeDtypeStruct((B, LANE, n), BF16)],
        grid=(B,),
        in_specs=[pl.BlockSpec((None, n, W), lambda b: (b, 0, 0)), pl.BlockSpec((None, n, W), lambda b: (b, 0, 0)),
                  _const_spec(pe.shape), _const_spec(wt.shape), _const_spec(wb.shape), _const_spec(w2.shape),
                  _const_spec((n, LANE)), _const_spec((n, LANE)), _const_spec((1, LANE)), _const_spec((LANE, LANE))],
        out_specs=[pl.BlockSpec((None, n, LANE), lambda b: (b, 0, 0)), pl.BlockSpec((None, LANE, n), lambda b: (b, 0, 0))],
        compiler_params=_params(("parallel",)),
        name="nsa_compress",
    )(xk, xv, pe, wt, wb, w2, cos, sin, gk, gavg)


def _nsa_body(qt_ref, kc_ref, vct_ref, ks_ref, vst_ref, kw_ref, vwt_ref, gt_ref, ovl_ref, o_ref, sel_ref):
    TQ = LANE
    HPG = NSA_H // NSA_G
    n_cmp = kc_ref.shape[0]
    n_sel = ovl_ref.shape[0]
    KC = vst_ref.shape[2]
    i = pl.program_id(1)
    s0 = i * TQ
    t_row = s0 + _iota((1, TQ), 1)
    t4 = jnp.concatenate([t_row] * HPG, axis=1)
    zeros_q = jnp.zeros((NSA_D, HPG * TQ), BF16)
    n_win = WIN // LANE + 1
    j0 = jnp.maximum(i - (n_win - 1), 0)
    w_start = pl.multiple_of(j0 * LANE, LANE)

    for g in range(NSA_G):
        q4 = jnp.concatenate([qt_ref[(g * HPG + h) * NSA_D:(g * HPG + h + 1) * NSA_D, :] for h in range(HPG)], axis=1)
        qpad = jnp.concatenate([q4, zeros_q] if g == 0 else [zeros_q, q4], axis=0)
        grp = slice(g * NSA_D, (g + 1) * NSA_D)

        sc = _dot(kc_ref[...], qpad)
        cend = _iota((n_cmp, 1), 0) * CMP_STRIDE + (CMP_LEN - 1)
        ok = cend <= t4
        m = jnp.max(jnp.where(ok, sc, NEG), axis=0, keepdims=True)
        p = jnp.where(ok, jnp.exp(sc - m), 0.0)
        p = p * (1.0 / jnp.maximum(jnp.sum(p, axis=0, keepdims=True), 1e-30))
        o_c = _dot(vct_ref[...], p.astype(BF16))[grp, :]

        psum = p[:, 0:TQ]
        for h in range(1, HPG):
            psum = psum + p[:, h * TQ:(h + 1) * TQ]
        imp = _dot_r2(ovl_ref[...], psum)
        blk = _iota((n_sel, 1), 0)
        cur = t_row >> (SLC_LEN.bit_length() - 1)
        valid = blk * SLC_LEN <= t_row
        forced = (blk == 0) | (blk == cur) | (blk == cur - 1)
        score = jnp.where(valid, jnp.where(forced, FORCED_SCORE, imp), NEG)
        chosen = jnp.zeros((n_sel, TQ), jnp.bool_)
        for _ in range(min(SLC_TOPK, n_sel)):
            best = jnp.max(score, axis=0, keepdims=True)
            first = jnp.min(jnp.where(score == best, blk, n_sel), axis=0, keepdims=True)
            hit = blk == first
            chosen = chosen | (hit & (best > 0.5 * NEG))
            score = jnp.where(hit, NEG, score)
        sel_ref[...] = jnp.where(chosen, 0.0, NEG)

        def sel_step(c, carry):
            m_i, l_i, acc = carry
            k0 = pl.multiple_of(c * KC, KC)
            s = _dot(ks_ref[pl.ds(k0, KC), :], qpad)
            rows = [jnp.broadcast_to(sel_ref[pl.ds(c * (KC // SLC_LEN) + r, 1), :], (SLC_LEN, TQ))
                    for r in range(KC // SLC_LEN)]
            bias = jnp.concatenate(rows, axis=0)
            bias = jnp.where(k0 + _iota((KC, 1), 0) <= t_row, bias, NEG)
            s = s + jnp.concatenate([bias] * HPG, axis=1)
            m_new = jnp.maximum(m_i, jnp.max(s, axis=0, keepdims=True))
            alpha = jnp.exp(m_i - m_new)
            pp = jnp.exp(s - m_new)
            l_new = alpha * l_i + jnp.sum(pp, axis=0, keepdims=True)
            acc_new = alpha * acc + _dot(vst_ref[c], pp.astype(BF16))[grp, :]
            return m_new, l_new, acc_new

        init = (jnp.full((1, HPG * TQ), NEG, F32), jnp.zeros((1, HPG * TQ), F32), jnp.zeros((NSA_D, HPG * TQ), F32))
        n_steps = (s0 + TQ + KC - 1) // KC
        _, l_s, acc_s = lax.fori_loop(0, n_steps, sel_step, init)
        o_s = acc_s * (1.0 / l_s)

        sw = _dot(kw_ref[pl.ds(w_start, n_win * LANE), :], qpad)
        kpos = w_start + _iota((n_win * LANE, 1), 0)
        okw = (kpos <= t4) & (kpos > t4 - WIN)
        mw = jnp.max(jnp.where(okw, sw, NEG), axis=0, keepdims=True)
        pw = jnp.where(okw, jnp.exp(sw - mw), 0.0)
        pw = (pw * (1.0 / jnp.maximum(jnp.sum(pw, axis=0, keepdims=True), 1e-30))).astype(BF16)
        o_w = jnp.zeros((NSA_D, HPG * TQ), F32)
        for r in range(n_win):
            o_w = o_w + _dot(vwt_ref[j0 + r], pw[r * LANE:(r + 1) * LANE, :])[grp, :]

        for hp in range(HPG // 2):
            tiles = []
            for h in (2 * hp, 2 * hp + 1):
                gr = (g * HPG + h) * 3
                cs = slice(h * TQ, (h + 1) * TQ)
                tiles.append(gt_ref[gr:gr + 1, :] * o_c[:, cs] + gt_ref[gr + 1:gr + 2, :] * o_s[:, cs]
                             + gt_ref[gr + 2:gr + 3, :] * o_w[:, cs])
            col = (g * HPG + 2 * hp) * NSA_D
            o_ref[:, col:col + 2 * NSA_D] = jnp.concatenate(tiles, axis=0).T


def _nsa_main(qt, kc, vct, ks, vst, kw, vwt, gt, ovl, B, S):
    HD = NSA_H * NSA_D
    TQ = LANE
    n_cmp = kc.shape[1]
    whole = lambda shape: pl.BlockSpec((None,) + shape, lambda b, i: (b,) + (0,) * len(shape),
                                       pipeline_mode=pl.Buffered(1))
    return pl.pallas_call(
        _nsa_body,
        out_shape=jax.ShapeDtypeStruct((B * S, HD), F32),
        grid=(B, S // TQ),
        in_specs=[pl.BlockSpec((None, HD, TQ), lambda b, i: (b, 0, i)),
                  whole((n_cmp, LANE)), whole((LANE, n_cmp)),
                  whole((S, LANE)), whole(vst.shape[1:]), whole((S, LANE)), whole(vwt.shape[1:]),
                  pl.BlockSpec((None, LANE, TQ), lambda b, i: (b, 0, i)),
                  _const_spec(ovl.shape)],
        out_specs=pl.BlockSpec((TQ, HD), lambda b, i: (b * (S // TQ) + i, 0)),
        scratch_shapes=[pltpu.VMEM((ovl.shape[0], TQ), F32)],
        compiler_params=_params(("parallel", "arbitrary")),
        name="nsa_attention",
    )(qt, kc, vct, ks, vst, kw, vwt, gt, ovl)


def _merge_body(x_ref, oa_ref, ob_ref, oc_ref, gates_ref, wbr_ref, wo_ref, o_ref):
    D = x_ref.shape[1]
    y = jnp.zeros(x_ref.shape, F32)
    for j, br in enumerate((oa_ref, ob_ref, oc_ref)):
        y = y + _sigmoid(gates_ref[:, j * D:(j + 1) * D]) * _dot(br[...].astype(BF16), wbr_ref[j])
    o_ref[...] = x_ref[...] + _dot(y.astype(BF16), wo_ref[...])


def _merge(x, o_a, o_b, o_c, gates, wbr, wo, tm=512):
    T, D = x.shape
    row = lambda w: pl.BlockSpec((tm, w), lambda i: (i, 0))
    return pl.pallas_call(
        _merge_body,
        out_shape=jax.ShapeDtypeStruct((T, D), F32),
        grid=(T // tm,),
        in_specs=[row(D), row(MIX_W), row(MIX_W), row(MIX_W), row(3 * D), _const_spec(wbr.shape), _const_spec(wo.shape)],
        out_specs=row(D),
        compiler_params=_params(("parallel",)),
        name="merge_out",
    )(x, o_a, o_b, o_c, gates, wbr, wo)


def _pack_in_proj(w_in, b_in):
    ws, bs = [], []
    for _, off, w, wp in _SEGS:
        ws.append(jnp.pad(w_in[..., off:off + w], ((0, 0), (0, 0), (0, wp - w))))
        bs.append(jnp.pad(b_in[..., off:off + w], ((0, 0), (0, wp - w))))
    return jnp.concatenate(ws, axis=-1).astype(BF16), jnp.concatenate(bs, axis=-1)[:, None, :]


def _rope_tables(pos):
    half = NSA_D // 2
    freqs = ROPE_THETA ** (-jnp.arange(half, dtype=F32) / half)
    ang = pos.astype(F32)[:, None] * freqs[None, :]
    cos, sin = jnp.cos(ang), jnp.sin(ang)
    reps = LANE // NSA_D
    return jnp.tile(jnp.concatenate([cos, cos], axis=1), (1, reps)), jnp.tile(jnp.concatenate([-sin, sin], axis=1), (1, reps))


def _cmp_weights(w1, w2):
    L = w1.shape[0]
    w1r = w1.reshape(L, 2, 2, CMP_STRIDE, NSA_D, CMP_HIDDEN)
    eye_g = jnp.eye(NSA_G, dtype=w1.dtype)
    ex = jnp.einsum('lwstdh,gk->lwstgdkh', w1r, eye_g)
    ex = ex.reshape(L, 2, 2, CMP_STRIDE * NSA_G * NSA_D, NSA_G * CMP_HIDDEN)
    w2x = jnp.einsum('lwhd,gk->lwghkd', w2, eye_g).reshape(L, 2, NSA_G * CMP_HIDDEN, NSA_G * NSA_D)
    return ex[:, :, 0].astype(BF16), ex[:, :, 1].astype(BF16), w2x.astype(BF16)


def _cmp_pos_rows(pe):
    L = pe.shape[0]
    r = pe.reshape(L, 2, 2, CMP_STRIDE, 1, NSA_D)
    return jnp.broadcast_to(r, (L, 2, 2, CMP_STRIDE, NSA_G, NSA_D)).reshape(L, 2, 2, CMP_STRIDE * NSA_G * NSA_D)


def _overlap_t(n_cmp_pad, n_sel):
    ci = np.arange(n_cmp_pad)[None, :] * CMP_STRIDE
    sj = np.arange(n_sel)[:, None] * SLC_LEN
    return jnp.asarray(((ci < sj + SLC_LEN) & (ci + CMP_LEN > sj)).astype(np.float32), BF16)


def _lane_gain(g):
    return jnp.tile(g, LANE // g.shape[-1])[None, :]


def _layer(x, lw, consts, B, S):
    cos, sin, cos_c, sin_c, gavg, ovl = consts
    T = B * S
    x = _ffn(x, lw["ffn1_norm"], lw["ffn1_wg"], lw["ffn1_wu"], lw["ffn1_wd"])
    z = dict(zip([s[0] for s in _SEGS], _proj(x, lw["mix_norm"], lw["w_in"], lw["b_in"])))
    o_a = _gla(z["a_q"], z["a_k"], z["a_v"], z["a_r"], z["a_lr"], lw["gla_wa"], lw["gla_ba"], lw["gla_gn"], B, S)
    o_c = _mlstm(z["c_qk"], z["c_v"], z["c_o"], z["c_if"], lw["conv_w"], lw["conv_b"], B, S)
    qn = lw["nsa_qk_norm"]
    qt, ks, kw, vst, vwt, gt = _nsa_prep(z["b_q"], z["b_ks"], z["b_kw"], z["b_vs"], z["b_vw"], z["b_g"], cos, sin,
                                         _lane_gain(qn[0]), _lane_gain(qn[2]), _lane_gain(qn[3]), gavg, B, S)
    n_blk = S // CMP_STRIDE
    xk = z["b_kc"].reshape(B, n_blk, CMP_STRIDE * LANE)
    xv = z["b_vc"].reshape(B, n_blk, CMP_STRIDE * LANE)
    kc, vct = _nsa_cmp(xk, xv, lw["cmp_pe"], lw["cmp_wt"], lw["cmp_wb"], lw["cmp_w2"], cos_c, sin_c,
                       _lane_gain(qn[1]), gavg)
    o_b = _nsa_main(qt, kc, vct, ks, vst, kw, vwt, gt, ovl, B, S)
    x = _merge(x, o_a, o_b, o_c, z["gates"], lw["w_branch"], lw["w_out"])
    return _ffn(x, lw["ffn2_norm"], lw["ffn2_wg"], lw["ffn2_wu"], lw["ffn2_wd"])


def kernel(x, ffn1_norm, ffn1_w_gate, ffn1_w_up, ffn1_w_down, mix_norm, w_in, b_in, gla_w_alpha, gla_b_alpha, gla_out_norm, nsa_qk_norm, nsa_cmp_pos, nsa_cmp_w1, nsa_cmp_w2, mlstm_conv_w, mlstm_conv_b, w_branch, w_out, ffn2_norm, ffn2_w_gate, ffn2_w_up, ffn2_w_down):
    B, S, D = x.shape
    w_in_p, b_in_p = _pack_in_proj(w_in, b_in)
    cmp_wt, cmp_wb, cmp_w2 = _cmp_weights(nsa_cmp_w1, nsa_cmp_w2)
    layers = {
        "ffn1_norm": ffn1_norm[:, None, :], "ffn1_wg": ffn1_w_gate.astype(BF16), "ffn1_wu": ffn1_w_up.astype(BF16),
        "ffn1_wd": ffn1_w_down.astype(BF16),
        "mix_norm": mix_norm[:, None, :], "w_in": w_in_p, "b_in": b_in_p,
        "gla_wa": jnp.pad(gla_w_alpha, ((0, 0), (0, LANE - GLA_RANK), (0, 0))), "gla_ba": gla_b_alpha[:, None, :],
        "gla_gn": gla_out_norm[:, None, :],
        "nsa_qk_norm": nsa_qk_norm, "cmp_pe": _cmp_pos_rows(nsa_cmp_pos), "cmp_wt": cmp_wt, "cmp_wb": cmp_wb,
        "cmp_w2": cmp_w2,
        "conv_w": mlstm_conv_w, "conv_b": mlstm_conv_b[:, None, :],
        "w_branch": w_branch.astype(BF16), "w_out": w_out.astype(BF16),
        "ffn2_norm": ffn2_norm[:, None, :], "ffn2_wg": ffn2_w_gate.astype(BF16), "ffn2_wu": ffn2_w_up.astype(BF16),
        "ffn2_wd": ffn2_w_down.astype(BF16),
    }
    n_blk = S // CMP_STRIDE
    cos, sin = _rope_tables(jnp.arange(S))
    cos_c, sin_c = _rope_tables(jnp.arange(n_blk) * CMP_STRIDE + CMP_LEN - 1)
    gavg = jnp.asarray(np.kron(np.eye(LANE // NSA_D), np.full((NSA_D, NSA_D), 1.0 / NSA_D)), BF16)
    consts = (cos, sin, cos_c, sin_c, gavg, _overlap_t(n_blk, S // SLC_LEN))

    def step(xc, lw):
        return _layer(xc, lw, consts, B, S), None

    out, _ = lax.scan(step, x.reshape(B * S, D), layers)
    return out.reshape(B, S, D)
```

```python
import functools

import numpy as np
import jax
import jax.numpy as jnp
from jax import lax
from jax.experimental import pallas as pl
from jax.experimental.pallas import tpu as pltpu

F32 = jnp.float32
BF16 = jnp.bfloat16

RMS_EPS = 1e-6
ROPE_THETA = 10000.0

GLA_H, GLA_DK, GLA_DV, GLA_RANK, GLA_GATE_NORM = 4, 64, 128, 16, 16.0
GLA_SUB = 16
NSA_H, NSA_G, NSA_D = 8, 2, 64
CMP_LEN, CMP_STRIDE, CMP_HIDDEN = 32, 16, 256
SLC_LEN, SLC_TOPK, WIN = 64, 16, 512
FORCED_SCORE = 1e4
MLSTM_H, MLSTM_DK, MLSTM_DV, CONV_W = 4, 64, 128, 4
MLSTM_CHUNK = 64
MIX_W = 512
LANE = 128
NEG = -1e30
LOG2E = 1.4426950408889634
SEL_KC = 512
VT_ROWS = 80

_SEGS = (
    ("a_q", 0, 256, 256), ("a_k", 256, 256, 256), ("a_v", 512, 512, 512), ("a_r", 1024, 512, 512),
    ("a_lr", 1536, 16, 128),
    ("b_q", 1552, 512, 512), ("b_kc", 2064, 128, 128), ("b_vc", 2192, 128, 128), ("b_ks", 2320, 128, 128),
    ("b_vs", 2448, 128, 128), ("b_kw", 2576, 128, 128), ("b_vw", 2704, 128, 128), ("b_g", 2832, 24, 128),
    ("c_qk", 2856, 512, 512), ("c_v", 3368, 512, 512), ("c_o", 3880, 512, 512), ("c_if", 4392, 8, 128),
    ("gates", 4400, 3072, 3072),
)
_N_PACK = sum(s[3] for s in _SEGS)


def _dot(a, b):
    return jnp.dot(a, b, preferred_element_type=F32)


def _dot_nt(a, b):
    return lax.dot_general(a, b, (((1,), (1,)), ((), ())), preferred_element_type=F32)


def _split2(a):
    hi = a.astype(BF16)
    lo = (a - hi.astype(F32)).astype(BF16)
    return hi, lo


def _dot_l2(a, b):
    hi, lo = _split2(a)
    return _dot(hi, b) + _dot(lo, b)


def _dot_r2(a, b):
    hi, lo = _split2(b)
    return _dot(a, hi) + _dot(a, lo)


def _log_sigmoid(x):
    return jnp.minimum(x, 0.0) - jnp.log(1.0 + jnp.exp(-jnp.abs(x)))


def _sigmoid(x):
    return 1.0 / (1.0 + jnp.exp(-x))


def _silu(x):
    return x * _sigmoid(x)


def _iota(shape, dim):
    return lax.broadcasted_iota(jnp.int32, shape, dim)


def _const_spec(shape):
    nd = len(shape)
    return pl.BlockSpec(shape, lambda *_: (0,) * nd, pipeline_mode=pl.Buffered(1))


def _params(sem, vmem_mb=56):
    return pltpu.CompilerParams(dimension_semantics=sem, vmem_limit_bytes=vmem_mb * 1024 * 1024)


def _ffn_body(x_ref, g_ref, wg_ref, wu_ref, wd_ref, o_ref, *, n_chunks):
    x = x_ref[...]
    ms = jnp.mean(x * x, axis=-1, keepdims=True)
    h = (x * lax.rsqrt(ms + RMS_EPS) * g_ref[...]).astype(BF16)
    fc = wg_ref.shape[1] // n_chunks
    acc = jnp.zeros(x.shape, F32)
    for c in range(n_chunks):
        a = _dot(h, wg_ref[:, c * fc:(c + 1) * fc])
        u = _dot(h, wu_ref[:, c * fc:(c + 1) * fc])
        t = (_silu(a) * u).astype(BF16)
        acc = acc + _dot(t, wd_ref[c * fc:(c + 1) * fc, :])
    o_ref[...] = x + 0.5 * acc


def _ffn(x, g, wg, wu, wd, tm=512):
    T, D = x.shape
    F = wg.shape[1]
    return pl.pallas_call(
        functools.partial(_ffn_body, n_chunks=2),
        out_shape=jax.ShapeDtypeStruct((T, D), F32),
        grid=(T // tm,),
        in_specs=[pl.BlockSpec((tm, D), lambda i: (i, 0)), _const_spec((1, D)),
                  _const_spec((D, F)), _const_spec((D, F)), _const_spec((F, D))],
        out_specs=pl.BlockSpec((tm, D), lambda i: (i, 0)),
        compiler_params=_params(("parallel",)),
        name="ffn",
    )(x, g, wg, wu, wd)


def _proj_body(x_ref, g_ref, w_ref, b_ref, *o_refs):
    x = x_ref[...]
    ms = jnp.mean(x * x, axis=-1, keepdims=True)
    h = (x * lax.rsqrt(ms + RMS_EPS) * g_ref[...]).astype(BF16)
    off = 0
    for o_ref in o_refs:
        w = o_ref.shape[1]
        o_ref[...] = _dot(h, w_ref[:, off:off + w]) + b_ref[:, off:off + w]
        off += w


def _proj(x, g, w, b, tm=256):
    T, D = x.shape
    return pl.pallas_call(
        _proj_body,
        out_shape=[jax.ShapeDtypeStruct((T, s[3]), F32) for s in _SEGS],
        grid=(T // tm,),
        in_specs=[pl.BlockSpec((tm, D), lambda i: (i, 0)), _const_spec((1, D)),
                  _const_spec((D, _N_PACK)), _const_spec((1, _N_PACK))],
        out_specs=[pl.BlockSpec((tm, s[3]), lambda i: (i, 0)) for s in _SEGS],
        compiler_params=_params(("parallel",)),
        name="in_proj",
    )(x, g, w, b)


def _gla_body(q_ref, k_ref, v_ref, r_ref, lr_ref, wa_ref, ba_ref, gn_ref, tri_ref, bones_ref, eh_ref, bdm_ref,
              o_ref, st_ref, qs_ref, c_ref, tot_ref, kt_ref, vt_ref, oi_ref):
    Lc = q_ref.shape[0]
    n_sub = Lc // GLA_SUB

    @pl.when(pl.program_id(1) == 0)
    def _():
        st_ref[...] = jnp.zeros(st_ref.shape, F32)

    lr_hi, lr_lo = _split2(lr_ref[...])
    wa_hi, wa_lo = _split2(wa_ref[...])
    u = _dot(lr_hi, wa_hi) + _dot(lr_hi, wa_lo) + _dot(lr_lo, wa_hi) + ba_ref[...]
    g = _log_sigmoid(u) * (1.0 / GLA_GATE_NORM)
    c = _dot_r2(tri_ref[...], g)
    tot = _dot_r2(bones_ref[...], g)
    k = k_ref[...]
    qs_ref[...] = q_ref[...] * (GLA_DK ** -0.5)
    c_ref[...] = c
    tot_ref[...] = tot
    kt_ref[...] = (k * jnp.exp(tot - c)).astype(BF16)
    vt_ref[...] = v_ref[...].T.astype(BF16)
    row_i = _iota((GLA_SUB, 1), 0)
    col_l = _iota((1, Lc), 1)

    def sub(s, carry):
        r0 = pl.multiple_of(s * GLA_SUB, GLA_SUB)
        qs = qs_ref[pl.ds(r0, GLA_SUB), :]
        cs = c_ref[pl.ds(r0, GLA_SUB), :]
        st = st_ref[...]
        inter = _dot_nt((qs * jnp.exp(cs)).astype(BF16), st.astype(BF16))
        xs = []
        for j in range(GLA_SUB):
            kj = k_ref[pl.ds(r0 + j, 1), :]
            cj = c_ref[pl.ds(r0 + j, 1), :]
            xs.append(qs * kj * jnp.exp(jnp.minimum(cs - cj, 0.0)))
        r_all = _dot_l2(jnp.concatenate(xs, axis=0), eh_ref[...])
        intra = jnp.zeros((GLA_SUB, r_all.shape[1]), F32)
        for j in range(GLA_SUB):
            vj = v_ref[pl.ds(r0 + j, 1), :]
            intra = intra + jnp.where(row_i >= j, r_all[j * GLA_SUB:(j + 1) * GLA_SUB, :], 0.0) * vj
        oi_ref[pl.ds(r0, GLA_SUB), :] = inter + intra
        dec = jnp.exp(tot_ref[pl.ds(r0, 1), :])
        in_sub = (col_l >= r0) & (col_l < r0 + GLA_SUB)
        vtm = jnp.where(in_sub, vt_ref[...], jnp.zeros((), BF16))
        kv = _dot(vtm, kt_ref[...])
        st_ref[...] = dec * st + kv * bdm_ref[...]
        return carry

    lax.fori_loop(0, n_sub, sub, 0)

    gn = gn_ref[...]
    for h in range(GLA_H):
        sl = slice(h * GLA_DV, (h + 1) * GLA_DV)
        o = oi_ref[:, sl]
        ms = jnp.mean(o * o, axis=-1, keepdims=True)
        o_ref[:, sl] = o * lax.rsqrt(ms + RMS_EPS) * gn * _silu(r_ref[:, sl])


def _gla_consts(Lc):
    r = np.arange(Lc)
    same = (r[:, None] // GLA_SUB) == (r[None, :] // GLA_SUB)
    tri = (same & (r[None, :] <= r[:, None])).astype(np.float32)
    bones = same.astype(np.float32)
    hk = np.arange(GLA_H * GLA_DK) // GLA_DK
    hv = np.arange(GLA_H * GLA_DV) // GLA_DV
    eh = (hk[:, None] == hv[None, :]).astype(np.float32)
    bdm = eh.T.copy()
    return jnp.asarray(tri, BF16), jnp.asarray(bones, BF16), jnp.asarray(eh, BF16), jnp.asarray(bdm, F32)


def _gla(a_q, a_k, a_v, a_r, a_lr, wa, ba, gn, B, S, Lc=256):
    HK, HV = GLA_H * GLA_DK, GLA_H * GLA_DV
    nb = S // Lc
    tri, bones, eh, bdm = _gla_consts(Lc)
    row = lambda w: pl.BlockSpec((Lc, w), lambda b, i: (b * nb + i, 0))
    return pl.pallas_call(
        _gla_body,
        out_shape=jax.ShapeDtypeStruct((B * S, HV), F32),
        grid=(B, nb),
        in_specs=[row(HK), row(HK), row(HV), row(HV), row(LANE),
                  _const_spec((LANE, HK)), _const_spec((1, HK)), _const_spec((1, GLA_DV)),
                  _const_spec((Lc, Lc)), _const_spec((Lc, Lc)), _const_spec((HK, HV)), _const_spec((HV, HK))],
        out_specs=row(HV),
        scratch_shapes=[pltpu.VMEM((HV, HK), F32), pltpu.VMEM((Lc, HK), F32), pltpu.VMEM((Lc, HK), F32),
                        pltpu.VMEM((Lc, HK), F32), pltpu.VMEM((Lc, HK), BF16), pltpu.VMEM((HV, Lc), BF16),
                        pltpu.VMEM((Lc, HV), F32)],
        compiler_params=_params(("parallel", "arbitrary")),
        name="gla",
    )(a_q, a_k, a_v, a_r, a_lr, wa, ba, gn, tri, bones, eh, bdm)


def _mlstm_body(qk_ref, v_ref, og_ref, if_ref, cw_ref, cb_ref, tri_ref, bdm_ref, e4_ref,
                o_ref, xx_ref, ct_ref, n_ref, m_ref, kw_ref, vt_ref, act_ref):
    Lc = v_ref.shape[0]
    L = MLSTM_CHUNK
    HK = MLSTM_H * MLSTM_DK
    tail = 8

    @pl.when(pl.program_id(1) == 0)
    def _():
        ct_ref[...] = jnp.zeros(ct_ref.shape, F32)
        n_ref[...] = jnp.zeros(n_ref.shape, F32)
        m_ref[...] = jnp.zeros(m_ref.shape, F32)
        xx_ref[0:tail, :] = jnp.zeros((tail, xx_ref.shape[1]), F32)

    xx_ref[tail:tail + Lc, :] = qk_ref[...]
    y = jnp.zeros((Lc, 2 * HK), F32) + cb_ref[...]
    for kk in range(CONV_W):
        y = y + cw_ref[kk:kk + 1, :] * xx_ref[pl.ds(tail - (CONV_W - 1) + kk, Lc), :]
    xx_ref[0:tail, :] = xx_ref[Lc:Lc + tail, :]
    act_ref[...] = _silu(y)

    gates = if_ref[...]
    logf = _log_sigmoid(gates)
    bcum = _dot_r2(tri_ref[...], logf)
    kw_ref[...] = jnp.zeros(kw_ref.shape, BF16)
    vt_ref[...] = v_ref[...].T.astype(BF16)
    eye = (_iota((L, L), 0) == _iota((L, L), 1)).astype(F32)
    causal = _iota((L, L), 0) >= _iota((L, L), 1)
    lane_hk = _iota((1, HK), 1) >> (MLSTM_DK.bit_length() - 1)
    col_l = _iota((1, Lc), 1)

    for ci in range(Lc // L):
        rows = slice(ci * L, (ci + 1) * L)
        q_all = act_ref[rows, 0:HK]
        k_all = act_ref[rows, HK:2 * HK] * (MLSTM_DK ** -0.5)
        ct = ct_ref[...]
        n_row = n_ref[...]
        inter_all = _dot_nt(q_all.astype(BF16), ct.astype(BF16))
        qn_all = _dot_l2(q_all * n_row, e4_ref[...])
        k_bf = k_all.astype(BF16)
        wk_full = jnp.zeros((L, HK), F32)
        dec_row = jnp.zeros((1, HK), F32)
        for h in range(MLSTM_H):
            icol = gates[rows, h:h + 1]
            bcol = bcum[rows, MLSTM_H + h:MLSTM_H + h + 1]
            m_st = m_ref[0:1, h:h + 1]
            r_row = jnp.sum((icol - bcol) * eye, axis=0, keepdims=True)
            dmat = jnp.where(causal, bcol + r_row, -jnp.inf)
            inter_log = bcol + m_st
            m_row = jnp.maximum(inter_log, jnp.max(dmat, axis=-1, keepdims=True))
            w_inter = jnp.exp(inter_log - m_row)
            head = lane_hk == h
            qh = jnp.where(head, q_all, 0.0).astype(BF16)
            s_qk = _dot_nt(qh, k_bf) * jnp.exp(dmat - m_row)
            vh = v_ref[rows, h * MLSTM_DV:(h + 1) * MLSTM_DV]
            num = w_inter * inter_all[:, h * MLSTM_DV:(h + 1) * MLSTM_DV] + _dot(s_qk.astype(BF16), vh.astype(BF16))
            den = w_inter * qn_all[:, h:h + 1] + jnp.sum(s_qk, axis=-1, keepdims=True)
            hh = num / jnp.maximum(jnp.abs(den), jnp.exp(-m_row))
            o_ref[rows, h * MLSTM_DV:(h + 1) * MLSTM_DV] = hh * _sigmoid(og_ref[rows, h * MLSTM_DV:(h + 1) * MLSTM_DV])
            m_new = m_row[L - 1:L, :]
            b_last = bcol[L - 1:L, :]
            w_k = jnp.exp(b_last - bcol + icol - m_new)
            decay = jnp.exp(b_last + m_st - m_new)
            wk_full = wk_full + jnp.where(head, w_k, 0.0)
            dec_row = dec_row + jnp.where(head, decay, 0.0)
            m_ref[0:1, h:h + 1] = m_new
        kw = k_all * wk_full
        kw_ref[rows, :] = kw.astype(BF16)
        in_chunk = (col_l >= ci * L) & (col_l < (ci + 1) * L)
        vtm = jnp.where(in_chunk, vt_ref[...], jnp.zeros((), BF16))
        ct_ref[...] = dec_row * ct + _dot(vtm, kw_ref[...]) * bdm_ref[...]
        n_ref[...] = dec_row * n_row + jnp.sum(kw, axis=0, keepdims=True)


def _mlstm_consts(Lc):
    r = np.arange(Lc)
    same = (r[:, None] // MLSTM_CHUNK) == (r[None, :] // MLSTM_CHUNK)
    tri = (same & (r[None, :] <= r[:, None])).astype(np.float32)
    hk = np.arange(MLSTM_H * MLSTM_DK) // MLSTM_DK
    hv = np.arange(MLSTM_H * MLSTM_DV) // MLSTM_DV
    eh = (hk[:, None] == hv[None, :]).astype(np.float32)
    e4 = (hk[:, None] == np.arange(LANE)[None, :]).astype(np.float32)
    return jnp.asarray(tri, BF16), jnp.asarray(eh.T.copy(), F32), jnp.asarray(e4, BF16)


def _mlstm(c_qk, c_v, c_o, c_if, cw, cb, B, S, Lc=256):
    HK, HV = MLSTM_H * MLSTM_DK, MLSTM_H * MLSTM_DV
    nb = S // Lc
    tri, bdm, e4 = _mlstm_consts(Lc)
    row = lambda w: pl.BlockSpec((Lc, w), lambda b, i: (b * nb + i, 0))
    return pl.pallas_call(
        _mlstm_body,
        out_shape=jax.ShapeDtypeStruct((B * S, HV), F32),
        grid=(B, nb),
        in_specs=[row(2 * HK), row(HV), row(HV), row(LANE),
                  _const_spec((CONV_W, 2 * HK)), _const_spec((1, 2 * HK)),
                  _const_spec((Lc, Lc)), _const_spec((HV, HK)), _const_spec((HK, LANE))],
        out_specs=row(HV),
        scratch_shapes=[pltpu.VMEM((Lc + 8, 2 * HK), F32), pltpu.VMEM((HV, HK), F32), pltpu.VMEM((1, HK), F32),
                        pltpu.VMEM((8, LANE), F32), pltpu.VMEM((Lc, HK), BF16), pltpu.VMEM((HV, Lc), BF16),
                        pltpu.VMEM((Lc, 2 * HK), F32)],
        compiler_params=_params(("parallel", "arbitrary")),
        name="mlstm",
    )(c_qk, c_v, c_o, c_if, cw, cb, tri, bdm, e4)


def _group_rms(x, gavg, gain):
    ms = _dot_l2(x * x, gavg)
    return x * lax.rsqrt(ms + RMS_EPS) * gain


def _rope_lanes(x, cos, sin_signed):
    half = NSA_D // 2
    first = (_iota((1, LANE), 1) & (NSA_D - 1)) < half
    swapped = jnp.where(first, pltpu.roll(x, LANE - half, 1), pltpu.roll(x, half, 1))
    return x * cos + swapped * sin_signed


def _store_vt_tiles(dst_ref, vt, width):
    ones = jnp.ones((VT_ROWS - NSA_D, width), BF16)
    for j in range(vt.shape[1] // width):
        for g in range(NSA_G):
            dst_ref[j, g, 0:NSA_D, :] = vt[g * NSA_D:(g + 1) * NSA_D, j * width:(j + 1) * width]
            dst_ref[j, g, NSA_D:VT_ROWS, :] = ones


def _nsa_prep_body(q_ref, ks_ref, kw_ref, vs_ref, vw_ref, g_ref, cos_ref, sin_ref, gq_ref, gs_ref, gw_ref, gavg_ref,
                   qt_ref, kso_ref, kwo_ref, vsl_ref, vsd_ref, vwd_ref, gt_ref):
    cos, sin = cos_ref[...], sin_ref[...]
    gavg = gavg_ref[...]
    for cb in range(NSA_H * NSA_D // LANE):
        sl = slice(cb * LANE, (cb + 1) * LANE)
        qn = _rope_lanes(_group_rms(q_ref[:, sl], gavg, gq_ref[...]), cos, sin) * (NSA_D ** -0.5 * LOG2E)
        qt_ref[sl, :] = qn.T.astype(BF16)
    kso_ref[...] = _rope_lanes(_group_rms(ks_ref[...], gavg, gs_ref[...]), cos, sin).astype(BF16)
    kwo_ref[...] = _rope_lanes(_group_rms(kw_ref[...], gavg, gw_ref[...]), cos, sin).astype(BF16)
    vst = vs_ref[...].T.astype(BF16)
    _store_vt_tiles(vsl_ref, vst, SEL_KC)
    _store_vt_tiles(vsd_ref, vst, LANE)
    _store_vt_tiles(vwd_ref, vw_ref[...].T.astype(BF16), LANE)
    gt_ref[...] = _sigmoid(g_ref[...]).T


def _nsa_prep(b_q, b_ks, b_kw, b_vs, b_vw, b_g, cos, sin, gq, gs, gw, gavg, B, S, tm=SEL_KC):
    nb = S // tm
    row = lambda w: pl.BlockSpec((tm, w), lambda b, i: (b * nb + i, 0))
    tab = pl.BlockSpec((tm, LANE), lambda b, i: (i, 0))
    HD = NSA_H * NSA_D
    vt_shape = lambda width: jax.ShapeDtypeStruct((B, S // width, NSA_G, VT_ROWS, width), BF16)
    vt_spec = lambda width: pl.BlockSpec((None, tm // width, NSA_G, VT_ROWS, width), lambda b, i: (b, i, 0, 0, 0))
    return pl.pallas_call(
        _nsa_prep_body,
        out_shape=[jax.ShapeDtypeStruct((B, HD, S), BF16),
                   jax.ShapeDtypeStruct((B, S, LANE), BF16),
                   jax.ShapeDtypeStruct((B, S, LANE), BF16),
                   vt_shape(SEL_KC), vt_shape(LANE),
                   vt_shape(LANE),
                   jax.ShapeDtypeStruct((B, LANE, S), F32)],
        grid=(B, nb),
        in_specs=[row(HD), row(LANE), row(LANE), row(LANE), row(LANE), row(LANE), tab, tab,
                  _const_spec((1, LANE)), _const_spec((1, LANE)), _const_spec((1, LANE)), _const_spec((LANE, LANE))],
        out_specs=[pl.BlockSpec((None, HD, tm), lambda b, i: (b, 0, i)),
                   pl.BlockSpec((None, tm, LANE), lambda b, i: (b, i, 0)),
                   pl.BlockSpec((None, tm, LANE), lambda b, i: (b, i, 0)),
                   vt_spec(SEL_KC), vt_spec(LANE), vt_spec(LANE),
                   pl.BlockSpec((None, LANE, tm), lambda b, i: (b, 0, i))],
        compiler_params=_params(("parallel", "parallel")),
        name="nsa_prep",
    )(b_q, b_ks, b_kw, b_vs, b_vw, b_g, cos, sin, gq, gs, gw, gavg)


def _gelu_tanh(x):
    return 0.5 * x * (1.0 + jnp.tanh(0.7978845608028654 * (x + 0.044715 * x * x * x)))


def _nsa_cmp_body(xk_ref, xv_ref, pe_ref, wt_ref, wb_ref, w2_ref, cos_ref, sin_ref, gk_ref, gavg_ref,
                  kc_ref, vct_ref):
    n = xk_ref.shape[0]

    def compress(x, which):
        u = _dot((x + pe_ref[which, 0:1, :]).astype(BF16), wt_ref[which])
        v = _dot((x + pe_ref[which, 1:2, :]).astype(BF16), wb_ref[which])
        hid = u + pltpu.roll(v, n - 1, 0)
        return _dot(_gelu_tanh(hid).astype(BF16), w2_ref[which])

    ck = compress(xk_ref[...], 0)
    kc_ref[...] = _rope_lanes(_group_rms(ck, gavg_ref[...], gk_ref[...]), cos_ref[...], sin_ref[...]).astype(BF16)
    vct_ref[...] = compress(xv_ref[...], 1).T.astype(BF16)


def _nsa_cmp(xk, xv, pe, wt, wb, w2, cos, sin, gk, gavg):
    B, n, W = xk.shape
    return pl.pallas_call(
        _nsa_cmp_body,
        out_shape=[jax.ShapeDtypeStruct((B, n, LANE), BF16), jax.ShapeDtypeStruct((B, LANE, n), BF16)],
        grid=(B,),
        in_specs=[pl.BlockSpec((None, n, W), lambda b: (b, 0, 0)), pl.BlockSpec((None, n, W), lambda b: (b, 0, 0)),
                  _const_spec(pe.shape), _const_spec(wt.shape), _const_spec(wb.shape), _const_spec(w2.shape),
                  _const_spec((n, LANE)), _const_spec((n, LANE)), _const_spec((1, LANE)), _const_spec((LANE, LANE))],
        out_specs=[pl.BlockSpec((None, n, LANE), lambda b: (b, 0, 0)), pl.BlockSpec((None, LANE, n), lambda b: (b, 0, 0))],
        compiler_params=_params(("parallel",)),
        name="nsa_compress",
    )(xk, xv, pe, wt, wb, w2, cos, sin, gk, gavg)


def _online_softmax_step(sb, vt, carry):
    m_i, acc = carry
    m_new = jnp.maximum(m_i, jnp.max(sb, axis=0, keepdims=True))
    p = jnp.exp2(sb - m_new).astype(BF16)
    return m_new, jnp.exp2(m_i - m_new) * acc + _dot(vt, p)


def _nsa_body(qt_ref, kc_ref, vct_ref, ks_ref, ksd_ref, vsl_ref, vsd_ref, kw_ref, vwd_ref, gt_ref, ovl_ref, o_ref,
              sel_ref, selm_ref, buf_a, buf_b):
    TQ = LANE
    HPG = NSA_H // NSA_G
    W = HPG * TQ
    n_cmp = kc_ref.shape[0]
    n_sel = ovl_ref.shape[0]
    n_kc = ks_ref.shape[0] // SEL_KC
    bps = SEL_KC // SLC_LEN
    i = pl.program_id(1)
    s0 = i * TQ
    t_row = s0 + _iota((1, TQ), 1)
    t4 = jnp.concatenate([t_row] * HPG, axis=1)
    zeros_q = jnp.zeros((NSA_D, W), BF16)
    n_win = WIN // LANE + 1
    j0 = jnp.maximum(i - (n_win - 1), 0)
    w_start = pl.multiple_of(j0 * LANE, LANE)
    diag_ok = _iota((TQ, 1), 0) <= _iota((1, TQ), 1)
    n_main = (s0 + SEL_KC - 1) // SEL_KC
    n_pairs = (n_main + 1) // 2

    for g in range(NSA_G):
        q4 = jnp.concatenate([qt_ref[(g * HPG + h) * NSA_D:(g * HPG + h + 1) * NSA_D, :] for h in range(HPG)], axis=1)
        qpad = jnp.concatenate([q4, zeros_q] if g == 0 else [zeros_q, q4], axis=0)
        grp = slice(g * NSA_D, (g + 1) * NSA_D)

        sc = _dot(kc_ref[...], qpad)
        cend = _iota((n_cmp, 1), 0) * CMP_STRIDE + (CMP_LEN - 1)
        ok = cend <= t4
        m = jnp.max(jnp.where(ok, sc, NEG), axis=0, keepdims=True)
        p = jnp.where(ok, jnp.exp2(sc - m), 0.0)
        p = p * (1.0 / jnp.maximum(jnp.sum(p, axis=0, keepdims=True), 1e-30))
        o_c = _dot(vct_ref[...], p.astype(BF16))[grp, :]

        psum = p[:, 0:TQ]
        for h in range(1, HPG):
            psum = psum + p[:, h * TQ:(h + 1) * TQ]
        imp = _dot_r2(ovl_ref[...], psum)
        blk = _iota((n_sel, 1), 0)
        cur = t_row >> (SLC_LEN.bit_length() - 1)
        valid = blk * SLC_LEN <= t_row
        forced = (blk == 0) | (blk == cur) | (blk == cur - 1)
        score = jnp.where(valid, jnp.where(forced, FORCED_SCORE, imp), NEG)
        chosen = jnp.zeros((n_sel, TQ), jnp.bool_)
        for _ in range(min(SLC_TOPK, n_sel)):
            best = jnp.max(score, axis=0, keepdims=True)
            first = jnp.min(jnp.where(score == best, blk, n_sel), axis=0, keepdims=True)
            hit = blk == first
            chosen = chosen | (hit & (best > 0.5 * NEG))
            score = jnp.where(hit, NEG, score)
        sel_ref[...] = jnp.where(chosen, 0.0, NEG)
        selm_ref[...] = jnp.where(chosen & (blk < 2 * i), 0.0, NEG)

        def scores(c, buf):
            k0 = pl.multiple_of(c * SEL_KC, SEL_KC)
            buf[...] = _dot(ks_ref[pl.ds(k0, SEL_KC), :], qpad)

        def main_step(c, buf, carry):
            parts = []
            for r in range(bps):
                brow = selm_ref[pl.ds(c * bps + r, 1), :]
                parts.append(buf[r * SLC_LEN:(r + 1) * SLC_LEN, :] + jnp.concatenate([brow] * HPG, axis=1))
            return _online_softmax_step(jnp.concatenate(parts, axis=0), vsl_ref[c, g], carry)

        def pair(cc, carry):
            c0 = 2 * cc
            scores(c0 + 1, buf_b)
            carry = main_step(c0, buf_a, carry)
            scores(jnp.minimum(c0 + 2, n_kc - 1), buf_a)
            return main_step(c0 + 1, buf_b, carry)

        scores(0, buf_a)
        carry = lax.fori_loop(0, n_pairs, pair, (jnp.full((1, W), NEG, F32), jnp.zeros((VT_ROWS, W), F32)))

        sd = _dot(ksd_ref[...], qpad)
        bd = jnp.concatenate([jnp.broadcast_to(sel_ref[pl.ds(2 * i + r, 1), :], (SLC_LEN, TQ))
                              for r in range(TQ // SLC_LEN)], axis=0)
        bd = jnp.where(diag_ok, bd, NEG)
        _, acc_s = _online_softmax_step(sd + jnp.concatenate([bd] * HPG, axis=1), vsd_ref[g], carry)
        o_s = acc_s[0:NSA_D, :] * (1.0 / acc_s[NSA_D:NSA_D + 1, :])

        sw = _dot(kw_ref[pl.ds(w_start, n_win * LANE), :], qpad)
        kpos = w_start + _iota((n_win * LANE, 1), 0)
        okw = (kpos <= t4) & (kpos > t4 - WIN)
        mw = jnp.max(jnp.where(okw, sw, NEG), axis=0, keepdims=True)
        pw = jnp.where(okw, jnp.exp2(sw - mw), 0.0).astype(BF16)
        acc_w = jnp.zeros((VT_ROWS, W), F32)
        for r in range(n_win):
            acc_w = acc_w + _dot(vwd_ref[j0 + r, g], pw[r * LANE:(r + 1) * LANE, :])
        o_w = acc_w[0:NSA_D, :] * (1.0 / acc_w[NSA_D:NSA_D + 1, :])

        for hp in range(HPG // 2):
            tiles = []
            for h in (2 * hp, 2 * hp + 1):
                gr = (g * HPG + h) * 3
                cs = slice(h * TQ, (h + 1) * TQ)
                tiles.append(gt_ref[gr:gr + 1, :] * o_c[:, cs] + gt_ref[gr + 1:gr + 2, :] * o_s[:, cs]
                             + gt_ref[gr + 2:gr + 3, :] * o_w[:, cs])
            col = (g * HPG + 2 * hp) * NSA_D
            o_ref[:, col:col + 2 * NSA_D] = jnp.concatenate(tiles, axis=0).T


def _nsa_main(qt, kc, vct, ks, vsl, vsd, kw, vwd, gt, ovl, B, S):
    HD = NSA_H * NSA_D
    TQ = LANE
    n_cmp = kc.shape[1]
    assert S % (2 * SEL_KC) == 0 and S >= WIN + TQ
    whole = lambda shape: pl.BlockSpec((None,) + shape, lambda b, i: (b,) + (0,) * len(shape),
                                       pipeline_mode=pl.Buffered(1))
    return pl.pallas_call(
        _nsa_body,
        out_shape=jax.ShapeDtypeStruct((B * S, HD), F32),
        grid=(B, S // TQ),
        in_specs=[pl.BlockSpec((None, HD, TQ), lambda b, i: (b, 0, i)),
                  whole((n_cmp, LANE)), whole((LANE, n_cmp)),
                  whole((S, LANE)), pl.BlockSpec((None, TQ, LANE), lambda b, i: (b, i, 0)),
                  whole(vsl.shape[1:]), pl.BlockSpec((None, None) + vsd.shape[2:], lambda b, i: (b, i, 0, 0, 0)),
                  whole((S, LANE)), whole(vwd.shape[1:]),
                  pl.BlockSpec((None, LANE, TQ), lambda b, i: (b, 0, i)),
                  _const_spec(ovl.shape)],
        out_specs=pl.BlockSpec((TQ, HD), lambda b, i: (b * (S // TQ) + i, 0)),
        scratch_shapes=[pltpu.VMEM((ovl.shape[0], TQ), F32), pltpu.VMEM((ovl.shape[0], TQ), F32),
                        pltpu.VMEM((SEL_KC, NSA_H // NSA_G * TQ), F32), pltpu.VMEM((SEL_KC, NSA_H // NSA_G * TQ), F32)],
        compiler_params=_params(("parallel", "arbitrary")),
        name="nsa_attention",
    )(qt, kc, vct, ks, ks, vsl, vsd, kw, vwd, gt, ovl)


def _merge_body(x_ref, oa_ref, ob_ref, oc_ref, gates_ref, wbr_ref, wo_ref, o_ref):
    D = x_ref.shape[1]
    y = jnp.zeros(x_ref.shape, F32)
    for j, br in enumerate((oa_ref, ob_ref, oc_ref)):
        y = y + _sigmoid(gates_ref[:, j * D:(j + 1) * D]) * _dot(br[...].astype(BF16), wbr_ref[j])
    o_ref[...] = x_ref[...] + _dot(y.astype(BF16), wo_ref[...])


def _merge(x, o_a, o_b, o_c, gates, wbr, wo, tm=512):
    T, D = x.shape
    row = lambda w: pl.BlockSpec((tm, w), lambda i: (i, 0))
    return pl.pallas_call(
        _merge_body,
        out_shape=jax.ShapeDtypeStruct((T, D), F32),
        grid=(T // tm,),
        in_specs=[row(D), row(MIX_W), row(MIX_W), row(MIX_W), row(3 * D), _const_spec(wbr.shape), _const_spec(wo.shape)],
        out_specs=row(D),
        compiler_params=_params(("parallel",)),
        name="merge_out",
    )(x, o_a, o_b, o_c, gates, wbr, wo)


def _pack_in_proj(w_in, b_in):
    ws, bs = [], []
    for _, off, w, wp in _SEGS:
        ws.append(jnp.pad(w_in[..., off:off + w], ((0, 0), (0, 0), (0, wp - w))))
        bs.append(jnp.pad(b_in[..., off:off + w], ((0, 0), (0, wp - w))))
    return jnp.concatenate(ws, axis=-1).astype(BF16), jnp.concatenate(bs, axis=-1)[:, None, :]


def _rope_tables(pos):
    half = NSA_D // 2
    freqs = ROPE_THETA ** (-jnp.arange(half, dtype=F32) / half)
    ang = pos.astype(F32)[:, None] * freqs[None, :]
    cos, sin = jnp.cos(ang), jnp.sin(ang)
    reps = LANE // NSA_D
    return jnp.tile(jnp.concatenate([cos, cos], axis=1), (1, reps)), jnp.tile(jnp.concatenate([-sin, sin], axis=1), (1, reps))


def _cmp_weights(w1, w2):
    L = w1.shape[0]
    w1r = w1.reshape(L, 2, 2, CMP_STRIDE, NSA_D, CMP_HIDDEN)
    eye_g = jnp.eye(NSA_G, dtype=w1.dtype)
    ex = jnp.einsum('lwstdh,gk->lwstgdkh', w1r, eye_g)
    ex = ex.reshape(L, 2, 2, CMP_STRIDE * NSA_G * NSA_D, NSA_G * CMP_HIDDEN)
    w2x = jnp.einsum('lwhd,gk->lwghkd', w2, eye_g).reshape(L, 2, NSA_G * CMP_HIDDEN, NSA_G * NSA_D)
    return ex[:, :, 0].astype(BF16), ex[:, :, 1].astype(BF16), w2x.astype(BF16)


def _cmp_pos_rows(pe):
    L = pe.shape[0]
    r = pe.reshape(L, 2, 2, CMP_STRIDE, 1, NSA_D)
    return jnp.broadcast_to(r, (L, 2, 2, CMP_STRIDE, NSA_G, NSA_D)).reshape(L, 2, 2, CMP_STRIDE * NSA_G * NSA_D)


def _overlap_t(n_cmp_pad, n_sel):
    ci = np.arange(n_cmp_pad)[None, :] * CMP_STRIDE
    sj = np.arange(n_sel)[:, None] * SLC_LEN
    return jnp.asarray(((ci < sj + SLC_LEN) & (ci + CMP_LEN > sj)).astype(np.float32), BF16)


def _lane_gain(g):
    return jnp.tile(g, LANE // g.shape[-1])[None, :]


def _layer(x, lw, consts, B, S):
    cos, sin, cos_c, sin_c, gavg, ovl = consts
    T = B * S
    x = _ffn(x, lw["ffn1_norm"], lw["ffn1_wg"], lw["ffn1_wu"], lw["ffn1_wd"])
    z = dict(zip([s[0] for s in _SEGS], _proj(x, lw["mix_norm"], lw["w_in"], lw["b_in"])))
    o_a = _gla(z["a_q"], z["a_k"], z["a_v"], z["a_r"], z["a_lr"], lw["gla_wa"], lw["gla_ba"], lw["gla_gn"], B, S)
    o_c = _mlstm(z["c_qk"], z["c_v"], z["c_o"], z["c_if"], lw["conv_w"], lw["conv_b"], B, S)
    qn = lw["nsa_qk_norm"]
    qt, ks, kw, vsl, vsd, vwd, gt = _nsa_prep(z["b_q"], z["b_ks"], z["b_kw"], z["b_vs"], z["b_vw"], z["b_g"], cos, sin,
                                         _lane_gain(qn[0]), _lane_gain(qn[2]), _lane_gain(qn[3]), gavg, B, S)
    n_blk = S // CMP_STRIDE
    xk = z["b_kc"].reshape(B, n_blk, CMP_STRIDE * LANE)
    xv = z["b_vc"].reshape(B, n_blk, CMP_STRIDE * LANE)
    kc, vct = _nsa_cmp(xk, xv, lw["cmp_pe"], lw["cmp_wt"], lw["cmp_wb"], lw["cmp_w2"], cos_c, sin_c,
                       _lane_gain(qn[1]), gavg)
    o_b = _nsa_main(qt, kc, vct, ks, vsl, vsd, kw, vwd, gt, ovl, B, S)
    x = _merge(x, o_a, o_b, o_c, z["gates"], lw["w_branch"], lw["w_out"])
    return _ffn(x, lw["ffn2_norm"], lw["ffn2_wg"], lw["ffn2_wu"], lw["ffn2_wd"])


def kernel(x, ffn1_norm, ffn1_w_gate, ffn1_w_up, ffn1_w_down, mix_norm, w_in, b_in, gla_w_alpha, gla_b_alpha, gla_out_norm, nsa_qk_norm, nsa_cmp_pos, nsa_cmp_w1, nsa_cmp_w2, mlstm_conv_w, mlstm_conv_b, w_branch, w_out, ffn2_norm, ffn2_w_gate, ffn2_w_up, ffn2_w_down):
    B, S, D = x.shape
    w_in_p, b_in_p = _pack_in_proj(w_in, b_in)
    cmp_wt, cmp_wb, cmp_w2 = _cmp_weights(nsa_cmp_w1, nsa_cmp_w2)
    layers = {
        "ffn1_norm": ffn1_norm[:, None, :], "ffn1_wg": ffn1_w_gate.astype(BF16), "ffn1_wu": ffn1_w_up.astype(BF16),
        "ffn1_wd": ffn1_w_down.astype(BF16),
        "mix_norm": mix_norm[:, None, :], "w_in": w_in_p, "b_in": b_in_p,
        "gla_wa": jnp.pad(gla_w_alpha, ((0, 0), (0, LANE - GLA_RANK), (0, 0))), "gla_ba": gla_b_alpha[:, None, :],
        "gla_gn": gla_out_norm[:, None, :],
        "nsa_qk_norm": nsa_qk_norm, "cmp_pe": _cmp_pos_rows(nsa_cmp_pos), "cmp_wt": cmp_wt, "cmp_wb": cmp_wb,
        "cmp_w2": cmp_w2,
        "conv_w": mlstm_conv_w, "conv_b": mlstm_conv_b[:, None, :],
        "w_branch": w_branch.astype(BF16), "w_out": w_out.astype(BF16),
        "ffn2_norm": ffn2_norm[:, None, :], "ffn2_wg": ffn2_w_gate.astype(BF16), "ffn2_wu": ffn2_w_up.astype(BF16),
        "ffn2_wd": ffn2_w_down.astype(BF16),
    }
    n_blk = S // CMP_STRIDE
    cos, sin = _rope_tables(jnp.arange(S))
    cos_c, sin_c = _rope_tables(jnp.arange(n_blk) * CMP_STRIDE + CMP_LEN - 1)
    gavg = jnp.asarray(np.kron(np.eye(LANE // NSA_D), np.full((NSA_D, NSA_D), 1.0 / NSA_D)), BF16)
    consts = (cos, sin, cos_c, sin_c, gavg, _overlap_t(n_blk, S // SLC_LEN))

    def step(xc, lw):
        return _layer(xc, lw, consts, B, S), None

    out, _ = lax.scan(step, x.reshape(B * S, D), layers)
    return out.reshape(B, S, D)
```

```python
import functools

import numpy as np
import jax
import jax.numpy as jnp
from jax import lax
from jax.experimental import pallas as pl
from jax.experimental.pallas import tpu as pltpu

F32 = jnp.float32
BF16 = jnp.bfloat16

RMS_EPS = 1e-6
ROPE_THETA = 10000.0

GLA_H, GLA_DK, GLA_DV, GLA_RANK, GLA_GATE_NORM = 4, 64, 128, 16, 16.0
GLA_SUB = 16
NSA_H, NSA_G, NSA_D = 8, 2, 64
CMP_LEN, CMP_STRIDE, CMP_HIDDEN = 32, 16, 256
SLC_LEN, SLC_TOPK, WIN = 64, 16, 512
FORCED_SCORE = 1e4
MLSTM_H, MLSTM_DK, MLSTM_DV, CONV_W = 4, 64, 128, 4
MLSTM_CHUNK = 64
MIX_W = 512
LANE = 128
NEG = -1e30
LOG2E = 1.4426950408889634
SEL_KC = 512
PS_PAD = 8
VT_ROWS = 80

_SEGS = (
    ("a_q", 0, 256, 256), ("a_k", 256, 256, 256), ("a_v", 512, 512, 512), ("a_r", 1024, 512, 512),
    ("a_lr", 1536, 16, 128),
    ("b_q", 1552, 512, 512), ("b_kc", 2064, 128, 128), ("b_vc", 2192, 128, 128), ("b_ks", 2320, 128, 128),
    ("b_vs", 2448, 128, 128), ("b_kw", 2576, 128, 128), ("b_vw", 2704, 128, 128), ("b_g", 2832, 24, 128),
    ("c_qk", 2856, 512, 512), ("c_v", 3368, 512, 512), ("c_o", 3880, 512, 512), ("c_if", 4392, 8, 128),
    ("gates", 4400, 3072, 3072),
)
_N_PACK = sum(s[3] for s in _SEGS)


def _dot(a, b):
    return jnp.dot(a, b, preferred_element_type=F32)


def _dot_nt(a, b):
    return lax.dot_general(a, b, (((1,), (1,)), ((), ())), preferred_element_type=F32)


def _split2(a):
    hi = a.astype(BF16)
    lo = (a - hi.astype(F32)).astype(BF16)
    return hi, lo


def _dot_l2(a, b):
    hi, lo = _split2(a)
    return _dot(hi, b) + _dot(lo, b)


def _dot_r2(a, b):
    hi, lo = _split2(b)
    return _dot(a, hi) + _dot(a, lo)


def _log_sigmoid(x):
    return jnp.minimum(x, 0.0) - jnp.log(1.0 + jnp.exp(-jnp.abs(x)))


def _sigmoid(x):
    return 1.0 / (1.0 + jnp.exp(-x))


def _silu(x):
    return x * _sigmoid(x)


def _iota(shape, dim):
    return lax.broadcasted_iota(jnp.int32, shape, dim)


def _const_spec(shape):
    nd = len(shape)
    return pl.BlockSpec(shape, lambda *_: (0,) * nd, pipeline_mode=pl.Buffered(1))


def _params(sem, vmem_mb=56):
    return pltpu.CompilerParams(dimension_semantics=sem, vmem_limit_bytes=vmem_mb * 1024 * 1024)


def _ffn_body(x_ref, g_ref, wg_ref, wu_ref, wd_ref, o_ref, *, n_chunks):
    x = x_ref[...]
    ms = jnp.mean(x * x, axis=-1, keepdims=True)
    h = (x * lax.rsqrt(ms + RMS_EPS) * g_ref[...]).astype(BF16)
    fc = wg_ref.shape[1] // n_chunks
    acc = jnp.zeros(x.shape, F32)
    for c in range(n_chunks):
        a = _dot(h, wg_ref[:, c * fc:(c + 1) * fc])
        u = _dot(h, wu_ref[:, c * fc:(c + 1) * fc])
        t = (_silu(a) * u).astype(BF16)
        acc = acc + _dot(t, wd_ref[c * fc:(c + 1) * fc, :])
    o_ref[...] = x + 0.5 * acc


def _ffn(x, g, wg, wu, wd, tm=512):
    T, D = x.shape
    F = wg.shape[1]
    return pl.pallas_call(
        functools.partial(_ffn_body, n_chunks=2),
        out_shape=jax.ShapeDtypeStruct((T, D), F32),
        grid=(T // tm,),
        in_specs=[pl.BlockSpec((tm, D), lambda i: (i, 0)), _const_spec((1, D)),
                  _const_spec((D, F)), _const_spec((D, F)), _const_spec((F, D))],
        out_specs=pl.BlockSpec((tm, D), lambda i: (i, 0)),
        compiler_params=_params(("parallel",)),
        name="ffn",
    )(x, g, wg, wu, wd)


def _proj_body(x_ref, g_ref, w_ref, b_ref, *o_refs):
    x = x_ref[...]
    ms = jnp.mean(x * x, axis=-1, keepdims=True)
    h = (x * lax.rsqrt(ms + RMS_EPS) * g_ref[...]).astype(BF16)
    off = 0
    for o_ref in o_refs:
        w = o_ref.shape[1]
        o_ref[...] = _dot(h, w_ref[:, off:off + w]) + b_ref[:, off:off + w]
        off += w


def _proj(x, g, w, b, tm=256):
    T, D = x.shape
    return pl.pallas_call(
        _proj_body,
        out_shape=[jax.ShapeDtypeStruct((T, s[3]), F32) for s in _SEGS],
        grid=(T // tm,),
        in_specs=[pl.BlockSpec((tm, D), lambda i: (i, 0)), _const_spec((1, D)),
                  _const_spec((D, _N_PACK)), _const_spec((1, _N_PACK))],
        out_specs=[pl.BlockSpec((tm, s[3]), lambda i: (i, 0)) for s in _SEGS],
        compiler_params=_params(("parallel",)),
        name="in_proj",
    )(x, g, w, b)


def _gla_body(q_ref, k_ref, v_ref, r_ref, lr_ref, wa_ref, ba_ref, gn_ref, tri_ref, bones_ref, eh_ref, bdm_ref,
              o_ref, st_ref, qs_ref, c_ref, tot_ref, kt_ref, vt_ref, oi_ref):
    Lc = q_ref.shape[0]
    n_sub = Lc // GLA_SUB

    @pl.when(pl.program_id(1) == 0)
    def _():
        st_ref[...] = jnp.zeros(st_ref.shape, F32)

    lr_hi, lr_lo = _split2(lr_ref[...])
    wa_hi, wa_lo = _split2(wa_ref[...])
    u = _dot(lr_hi, wa_hi) + _dot(lr_hi, wa_lo) + _dot(lr_lo, wa_hi) + ba_ref[...]
    g = _log_sigmoid(u) * (1.0 / GLA_GATE_NORM)
    c = _dot_r2(tri_ref[...], g)
    tot = _dot_r2(bones_ref[...], g)
    k = k_ref[...]
    qs_ref[...] = q_ref[...] * (GLA_DK ** -0.5)
    c_ref[...] = c
    tot_ref[...] = tot
    kt_ref[...] = (k * jnp.exp(tot - c)).astype(BF16)
    vt_ref[...] = v_ref[...].T.astype(BF16)
    row_i = _iota((GLA_SUB, 1), 0)
    col_l = _iota((1, Lc), 1)

    def sub(s, carry):
        r0 = pl.multiple_of(s * GLA_SUB, GLA_SUB)
        qs = qs_ref[pl.ds(r0, GLA_SUB), :]
        cs = c_ref[pl.ds(r0, GLA_SUB), :]
        st = st_ref[...]
        inter = _dot_nt((qs * jnp.exp(cs)).astype(BF16), st.astype(BF16))
        xs = []
        for j in range(GLA_SUB):
            kj = k_ref[pl.ds(r0 + j, 1), :]
            cj = c_ref[pl.ds(r0 + j, 1), :]
            xs.append(qs * kj * jnp.exp(jnp.minimum(cs - cj, 0.0)))
        r_all = _dot_l2(jnp.concatenate(xs, axis=0), eh_ref[...])
        intra = jnp.zeros((GLA_SUB, r_all.shape[1]), F32)
        for j in range(GLA_SUB):
            vj = v_ref[pl.ds(r0 + j, 1), :]
            intra = intra + jnp.where(row_i >= j, r_all[j * GLA_SUB:(j + 1) * GLA_SUB, :], 0.0) * vj
        oi_ref[pl.ds(r0, GLA_SUB), :] = inter + intra
        dec = jnp.exp(tot_ref[pl.ds(r0, 1), :])
        in_sub = (col_l >= r0) & (col_l < r0 + GLA_SUB)
        vtm = jnp.where(in_sub, vt_ref[...], jnp.zeros((), BF16))
        kv = _dot(vtm, kt_ref[...])
        st_ref[...] = dec * st + kv * bdm_ref[...]
        return carry

    lax.fori_loop(0, n_sub, sub, 0)

    gn = gn_ref[...]
    for h in range(GLA_H):
        sl = slice(h * GLA_DV, (h + 1) * GLA_DV)
        o = oi_ref[:, sl]
        ms = jnp.mean(o * o, axis=-1, keepdims=True)
        o_ref[:, sl] = o * lax.rsqrt(ms + RMS_EPS) * gn * _silu(r_ref[:, sl])


def _gla_consts(Lc):
    r = np.arange(Lc)
    same = (r[:, None] // GLA_SUB) == (r[None, :] // GLA_SUB)
    tri = (same & (r[None, :] <= r[:, None])).astype(np.float32)
    bones = same.astype(np.float32)
    hk = np.arange(GLA_H * GLA_DK) // GLA_DK
    hv = np.arange(GLA_H * GLA_DV) // GLA_DV
    eh = (hk[:, None] == hv[None, :]).astype(np.float32)
    bdm = eh.T.copy()
    return jnp.asarray(tri, BF16), jnp.asarray(bones, BF16), jnp.asarray(eh, BF16), jnp.asarray(bdm, F32)


def _gla(a_q, a_k, a_v, a_r, a_lr, wa, ba, gn, B, S, Lc=256):
    HK, HV = GLA_H * GLA_DK, GLA_H * GLA_DV
    nb = S // Lc
    tri, bones, eh, bdm = _gla_consts(Lc)
    row = lambda w: pl.BlockSpec((Lc, w), lambda b, i: (b * nb + i, 0))
    return pl.pallas_call(
        _gla_body,
        out_shape=jax.ShapeDtypeStruct((B * S, HV), F32),
        grid=(B, nb),
        in_specs=[row(HK), row(HK), row(HV), row(HV), row(LANE),
                  _const_spec((LANE, HK)), _const_spec((1, HK)), _const_spec((1, GLA_DV)),
                  _const_spec((Lc, Lc)), _const_spec((Lc, Lc)), _const_spec((HK, HV)), _const_spec((HV, HK))],
        out_specs=row(HV),
        scratch_shapes=[pltpu.VMEM((HV, HK), F32), pltpu.VMEM((Lc, HK), F32), pltpu.VMEM((Lc, HK), F32),
                        pltpu.VMEM((Lc, HK), F32), pltpu.VMEM((Lc, HK), BF16), pltpu.VMEM((HV, Lc), BF16),
                        pltpu.VMEM((Lc, HV), F32)],
        compiler_params=_params(("parallel", "arbitrary")),
        name="gla",
    )(a_q, a_k, a_v, a_r, a_lr, wa, ba, gn, tri, bones, eh, bdm)


def _mlstm_body(qk_ref, v_ref, og_ref, if_ref, cw_ref, cb_ref, tri_ref, bdm_ref, e4_ref,
                o_ref, xx_ref, ct_ref, n_ref, m_ref, kw_ref, vt_ref, act_ref):
    Lc = v_ref.shape[0]
    L = MLSTM_CHUNK
    HK = MLSTM_H * MLSTM_DK
    tail = 8

    @pl.when(pl.program_id(1) == 0)
    def _():
        ct_ref[...] = jnp.zeros(ct_ref.shape, F32)
        n_ref[...] = jnp.zeros(n_ref.shape, F32)
        m_ref[...] = jnp.zeros(m_ref.shape, F32)
        xx_ref[0:tail, :] = jnp.zeros((tail, xx_ref.shape[1]), F32)

    xx_ref[tail:tail + Lc, :] = qk_ref[...]
    y = jnp.zeros((Lc, 2 * HK), F32) + cb_ref[...]
    for kk in range(CONV_W):
        y = y + cw_ref[kk:kk + 1, :] * xx_ref[pl.ds(tail - (CONV_W - 1) + kk, Lc), :]
    xx_ref[0:tail, :] = xx_ref[Lc:Lc + tail, :]
    act_ref[...] = _silu(y)

    gates = if_ref[...]
    logf = _log_sigmoid(gates)
    bcum = _dot_r2(tri_ref[...], logf)
    kw_ref[...] = jnp.zeros(kw_ref.shape, BF16)
    vt_ref[...] = v_ref[...].T.astype(BF16)
    eye = (_iota((L, L), 0) == _iota((L, L), 1)).astype(F32)
    causal = _iota((L, L), 0) >= _iota((L, L), 1)
    lane_hk = _iota((1, HK), 1) >> (MLSTM_DK.bit_length() - 1)
    col_l = _iota((1, Lc), 1)

    for ci in range(Lc // L):
        rows = slice(ci * L, (ci + 1) * L)
        q_all = act_ref[rows, 0:HK]
        k_all = act_ref[rows, HK:2 * HK] * (MLSTM_DK ** -0.5)
        ct = ct_ref[...]
        n_row = n_ref[...]
        inter_all = _dot_nt(q_all.astype(BF16), ct.astype(BF16))
        qn_all = _dot_l2(q_all * n_row, e4_ref[...])
        k_bf = k_all.astype(BF16)
        wk_full = jnp.zeros((L, HK), F32)
        dec_row = jnp.zeros((1, HK), F32)
        for h in range(MLSTM_H):
            icol = gates[rows, h:h + 1]
            bcol = bcum[rows, MLSTM_H + h:MLSTM_H + h + 1]
            m_st = m_ref[0:1, h:h + 1]
            r_row = jnp.sum((icol - bcol) * eye, axis=0, keepdims=True)
            dmat = jnp.where(causal, bcol + r_row, -jnp.inf)
            inter_log = bcol + m_st
            m_row = jnp.maximum(inter_log, jnp.max(dmat, axis=-1, keepdims=True))
            w_inter = jnp.exp(inter_log - m_row)
            head = lane_hk == h
            qh = jnp.where(head, q_all, 0.0).astype(BF16)
            s_qk = _dot_nt(qh, k_bf) * jnp.exp(dmat - m_row)
            vh = v_ref[rows, h * MLSTM_DV:(h + 1) * MLSTM_DV]
            num = w_inter * inter_all[:, h * MLSTM_DV:(h + 1) * MLSTM_DV] + _dot(s_qk.astype(BF16), vh.astype(BF16))
            den = w_inter * qn_all[:, h:h + 1] + jnp.sum(s_qk, axis=-1, keepdims=True)
            hh = num / jnp.maximum(jnp.abs(den), jnp.exp(-m_row))
            o_ref[rows, h * MLSTM_DV:(h + 1) * MLSTM_DV] = hh * _sigmoid(og_ref[rows, h * MLSTM_DV:(h + 1) * MLSTM_DV])
            m_new = m_row[L - 1:L, :]
            b_last = bcol[L - 1:L, :]
            w_k = jnp.exp(b_last - bcol + icol - m_new)
            decay = jnp.exp(b_last + m_st - m_new)
            wk_full = wk_full + jnp.where(head, w_k, 0.0)
            dec_row = dec_row + jnp.where(head, decay, 0.0)
            m_ref[0:1, h:h + 1] = m_new
        kw = k_all * wk_full
        kw_ref[rows, :] = kw.astype(BF16)
        in_chunk = (col_l >= ci * L) & (col_l < (ci + 1) * L)
        vtm = jnp.where(in_chunk, vt_ref[...], jnp.zeros((), BF16))
        ct_ref[...] = dec_row * ct + _dot(vtm, kw_ref[...]) * bdm_ref[...]
        n_ref[...] = dec_row * n_row + jnp.sum(kw, axis=0, keepdims=True)


def _mlstm_consts(Lc):
    r = np.arange(Lc)
    same = (r[:, None] // MLSTM_CHUNK) == (r[None, :] // MLSTM_CHUNK)
    tri = (same & (r[None, :] <= r[:, None])).astype(np.float32)
    hk = np.arange(MLSTM_H * MLSTM_DK) // MLSTM_DK
    hv = np.arange(MLSTM_H * MLSTM_DV) // MLSTM_DV
    eh = (hk[:, None] == hv[None, :]).astype(np.float32)
    e4 = (hk[:, None] == np.arange(LANE)[None, :]).astype(np.float32)
    return jnp.asarray(tri, BF16), jnp.asarray(eh.T.copy(), F32), jnp.asarray(e4, BF16)


def _mlstm(c_qk, c_v, c_o, c_if, cw, cb, B, S, Lc=256):
    HK, HV = MLSTM_H * MLSTM_DK, MLSTM_H * MLSTM_DV
    nb = S // Lc
    tri, bdm, e4 = _mlstm_consts(Lc)
    row = lambda w: pl.BlockSpec((Lc, w), lambda b, i: (b * nb + i, 0))
    return pl.pallas_call(
        _mlstm_body,
        out_shape=jax.ShapeDtypeStruct((B * S, HV), F32),
        grid=(B, nb),
        in_specs=[row(2 * HK), row(HV), row(HV), row(LANE),
                  _const_spec((CONV_W, 2 * HK)), _const_spec((1, 2 * HK)),
                  _const_spec((Lc, Lc)), _const_spec((HV, HK)), _const_spec((HK, LANE))],
        out_specs=row(HV),
        scratch_shapes=[pltpu.VMEM((Lc + 8, 2 * HK), F32), pltpu.VMEM((HV, HK), F32), pltpu.VMEM((1, HK), F32),
                        pltpu.VMEM((8, LANE), F32), pltpu.VMEM((Lc, HK), BF16), pltpu.VMEM((HV, Lc), BF16),
                        pltpu.VMEM((Lc, 2 * HK), F32)],
        compiler_params=_params(("parallel", "arbitrary")),
        name="mlstm",
    )(c_qk, c_v, c_o, c_if, cw, cb, tri, bdm, e4)


def _group_rms(x, gavg, gain):
    ms = _dot_l2(x * x, gavg)
    return x * lax.rsqrt(ms + RMS_EPS) * gain


def _rope_lanes(x, cos, sin_signed):
    half = NSA_D // 2
    first = (_iota((1, LANE), 1) & (NSA_D - 1)) < half
    swapped = jnp.where(first, pltpu.roll(x, LANE - half, 1), pltpu.roll(x, half, 1))
    return x * cos + swapped * sin_signed


def _store_vt_tiles(dst_ref, vt, width):
    ones = jnp.ones((VT_ROWS - NSA_D, width), BF16)
    for j in range(vt.shape[1] // width):
        for g in range(NSA_G):
            dst_ref[j, g, 0:NSA_D, :] = vt[g * NSA_D:(g + 1) * NSA_D, j * width:(j + 1) * width]
            dst_ref[j, g, NSA_D:VT_ROWS, :] = ones


def _nsa_prep_body(q_ref, ks_ref, kw_ref, vs_ref, vw_ref, g_ref, cos_ref, sin_ref, gq_ref, gs_ref, gw_ref, gavg_ref,
                   qt_ref, kso_ref, kwo_ref, vsl_ref, vsd_ref, vwd_ref, gt_ref):
    cos, sin = cos_ref[...], sin_ref[...]
    gavg = gavg_ref[...]
    for cb in range(NSA_H * NSA_D // LANE):
        sl = slice(cb * LANE, (cb + 1) * LANE)
        qn = _rope_lanes(_group_rms(q_ref[:, sl], gavg, gq_ref[...]), cos, sin) * (NSA_D ** -0.5 * LOG2E)
        qt_ref[sl, :] = qn.T.astype(BF16)
    kso_ref[...] = _rope_lanes(_group_rms(ks_ref[...], gavg, gs_ref[...]), cos, sin).astype(BF16)
    kwo_ref[...] = _rope_lanes(_group_rms(kw_ref[...], gavg, gw_ref[...]), cos, sin).astype(BF16)
    vst = vs_ref[...].T.astype(BF16)
    _store_vt_tiles(vsl_ref, vst, SEL_KC)
    _store_vt_tiles(vsd_ref, vst, LANE)
    _store_vt_tiles(vwd_ref, vw_ref[...].T.astype(BF16), LANE)
    gt_ref[...] = _sigmoid(g_ref[...]).T


def _nsa_prep(b_q, b_ks, b_kw, b_vs, b_vw, b_g, cos, sin, gq, gs, gw, gavg, B, S, tm=SEL_KC):
    nb = S // tm
    row = lambda w: pl.BlockSpec((tm, w), lambda b, i: (b * nb + i, 0))
    tab = pl.BlockSpec((tm, LANE), lambda b, i: (i, 0))
    HD = NSA_H * NSA_D
    vt_shape = lambda width: jax.ShapeDtypeStruct((B, S // width, NSA_G, VT_ROWS, width), BF16)
    vt_spec = lambda width: pl.BlockSpec((None, tm // width, NSA_G, VT_ROWS, width), lambda b, i: (b, i, 0, 0, 0))
    return pl.pallas_call(
        _nsa_prep_body,
        out_shape=[jax.ShapeDtypeStruct((B, HD, S), BF16),
                   jax.ShapeDtypeStruct((B, S, LANE), BF16),
                   jax.ShapeDtypeStruct((B, S, LANE), BF16),
                   vt_shape(SEL_KC), vt_shape(LANE),
                   vt_shape(LANE),
                   jax.ShapeDtypeStruct((B, LANE, S), F32)],
        grid=(B, nb),
        in_specs=[row(HD), row(LANE), row(LANE), row(LANE), row(LANE), row(LANE), tab, tab,
                  _const_spec((1, LANE)), _const_spec((1, LANE)), _const_spec((1, LANE)), _const_spec((LANE, LANE))],
        out_specs=[pl.BlockSpec((None, HD, tm), lambda b, i: (b, 0, i)),
                   pl.BlockSpec((None, tm, LANE), lambda b, i: (b, i, 0)),
                   pl.BlockSpec((None, tm, LANE), lambda b, i: (b, i, 0)),
                   vt_spec(SEL_KC), vt_spec(LANE), vt_spec(LANE),
                   pl.BlockSpec((None, LANE, tm), lambda b, i: (b, 0, i))],
        compiler_params=_params(("parallel", "parallel")),
        name="nsa_prep",
    )(b_q, b_ks, b_kw, b_vs, b_vw, b_g, cos, sin, gq, gs, gw, gavg)


def _gelu_tanh(x):
    return 0.5 * x * (1.0 + jnp.tanh(0.7978845608028654 * (x + 0.044715 * x * x * x)))


def _nsa_cmp_body(xk_ref, xv_ref, pe_ref, wt_ref, wb_ref, w2_ref, cos_ref, sin_ref, gk_ref, gavg_ref,
                  kc_ref, vct_ref):
    n = xk_ref.shape[0]

    def compress(x, which):
        u = _dot((x + pe_ref[which, 0:1, :]).astype(BF16), wt_ref[which])
        v = _dot((x + pe_ref[which, 1:2, :]).astype(BF16), wb_ref[which])
        hid = u + pltpu.roll(v, n - 1, 0)
        return _dot(_gelu_tanh(hid).astype(BF16), w2_ref[which])

    ck = compress(xk_ref[...], 0)
    kc_ref[...] = _rope_lanes(_group_rms(ck, gavg_ref[...], gk_ref[...]), cos_ref[...], sin_ref[...]).astype(BF16)
    vct = compress(xv_ref[...], 1).T.astype(BF16)
    for g in range(NSA_G):
        vct_ref[g] = vct[g * NSA_D:(g + 1) * NSA_D, :]


def _nsa_cmp(xk, xv, pe, wt, wb, w2, cos, sin, gk, gavg):
    B, n, W = xk.shape
    return pl.pallas_call(
        _nsa_cmp_body,
        out_shape=[jax.ShapeDtypeStruct((B, n, LANE), BF16),
                   jax.ShapeDtypeStruct((B, NSA_G, NSA_D, n), BF16)],
        grid=(B,),
        in_specs=[pl.BlockSpec((None, n, W), lambda b: (b, 0, 0)), pl.BlockSpec((None, n, W), lambda b: (b, 0, 0)),
                  _const_spec(pe.shape), _const_spec(wt.shape), _const_spec(wb.shape), _const_spec(w2.shape),
                  _const_spec((n, LANE)), _const_spec((n, LANE)), _const_spec((1, LANE)), _const_spec((LANE, LANE))],
        out_specs=[pl.BlockSpec((None, n, LANE), lambda b: (b, 0, 0)),
                   pl.BlockSpec((None, NSA_G, NSA_D, n), lambda b: (b, 0, 0, 0))],
        compiler_params=_params(("parallel",)),
        name="nsa_compress",
    )(xk, xv, pe, wt, wb, w2, cos, sin, gk, gavg)


def _online_softmax_step(sb, vt, carry):
    m_i, acc = carry
    m_new = jnp.maximum(m_i, jnp.max(sb, axis=0, keepdims=True))
    p = jnp.exp2(sb - m_new).astype(BF16)
    return m_new, jnp.exp2(m_i - m_new) * acc + _dot(vt, p)


def _nsa_body(qt_ref, kc_ref, vct_ref, ks_ref, ksd_ref, vsl_ref, vsd_ref, kw_ref, vwd_ref, gt_ref, oneh_ref, o_ref,
              sel_ref, selm_ref, buf_a, buf_b, ps_buf, cw_ref):
    TQ = LANE
    HPG = NSA_H // NSA_G
    W = HPG * TQ
    n_cmp = kc_ref.shape[0]
    n_sel = ks_ref.shape[0] // SLC_LEN
    n_kc = ks_ref.shape[0] // SEL_KC
    bps = SEL_KC // SLC_LEN
    i = pl.program_id(1)
    s0 = i * TQ
    t_row = s0 + _iota((1, TQ), 1)
    t4 = jnp.concatenate([t_row] * HPG, axis=1)
    zeros_q = jnp.zeros((NSA_D, W), BF16)
    n_win = WIN // LANE + 1
    j0 = jnp.maximum(i - (n_win - 1), 0)
    w_start = pl.multiple_of(j0 * LANE, LANE)
    diag_ok = _iota((TQ, 1), 0) <= _iota((1, TQ), 1)
    n_main = (s0 + SEL_KC - 1) // SEL_KC
    n_pairs = (n_main + 1) // 2
    ps_buf[:, 0:PS_PAD, :] = jnp.zeros((NSA_G, PS_PAD, TQ), F32)

    def group_queries(g):
        q4 = jnp.concatenate([qt_ref[(g * HPG + h) * NSA_D:(g * HPG + h + 1) * NSA_D, :] for h in range(HPG)], axis=1)
        return jnp.concatenate([q4, zeros_q] if g == 0 else [zeros_q, q4], axis=0)

    for g in range(NSA_G):
        qpad = group_queries(g)

        sc = _dot(kc_ref[...], qpad)
        cend = _iota((n_cmp, 1), 0) * CMP_STRIDE + (CMP_LEN - 1)
        ok = cend <= t4
        m = jnp.max(jnp.where(ok, sc, NEG), axis=0, keepdims=True)
        p = jnp.where(ok, jnp.exp2(sc - m), 0.0)
        p = p * (1.0 / jnp.maximum(jnp.sum(p, axis=0, keepdims=True), 1e-30))
        o_c = _dot(vct_ref[g], p.astype(BF16))
        psum = p[:, 0:TQ]
        for h in range(1, HPG):
            psum = psum + p[:, h * TQ:(h + 1) * TQ]
        ps_buf[g, PS_PAD:PS_PAD + n_cmp, :] = psum

        per = SLC_LEN // CMP_STRIDE
        imp = ps_buf[g, pl.ds(PS_PAD - 1, n_sel, stride=per), :]
        for jj in range(per):
            imp = imp + ps_buf[g, pl.ds(PS_PAD + jj, n_sel, stride=per), :]
        blk = _iota((n_sel, 1), 0)
        cur = t_row >> (SLC_LEN.bit_length() - 1)
        valid = blk * SLC_LEN <= t_row
        forced = (blk == 0) | (blk == cur) | (blk == cur - 1)
        score0 = jnp.where(valid, jnp.where(forced, FORCED_SCORE, imp), NEG)
        score = score0
        for _ in range(min(SLC_TOPK, n_sel)):
            best = jnp.max(score, axis=0, keepdims=True)
            first = jnp.min(jnp.where(score == best, blk, n_sel), axis=0, keepdims=True)
            score = jnp.where(blk == first, NEG, score)
        chosen = score < score0
        sel_ref[g] = jnp.where(chosen, 0.0, NEG)
        selm_ref[g] = jnp.where(chosen & (blk < 2 * i), 0.0, NEG)

        sw = _dot(kw_ref[pl.ds(w_start, n_win * LANE), :], qpad)
        kpos = w_start + _iota((n_win * LANE, 1), 0)
        okw = (kpos <= t4) & (kpos > t4 - WIN)
        mw = jnp.max(jnp.where(okw, sw, NEG), axis=0, keepdims=True)
        pw = jnp.where(okw, jnp.exp2(sw - mw), 0.0).astype(BF16)
        acc_w = jnp.zeros((VT_ROWS, W), F32)
        for r in range(n_win):
            acc_w = acc_w + _dot(vwd_ref[j0 + r, g], pw[r * LANE:(r + 1) * LANE, :])
        o_w = acc_w[0:NSA_D, :] * (1.0 / acc_w[NSA_D:NSA_D + 1, :])

        for h in range(HPG):
            gr = (g * HPG + h) * 3
            cs = slice(h * TQ, (h + 1) * TQ)
            cw_ref[g, :, cs] = gt_ref[gr:gr + 1, :] * o_c[:, cs] + gt_ref[gr + 2:gr + 3, :] * o_w[:, cs]

    for g in range(NSA_G):
        qpad = group_queries(g)

        def scores(c, buf):
            k0 = pl.multiple_of(c * SEL_KC, SEL_KC)
            brows = selm_ref[g, pl.ds(pl.multiple_of(c * bps, bps), bps), :]
            brows = jnp.concatenate([jnp.concatenate([brows] * HPG, axis=1), jnp.zeros((bps, W), F32)], axis=0)
            q_aug = jnp.concatenate([qpad, brows.astype(BF16), jnp.zeros((LANE - 2 * bps, W), BF16)], axis=0)
            k_aug = jnp.concatenate([ks_ref[pl.ds(k0, SEL_KC), :], oneh_ref[...]], axis=1)
            buf[...] = _dot(k_aug, q_aug)

        def main_step(c, buf, carry):
            return _online_softmax_step(buf[...], vsl_ref[c, g], carry)

        def pair(cc, carry):
            c0 = 2 * cc
            scores(c0 + 1, buf_b)
            carry = main_step(c0, buf_a, carry)
            scores(jnp.minimum(c0 + 2, n_kc - 1), buf_a)
            return main_step(c0 + 1, buf_b, carry)

        scores(0, buf_a)
        carry = lax.fori_loop(0, n_pairs, pair, (jnp.full((1, W), NEG, F32), jnp.zeros((VT_ROWS, W), F32)))

        sd = _dot(ksd_ref[...], qpad)
        bd = jnp.concatenate([jnp.broadcast_to(sel_ref[g, pl.ds(2 * i + r, 1), :], (SLC_LEN, TQ))
                              for r in range(TQ // SLC_LEN)], axis=0)
        bd = jnp.where(diag_ok, bd, NEG)
        _, acc_s = _online_softmax_step(sd + jnp.concatenate([bd] * HPG, axis=1), vsd_ref[g], carry)
        o_s = acc_s[0:NSA_D, :] * (1.0 / acc_s[NSA_D:NSA_D + 1, :])

        for hp in range(HPG // 2):
            tiles = []
            for h in (2 * hp, 2 * hp + 1):
                gr = (g * HPG + h) * 3
                cs = slice(h * TQ, (h + 1) * TQ)
                tiles.append(cw_ref[g, :, cs] + gt_ref[gr + 1:gr + 2, :] * o_s[:, cs])
            col = (g * HPG + 2 * hp) * NSA_D
            o_ref[:, col:col + 2 * NSA_D] = jnp.concatenate(tiles, axis=0).T


def _nsa_main(qt, kc, vct, ks, vsl, vsd, kw, vwd, gt, B, S):
    HD = NSA_H * NSA_D
    TQ = LANE
    W = NSA_H // NSA_G * TQ
    n_cmp = kc.shape[1]
    n_sel = S // SLC_LEN
    assert S % (2 * SEL_KC) == 0 and S >= WIN + TQ and CMP_LEN == 2 * CMP_STRIDE
    whole = lambda shape: pl.BlockSpec((None,) + shape, lambda b, i: (b,) + (0,) * len(shape),
                                       pipeline_mode=pl.Buffered(1))
    oneh = jnp.asarray(np.arange(SEL_KC)[:, None] // SLC_LEN == np.arange(LANE)[None, :], BF16)
    return pl.pallas_call(
        _nsa_body,
        out_shape=jax.ShapeDtypeStruct((B * S, HD), F32),
        grid=(B, S // TQ),
        in_specs=[pl.BlockSpec((None, HD, TQ), lambda b, i: (b, 0, i)),
                  whole((n_cmp, LANE)), whole(vct.shape[1:]),
                  whole((S, LANE)), pl.BlockSpec((None, TQ, LANE), lambda b, i: (b, i, 0)),
                  whole(vsl.shape[1:]), pl.BlockSpec((None, None) + vsd.shape[2:], lambda b, i: (b, i, 0, 0, 0)),
                  whole((S, LANE)), whole(vwd.shape[1:]),
                  pl.BlockSpec((None, LANE, TQ), lambda b, i: (b, 0, i)),
                  _const_spec((SEL_KC, LANE))],
        out_specs=pl.BlockSpec((TQ, HD), lambda b, i: (b * (S // TQ) + i, 0)),
        scratch_shapes=[pltpu.VMEM((NSA_G, n_sel, TQ), F32), pltpu.VMEM((NSA_G, n_sel, TQ), F32),
                        pltpu.VMEM((SEL_KC, W), F32), pltpu.VMEM((SEL_KC, W), F32),
                        pltpu.VMEM((NSA_G, n_cmp + PS_PAD, TQ), F32), pltpu.VMEM((NSA_G, NSA_D, W), F32)],
        compiler_params=_params(("parallel", "arbitrary")),
        name="nsa_attention",
    )(qt, kc, vct, ks, ks, vsl, vsd, kw, vwd, gt, oneh)


def _merge_body(x_ref, oa_ref, ob_ref, oc_ref, gates_ref, wbr_ref, wo_ref, o_ref):
    D = x_ref.shape[1]
    y = jnp.zeros(x_ref.shape, F32)
    for j, br in enumerate((oa_ref, ob_ref, oc_ref)):
        y = y + _sigmoid(gates_ref[:, j * D:(j + 1) * D]) * _dot(br[...].astype(BF16), wbr_ref[j])
    o_ref[...] = x_ref[...] + _dot(y.astype(BF16), wo_ref[...])


def _merge(x, o_a, o_b, o_c, gates, wbr, wo, tm=512):
    T, D = x.shape
    row = lambda w: pl.BlockSpec((tm, w), lambda i: (i, 0))
    return pl.pallas_call(
        _merge_body,
        out_shape=jax.ShapeDtypeStruct((T, D), F32),
        grid=(T // tm,),
        in_specs=[row(D), row(MIX_W), row(MIX_W), row(MIX_W), row(3 * D), _const_spec(wbr.shape), _const_spec(wo.shape)],
        out_specs=row(D),
        compiler_params=_params(("parallel",)),
        name="merge_out",
    )(x, o_a, o_b, o_c, gates, wbr, wo)


def _pack_in_proj(w_in, b_in):
    ws, bs = [], []
    for _, off, w, wp in _SEGS:
        ws.append(jnp.pad(w_in[..., off:off + w], ((0, 0), (0, 0), (0, wp - w))))
        bs.append(jnp.pad(b_in[..., off:off + w], ((0, 0), (0, wp - w))))
    return jnp.concatenate(ws, axis=-1).astype(BF16), jnp.concatenate(bs, axis=-1)[:, None, :]


def _rope_tables(pos):
    half = NSA_D // 2
    freqs = ROPE_THETA ** (-jnp.arange(half, dtype=F32) / half)
    ang = pos.astype(F32)[:, None] * freqs[None, :]
    cos, sin = jnp.cos(ang), jnp.sin(ang)
    reps = LANE // NSA_D
    return jnp.tile(jnp.concatenate([cos, cos], axis=1), (1, reps)), jnp.tile(jnp.concatenate([-sin, sin], axis=1), (1, reps))


def _cmp_weights(w1, w2):
    L = w1.shape[0]
    w1r = w1.reshape(L, 2, 2, CMP_STRIDE, NSA_D, CMP_HIDDEN)
    eye_g = jnp.eye(NSA_G, dtype=w1.dtype)
    ex = jnp.einsum('lwstdh,gk->lwstgdkh', w1r, eye_g)
    ex = ex.reshape(L, 2, 2, CMP_STRIDE * NSA_G * NSA_D, NSA_G * CMP_HIDDEN)
    w2x = jnp.einsum('lwhd,gk->lwghkd', w2, eye_g).reshape(L, 2, NSA_G * CMP_HIDDEN, NSA_G * NSA_D)
    return ex[:, :, 0].astype(BF16), ex[:, :, 1].astype(BF16), w2x.astype(BF16)


def _cmp_pos_rows(pe):
    L = pe.shape[0]
    r = pe.reshape(L, 2, 2, CMP_STRIDE, 1, NSA_D)
    return jnp.broadcast_to(r, (L, 2, 2, CMP_STRIDE, NSA_G, NSA_D)).reshape(L, 2, 2, CMP_STRIDE * NSA_G * NSA_D)


def _lane_gain(g):
    return jnp.tile(g, LANE // g.shape[-1])[None, :]


def _layer(x, lw, consts, B, S):
    cos, sin, cos_c, sin_c, gavg = consts
    T = B * S
    x = _ffn(x, lw["ffn1_norm"], lw["ffn1_wg"], lw["ffn1_wu"], lw["ffn1_wd"])
    z = dict(zip([s[0] for s in _SEGS], _proj(x, lw["mix_norm"], lw["w_in"], lw["b_in"])))
    o_a = _gla(z["a_q"], z["a_k"], z["a_v"], z["a_r"], z["a_lr"], lw["gla_wa"], lw["gla_ba"], lw["gla_gn"], B, S)
    o_c = _mlstm(z["c_qk"], z["c_v"], z["c_o"], z["c_if"], lw["conv_w"], lw["conv_b"], B, S)
    qn = lw["nsa_qk_norm"]
    qt, ks, kw, vsl, vsd, vwd, gt = _nsa_prep(z["b_q"], z["b_ks"], z["b_kw"], z["b_vs"], z["b_vw"], z["b_g"], cos, sin,
                                         _lane_gain(qn[0]), _lane_gain(qn[2]), _lane_gain(qn[3]), gavg, B, S)
    n_blk = S // CMP_STRIDE
    xk = z["b_kc"].reshape(B, n_blk, CMP_STRIDE * LANE)
    xv = z["b_vc"].reshape(B, n_blk, CMP_STRIDE * LANE)
    kc, vct = _nsa_cmp(xk, xv, lw["cmp_pe"], lw["cmp_wt"], lw["cmp_wb"], lw["cmp_w2"], cos_c, sin_c,
                       _lane_gain(qn[1]), gavg)
    o_b = _nsa_main(qt, kc, vct, ks, vsl, vsd, kw, vwd, gt, B, S)
    x = _merge(x, o_a, o_b, o_c, z["gates"], lw["w_branch"], lw["w_out"])
    return _ffn(x, lw["ffn2_norm"], lw["ffn2_wg"], lw["ffn2_wu"], lw["ffn2_wd"])


def kernel(x, ffn1_norm, ffn1_w_gate, ffn1_w_up, ffn1_w_down, mix_norm, w_in, b_in, gla_w_alpha, gla_b_alpha, gla_out_norm, nsa_qk_norm, nsa_cmp_pos, nsa_cmp_w1, nsa_cmp_w2, mlstm_conv_w, mlstm_conv_b, w_branch, w_out, ffn2_norm, ffn2_w_gate, ffn2_w_up, ffn2_w_down):
    B, S, D = x.shape
    w_in_p, b_in_p = _pack_in_proj(w_in, b_in)
    cmp_wt, cmp_wb, cmp_w2 = _cmp_weights(nsa_cmp_w1, nsa_cmp_w2)
    layers = {
        "ffn1_norm": ffn1_norm[:, None, :], "ffn1_wg": ffn1_w_gate.astype(BF16), "ffn1_wu": ffn1_w_up.astype(BF16),
        "ffn1_wd": ffn1_w_down.astype(BF16),
        "mix_norm": mix_norm[:, None, :], "w_in": w_in_p, "b_in": b_in_p,
        "gla_wa": jnp.pad(gla_w_alpha, ((0, 0), (0, LANE - GLA_RANK), (0, 0))), "gla_ba": gla_b_alpha[:, None, :],
        "gla_gn": gla_out_norm[:, None, :],
        "nsa_qk_norm": nsa_qk_norm, "cmp_pe": _cmp_pos_rows(nsa_cmp_pos), "cmp_wt": cmp_wt, "cmp_wb": cmp_wb,
        "cmp_w2": cmp_w2,
        "conv_w": mlstm_conv_w, "conv_b": mlstm_conv_b[:, None, :],
        "w_branch": w_branch.astype(BF16), "w_out": w_out.astype(BF16),
        "ffn2_norm": ffn2_norm[:, None, :], "ffn2_wg": ffn2_w_gate.astype(BF16), "ffn2_wu": ffn2_w_up.astype(BF16),
        "ffn2_wd": ffn2_w_down.astype(BF16),
    }
    n_blk = S // CMP_STRIDE
    cos, sin = _rope_tables(jnp.arange(S))
    cos_c, sin_c = _rope_tables(jnp.arange(n_blk) * CMP_STRIDE + CMP_LEN - 1)
    gavg = jnp.asarray(np.kron(np.eye(LANE // NSA_D), np.full((NSA_D, NSA_D), 1.0 / NSA_D)), BF16)
    consts = (cos, sin, cos_c, sin_c, gavg)

    def step(xc, lw):
        return _layer(xc, lw, consts, B, S), None

    out, _ = lax.scan(step, x.reshape(B * S, D), layers)
    return out.reshape(B, S, D)
```

```python
import functools

import numpy as np
import jax
import jax.numpy as jnp
from jax import lax
from jax.experimental import pallas as pl
from jax.experimental.pallas import tpu as pltpu

F32 = jnp.float32
BF16 = jnp.bfloat16

RMS_EPS = 1e-6
ROPE_THETA = 10000.0

GLA_H, GLA_DK, GLA_DV, GLA_RANK, GLA_GATE_NORM = 4, 64, 128, 16, 16.0
GLA_SUB = 16
NSA_H, NSA_G, NSA_D = 8, 2, 64
CMP_LEN, CMP_STRIDE, CMP_HIDDEN = 32, 16, 256
SLC_LEN, SLC_TOPK, WIN = 64, 16, 512
FORCED_SCORE = 1e4
MLSTM_H, MLSTM_DK, MLSTM_DV, CONV_W = 4, 64, 128, 4
MLSTM_CHUNK = 64
MIX_W = 512
LANE = 128
NEG = -1e30
LOG2E = 1.4426950408889634
SEL_KC = 512
SEL_TQ = 512
PS_PAD = 8
VT_ROWS = 80

_SEGS = (
    ("a_q", 0, 256, 256), ("a_k", 256, 256, 256), ("a_v", 512, 512, 512), ("a_r", 1024, 512, 512),
    ("a_lr", 1536, 16, 128),
    ("b_q", 1552, 512, 512), ("b_kc", 2064, 128, 128), ("b_vc", 2192, 128, 128), ("b_ks", 2320, 128, 128),
    ("b_vs", 2448, 128, 128), ("b_kw", 2576, 128, 128), ("b_vw", 2704, 128, 128), ("b_g", 2832, 24, 128),
    ("c_qk", 2856, 512, 512), ("c_v", 3368, 512, 512), ("c_o", 3880, 512, 512), ("c_if", 4392, 8, 128),
    ("gates", 4400, 3072, 3072),
)
_N_PACK = sum(s[3] for s in _SEGS)


def _dot(a, b):
    return jnp.dot(a, b, preferred_element_type=F32)


def _dot_nt(a, b):
    return lax.dot_general(a, b, (((1,), (1,)), ((), ())), preferred_element_type=F32)


def _split2(a):
    hi = a.astype(BF16)
    lo = (a - hi.astype(F32)).astype(BF16)
    return hi, lo


def _dot_l2(a, b):
    hi, lo = _split2(a)
    return _dot(hi, b) + _dot(lo, b)


def _dot_r2(a, b):
    hi, lo = _split2(b)
    return _dot(a, hi) + _dot(a, lo)


def _log_sigmoid(x):
    return jnp.minimum(x, 0.0) - jnp.log(1.0 + jnp.exp(-jnp.abs(x)))


def _sigmoid(x):
    return 1.0 / (1.0 + jnp.exp(-x))


def _silu(x):
    return x * _sigmoid(x)


def _iota(shape, dim):
    return lax.broadcasted_iota(jnp.int32, shape, dim)


def _const_spec(shape):
    nd = len(shape)
    return pl.BlockSpec(shape, lambda *_: (0,) * nd, pipeline_mode=pl.Buffered(1))


def _params(sem, vmem_mb=56):
    return pltpu.CompilerParams(dimension_semantics=sem, vmem_limit_bytes=vmem_mb * 1024 * 1024)


def _ffn_body(x_ref, g_ref, wg_ref, wu_ref, wd_ref, o_ref, *, n_chunks):
    x = x_ref[...]
    ms = jnp.mean(x * x, axis=-1, keepdims=True)
    h = (x * lax.rsqrt(ms + RMS_EPS) * g_ref[...]).astype(BF16)
    fc = wg_ref.shape[1] // n_chunks
    acc = jnp.zeros(x.shape, F32)
    for c in range(n_chunks):
        a = _dot(h, wg_ref[:, c * fc:(c + 1) * fc])
        u = _dot(h, wu_ref[:, c * fc:(c + 1) * fc])
        t = (_silu(a) * u).astype(BF16)
        acc = acc + _dot(t, wd_ref[c * fc:(c + 1) * fc, :])
    o_ref[...] = x + 0.5 * acc


def _ffn(x, g, wg, wu, wd, tm=512):
    T, D = x.shape
    F = wg.shape[1]
    return pl.pallas_call(
        functools.partial(_ffn_body, n_chunks=2),
        out_shape=jax.ShapeDtypeStruct((T, D), F32),
        grid=(T // tm,),
        in_specs=[pl.BlockSpec((tm, D), lambda i: (i, 0)), _const_spec((1, D)),
                  _const_spec((D, F)), _const_spec((D, F)), _const_spec((F, D))],
        out_specs=pl.BlockSpec((tm, D), lambda i: (i, 0)),
        compiler_params=_params(("parallel",)),
        name="ffn",
    )(x, g, wg, wu, wd)


def _proj_body(x_ref, g_ref, w_ref, b_ref, *o_refs):
    x = x_ref[...]
    ms = jnp.mean(x * x, axis=-1, keepdims=True)
    h = (x * lax.rsqrt(ms + RMS_EPS) * g_ref[...]).astype(BF16)
    off = 0
    for o_ref in o_refs:
        w = o_ref.shape[1]
        o_ref[...] = _dot(h, w_ref[:, off:off + w]) + b_ref[:, off:off + w]
        off += w


def _proj(x, g, w, b, tm=256):
    T, D = x.shape
    return pl.pallas_call(
        _proj_body,
        out_shape=[jax.ShapeDtypeStruct((T, s[3]), F32) for s in _SEGS],
        grid=(T // tm,),
        in_specs=[pl.BlockSpec((tm, D), lambda i: (i, 0)), _const_spec((1, D)),
                  _const_spec((D, _N_PACK)), _const_spec((1, _N_PACK))],
        out_specs=[pl.BlockSpec((tm, s[3]), lambda i: (i, 0)) for s in _SEGS],
        compiler_params=_params(("parallel",)),
        name="in_proj",
    )(x, g, w, b)


def _gla_body(q_ref, k_ref, v_ref, r_ref, lr_ref, wa_ref, ba_ref, gn_ref, tri_ref, bones_ref, eh_ref, bdm_ref,
              o_ref, st_ref, qs_ref, c_ref, tot_ref, kt_ref, vt_ref, oi_ref):
    Lc = q_ref.shape[0]
    n_sub = Lc // GLA_SUB

    @pl.when(pl.program_id(1) == 0)
    def _():
        st_ref[...] = jnp.zeros(st_ref.shape, F32)

    lr_hi, lr_lo = _split2(lr_ref[...])
    wa_hi, wa_lo = _split2(wa_ref[...])
    u = _dot(lr_hi, wa_hi) + _dot(lr_hi, wa_lo) + _dot(lr_lo, wa_hi) + ba_ref[...]
    g = _log_sigmoid(u) * (1.0 / GLA_GATE_NORM)
    c = _dot_r2(tri_ref[...], g)
    tot = _dot_r2(bones_ref[...], g)
    k = k_ref[...]
    qs_ref[...] = q_ref[...] * (GLA_DK ** -0.5)
    c_ref[...] = c
    tot_ref[...] = tot
    kt_ref[...] = (k * jnp.exp(tot - c)).astype(BF16)
    vt_ref[...] = v_ref[...].T.astype(BF16)
    row_i = _iota((GLA_SUB, 1), 0)
    col_l = _iota((1, Lc), 1)

    def sub(s, carry):
        r0 = pl.multiple_of(s * GLA_SUB, GLA_SUB)
        qs = qs_ref[pl.ds(r0, GLA_SUB), :]
        cs = c_ref[pl.ds(r0, GLA_SUB), :]
        st = st_ref[...]
        inter = _dot_nt((qs * jnp.exp(cs)).astype(BF16), st.astype(BF16))
        xs = []
        for j in range(GLA_SUB):
            kj = k_ref[pl.ds(r0 + j, 1), :]
            cj = c_ref[pl.ds(r0 + j, 1), :]
            xs.append(qs * kj * jnp.exp(jnp.minimum(cs - cj, 0.0)))
        r_all = _dot_l2(jnp.concatenate(xs, axis=0), eh_ref[...])
        intra = jnp.zeros((GLA_SUB, r_all.shape[1]), F32)
        for j in range(GLA_SUB):
            vj = v_ref[pl.ds(r0 + j, 1), :]
            intra = intra + jnp.where(row_i >= j, r_all[j * GLA_SUB:(j + 1) * GLA_SUB, :], 0.0) * vj
        oi_ref[pl.ds(r0, GLA_SUB), :] = inter + intra
        dec = jnp.exp(tot_ref[pl.ds(r0, 1), :])
        in_sub = (col_l >= r0) & (col_l < r0 + GLA_SUB)
        vtm = jnp.where(in_sub, vt_ref[...], jnp.zeros((), BF16))
        kv = _dot(vtm, kt_ref[...])
        st_ref[...] = dec * st + kv * bdm_ref[...]
        return carry

    lax.fori_loop(0, n_sub, sub, 0)

    gn = gn_ref[...]
    for h in range(GLA_H):
        sl = slice(h * GLA_DV, (h + 1) * GLA_DV)
        o = oi_ref[:, sl]
        ms = jnp.mean(o * o, axis=-1, keepdims=True)
        o_ref[:, sl] = o * lax.rsqrt(ms + RMS_EPS) * gn * _silu(r_ref[:, sl])


def _gla_consts(Lc):
    r = np.arange(Lc)
    same = (r[:, None] // GLA_SUB) == (r[None, :] // GLA_SUB)
    tri = (same & (r[None, :] <= r[:, None])).astype(np.float32)
    bones = same.astype(np.float32)
    hk = np.arange(GLA_H * GLA_DK) // GLA_DK
    hv = np.arange(GLA_H * GLA_DV) // GLA_DV
    eh = (hk[:, None] == hv[None, :]).astype(np.float32)
    bdm = eh.T.copy()
    return jnp.asarray(tri, BF16), jnp.asarray(bones, BF16), jnp.asarray(eh, BF16), jnp.asarray(bdm, F32)


def _gla(a_q, a_k, a_v, a_r, a_lr, wa, ba, gn, B, S, Lc=256):
    HK, HV = GLA_H * GLA_DK, GLA_H * GLA_DV
    nb = S // Lc
    tri, bones, eh, bdm = _gla_consts(Lc)
    row = lambda w: pl.BlockSpec((Lc, w), lambda b, i: (b * nb + i, 0))
    return pl.pallas_call(
        _gla_body,
        out_shape=jax.ShapeDtypeStruct((B * S, HV), F32),
        grid=(B, nb),
        in_specs=[row(HK), row(HK), row(HV), row(HV), row(LANE),
                  _const_spec((LANE, HK)), _const_spec((1, HK)), _const_spec((1, GLA_DV)),
                  _const_spec((Lc, Lc)), _const_spec((Lc, Lc)), _const_spec((HK, HV)), _const_spec((HV, HK))],
        out_specs=row(HV),
        scratch_shapes=[pltpu.VMEM((HV, HK), F32), pltpu.VMEM((Lc, HK), F32), pltpu.VMEM((Lc, HK), F32),
                        pltpu.VMEM((Lc, HK), F32), pltpu.VMEM((Lc, HK), BF16), pltpu.VMEM((HV, Lc), BF16),
                        pltpu.VMEM((Lc, HV), F32)],
        compiler_params=_params(("parallel", "arbitrary")),
        name="gla",
    )(a_q, a_k, a_v, a_r, a_lr, wa, ba, gn, tri, bones, eh, bdm)


def _mlstm_body(qk_ref, v_ref, og_ref, if_ref, cw_ref, cb_ref, tri_ref, bdm_ref, e4_ref,
                o_ref, xx_ref, ct_ref, n_ref, m_ref, kw_ref, vt_ref, act_ref):
    Lc = v_ref.shape[0]
    L = MLSTM_CHUNK
    HK = MLSTM_H * MLSTM_DK
    tail = 8

    @pl.when(pl.program_id(1) == 0)
    def _():
        ct_ref[...] = jnp.zeros(ct_ref.shape, F32)
        n_ref[...] = jnp.zeros(n_ref.shape, F32)
        m_ref[...] = jnp.zeros(m_ref.shape, F32)
        xx_ref[0:tail, :] = jnp.zeros((tail, xx_ref.shape[1]), F32)

    xx_ref[tail:tail + Lc, :] = qk_ref[...]
    y = jnp.zeros((Lc, 2 * HK), F32) + cb_ref[...]
    for kk in range(CONV_W):
        y = y + cw_ref[kk:kk + 1, :] * xx_ref[pl.ds(tail - (CONV_W - 1) + kk, Lc), :]
    xx_ref[0:tail, :] = xx_ref[Lc:Lc + tail, :]
    act_ref[...] = _silu(y)

    gates = if_ref[...]
    logf = _log_sigmoid(gates)
    bcum = _dot_r2(tri_ref[...], logf)
    kw_ref[...] = jnp.zeros(kw_ref.shape, BF16)
    vt_ref[...] = v_ref[...].T.astype(BF16)
    eye = (_iota((L, L), 0) == _iota((L, L), 1)).astype(F32)
    causal = _iota((L, L), 0) >= _iota((L, L), 1)
    lane_hk = _iota((1, HK), 1) >> (MLSTM_DK.bit_length() - 1)
    col_l = _iota((1, Lc), 1)

    for ci in range(Lc // L):
        rows = slice(ci * L, (ci + 1) * L)
        q_all = act_ref[rows, 0:HK]
        k_all = act_ref[rows, HK:2 * HK] * (MLSTM_DK ** -0.5)
        ct = ct_ref[...]
        n_row = n_ref[...]
        inter_all = _dot_nt(q_all.astype(BF16), ct.astype(BF16))
        qn_all = _dot_l2(q_all * n_row, e4_ref[...])
        k_bf = k_all.astype(BF16)
        wk_full = jnp.zeros((L, HK), F32)
        dec_row = jnp.zeros((1, HK), F32)
        for h in range(MLSTM_H):
            icol = gates[rows, h:h + 1]
            bcol = bcum[rows, MLSTM_H + h:MLSTM_H + h + 1]
            m_st = m_ref[0:1, h:h + 1]
            r_row = jnp.sum((icol - bcol) * eye, axis=0, keepdims=True)
            dmat = jnp.where(causal, bcol + r_row, -jnp.inf)
            inter_log = bcol + m_st
            m_row = jnp.maximum(inter_log, jnp.max(dmat, axis=-1, keepdims=True))
            w_inter = jnp.exp(inter_log - m_row)
            head = lane_hk == h
            qh = jnp.where(head, q_all, 0.0).astype(BF16)
            s_qk = _dot_nt(qh, k_bf) * jnp.exp(dmat - m_row)
            vh = v_ref[rows, h * MLSTM_DV:(h + 1) * MLSTM_DV]
            num = w_inter * inter_all[:, h * MLSTM_DV:(h + 1) * MLSTM_DV] + _dot(s_qk.astype(BF16), vh.astype(BF16))
            den = w_inter * qn_all[:, h:h + 1] + jnp.sum(s_qk, axis=-1, keepdims=True)
            hh = num / jnp.maximum(jnp.abs(den), jnp.exp(-m_row))
            o_ref[rows, h * MLSTM_DV:(h + 1) * MLSTM_DV] = hh * _sigmoid(og_ref[rows, h * MLSTM_DV:(h + 1) * MLSTM_DV])
            m_new = m_row[L - 1:L, :]
            b_last = bcol[L - 1:L, :]
            w_k = jnp.exp(b_last - bcol + icol - m_new)
            decay = jnp.exp(b_last + m_st - m_new)
            wk_full = wk_full + jnp.where(head, w_k, 0.0)
            dec_row = dec_row + jnp.where(head, decay, 0.0)
            m_ref[0:1, h:h + 1] = m_new
        kw = k_all * wk_full
        kw_ref[rows, :] = kw.astype(BF16)
        in_chunk = (col_l >= ci * L) & (col_l < (ci + 1) * L)
        vtm = jnp.where(in_chunk, vt_ref[...], jnp.zeros((), BF16))
        ct_ref[...] = dec_row * ct + _dot(vtm, kw_ref[...]) * bdm_ref[...]
        n_ref[...] = dec_row * n_row + jnp.sum(kw, axis=0, keepdims=True)


def _mlstm_consts(Lc):
    r = np.arange(Lc)
    same = (r[:, None] // MLSTM_CHUNK) == (r[None, :] // MLSTM_CHUNK)
    tri = (same & (r[None, :] <= r[:, None])).astype(np.float32)
    hk = np.arange(MLSTM_H * MLSTM_DK) // MLSTM_DK
    hv = np.arange(MLSTM_H * MLSTM_DV) // MLSTM_DV
    eh = (hk[:, None] == hv[None, :]).astype(np.float32)
    e4 = (hk[:, None] == np.arange(LANE)[None, :]).astype(np.float32)
    return jnp.asarray(tri, BF16), jnp.asarray(eh.T.copy(), F32), jnp.asarray(e4, BF16)


def _mlstm(c_qk, c_v, c_o, c_if, cw, cb, B, S, Lc=256):
    HK, HV = MLSTM_H * MLSTM_DK, MLSTM_H * MLSTM_DV
    nb = S // Lc
    tri, bdm, e4 = _mlstm_consts(Lc)
    row = lambda w: pl.BlockSpec((Lc, w), lambda b, i: (b * nb + i, 0))
    return pl.pallas_call(
        _mlstm_body,
        out_shape=jax.ShapeDtypeStruct((B * S, HV), F32),
        grid=(B, nb),
        in_specs=[row(2 * HK), row(HV), row(HV), row(LANE),
                  _const_spec((CONV_W, 2 * HK)), _const_spec((1, 2 * HK)),
                  _const_spec((Lc, Lc)), _const_spec((HV, HK)), _const_spec((HK, LANE))],
        out_specs=row(HV),
        scratch_shapes=[pltpu.VMEM((Lc + 8, 2 * HK), F32), pltpu.VMEM((HV, HK), F32), pltpu.VMEM((1, HK), F32),
                        pltpu.VMEM((8, LANE), F32), pltpu.VMEM((Lc, HK), BF16), pltpu.VMEM((HV, Lc), BF16),
                        pltpu.VMEM((Lc, 2 * HK), F32)],
        compiler_params=_params(("parallel", "arbitrary")),
        name="mlstm",
    )(c_qk, c_v, c_o, c_if, cw, cb, tri, bdm, e4)


def _group_rms(x, gavg, gain):
    ms = _dot_l2(x * x, gavg)
    return x * lax.rsqrt(ms + RMS_EPS) * gain


def _rope_lanes(x, cos, sin_signed):
    half = NSA_D // 2
    first = (_iota((1, LANE), 1) & (NSA_D - 1)) < half
    swapped = jnp.where(first, pltpu.roll(x, LANE - half, 1), pltpu.roll(x, half, 1))
    return x * cos + swapped * sin_signed


def _store_vt_tiles(dst_ref, vt, width):
    ones = jnp.ones((VT_ROWS - NSA_D, width), BF16)
    for j in range(vt.shape[1] // width):
        for g in range(NSA_G):
            dst_ref[j, g, 0:NSA_D, :] = vt[g * NSA_D:(g + 1) * NSA_D, j * width:(j + 1) * width]
            dst_ref[j, g, NSA_D:VT_ROWS, :] = ones


def _nsa_prep_body(q_ref, ks_ref, kw_ref, vs_ref, vw_ref, g_ref, cos_ref, sin_ref, gq_ref, gs_ref, gw_ref, gavg_ref,
                   qt_ref, kso_ref, kwo_ref, vsl_ref, vwd_ref, gt_ref):
    cos, sin = cos_ref[...], sin_ref[...]
    gavg = gavg_ref[...]
    for cb in range(NSA_H * NSA_D // LANE):
        sl = slice(cb * LANE, (cb + 1) * LANE)
        qn = _rope_lanes(_group_rms(q_ref[:, sl], gavg, gq_ref[...]), cos, sin) * (NSA_D ** -0.5 * LOG2E)
        qt_ref[sl, :] = qn.T.astype(BF16)
    kso_ref[...] = _rope_lanes(_group_rms(ks_ref[...], gavg, gs_ref[...]), cos, sin).astype(BF16)
    kwo_ref[...] = _rope_lanes(_group_rms(kw_ref[...], gavg, gw_ref[...]), cos, sin).astype(BF16)
    vst = vs_ref[...].T.astype(BF16)
    _store_vt_tiles(vsl_ref, vst, SEL_KC)
    _store_vt_tiles(vwd_ref, vw_ref[...].T.astype(BF16), LANE)
    gt_ref[...] = _sigmoid(g_ref[...]).T


def _nsa_prep(b_q, b_ks, b_kw, b_vs, b_vw, b_g, cos, sin, gq, gs, gw, gavg, B, S, tm=SEL_KC):
    nb = S // tm
    row = lambda w: pl.BlockSpec((tm, w), lambda b, i: (b * nb + i, 0))
    tab = pl.BlockSpec((tm, LANE), lambda b, i: (i, 0))
    HD = NSA_H * NSA_D
    vt_shape = lambda width: jax.ShapeDtypeStruct((B, S // width, NSA_G, VT_ROWS, width), BF16)
    vt_spec = lambda width: pl.BlockSpec((None, tm // width, NSA_G, VT_ROWS, width), lambda b, i: (b, i, 0, 0, 0))
    return pl.pallas_call(
        _nsa_prep_body,
        out_shape=[jax.ShapeDtypeStruct((B, HD, S), BF16),
                   jax.ShapeDtypeStruct((B, S, LANE), BF16),
                   jax.ShapeDtypeStruct((B, S, LANE), BF16),
                   vt_shape(SEL_KC),
                   vt_shape(LANE),
                   jax.ShapeDtypeStruct((B, LANE, S), F32)],
        grid=(B, nb),
        in_specs=[row(HD), row(LANE), row(LANE), row(LANE), row(LANE), row(LANE), tab, tab,
                  _const_spec((1, LANE)), _const_spec((1, LANE)), _const_spec((1, LANE)), _const_spec((LANE, LANE))],
        out_specs=[pl.BlockSpec((None, HD, tm), lambda b, i: (b, 0, i)),
                   pl.BlockSpec((None, tm, LANE), lambda b, i: (b, i, 0)),
                   pl.BlockSpec((None, tm, LANE), lambda b, i: (b, i, 0)),
                   vt_spec(SEL_KC), vt_spec(LANE),
                   pl.BlockSpec((None, LANE, tm), lambda b, i: (b, 0, i))],
        compiler_params=_params(("parallel", "parallel")),
        name="nsa_prep",
    )(b_q, b_ks, b_kw, b_vs, b_vw, b_g, cos, sin, gq, gs, gw, gavg)


def _gelu_tanh(x):
    return 0.5 * x * (1.0 + jnp.tanh(0.7978845608028654 * (x + 0.044715 * x * x * x)))


def _nsa_cmp_body(xk_ref, xv_ref, pe_ref, wt_ref, wb_ref, w2_ref, cos_ref, sin_ref, gk_ref, gavg_ref,
                  kc_ref, vct_ref):
    n = xk_ref.shape[0]

    def compress(x, which):
        u = _dot((x + pe_ref[which, 0:1, :]).astype(BF16), wt_ref[which])
        v = _dot((x + pe_ref[which, 1:2, :]).astype(BF16), wb_ref[which])
        hid = u + pltpu.roll(v, n - 1, 0)
        return _dot(_gelu_tanh(hid).astype(BF16), w2_ref[which])

    ck = compress(xk_ref[...], 0)
    kc_ref[...] = _rope_lanes(_group_rms(ck, gavg_ref[...], gk_ref[...]), cos_ref[...], sin_ref[...]).astype(BF16)
    vct = compress(xv_ref[...], 1).T.astype(BF16)
    for g in range(NSA_G):
        vct_ref[g] = vct[g * NSA_D:(g + 1) * NSA_D, :]


def _nsa_cmp(xk, xv, pe, wt, wb, w2, cos, sin, gk, gavg):
    B, n, W = xk.shape
    return pl.pallas_call(
        _nsa_cmp_body,
        out_shape=[jax.ShapeDtypeStruct((B, n, LANE), BF16),
                   jax.ShapeDtypeStruct((B, NSA_G, NSA_D, n), BF16)],
        grid=(B,),
        in_specs=[pl.BlockSpec((None, n, W), lambda b: (b, 0, 0)), pl.BlockSpec((None, n, W), lambda b: (b, 0, 0)),
                  _const_spec(pe.shape), _const_spec(wt.shape), _const_spec(wb.shape), _const_spec(w2.shape),
                  _const_spec((n, LANE)), _const_spec((n, LANE)), _const_spec((1, LANE)), _const_spec((LANE, LANE))],
        out_specs=[pl.BlockSpec((None, n, LANE), lambda b: (b, 0, 0)),
                   pl.BlockSpec((None, NSA_G, NSA_D, n), lambda b: (b, 0, 0, 0))],
        compiler_params=_params(("parallel",)),
        name="nsa_compress",
    )(xk, xv, pe, wt, wb, w2, cos, sin, gk, gavg)


def _group_queries(qt_ref, g):
    HPG = NSA_H // NSA_G
    q4 = jnp.concatenate([qt_ref[(g * HPG + h) * NSA_D:(g * HPG + h + 1) * NSA_D, :] for h in range(HPG)], axis=1)
    parts = [jnp.zeros(q4.shape, BF16)] * NSA_G
    parts[g] = q4
    return jnp.concatenate(parts, axis=0)


def _nsa_select_body(qt_ref, kc_ref, vct_ref, kw_ref, vwd_ref, gt_ref, part_ref, sel_ref, selm_ref, ps_buf,
                     *, sel_tile):
    TQ = LANE
    HPG = NSA_H // NSA_G
    W = HPG * TQ
    n_cmp = kc_ref.shape[0]
    n_sel = sel_ref.shape[1]
    i = pl.program_id(1)
    s0 = i * TQ
    t_row = s0 + _iota((1, TQ), 1)
    n_win = WIN // LANE + 1
    j0 = jnp.maximum(i - (n_win - 1), 0)
    w_start = pl.multiple_of(j0 * LANE, LANE)
    own_first = (s0 // sel_tile) * (sel_tile // SLC_LEN)
    ps_buf[:, 0:PS_PAD, :] = jnp.zeros((NSA_G, PS_PAD, TQ), F32)

    for g in range(NSA_G):
        qpad = _group_queries(qt_ref, g)

        cend = _iota((n_cmp, 1), 0) * CMP_STRIDE + (CMP_LEN - 1)
        bias_c = jnp.where(cend <= t_row, 0.0, NEG)
        sc = _dot(kc_ref[...], qpad) + jnp.concatenate([bias_c] * HPG, axis=1)
        m = jnp.max(sc, axis=0, keepdims=True)
        m = jnp.where(m > 0.5 * NEG, m, 0.0)
        p = jnp.exp2(sc - m)
        p = p * (1.0 / jnp.maximum(jnp.sum(p, axis=0, keepdims=True), 1e-30))
        o_c = _dot(vct_ref[g], p.astype(BF16))
        psum = p[:, 0:TQ]
        for h in range(1, HPG):
            psum = psum + p[:, h * TQ:(h + 1) * TQ]
        ps_buf[g, PS_PAD:PS_PAD + n_cmp, :] = psum

        per = SLC_LEN // CMP_STRIDE
        imp = ps_buf[g, pl.ds(PS_PAD - 1, n_sel, stride=per), :]
        for jj in range(per):
            imp = imp + ps_buf[g, pl.ds(PS_PAD + jj, n_sel, stride=per), :]
        blk = _iota((n_sel, 1), 0)
        cur = t_row >> (SLC_LEN.bit_length() - 1)
        valid = blk * SLC_LEN <= t_row
        forced = (blk == 0) | (blk == cur) | (blk == cur - 1)
        score0 = jnp.where(valid, jnp.where(forced, FORCED_SCORE, imp), NEG)
        score = score0
        for _ in range(min(SLC_TOPK, n_sel)):
            best = jnp.max(score, axis=0, keepdims=True)
            first = jnp.min(jnp.where(score == best, blk, n_sel), axis=0, keepdims=True)
            score = jnp.where(blk == first, NEG, score)
        chosen = score < score0
        sel_ref[g] = jnp.where(chosen, 0.0, NEG)
        selm_ref[g] = jnp.where(chosen & (blk < own_first), 0.0, NEG)

        kpos = w_start + _iota((n_win * LANE, 1), 0)
        bias_w = jnp.where((kpos <= t_row) & (kpos > t_row - WIN), 0.0, NEG)
        sw = _dot(kw_ref[pl.ds(w_start, n_win * LANE), :], qpad) + jnp.concatenate([bias_w] * HPG, axis=1)
        pw = jnp.exp2(sw - jnp.max(sw, axis=0, keepdims=True)).astype(BF16)
        acc_w = jnp.zeros((VT_ROWS, W), F32)
        for r in range(n_win):
            acc_w = acc_w + _dot(vwd_ref[j0 + r, g], pw[r * LANE:(r + 1) * LANE, :])
        o_w = acc_w[0:NSA_D, :] * (1.0 / acc_w[NSA_D:NSA_D + 1, :])

        for hp in range(HPG // 2):
            tiles = []
            for h in (2 * hp, 2 * hp + 1):
                gr = (g * HPG + h) * 3
                cs = slice(h * TQ, (h + 1) * TQ)
                tiles.append(gt_ref[gr:gr + 1, :] * o_c[:, cs] + gt_ref[gr + 2:gr + 3, :] * o_w[:, cs])
            col = (g * HPG + 2 * hp) * NSA_D
            part_ref[:, col:col + 2 * NSA_D] = jnp.concatenate(tiles, axis=0).T


def _nsa_select(qt, kc, vct, kw, vwd, gt, B, S):
    HD = NSA_H * NSA_D
    TQ = LANE
    n_cmp = kc.shape[1]
    n_sel = S // SLC_LEN
    assert S >= WIN + TQ and CMP_LEN == 2 * CMP_STRIDE
    whole = lambda shape: pl.BlockSpec((None,) + shape, lambda b, i: (b,) + (0,) * len(shape),
                                       pipeline_mode=pl.Buffered(1))
    mask_shape = jax.ShapeDtypeStruct((B, NSA_G, n_sel, S), F32)
    mask_spec = pl.BlockSpec((None, NSA_G, n_sel, TQ), lambda b, i: (b, 0, 0, i))
    return pl.pallas_call(
        functools.partial(_nsa_select_body, sel_tile=SEL_TQ),
        out_shape=[jax.ShapeDtypeStruct((B * S, HD), F32), mask_shape, mask_shape],
        grid=(B, S // TQ),
        in_specs=[pl.BlockSpec((None, HD, TQ), lambda b, i: (b, 0, i)),
                  whole((n_cmp, LANE)), whole(vct.shape[1:]), whole((S, LANE)), whole(vwd.shape[1:]),
                  pl.BlockSpec((None, LANE, TQ), lambda b, i: (b, 0, i))],
        out_specs=[pl.BlockSpec((TQ, HD), lambda b, i: (b * (S // TQ) + i, 0)), mask_spec, mask_spec],
        scratch_shapes=[pltpu.VMEM((NSA_G, n_cmp + PS_PAD, TQ), F32)],
        compiler_params=_params(("parallel", "parallel")),
        name="nsa_select",
    )(qt, kc, vct, kw, vwd, gt)


def _nsa_selected_body(qt_ref, ks_ref, ksd_ref, vsl_ref, selm_ref, seld_ref, gt_ref, part_ref, oneh_ref, o_ref,
                       s_a, s_b, p_a, p_b, al_a, al_b, m_ref, acc_ref):
    TQ = qt_ref.shape[1]
    HPG = NSA_H // NSA_G
    W = HPG * TQ
    n_kc = ks_ref.shape[0] // SEL_KC
    bps = SEL_KC // SLC_LEN
    i = pl.program_id(1)
    n_main = i * (TQ // SEL_KC)
    n_pairs = (n_main + 1) // 2
    diag_ok = _iota((TQ, 1), 0) <= _iota((1, TQ), 1)
    pad_rows = jnp.zeros((LANE - 2 * bps, W), BF16)

    def q_aug(qpad, brows):
        brows = jnp.concatenate([jnp.concatenate([brows] * HPG, axis=1), jnp.zeros((bps, W), F32)], axis=0)
        return jnp.concatenate([qpad, brows.astype(BF16), pad_rows], axis=0)

    heads = [slice(h * TQ, (h + 1) * TQ) for h in range(HPG)]

    def softmax(s_buf, p_buf, al_ref, cs):
        sb = s_buf[:, cs]
        m_i = m_ref[:, cs]
        m_new = jnp.maximum(m_i, jnp.max(sb, axis=0, keepdims=True).astype(F32))
        p_buf[:, cs] = jnp.exp2(sb - m_new.astype(BF16))
        al_ref[:, cs] = jnp.exp2(m_i - m_new)
        m_ref[:, cs] = m_new

    def apply_values(p_buf, al_ref, vt, cs):
        acc_ref[:, cs] = al_ref[:, cs] * acc_ref[:, cs] + _dot(vt, p_buf[:, cs])

    for g in range(NSA_G):
        qpad = _group_queries(qt_ref, g)
        m_ref[...] = jnp.full(m_ref.shape, NEG, F32)
        acc_ref[...] = jnp.zeros(acc_ref.shape, F32)
        p_b[...] = jnp.zeros(p_b.shape, BF16)
        al_b[...] = jnp.ones(al_b.shape, F32)

        def step_operands(c):
            k0 = pl.multiple_of(c * SEL_KC, SEL_KC)
            brows = selm_ref[g, pl.ds(pl.multiple_of(c * bps, bps), bps), :]
            return jnp.concatenate([ks_ref[pl.ds(k0, SEL_KC), :], oneh_ref[...]], axis=1), q_aug(qpad, brows)

        def scores(ops, s_buf, cs):
            s_buf[:, cs] = _dot(ops[0], ops[1][:, cs]).astype(BF16)

        def pair(cc, carry):
            c0 = 2 * cc
            ops, vt = step_operands(c0 + 1), vsl_ref[jnp.maximum(c0 - 1, 0), g]
            for cs in heads:
                scores(ops, s_b, cs)
                softmax(s_a, p_a, al_a, cs)
                apply_values(p_b, al_b, vt, cs)
            ops, vt = step_operands(jnp.minimum(c0 + 2, n_kc - 1)), vsl_ref[c0, g]
            for cs in heads:
                scores(ops, s_a, cs)
                softmax(s_b, p_b, al_b, cs)
                apply_values(p_a, al_a, vt, cs)
            return carry

        ops = step_operands(0)
        for cs in heads:
            scores(ops, s_a, cs)
        lax.fori_loop(0, n_pairs, pair, 0)
        vt = vsl_ref[jnp.maximum(2 * n_pairs - 1, 0), g]
        for cs in heads:
            apply_values(p_b, al_b, vt, cs)

        for d in range(TQ // SEL_KC):
            k_aug = jnp.concatenate([ksd_ref[d * SEL_KC:(d + 1) * SEL_KC, :], oneh_ref[...]], axis=1)
            qa = q_aug(qpad, seld_ref[g, d * bps:(d + 1) * bps, :])
            vt = vsl_ref[i * (TQ // SEL_KC) + d, g]
            for cs in heads:
                sd = _dot(k_aug, qa[:, cs])
                s_a[:, cs] = jnp.where(diag_ok[d * SEL_KC:(d + 1) * SEL_KC, :], sd, NEG).astype(BF16)
                softmax(s_a, p_a, al_a, cs)
                apply_values(p_a, al_a, vt, cs)
        acc_s = acc_ref[...]
        o_s = acc_s[0:NSA_D, :] * (1.0 / acc_s[NSA_D:NSA_D + 1, :])

        for hp in range(HPG // 2):
            tiles = []
            for h in (2 * hp, 2 * hp + 1):
                gr = (g * HPG + h) * 3 + 1
                tiles.append(gt_ref[gr:gr + 1, :] * o_s[:, h * TQ:(h + 1) * TQ])
            cols = slice((g * HPG + 2 * hp) * NSA_D, (g * HPG + 2 * hp + 2) * NSA_D)
            o_ref[:, cols] = part_ref[:, cols] + jnp.concatenate(tiles, axis=0).T


def _nsa_selected(qt, ks, vsl, selm, sel, gt, part, B, S):
    HD = NSA_H * NSA_D
    TQ = SEL_TQ
    W = NSA_H // NSA_G * TQ
    n_sel = S // SLC_LEN
    assert S % (2 * SEL_KC) == 0 and TQ % SEL_KC == 0
    whole = lambda shape: pl.BlockSpec((None,) + shape, lambda b, i: (b,) + (0,) * len(shape),
                                       pipeline_mode=pl.Buffered(1))
    oneh = jnp.asarray(np.arange(SEL_KC)[:, None] // SLC_LEN == np.arange(LANE)[None, :], BF16)
    return pl.pallas_call(
        _nsa_selected_body,
        out_shape=jax.ShapeDtypeStruct((B * S, HD), F32),
        grid=(B, S // TQ),
        in_specs=[pl.BlockSpec((None, HD, TQ), lambda b, i: (b, 0, i)),
                  whole((S, LANE)), pl.BlockSpec((None, TQ, LANE), lambda b, i: (b, i, 0)),
                  whole(vsl.shape[1:]),
                  pl.BlockSpec((None, NSA_G, n_sel, TQ), lambda b, i: (b, 0, 0, i)),
                  pl.BlockSpec((None, NSA_G, TQ // SLC_LEN, TQ), lambda b, i: (b, 0, i, i)),
                  pl.BlockSpec((None, LANE, TQ), lambda b, i: (b, 0, i)),
                  pl.BlockSpec((TQ, HD), lambda b, i: (b * (S // TQ) + i, 0)),
                  _const_spec((SEL_KC, LANE))],
        out_specs=pl.BlockSpec((TQ, HD), lambda b, i: (b * (S // TQ) + i, 0)),
        scratch_shapes=[pltpu.VMEM((SEL_KC, W), BF16)] * 4 + [pltpu.VMEM((1, W), F32)] * 3
                       + [pltpu.VMEM((VT_ROWS, W), F32)],
        compiler_params=_params(("parallel", "arbitrary")),
        name="nsa_selected",
    )(qt, ks, ks, vsl, selm, sel, gt, part, oneh)


def _merge_body(x_ref, oa_ref, ob_ref, oc_ref, gates_ref, wbr_ref, wo_ref, o_ref):
    D = x_ref.shape[1]
    y = jnp.zeros(x_ref.shape, F32)
    for j, br in enumerate((oa_ref, ob_ref, oc_ref)):
        y = y + _sigmoid(gates_ref[:, j * D:(j + 1) * D]) * _dot(br[...].astype(BF16), wbr_ref[j])
    o_ref[...] = x_ref[...] + _dot(y.astype(BF16), wo_ref[...])


def _merge(x, o_a, o_b, o_c, gates, wbr, wo, tm=512):
    T, D = x.shape
    row = lambda w: pl.BlockSpec((tm, w), lambda i: (i, 0))
    return pl.pallas_call(
        _merge_body,
        out_shape=jax.ShapeDtypeStruct((T, D), F32),
        grid=(T // tm,),
        in_specs=[row(D), row(MIX_W), row(MIX_W), row(MIX_W), row(3 * D), _const_spec(wbr.shape), _const_spec(wo.shape)],
        out_specs=row(D),
        compiler_params=_params(("parallel",)),
        name="merge_out",
    )(x, o_a, o_b, o_c, gates, wbr, wo)


def _pack_in_proj(w_in, b_in):
    ws, bs = [], []
    for _, off, w, wp in _SEGS:
        ws.append(jnp.pad(w_in[..., off:off + w], ((0, 0), (0, 0), (0, wp - w))))
        bs.append(jnp.pad(b_in[..., off:off + w], ((0, 0), (0, wp - w))))
    return jnp.concatenate(ws, axis=-1).astype(BF16), jnp.concatenate(bs, axis=-1)[:, None, :]


def _rope_tables(pos):
    half = NSA_D // 2
    freqs = ROPE_THETA ** (-jnp.arange(half, dtype=F32) / half)
    ang = pos.astype(F32)[:, None] * freqs[None, :]
    cos, sin = jnp.cos(ang), jnp.sin(ang)
    reps = LANE // NSA_D
    return jnp.tile(jnp.concatenate([cos, cos], axis=1), (1, reps)), jnp.tile(jnp.concatenate([-sin, sin], axis=1), (1, reps))


def _cmp_weights(w1, w2):
    L = w1.shape[0]
    w1r = w1.reshape(L, 2, 2, CMP_STRIDE, NSA_D, CMP_HIDDEN)
    eye_g = jnp.eye(NSA_G, dtype=w1.dtype)
    ex = jnp.einsum('lwstdh,gk->lwstgdkh', w1r, eye_g)
    ex = ex.reshape(L, 2, 2, CMP_STRIDE * NSA_G * NSA_D, NSA_G * CMP_HIDDEN)
    w2x = jnp.einsum('lwhd,gk->lwghkd', w2, eye_g).reshape(L, 2, NSA_G * CMP_HIDDEN, NSA_G * NSA_D)
    return ex[:, :, 0].astype(BF16), ex[:, :, 1].astype(BF16), w2x.astype(BF16)


def _cmp_pos_rows(pe):
    L = pe.shape[0]
    r = pe.reshape(L, 2, 2, CMP_STRIDE, 1, NSA_D)
    return jnp.broadcast_to(r, (L, 2, 2, CMP_STRIDE, NSA_G, NSA_D)).reshape(L, 2, 2, CMP_STRIDE * NSA_G * NSA_D)


def _lane_gain(g):
    return jnp.tile(g, LANE // g.shape[-1])[None, :]


def _layer(x, lw, consts, B, S):
    cos, sin, cos_c, sin_c, gavg = consts
    T = B * S
    x = _ffn(x, lw["ffn1_norm"], lw["ffn1_wg"], lw["ffn1_wu"], lw["ffn1_wd"])
    z = dict(zip([s[0] for s in _SEGS], _proj(x, lw["mix_norm"], lw["w_in"], lw["b_in"])))
    o_a = _gla(z["a_q"], z["a_k"], z["a_v"], z["a_r"], z["a_lr"], lw["gla_wa"], lw["gla_ba"], lw["gla_gn"], B, S)
    o_c = _mlstm(z["c_qk"], z["c_v"], z["c_o"], z["c_if"], lw["conv_w"], lw["conv_b"], B, S)
    qn = lw["nsa_qk_norm"]
    qt, ks, kw, vsl, vwd, gt = _nsa_prep(z["b_q"], z["b_ks"], z["b_kw"], z["b_vs"], z["b_vw"], z["b_g"], cos, sin,
                                         _lane_gain(qn[0]), _lane_gain(qn[2]), _lane_gain(qn[3]), gavg, B, S)
    n_blk = S // CMP_STRIDE
    xk = z["b_kc"].reshape(B, n_blk, CMP_STRIDE * LANE)
    xv = z["b_vc"].reshape(B, n_blk, CMP_STRIDE * LANE)
    kc, vct = _nsa_cmp(xk, xv, lw["cmp_pe"], lw["cmp_wt"], lw["cmp_wb"], lw["cmp_w2"], cos_c, sin_c,
                       _lane_gain(qn[1]), gavg)
    part, sel, selm = _nsa_select(qt, kc, vct, kw, vwd, gt, B, S)
    o_b = _nsa_selected(qt, ks, vsl, selm, sel, gt, part, B, S)
    x = _merge(x, o_a, o_b, o_c, z["gates"], lw["w_branch"], lw["w_out"])
    return _ffn(x, lw["ffn2_norm"], lw["ffn2_wg"], lw["ffn2_wu"], lw["ffn2_wd"])


def kernel(x, ffn1_norm, ffn1_w_gate, ffn1_w_up, ffn1_w_down, mix_norm, w_in, b_in, gla_w_alpha, gla_b_alpha, gla_out_norm, nsa_qk_norm, nsa_cmp_pos, nsa_cmp_w1, nsa_cmp_w2, mlstm_conv_w, mlstm_conv_b, w_branch, w_out, ffn2_norm, ffn2_w_gate, ffn2_w_up, ffn2_w_down):
    B, S, D = x.shape
    w_in_p, b_in_p = _pack_in_proj(w_in, b_in)
    cmp_wt, cmp_wb, cmp_w2 = _cmp_weights(nsa_cmp_w1, nsa_cmp_w2)
    layers = {
        "ffn1_norm": ffn1_norm[:, None, :], "ffn1_wg": ffn1_w_gate.astype(BF16), "ffn1_wu": ffn1_w_up.astype(BF16),
        "ffn1_wd": ffn1_w_down.astype(BF16),
        "mix_norm": mix_norm[:, None, :], "w_in": w_in_p, "b_in": b_in_p,
        "gla_wa": jnp.pad(gla_w_alpha, ((0, 0), (0, LANE - GLA_RANK), (0, 0))), "gla_ba": gla_b_alpha[:, None, :],
        "gla_gn": gla_out_norm[:, None, :],
        "nsa_qk_norm": nsa_qk_norm, "cmp_pe": _cmp_pos_rows(nsa_cmp_pos), "cmp_wt": cmp_wt, "cmp_wb": cmp_wb,
        "cmp_w2": cmp_w2,
        "conv_w": mlstm_conv_w, "conv_b": mlstm_conv_b[:, None, :],
        "w_branch": w_branch.astype(BF16), "w_out": w_out.astype(BF16),
        "ffn2_norm": ffn2_norm[:, None, :], "ffn2_wg": ffn2_w_gate.astype(BF16), "ffn2_wu": ffn2_w_up.astype(BF16),
        "ffn2_wd": ffn2_w_down.astype(BF16),
    }
    n_blk = S // CMP_STRIDE
    cos, sin = _rope_tables(jnp.arange(S))
    cos_c, sin_c = _rope_tables(jnp.arange(n_blk) * CMP_STRIDE + CMP_LEN - 1)
    gavg = jnp.asarray(np.kron(np.eye(LANE // NSA_D), np.full((NSA_D, NSA_D), 1.0 / NSA_D)), BF16)
    consts = (cos, sin, cos_c, sin_c, gavg)

    def step(xc, lw):
        return _layer(xc, lw, consts, B, S), None

    out, _ = lax.scan(step, x.reshape(B * S, D), layers)
    return out.reshape(B, S, D)
```

```python
import functools

import numpy as np
import jax
import jax.numpy as jnp
from jax import lax
from jax.experimental import pallas as pl
from jax.experimental.pallas import tpu as pltpu

F32 = jnp.float32
BF16 = jnp.bfloat16

RMS_EPS = 1e-6
ROPE_THETA = 10000.0

GLA_H, GLA_DK, GLA_DV, GLA_RANK, GLA_GATE_NORM = 4, 64, 128, 16, 16.0
GLA_SUB = 16
NSA_H, NSA_G, NSA_D = 8, 2, 64
CMP_LEN, CMP_STRIDE, CMP_HIDDEN = 32, 16, 256
SLC_LEN, SLC_TOPK, WIN = 64, 16, 512
FORCED_SCORE = 1e4
MLSTM_H, MLSTM_DK, MLSTM_DV, CONV_W = 4, 64, 128, 4
MLSTM_CHUNK = 64
MIX_W = 512
LANE = 128
NEG = -1e30
LOG2E = 1.4426950408889634
SEL_KC = 512
SEL_TQ = 512
PS_PAD = 8
VT_ROWS = 80

_SEGS = (
    ("a_q", 0, 256, 256), ("a_k", 256, 256, 256), ("a_v", 512, 512, 512), ("a_r", 1024, 512, 512),
    ("a_lr", 1536, 16, 128),
    ("b_q", 1552, 512, 512), ("b_kc", 2064, 128, 128), ("b_vc", 2192, 128, 128), ("b_ks", 2320, 128, 128),
    ("b_vs", 2448, 128, 128), ("b_kw", 2576, 128, 128), ("b_vw", 2704, 128, 128), ("b_g", 2832, 24, 128),
    ("c_qk", 2856, 512, 512), ("c_v", 3368, 512, 512), ("c_o", 3880, 512, 512), ("c_if", 4392, 8, 128),
    ("gates", 4400, 3072, 3072),
)
_N_PACK = sum(s[3] for s in _SEGS)


def _dot(a, b):
    return jnp.dot(a, b, preferred_element_type=F32)


def _dot_nt(a, b):
    return lax.dot_general(a, b, (((1,), (1,)), ((), ())), preferred_element_type=F32)


def _split2(a):
    hi = a.astype(BF16)
    lo = (a - hi.astype(F32)).astype(BF16)
    return hi, lo


def _dot_l2(a, b):
    hi, lo = _split2(a)
    return _dot(hi, b) + _dot(lo, b)


def _dot_r2(a, b):
    hi, lo = _split2(b)
    return _dot(a, hi) + _dot(a, lo)


def _log_sigmoid(x):
    return jnp.minimum(x, 0.0) - jnp.log(1.0 + jnp.exp(-jnp.abs(x)))


def _sigmoid(x):
    return 1.0 / (1.0 + jnp.exp(-x))


def _silu(x):
    return x * _sigmoid(x)


def _iota(shape, dim):
    return lax.broadcasted_iota(jnp.int32, shape, dim)


def _const_spec(shape):
    nd = len(shape)
    return pl.BlockSpec(shape, lambda *_: (0,) * nd, pipeline_mode=pl.Buffered(1))


def _params(sem, vmem_mb=56):
    return pltpu.CompilerParams(dimension_semantics=sem, vmem_limit_bytes=vmem_mb * 1024 * 1024)


def _ffn_body(x_ref, g_ref, wg_ref, wu_ref, wd_ref, o_ref, *, n_chunks):
    x = x_ref[...]
    ms = jnp.mean(x * x, axis=-1, keepdims=True)
    h = (x * lax.rsqrt(ms + RMS_EPS) * g_ref[...]).astype(BF16)
    fc = wg_ref.shape[1] // n_chunks
    acc = jnp.zeros(x.shape, F32)
    for c in range(n_chunks):
        a = _dot(h, wg_ref[:, c * fc:(c + 1) * fc])
        u = _dot(h, wu_ref[:, c * fc:(c + 1) * fc])
        t = (_silu(a) * u).astype(BF16)
        acc = acc + _dot(t, wd_ref[c * fc:(c + 1) * fc, :])
    o_ref[...] = x + 0.5 * acc


def _ffn(x, g, wg, wu, wd, tm=512):
    T, D = x.shape
    F = wg.shape[1]
    return pl.pallas_call(
        functools.partial(_ffn_body, n_chunks=2),
        out_shape=jax.ShapeDtypeStruct((T, D), F32),
        grid=(T // tm,),
        in_specs=[pl.BlockSpec((tm, D), lambda i: (i, 0)), _const_spec((1, D)),
                  _const_spec((D, F)), _const_spec((D, F)), _const_spec((F, D))],
        out_specs=pl.BlockSpec((tm, D), lambda i: (i, 0)),
        compiler_params=_params(("parallel",)),
        name="ffn",
    )(x, g, wg, wu, wd)


def _proj_body(x_ref, g_ref, w_ref, b_ref, *o_refs):
    x = x_ref[...]
    ms = jnp.mean(x * x, axis=-1, keepdims=True)
    h = (x * lax.rsqrt(ms + RMS_EPS) * g_ref[...]).astype(BF16)
    off = 0
    for o_ref in o_refs:
        w = o_ref.shape[1]
        o_ref[...] = _dot(h, w_ref[:, off:off + w]) + b_ref[:, off:off + w]
        off += w


def _proj(x, g, w, b, tm=256):
    T, D = x.shape
    return pl.pallas_call(
        _proj_body,
        out_shape=[jax.ShapeDtypeStruct((T, s[3]), F32) for s in _SEGS],
        grid=(T // tm,),
        in_specs=[pl.BlockSpec((tm, D), lambda i: (i, 0)), _const_spec((1, D)),
                  _const_spec((D, _N_PACK)), _const_spec((1, _N_PACK))],
        out_specs=[pl.BlockSpec((tm, s[3]), lambda i: (i, 0)) for s in _SEGS],
        compiler_params=_params(("parallel",)),
        name="in_proj",
    )(x, g, w, b)


def _gla_body(q_ref, k_ref, v_ref, r_ref, lr_ref, wa_ref, ba_ref, gn_ref, tri_ref, bones_ref, eh_ref,
              o_ref, st_ref, qs_ref, c_ref, tot_ref, kst_ref, vt4_ref, oi_ref):
    nB, Lc = q_ref.shape[0], q_ref.shape[1]
    n_sub = Lc // GLA_SUB
    HK = GLA_H * GLA_DK

    @pl.when(pl.program_id(0) == 0)
    def _():
        st_ref[...] = jnp.zeros(st_ref.shape, F32)

    lane_h = _iota((1, HK), 1) >> (GLA_DK.bit_length() - 1)
    row_i = _iota((GLA_SUB, 1), 0)
    col_r = _iota((1, GLA_H * Lc), 1) & (Lc - 1)
    wa_hi, wa_lo = _split2(wa_ref[...])

    for b in range(nB):
        lr_hi, lr_lo = _split2(lr_ref[b])
        u = _dot(lr_hi, wa_hi) + _dot(lr_hi, wa_lo) + _dot(lr_lo, wa_hi) + ba_ref[...]
        g = _log_sigmoid(u) * (1.0 / GLA_GATE_NORM)
        c = _dot_r2(tri_ref[...], g)
        tot = _dot_r2(bones_ref[...], g)
        qs_ref[b] = q_ref[b] * (GLA_DK ** -0.5)
        c_ref[b] = c
        tot_ref[b] = tot
        kt = k_ref[b] * jnp.exp(tot - c)
        vt = v_ref[b].T.astype(BF16)
        for h in range(GLA_H):
            kst_ref[b, h * Lc:(h + 1) * Lc, :] = jnp.where(lane_h == h, kt, 0.0).astype(BF16)
            vt4_ref[b, :, h * Lc:(h + 1) * Lc] = vt[h * GLA_DV:(h + 1) * GLA_DV, :]

    def sub(b, s):
        rows = slice(s * GLA_SUB, (s + 1) * GLA_SUB)
        qs = qs_ref[b, rows, :]
        cs = c_ref[b, rows, :]
        st = st_ref[b]
        qd = qs * jnp.exp(cs)
        q4 = jnp.concatenate([jnp.where(lane_h == h, qd, 0.0) for h in range(GLA_H)], axis=0).astype(BF16)
        inter4 = _dot_nt(q4, st.astype(BF16))
        inter = jnp.concatenate([inter4[h * GLA_SUB:(h + 1) * GLA_SUB, :] for h in range(GLA_H)], axis=1)
        xs = []
        for j in range(GLA_SUB):
            r = s * GLA_SUB + j
            x = qs * k_ref[b, r:r + 1, :] * jnp.exp(jnp.minimum(cs - c_ref[b, r:r + 1, :], 0.0))
            xs.append(jnp.where(row_i >= j, x, 0.0))
        r_all = _dot(jnp.concatenate(xs, axis=0).astype(BF16), eh_ref[...])
        intra = jnp.zeros((GLA_SUB, r_all.shape[1]), F32)
        for j in range(GLA_SUB):
            r = s * GLA_SUB + j
            intra = intra + r_all[j * GLA_SUB:(j + 1) * GLA_SUB, :] * v_ref[b, r:r + 1, :]
        oi_ref[b, rows, :] = inter + intra
        dec = jnp.exp(tot_ref[b, s * GLA_SUB:s * GLA_SUB + 1, :])
        in_sub = (col_r >= s * GLA_SUB) & (col_r < (s + 1) * GLA_SUB)
        vtm = jnp.where(in_sub, vt4_ref[b], jnp.zeros((), BF16))
        st_ref[b] = dec * st + _dot(vtm, kst_ref[b])

    for s in range(n_sub):
        for b in range(nB):
            sub(b, s)

    gn = gn_ref[...]
    for b in range(nB):
        for h in range(GLA_H):
            sl = slice(h * GLA_DV, (h + 1) * GLA_DV)
            o = oi_ref[b, :, sl]
            ms = jnp.mean(o * o, axis=-1, keepdims=True)
            o_ref[b, :, sl] = o * lax.rsqrt(ms + RMS_EPS) * gn * _silu(r_ref[b, :, sl])


def _gla_consts(Lc):
    r = np.arange(Lc)
    same = (r[:, None] // GLA_SUB) == (r[None, :] // GLA_SUB)
    tri = (same & (r[None, :] <= r[:, None])).astype(np.float32)
    bones = same.astype(np.float32)
    hk = np.arange(GLA_H * GLA_DK) // GLA_DK
    hv = np.arange(GLA_H * GLA_DV) // GLA_DV
    eh = (hk[:, None] == hv[None, :]).astype(np.float32)
    return jnp.asarray(tri, BF16), jnp.asarray(bones, BF16), jnp.asarray(eh, BF16)


def _gla(a_q, a_k, a_v, a_r, a_lr, wa, ba, gn, B, S, Lc=128):
    HK, HV = GLA_H * GLA_DK, GLA_H * GLA_DV
    nb = S // Lc
    assert Lc & (Lc - 1) == 0
    tri, bones, eh = _gla_consts(Lc)
    row = lambda w: pl.BlockSpec((B, Lc, w), lambda i: (0, i, 0))
    seq = lambda t: t.reshape(B, S, t.shape[-1])
    out = pl.pallas_call(
        _gla_body,
        out_shape=jax.ShapeDtypeStruct((B, S, HV), F32),
        grid=(nb,),
        in_specs=[row(HK), row(HK), row(HV), row(HV), row(LANE),
                  _const_spec((LANE, HK)), _const_spec((1, HK)), _const_spec((1, GLA_DV)),
                  _const_spec((Lc, Lc)), _const_spec((Lc, Lc)), _const_spec((HK, HV))],
        out_specs=row(HV),
        scratch_shapes=[pltpu.VMEM((B, GLA_DV, HK), F32), pltpu.VMEM((B, Lc, HK), F32), pltpu.VMEM((B, Lc, HK), F32),
                        pltpu.VMEM((B, Lc, HK), F32), pltpu.VMEM((B, GLA_H * Lc, HK), BF16),
                        pltpu.VMEM((B, GLA_DV, GLA_H * Lc), BF16), pltpu.VMEM((B, Lc, HV), F32)],
        compiler_params=_params(("arbitrary",)),
        name="gla",
    )(seq(a_q), seq(a_k), seq(a_v), seq(a_r), seq(a_lr), wa, ba, gn, tri, bones, eh)
    return out.reshape(B * S, HV)


def _mlstm_body(qk_ref, v_ref, og_ref, if_ref, cw_ref, cb_ref, tri_ref, bdm_ref, e4_ref,
                o_ref, xx_ref, ct_ref, n_ref, m_ref, kw_ref, vt_ref, act_ref):
    Lc = v_ref.shape[0]
    L = MLSTM_CHUNK
    HK = MLSTM_H * MLSTM_DK
    tail = 8

    @pl.when(pl.program_id(1) == 0)
    def _():
        ct_ref[...] = jnp.zeros(ct_ref.shape, F32)
        n_ref[...] = jnp.zeros(n_ref.shape, F32)
        m_ref[...] = jnp.zeros(m_ref.shape, F32)
        xx_ref[0:tail, :] = jnp.zeros((tail, xx_ref.shape[1]), F32)

    xx_ref[tail:tail + Lc, :] = qk_ref[...]
    y = jnp.zeros((Lc, 2 * HK), F32) + cb_ref[...]
    for kk in range(CONV_W):
        y = y + cw_ref[kk:kk + 1, :] * xx_ref[pl.ds(tail - (CONV_W - 1) + kk, Lc), :]
    xx_ref[0:tail, :] = xx_ref[Lc:Lc + tail, :]
    act_ref[...] = _silu(y)

    gates = if_ref[...]
    logf = _log_sigmoid(gates)
    bcum = _dot_r2(tri_ref[...], logf)
    kw_ref[...] = jnp.zeros(kw_ref.shape, BF16)
    vt_ref[...] = v_ref[...].T.astype(BF16)
    eye = (_iota((L, L), 0) == _iota((L, L), 1)).astype(F32)
    causal = _iota((L, L), 0) >= _iota((L, L), 1)
    lane_hk = _iota((1, HK), 1) >> (MLSTM_DK.bit_length() - 1)
    col_l = _iota((1, Lc), 1)

    for ci in range(Lc // L):
        rows = slice(ci * L, (ci + 1) * L)
        q_all = act_ref[rows, 0:HK]
        k_all = act_ref[rows, HK:2 * HK] * (MLSTM_DK ** -0.5)
        ct = ct_ref[...]
        n_row = n_ref[...]
        inter_all = _dot_nt(q_all.astype(BF16), ct.astype(BF16))
        qn_all = _dot_l2(q_all * n_row, e4_ref[...])
        k_bf = k_all.astype(BF16)
        wk_full = jnp.zeros((L, HK), F32)
        dec_row = jnp.zeros((1, HK), F32)
        for h in range(MLSTM_H):
            icol = gates[rows, h:h + 1]
            bcol = bcum[rows, MLSTM_H + h:MLSTM_H + h + 1]
            m_st = m_ref[0:1, h:h + 1]
            r_row = jnp.sum((icol - bcol) * eye, axis=0, keepdims=True)
            dmat = jnp.where(causal, bcol + r_row, -jnp.inf)
            inter_log = bcol + m_st
            m_row = jnp.maximum(inter_log, jnp.max(dmat, axis=-1, keepdims=True))
            w_inter = jnp.exp(inter_log - m_row)
            head = lane_hk == h
            qh = jnp.where(head, q_all, 0.0).astype(BF16)
            s_qk = _dot_nt(qh, k_bf) * jnp.exp(dmat - m_row)
            vh = v_ref[rows, h * MLSTM_DV:(h + 1) * MLSTM_DV]
            num = w_inter * inter_all[:, h * MLSTM_DV:(h + 1) * MLSTM_DV] + _dot(s_qk.astype(BF16), vh.astype(BF16))
            den = w_inter * qn_all[:, h:h + 1] + jnp.sum(s_qk, axis=-1, keepdims=True)
            hh = num / jnp.maximum(jnp.abs(den), jnp.exp(-m_row))
            o_ref[rows, h * MLSTM_DV:(h + 1) * MLSTM_DV] = hh * _sigmoid(og_ref[rows, h * MLSTM_DV:(h + 1) * MLSTM_DV])
            m_new = m_row[L - 1:L, :]
            b_last = bcol[L - 1:L, :]
            w_k = jnp.exp(b_last - bcol + icol - m_new)
            decay = jnp.exp(b_last + m_st - m_new)
            wk_full = wk_full + jnp.where(head, w_k, 0.0)
            dec_row = dec_row + jnp.where(head, decay, 0.0)
            m_ref[0:1, h:h + 1] = m_new
        kw = k_all * wk_full
        kw_ref[rows, :] = kw.astype(BF16)
        in_chunk = (col_l >= ci * L) & (col_l < (ci + 1) * L)
        vtm = jnp.where(in_chunk, vt_ref[...], jnp.zeros((), BF16))
        ct_ref[...] = dec_row * ct + _dot(vtm, kw_ref[...]) * bdm_ref[...]
        n_ref[...] = dec_row * n_row + jnp.sum(kw, axis=0, keepdims=True)


def _mlstm_consts(Lc):
    r = np.arange(Lc)
    same = (r[:, None] // MLSTM_CHUNK) == (r[None, :] // MLSTM_CHUNK)
    tri = (same & (r[None, :] <= r[:, None])).astype(np.float32)
    hk = np.arange(MLSTM_H * MLSTM_DK) // MLSTM_DK
    hv = np.arange(MLSTM_H * MLSTM_DV) // MLSTM_DV
    eh = (hk[:, None] == hv[None, :]).astype(np.float32)
    e4 = (hk[:, None] == np.arange(LANE)[None, :]).astype(np.float32)
    return jnp.asarray(tri, BF16), jnp.asarray(eh.T.copy(), F32), jnp.asarray(e4, BF16)


def _mlstm(c_qk, c_v, c_o, c_if, cw, cb, B, S, Lc=256):
    HK, HV = MLSTM_H * MLSTM_DK, MLSTM_H * MLSTM_DV
    nb = S // Lc
    tri, bdm, e4 = _mlstm_consts(Lc)
    row = lambda w: pl.BlockSpec((Lc, w), lambda b, i: (b * nb + i, 0))
    return pl.pallas_call(
        _mlstm_body,
        out_shape=jax.ShapeDtypeStruct((B * S, HV), F32),
        grid=(B, nb),
        in_specs=[row(2 * HK), row(HV), row(HV), row(LANE),
                  _const_spec((CONV_W, 2 * HK)), _const_spec((1, 2 * HK)),
                  _const_spec((Lc, Lc)), _const_spec((HV, HK)), _const_spec((HK, LANE))],
        out_specs=row(HV),
        scratch_shapes=[pltpu.VMEM((Lc + 8, 2 * HK), F32), pltpu.VMEM((HV, HK), F32), pltpu.VMEM((1, HK), F32),
                        pltpu.VMEM((8, LANE), F32), pltpu.VMEM((Lc, HK), BF16), pltpu.VMEM((HV, Lc), BF16),
                        pltpu.VMEM((Lc, 2 * HK), F32)],
        compiler_params=_params(("parallel", "arbitrary")),
        name="mlstm",
    )(c_qk, c_v, c_o, c_if, cw, cb, tri, bdm, e4)


def _group_rms(x, gavg, gain):
    ms = _dot_l2(x * x, gavg)
    return x * lax.rsqrt(ms + RMS_EPS) * gain


def _rope_lanes(x, cos, sin_signed):
    half = NSA_D // 2
    first = (_iota((1, LANE), 1) & (NSA_D - 1)) < half
    swapped = jnp.where(first, pltpu.roll(x, LANE - half, 1), pltpu.roll(x, half, 1))
    return x * cos + swapped * sin_signed


def _store_vt_tiles(dst_ref, vt, width):
    ones = jnp.ones((VT_ROWS - NSA_D, width), BF16)
    for j in range(vt.shape[1] // width):
        for g in range(NSA_G):
            dst_ref[j, g, 0:NSA_D, :] = vt[g * NSA_D:(g + 1) * NSA_D, j * width:(j + 1) * width]
            dst_ref[j, g, NSA_D:VT_ROWS, :] = ones


def _nsa_prep_body(q_ref, ks_ref, kw_ref, vs_ref, vw_ref, g_ref, cos_ref, sin_ref, gq_ref, gs_ref, gw_ref, gavg_ref,
                   qt_ref, kso_ref, kwo_ref, vsl_ref, vwd_ref, gt_ref):
    cos, sin = cos_ref[...], sin_ref[...]
    gavg = gavg_ref[...]
    for cb in range(NSA_H * NSA_D // LANE):
        sl = slice(cb * LANE, (cb + 1) * LANE)
        qn = _rope_lanes(_group_rms(q_ref[:, sl], gavg, gq_ref[...]), cos, sin) * (NSA_D ** -0.5 * LOG2E)
        qt_ref[sl, :] = qn.T.astype(BF16)
    kso_ref[...] = _rope_lanes(_group_rms(ks_ref[...], gavg, gs_ref[...]), cos, sin).astype(BF16)
    kwo_ref[...] = _rope_lanes(_group_rms(kw_ref[...], gavg, gw_ref[...]), cos, sin).astype(BF16)
    vst = vs_ref[...].T.astype(BF16)
    _store_vt_tiles(vsl_ref, vst, SEL_KC)
    _store_vt_tiles(vwd_ref, vw_ref[...].T.astype(BF16), LANE)
    gt_ref[...] = _sigmoid(g_ref[...]).T


def _nsa_prep(b_q, b_ks, b_kw, b_vs, b_vw, b_g, cos, sin, gq, gs, gw, gavg, B, S, tm=SEL_KC):
    nb = S // tm
    row = lambda w: pl.BlockSpec((tm, w), lambda b, i: (b * nb + i, 0))
    tab = pl.BlockSpec((tm, LANE), lambda b, i: (i, 0))
    HD = NSA_H * NSA_D
    vt_shape = lambda width: jax.ShapeDtypeStruct((B, S // width, NSA_G, VT_ROWS, width), BF16)
    vt_spec = lambda width: pl.BlockSpec((None, tm // width, NSA_G, VT_ROWS, width), lambda b, i: (b, i, 0, 0, 0))
    return pl.pallas_call(
        _nsa_prep_body,
        out_shape=[jax.ShapeDtypeStruct((B, HD, S), BF16),
                   jax.ShapeDtypeStruct((B, S, LANE), BF16),
                   jax.ShapeDtypeStruct((B, S, LANE), BF16),
                   vt_shape(SEL_KC),
                   vt_shape(LANE),
                   jax.ShapeDtypeStruct((B, LANE, S), F32)],
        grid=(B, nb),
        in_specs=[row(HD), row(LANE), row(LANE), row(LANE), row(LANE), row(LANE), tab, tab,
                  _const_spec((1, LANE)), _const_spec((1, LANE)), _const_spec((1, LANE)), _const_spec((LANE, LANE))],
        out_specs=[pl.BlockSpec((None, HD, tm), lambda b, i: (b, 0, i)),
                   pl.BlockSpec((None, tm, LANE), lambda b, i: (b, i, 0)),
                   pl.BlockSpec((None, tm, LANE), lambda b, i: (b, i, 0)),
                   vt_spec(SEL_KC), vt_spec(LANE),
                   pl.BlockSpec((None, LANE, tm), lambda b, i: (b, 0, i))],
        compiler_params=_params(("parallel", "parallel")),
        name="nsa_prep",
    )(b_q, b_ks, b_kw, b_vs, b_vw, b_g, cos, sin, gq, gs, gw, gavg)


def _gelu_tanh(x):
    return 0.5 * x * (1.0 + jnp.tanh(0.7978845608028654 * (x + 0.044715 * x * x * x)))


def _nsa_cmp_body(xk_ref, xv_ref, pe_ref, wt_ref, wb_ref, w2_ref, cos_ref, sin_ref, gk_ref, gavg_ref,
                  kc_ref, vct_ref):
    n = xk_ref.shape[0]

    def compress(x, which):
        u = _dot((x + pe_ref[which, 0:1, :]).astype(BF16), wt_ref[which])
        v = _dot((x + pe_ref[which, 1:2, :]).astype(BF16), wb_ref[which])
        hid = u + pltpu.roll(v, n - 1, 0)
        return _dot(_gelu_tanh(hid).astype(BF16), w2_ref[which])

    ck = compress(xk_ref[...], 0)
    kc_ref[...] = _rope_lanes(_group_rms(ck, gavg_ref[...], gk_ref[...]), cos_ref[...], sin_ref[...]).astype(BF16)
    vct = compress(xv_ref[...], 1).T.astype(BF16)
    for g in range(NSA_G):
        vct_ref[g] = vct[g * NSA_D:(g + 1) * NSA_D, :]


def _nsa_cmp(xk, xv, pe, wt, wb, w2, cos, sin, gk, gavg):
    B, n, W = xk.shape
    return pl.pallas_call(
        _nsa_cmp_body,
        out_shape=[jax.ShapeDtypeStruct((B, n, LANE), BF16),
                   jax.ShapeDtypeStruct((B, NSA_G, NSA_D, n), BF16)],
        grid=(B,),
        in_specs=[pl.BlockSpec((None, n, W), lambda b: (b, 0, 0)), pl.BlockSpec((None, n, W), lambda b: (b, 0, 0)),
                  _const_spec(pe.shape), _const_spec(wt.shape), _const_spec(wb.shape), _const_spec(w2.shape),
                  _const_spec((n, LANE)), _const_spec((n, LANE)), _const_spec((1, LANE)), _const_spec((LANE, LANE))],
        out_specs=[pl.BlockSpec((None, n, LANE), lambda b: (b, 0, 0)),
                   pl.BlockSpec((None, NSA_G, NSA_D, n), lambda b: (b, 0, 0, 0))],
        compiler_params=_params(("parallel",)),
        name="nsa_compress",
    )(xk, xv, pe, wt, wb, w2, cos, sin, gk, gavg)


def _group_queries(qt_ref, g):
    HPG = NSA_H // NSA_G
    q4 = jnp.concatenate([qt_ref[(g * HPG + h) * NSA_D:(g * HPG + h + 1) * NSA_D, :] for h in range(HPG)], axis=1)
    parts = [jnp.zeros(q4.shape, BF16)] * NSA_G
    parts[g] = q4
    return jnp.concatenate(parts, axis=0)


def _nsa_select_body(qt_ref, kc_ref, vct_ref, kw_ref, vwd_ref, gt_ref, part_ref, sel_ref, selm_ref, ps_buf,
                     *, sel_tile):
    TQ = LANE
    HPG = NSA_H // NSA_G
    W = HPG * TQ
    n_cmp = kc_ref.shape[0]
    n_sel = sel_ref.shape[1]
    i = pl.program_id(1)
    s0 = i * TQ
    t_row = s0 + _iota((1, TQ), 1)
    n_win = WIN // LANE + 1
    j0 = jnp.maximum(i - (n_win - 1), 0)
    w_start = pl.multiple_of(j0 * LANE, LANE)
    own_first = (s0 // sel_tile) * (sel_tile // SLC_LEN)
    ps_buf[:, 0:PS_PAD, :] = jnp.zeros((NSA_G, PS_PAD, TQ), F32)

    for g in range(NSA_G):
        qpad = _group_queries(qt_ref, g)

        cend = _iota((n_cmp, 1), 0) * CMP_STRIDE + (CMP_LEN - 1)
        bias_c = jnp.where(cend <= t_row, 0.0, NEG)
        sc = _dot(kc_ref[...], qpad) + jnp.concatenate([bias_c] * HPG, axis=1)
        m = jnp.max(sc, axis=0, keepdims=True)
        m = jnp.where(m > 0.5 * NEG, m, 0.0)
        p = jnp.exp2(sc - m)
        p = p * (1.0 / jnp.maximum(jnp.sum(p, axis=0, keepdims=True), 1e-30))
        o_c = _dot(vct_ref[g], p.astype(BF16))
        psum = p[:, 0:TQ]
        for h in range(1, HPG):
            psum = psum + p[:, h * TQ:(h + 1) * TQ]
        ps_buf[g, PS_PAD:PS_PAD + n_cmp, :] = psum

        per = SLC_LEN // CMP_STRIDE
        imp = ps_buf[g, pl.ds(PS_PAD - 1, n_sel, stride=per), :]
        for jj in range(per):
            imp = imp + ps_buf[g, pl.ds(PS_PAD + jj, n_sel, stride=per), :]
        blk = _iota((n_sel, 1), 0)
        cur = t_row >> (SLC_LEN.bit_length() - 1)
        valid = blk * SLC_LEN <= t_row
        forced = (blk == 0) | (blk == cur) | (blk == cur - 1)
        score0 = jnp.where(valid, jnp.where(forced, FORCED_SCORE, imp), NEG)
        score = score0
        for _ in range(min(SLC_TOPK, n_sel)):
            best = jnp.max(score, axis=0, keepdims=True)
            first = jnp.min(jnp.where(score == best, blk, n_sel), axis=0, keepdims=True)
            score = jnp.where(blk == first, NEG, score)
        chosen = score < score0
        sel_ref[g] = jnp.where(chosen, 0.0, NEG)
        selm_ref[g] = jnp.where(chosen & (blk < own_first), 0.0, NEG)

        kpos = w_start + _iota((n_win * LANE, 1), 0)
        bias_w = jnp.where((kpos <= t_row) & (kpos > t_row - WIN), 0.0, NEG)
        sw = _dot(kw_ref[pl.ds(w_start, n_win * LANE), :], qpad) + jnp.concatenate([bias_w] * HPG, axis=1)
        pw = jnp.exp2(sw - jnp.max(sw, axis=0, keepdims=True)).astype(BF16)
        acc_w = jnp.zeros((VT_ROWS, W), F32)
        for r in range(n_win):
            acc_w = acc_w + _dot(vwd_ref[j0 + r, g], pw[r * LANE:(r + 1) * LANE, :])
        o_w = acc_w[0:NSA_D, :] * (1.0 / acc_w[NSA_D:NSA_D + 1, :])

        for hp in range(HPG // 2):
            tiles = []
            for h in (2 * hp, 2 * hp + 1):
                gr = (g * HPG + h) * 3
                cs = slice(h * TQ, (h + 1) * TQ)
                tiles.append(gt_ref[gr:gr + 1, :] * o_c[:, cs] + gt_ref[gr + 2:gr + 3, :] * o_w[:, cs])
            col = (g * HPG + 2 * hp) * NSA_D
            part_ref[:, col:col + 2 * NSA_D] = jnp.concatenate(tiles, axis=0).T


def _nsa_select(qt, kc, vct, kw, vwd, gt, B, S):
    HD = NSA_H * NSA_D
    TQ = LANE
    n_cmp = kc.shape[1]
    n_sel = S // SLC_LEN
    assert S >= WIN + TQ and CMP_LEN == 2 * CMP_STRIDE
    whole = lambda shape: pl.BlockSpec((None,) + shape, lambda b, i: (b,) + (0,) * len(shape),
                                       pipeline_mode=pl.Buffered(1))
    mask_shape = jax.ShapeDtypeStruct((B, NSA_G, n_sel, S), F32)
    mask_spec = pl.BlockSpec((None, NSA_G, n_sel, TQ), lambda b, i: (b, 0, 0, i))
    return pl.pallas_call(
        functools.partial(_nsa_select_body, sel_tile=SEL_TQ),
        out_shape=[jax.ShapeDtypeStruct((B * S, HD), F32), mask_shape, mask_shape],
        grid=(B, S // TQ),
        in_specs=[pl.BlockSpec((None, HD, TQ), lambda b, i: (b, 0, i)),
                  whole((n_cmp, LANE)), whole(vct.shape[1:]), whole((S, LANE)), whole(vwd.shape[1:]),
                  pl.BlockSpec((None, LANE, TQ), lambda b, i: (b, 0, i))],
        out_specs=[pl.BlockSpec((TQ, HD), lambda b, i: (b * (S // TQ) + i, 0)), mask_spec, mask_spec],
        scratch_shapes=[pltpu.VMEM((NSA_G, n_cmp + PS_PAD, TQ), F32)],
        compiler_params=_params(("parallel", "parallel")),
        name="nsa_select",
    )(qt, kc, vct, kw, vwd, gt)


def _nsa_selected_body(qt_ref, ks_ref, ksd_ref, vsl_ref, selm_ref, seld_ref, gt_ref, part_ref, oneh_ref, o_ref,
                       s_a, s_b, p_a, p_b, al_a, al_b, m_ref, acc_ref):
    TQ = qt_ref.shape[1]
    HPG = NSA_H // NSA_G
    W = HPG * TQ
    n_kc = ks_ref.shape[0] // SEL_KC
    bps = SEL_KC // SLC_LEN
    i = pl.program_id(1)
    n_main = i * (TQ // SEL_KC)
    n_pairs = (n_main + 1) // 2
    diag_ok = _iota((TQ, 1), 0) <= _iota((1, TQ), 1)
    pad_rows = jnp.zeros((LANE - 2 * bps, W), BF16)

    def q_aug(qpad, brows):
        brows = jnp.concatenate([jnp.concatenate([brows] * HPG, axis=1), jnp.zeros((bps, W), F32)], axis=0)
        return jnp.concatenate([qpad, brows.astype(BF16), pad_rows], axis=0)

    heads = [slice(h * TQ, (h + 1) * TQ) for h in range(HPG)]

    def softmax(s_buf, p_buf, al_ref, cs):
        sb = s_buf[:, cs]
        m_i = m_ref[:, cs]
        m_new = jnp.maximum(m_i, jnp.max(sb, axis=0, keepdims=True).astype(F32))
        p_buf[:, cs] = jnp.exp2(sb - m_new.astype(BF16))
        al_ref[:, cs] = jnp.exp2(m_i - m_new)
        m_ref[:, cs] = m_new

    def apply_values(p_buf, al_ref, vt, cs):
        acc_ref[:, cs] = al_ref[:, cs] * acc_ref[:, cs] + _dot(vt, p_buf[:, cs])

    for g in range(NSA_G):
        qpad = _group_queries(qt_ref, g)
        m_ref[...] = jnp.full(m_ref.shape, NEG, F32)
        acc_ref[...] = jnp.zeros(acc_ref.shape, F32)
        p_b[...] = jnp.zeros(p_b.shape, BF16)
        al_b[...] = jnp.ones(al_b.shape, F32)

        def step_operands(c):
            k0 = pl.multiple_of(c * SEL_KC, SEL_KC)
            brows = selm_ref[g, pl.ds(pl.multiple_of(c * bps, bps), bps), :]
            return jnp.concatenate([ks_ref[pl.ds(k0, SEL_KC), :], oneh_ref[...]], axis=1), q_aug(qpad, brows)

        def scores(ops, s_buf, cs):
            s_buf[:, cs] = _dot(ops[0], ops[1][:, cs]).astype(BF16)

        def pair(cc, carry):
            c0 = 2 * cc
            ops, vt = step_operands(c0 + 1), vsl_ref[jnp.maximum(c0 - 1, 0), g]
            for cs in heads:
                scores(ops, s_b, cs)
                softmax(s_a, p_a, al_a, cs)
                apply_values(p_b, al_b, vt, cs)
            ops, vt = step_operands(jnp.minimum(c0 + 2, n_kc - 1)), vsl_ref[c0, g]
            for cs in heads:
                scores(ops, s_a, cs)
                softmax(s_b, p_b, al_b, cs)
                apply_values(p_a, al_a, vt, cs)
            return carry

        ops = step_operands(0)
        for cs in heads:
            scores(ops, s_a, cs)
        lax.fori_loop(0, n_pairs, pair, 0)
        vt = vsl_ref[jnp.maximum(2 * n_pairs - 1, 0), g]
        for cs in heads:
            apply_values(p_b, al_b, vt, cs)

        for d in range(TQ // SEL_KC):
            k_aug = jnp.concatenate([ksd_ref[d * SEL_KC:(d + 1) * SEL_KC, :], oneh_ref[...]], axis=1)
            qa = q_aug(qpad, seld_ref[g, d * bps:(d + 1) * bps, :])
            vt = vsl_ref[i * (TQ // SEL_KC) + d, g]
            for cs in heads:
                sd = _dot(k_aug, qa[:, cs])
                s_a[:, cs] = jnp.where(diag_ok[d * SEL_KC:(d + 1) * SEL_KC, :], sd, NEG).astype(BF16)
                softmax(s_a, p_a, al_a, cs)
                apply_values(p_a, al_a, vt, cs)
        acc_s = acc_ref[...]
        o_s = acc_s[0:NSA_D, :] * (1.0 / acc_s[NSA_D:NSA_D + 1, :])

        for hp in range(HPG // 2):
            tiles = []
            for h in (2 * hp, 2 * hp + 1):
                gr = (g * HPG + h) * 3 + 1
                tiles.append(gt_ref[gr:gr + 1, :] * o_s[:, h * TQ:(h + 1) * TQ])
            cols = slice((g * HPG + 2 * hp) * NSA_D, (g * HPG + 2 * hp + 2) * NSA_D)
            o_ref[:, cols] = part_ref[:, cols] + jnp.concatenate(tiles, axis=0).T


def _nsa_selected(qt, ks, vsl, selm, sel, gt, part, B, S):
    HD = NSA_H * NSA_D
    TQ = SEL_TQ
    W = NSA_H // NSA_G * TQ
    n_sel = S // SLC_LEN
    assert S % (2 * SEL_KC) == 0 and TQ % SEL_KC == 0
    whole = lambda shape: pl.BlockSpec((None,) + shape, lambda b, i: (b,) + (0,) * len(shape),
                                       pipeline_mode=pl.Buffered(1))
    oneh = jnp.asarray(np.arange(SEL_KC)[:, None] // SLC_LEN == np.arange(LANE)[None, :], BF16)
    return pl.pallas_call(
        _nsa_selected_body,
        out_shape=jax.ShapeDtypeStruct((B * S, HD), F32),
        grid=(B, S // TQ),
        in_specs=[pl.BlockSpec((None, HD, TQ), lambda b, i: (b, 0, i)),
                  whole((S, LANE)), pl.BlockSpec((None, TQ, LANE), lambda b, i: (b, i, 0)),
                  whole(vsl.shape[1:]),
                  pl.BlockSpec((None, NSA_G, n_sel, TQ), lambda b, i: (b, 0, 0, i)),
                  pl.BlockSpec((None, NSA_G, TQ // SLC_LEN, TQ), lambda b, i: (b, 0, i, i)),
                  pl.BlockSpec((None, LANE, TQ), lambda b, i: (b, 0, i)),
                  pl.BlockSpec((TQ, HD), lambda b, i: (b * (S // TQ) + i, 0)),
                  _const_spec((SEL_KC, LANE))],
        out_specs=pl.BlockSpec((TQ, HD), lambda b, i: (b * (S // TQ) + i, 0)),
        scratch_shapes=[pltpu.VMEM((SEL_KC, W), BF16)] * 4 + [pltpu.VMEM((1, W), F32)] * 3
                       + [pltpu.VMEM((VT_ROWS, W), F32)],
        compiler_params=_params(("parallel", "arbitrary")),
        name="nsa_selected",
    )(qt, ks, ks, vsl, selm, sel, gt, part, oneh)


def _merge_body(x_ref, oa_ref, ob_ref, oc_ref, gates_ref, wbr_ref, wo_ref, o_ref):
    D = x_ref.shape[1]
    y = jnp.zeros(x_ref.shape, F32)
    for j, br in enumerate((oa_ref, ob_ref, oc_ref)):
        y = y + _sigmoid(gates_ref[:, j * D:(j + 1) * D]) * _dot(br[...].astype(BF16), wbr_ref[j])
    o_ref[...] = x_ref[...] + _dot(y.astype(BF16), wo_ref[...])


def _merge(x, o_a, o_b, o_c, gates, wbr, wo, tm=512):
    T, D = x.shape
    row = lambda w: pl.BlockSpec((tm, w), lambda i: (i, 0))
    return pl.pallas_call(
        _merge_body,
        out_shape=jax.ShapeDtypeStruct((T, D), F32),
        grid=(T // tm,),
        in_specs=[row(D), row(MIX_W), row(MIX_W), row(MIX_W), row(3 * D), _const_spec(wbr.shape), _const_spec(wo.shape)],
        out_specs=row(D),
        compiler_params=_params(("parallel",)),
        name="merge_out",
    )(x, o_a, o_b, o_c, gates, wbr, wo)


def _pack_in_proj(w_in, b_in):
    ws, bs = [], []
    for _, off, w, wp in _SEGS:
        ws.append(jnp.pad(w_in[..., off:off + w], ((0, 0), (0, 0), (0, wp - w))))
        bs.append(jnp.pad(b_in[..., off:off + w], ((0, 0), (0, wp - w))))
    return jnp.concatenate(ws, axis=-1).astype(BF16), jnp.concatenate(bs, axis=-1)[:, None, :]


def _rope_tables(pos):
    half = NSA_D // 2
    freqs = ROPE_THETA ** (-jnp.arange(half, dtype=F32) / half)
    ang = pos.astype(F32)[:, None] * freqs[None, :]
    cos, sin = jnp.cos(ang), jnp.sin(ang)
    reps = LANE // NSA_D
    return jnp.tile(jnp.concatenate([cos, cos], axis=1), (1, reps)), jnp.tile(jnp.concatenate([-sin, sin], axis=1), (1, reps))


def _cmp_weights(w1, w2):
    L = w1.shape[0]
    w1r = w1.reshape(L, 2, 2, CMP_STRIDE, NSA_D, CMP_HIDDEN)
    eye_g = jnp.eye(NSA_G, dtype=w1.dtype)
    ex = jnp.einsum('lwstdh,gk->lwstgdkh', w1r, eye_g)
    ex = ex.reshape(L, 2, 2, CMP_STRIDE * NSA_G * NSA_D, NSA_G * CMP_HIDDEN)
    w2x = jnp.einsum('lwhd,gk->lwghkd', w2, eye_g).reshape(L, 2, NSA_G * CMP_HIDDEN, NSA_G * NSA_D)
    return ex[:, :, 0].astype(BF16), ex[:, :, 1].astype(BF16), w2x.astype(BF16)


def _cmp_pos_rows(pe):
    L = pe.shape[0]
    r = pe.reshape(L, 2, 2, CMP_STRIDE, 1, NSA_D)
    return jnp.broadcast_to(r, (L, 2, 2, CMP_STRIDE, NSA_G, NSA_D)).reshape(L, 2, 2, CMP_STRIDE * NSA_G * NSA_D)


def _lane_gain(g):
    return jnp.tile(g, LANE // g.shape[-1])[None, :]


def _layer(x, lw, consts, B, S):
    cos, sin, cos_c, sin_c, gavg = consts
    T = B * S
    x = _ffn(x, lw["ffn1_norm"], lw["ffn1_wg"], lw["ffn1_wu"], lw["ffn1_wd"])
    z = dict(zip([s[0] for s in _SEGS], _proj(x, lw["mix_norm"], lw["w_in"], lw["b_in"])))
    o_a = _gla(z["a_q"], z["a_k"], z["a_v"], z["a_r"], z["a_lr"], lw["gla_wa"], lw["gla_ba"], lw["gla_gn"], B, S)
    o_c = _mlstm(z["c_qk"], z["c_v"], z["c_o"], z["c_if"], lw["conv_w"], lw["conv_b"], B, S)
    qn = lw["nsa_qk_norm"]
    qt, ks, kw, vsl, vwd, gt = _nsa_prep(z["b_q"], z["b_ks"], z["b_kw"], z["b_vs"], z["b_vw"], z["b_g"], cos, sin,
                                         _lane_gain(qn[0]), _lane_gain(qn[2]), _lane_gain(qn[3]), gavg, B, S)
    n_blk = S // CMP_STRIDE
    xk = z["b_kc"].reshape(B, n_blk, CMP_STRIDE * LANE)
    xv = z["b_vc"].reshape(B, n_blk, CMP_STRIDE * LANE)
    kc, vct = _nsa_cmp(xk, xv, lw["cmp_pe"], lw["cmp_wt"], lw["cmp_wb"], lw["cmp_w2"], cos_c, sin_c,
                       _lane_gain(qn[1]), gavg)
    part, sel, selm = _nsa_select(qt, kc, vct, kw, vwd, gt, B, S)
    o_b = _nsa_selected(qt, ks, vsl, selm, sel, gt, part, B, S)
    x = _merge(x, o_a, o_b, o_c, z["gates"], lw["w_branch"], lw["w_out"])
    return _ffn(x, lw["ffn2_norm"], lw["ffn2_wg"], lw["ffn2_wu"], lw["ffn2_wd"])


def kernel(x, ffn1_norm, ffn1_w_gate, ffn1_w_up, ffn1_w_down, mix_norm, w_in, b_in, gla_w_alpha, gla_b_alpha, gla_out_norm, nsa_qk_norm, nsa_cmp_pos, nsa_cmp_w1, nsa_cmp_w2, mlstm_conv_w, mlstm_conv_b, w_branch, w_out, ffn2_norm, ffn2_w_gate, ffn2_w_up, ffn2_w_down):
    B, S, D = x.shape
    w_in_p, b_in_p = _pack_in_proj(w_in, b_in)
    cmp_wt, cmp_wb, cmp_w2 = _cmp_weights(nsa_cmp_w1, nsa_cmp_w2)
    layers = {
        "ffn1_norm": ffn1_norm[:, None, :], "ffn1_wg": ffn1_w_gate.astype(BF16), "ffn1_wu": ffn1_w_up.astype(BF16),
        "ffn1_wd": ffn1_w_down.astype(BF16),
        "mix_norm": mix_norm[:, None, :], "w_in": w_in_p, "b_in": b_in_p,
        "gla_wa": jnp.pad(gla_w_alpha, ((0, 0), (0, LANE - GLA_RANK), (0, 0))), "gla_ba": gla_b_alpha[:, None, :],
        "gla_gn": gla_out_norm[:, None, :],
        "nsa_qk_norm": nsa_qk_norm, "cmp_pe": _cmp_pos_rows(nsa_cmp_pos), "cmp_wt": cmp_wt, "cmp_wb": cmp_wb,
        "cmp_w2": cmp_w2,
        "conv_w": mlstm_conv_w, "conv_b": mlstm_conv_b[:, None, :],
        "w_branch": w_branch.astype(BF16), "w_out": w_out.astype(BF16),
        "ffn2_norm": ffn2_norm[:, None, :], "ffn2_wg": ffn2_w_gate.astype(BF16), "ffn2_wu": ffn2_w_up.astype(BF16),
        "ffn2_wd": ffn2_w_down.astype(BF16),
    }
    n_blk = S // CMP_STRIDE
    cos, sin = _rope_tables(jnp.arange(S))
    cos_c, sin_c = _rope_tables(jnp.arange(n_blk) * CMP_STRIDE + CMP_LEN - 1)
    gavg = jnp.asarray(np.kron(np.eye(LANE // NSA_D), np.full((NSA_D, NSA_D), 1.0 / NSA_D)), BF16)
    consts = (cos, sin, cos_c, sin_c, gavg)

    def step(xc, lw):
        return _layer(xc, lw, consts, B, S), None

    out, _ = lax.scan(step, x.reshape(B * S, D), layers)
    return out.reshape(B, S, D)
```

```python
import functools

import numpy as np
import jax
import jax.numpy as jnp
from jax import lax
from jax.experimental import pallas as pl
from jax.experimental.pallas import tpu as pltpu

F32 = jnp.float32
BF16 = jnp.bfloat16

RMS_EPS = 1e-6
ROPE_THETA = 10000.0

GLA_H, GLA_DK, GLA_DV, GLA_RANK, GLA_GATE_NORM = 4, 64, 128, 16, 16.0
GLA_SUB = 16
NSA_H, NSA_G, NSA_D = 8, 2, 64
CMP_LEN, CMP_STRIDE, CMP_HIDDEN = 32, 16, 256
SLC_LEN, SLC_TOPK, WIN = 64, 16, 512
FORCED_SCORE = 1e4
MLSTM_H, MLSTM_DK, MLSTM_DV, CONV_W = 4, 64, 128, 4
MLSTM_CHUNK = 64
MIX_W = 512
LANE = 128
MXU_N = 256
NEG = -1e30
LOG2E = 1.4426950408889634
SEL_KC = 512
SEL_TQ = 512
PS_PAD = 8
VT_ROWS = 80

_SEGS = (
    ("a_q", 0, 256, 256), ("a_k", 256, 256, 256), ("a_v", 512, 512, 512), ("a_r", 1024, 512, 512),
    ("b_q", 1552, 512, 512), ("c_qk", 2856, 512, 512), ("c_v", 3368, 512, 512), ("c_o", 3880, 512, 512),
    ("gates", 4400, 3072, 3072),
    ("a_lr", 1536, 16, 128), ("b_kc", 2064, 128, 128), ("b_vc", 2192, 128, 128), ("b_ks", 2320, 128, 128),
    ("b_vs", 2448, 128, 128), ("b_kw", 2576, 128, 128), ("b_vw", 2704, 128, 128), ("b_g", 2832, 24, 128),
    ("c_if", 4392, 8, 128),
)
_N_PACK = sum(s[3] for s in _SEGS)
_BF16_SEGS = ("gates", "c_v", "b_vs", "b_vw")


def _dot(a, b):
    return jnp.dot(a, b, preferred_element_type=F32)


def _dot_nt(a, b):
    return lax.dot_general(a, b, (((1,), (1,)), ((), ())), preferred_element_type=F32)


def _split2(a):
    hi = a.astype(BF16)
    lo = (a - hi.astype(F32)).astype(BF16)
    return hi, lo


def _dot_l2(a, b):
    hi, lo = _split2(a)
    return _dot(hi, b) + _dot(lo, b)


def _dot_r2(a, b):
    hi, lo = _split2(b)
    return _dot(a, hi) + _dot(a, lo)


def _log_sigmoid(x):
    return jnp.minimum(x, 0.0) - jnp.log(1.0 + jnp.exp(-jnp.abs(x)))


def _sigmoid(x):
    return 1.0 / (1.0 + jnp.exp(-x))


def _silu(x):
    return x * _sigmoid(x)


def _iota(shape, dim):
    return lax.broadcasted_iota(jnp.int32, shape, dim)


def _const_spec(shape):
    nd = len(shape)
    return pl.BlockSpec(shape, lambda *_: (0,) * nd, pipeline_mode=pl.Buffered(1))


def _params(sem, vmem_mb=56):
    return pltpu.CompilerParams(dimension_semantics=sem, vmem_limit_bytes=vmem_mb * 1024 * 1024)


def _ffn_body(x_ref, g_ref, wg_ref, wu_ref, wd_ref, o_ref, *, bounds):
    x = x_ref[...]
    ms = jnp.mean(x * x, axis=-1, keepdims=True)
    h = (x * lax.rsqrt(ms + RMS_EPS) * g_ref[...]).astype(BF16)
    acc = jnp.zeros(x.shape, F32)
    for lo, hi in zip(bounds[:-1], bounds[1:]):
        a = _dot(h, wg_ref[:, lo:hi])
        u = _dot(h, wu_ref[:, lo:hi])
        t = (_silu(a) * u).astype(BF16)
        acc = acc + _dot(t, wd_ref[lo:hi, :])
    o_ref[...] = x + 0.5 * acc


def _ffn(x, g, wg, wu, wd, tm=512):
    T, D = x.shape
    F = wg.shape[1]
    bounds = (0, -(-F // (2 * MXU_N)) * MXU_N, F)
    return pl.pallas_call(
        functools.partial(_ffn_body, bounds=bounds),
        out_shape=jax.ShapeDtypeStruct((T, D), F32),
        grid=(T // tm,),
        in_specs=[pl.BlockSpec((tm, D), lambda i: (i, 0)), _const_spec((1, D)),
                  _const_spec((D, F)), _const_spec((D, F)), _const_spec((F, D))],
        out_specs=pl.BlockSpec((tm, D), lambda i: (i, 0)),
        compiler_params=_params(("parallel",)),
        name="ffn",
    )(x, g, wg, wu, wd)


def _proj_body(x_ref, g_ref, w_ref, b_ref, *o_refs):
    x = x_ref[...]
    ms = jnp.mean(x * x, axis=-1, keepdims=True)
    h = (x * lax.rsqrt(ms + RMS_EPS) * g_ref[...]).astype(BF16)
    off, i = 0, 0
    while i < len(o_refs):
        group, w = [], 0
        while i < len(o_refs) and (not group or w % MXU_N):
            group.append(o_refs[i])
            w += o_refs[i].shape[1]
            i += 1
        z = _dot(h, w_ref[:, off:off + w]) + b_ref[:, off:off + w]
        c = 0
        for o_ref in group:
            o_ref[...] = z[:, c:c + o_ref.shape[1]].astype(o_ref.dtype)
            c += o_ref.shape[1]
        off += w


def _proj(x, g, w, b, tm=256):
    T, D = x.shape
    return pl.pallas_call(
        _proj_body,
        out_shape=[jax.ShapeDtypeStruct((T, s[3]), BF16 if s[0] in _BF16_SEGS else F32) for s in _SEGS],
        grid=(T // tm,),
        in_specs=[pl.BlockSpec((tm, D), lambda i: (i, 0)), _const_spec((1, D)),
                  _const_spec((D, _N_PACK)), _const_spec((1, _N_PACK))],
        out_specs=[pl.BlockSpec((tm, s[3]), lambda i: (i, 0)) for s in _SEGS],
        compiler_params=_params(("parallel",)),
        name="in_proj",
    )(x, g, w, b)


def _gla_body(q_ref, k_ref, v_ref, r_ref, lr_ref, wa_ref, ba_ref, gn_ref, tri_ref, bones_ref, eh_ref,
              o_ref, st_ref, qs_ref, c_ref, tot_ref, kst_ref, vt4_ref, oi_ref):
    nB, Lc = q_ref.shape[0], q_ref.shape[1]
    n_sub = Lc // GLA_SUB
    HK = GLA_H * GLA_DK

    @pl.when(pl.program_id(0) == 0)
    def _():
        st_ref[...] = jnp.zeros(st_ref.shape, F32)

    lane_h = _iota((1, HK), 1) >> (GLA_DK.bit_length() - 1)
    row_i = _iota((GLA_SUB, 1), 0)
    col_r = _iota((1, GLA_H * Lc), 1) & (Lc - 1)
    wa_hi, wa_lo = _split2(wa_ref[...])

    for b in range(nB):
        lr_hi, lr_lo = _split2(lr_ref[b])
        u = _dot(lr_hi, wa_hi) + _dot(lr_hi, wa_lo) + _dot(lr_lo, wa_hi) + ba_ref[...]
        g = _log_sigmoid(u) * (1.0 / GLA_GATE_NORM)
        c = _dot_r2(tri_ref[...], g)
        tot = _dot_r2(bones_ref[...], g)
        qs_ref[b] = q_ref[b] * (GLA_DK ** -0.5)
        c_ref[b] = c
        tot_ref[b] = tot
        kt = k_ref[b] * jnp.exp(tot - c)
        vt = v_ref[b].T.astype(BF16)
        for h in range(GLA_H):
            kst_ref[b, h * Lc:(h + 1) * Lc, :] = jnp.where(lane_h == h, kt, 0.0).astype(BF16)
            vt4_ref[b, :, h * Lc:(h + 1) * Lc] = vt[h * GLA_DV:(h + 1) * GLA_DV, :]

    def sub(b, s):
        rows = slice(s * GLA_SUB, (s + 1) * GLA_SUB)
        qs = qs_ref[b, rows, :]
        cs = c_ref[b, rows, :]
        st = st_ref[b]
        qd = qs * jnp.exp(cs)
        q4 = jnp.concatenate([jnp.where(lane_h == h, qd, 0.0) for h in range(GLA_H)], axis=0).astype(BF16)
        inter4 = _dot_nt(q4, st.astype(BF16))
        inter = jnp.concatenate([inter4[h * GLA_SUB:(h + 1) * GLA_SUB, :] for h in range(GLA_H)], axis=1)
        xs = []
        for j in range(GLA_SUB):
            r = s * GLA_SUB + j
            x = qs * k_ref[b, r:r + 1, :] * jnp.exp(jnp.minimum(cs - c_ref[b, r:r + 1, :], 0.0))
            xs.append(jnp.where(row_i >= j, x, 0.0))
        r_all = _dot(jnp.concatenate(xs, axis=0).astype(BF16), eh_ref[...])
        intra = jnp.zeros((GLA_SUB, r_all.shape[1]), F32)
        for j in range(GLA_SUB):
            r = s * GLA_SUB + j
            intra = intra + r_all[j * GLA_SUB:(j + 1) * GLA_SUB, :] * v_ref[b, r:r + 1, :]
        oi_ref[b, rows, :] = inter + intra
        dec = jnp.exp(tot_ref[b, s * GLA_SUB:s * GLA_SUB + 1, :])
        in_sub = (col_r >= s * GLA_SUB) & (col_r < (s + 1) * GLA_SUB)
        vtm = jnp.where(in_sub, vt4_ref[b], jnp.zeros((), BF16))
        st_ref[b] = dec * st + _dot(vtm, kst_ref[b])

    for s in range(n_sub):
        for b in range(nB):
            sub(b, s)

    gn = gn_ref[...]
    for b in range(nB):
        for h in range(GLA_H):
            sl = slice(h * GLA_DV, (h + 1) * GLA_DV)
            o = oi_ref[b, :, sl]
            ms = jnp.mean(o * o, axis=-1, keepdims=True)
            o_ref[b, :, sl] = o * lax.rsqrt(ms + RMS_EPS) * gn * _silu(r_ref[b, :, sl])


def _gla_consts(Lc):
    r = np.arange(Lc)
    same = (r[:, None] // GLA_SUB) == (r[None, :] // GLA_SUB)
    tri = (same & (r[None, :] <= r[:, None])).astype(np.float32)
    bones = same.astype(np.float32)
    hk = np.arange(GLA_H * GLA_DK) // GLA_DK
    hv = np.arange(GLA_H * GLA_DV) // GLA_DV
    eh = (hk[:, None] == hv[None, :]).astype(np.float32)
    return jnp.asarray(tri, BF16), jnp.asarray(bones, BF16), jnp.asarray(eh, BF16)


def _gla(a_q, a_k, a_v, a_r, a_lr, wa, ba, gn, B, S, Lc=128):
    HK, HV = GLA_H * GLA_DK, GLA_H * GLA_DV
    nb = S // Lc
    assert Lc & (Lc - 1) == 0
    tri, bones, eh = _gla_consts(Lc)
    row = lambda w: pl.BlockSpec((B, Lc, w), lambda i: (0, i, 0))
    seq = lambda t: t.reshape(B, S, t.shape[-1])
    out = pl.pallas_call(
        _gla_body,
        out_shape=jax.ShapeDtypeStruct((B, S, HV), F32),
        grid=(nb,),
        in_specs=[row(HK), row(HK), row(HV), row(HV), row(LANE),
                  _const_spec((LANE, HK)), _const_spec((1, HK)), _const_spec((1, GLA_DV)),
                  _const_spec((Lc, Lc)), _const_spec((Lc, Lc)), _const_spec((HK, HV))],
        out_specs=row(HV),
        scratch_shapes=[pltpu.VMEM((B, GLA_DV, HK), F32), pltpu.VMEM((B, Lc, HK), F32), pltpu.VMEM((B, Lc, HK), F32),
                        pltpu.VMEM((B, Lc, HK), F32), pltpu.VMEM((B, GLA_H * Lc, HK), BF16),
                        pltpu.VMEM((B, GLA_DV, GLA_H * Lc), BF16), pltpu.VMEM((B, Lc, HV), F32)],
        compiler_params=_params(("arbitrary",)),
        name="gla",
    )(seq(a_q), seq(a_k), seq(a_v), seq(a_r), seq(a_lr), wa, ba, gn, tri, bones, eh)
    return out.reshape(B * S, HV)


def _mlstm_body(qk_ref, v_ref, og_ref, if_ref, cw_ref, cb_ref, tri_ref, bdm_ref, e4_ref,
                o_ref, xx_ref, ct_ref, n_ref, m_ref, kw_ref, vt_ref, act_ref):
    Lc = v_ref.shape[0]
    L = MLSTM_CHUNK
    HK = MLSTM_H * MLSTM_DK
    tail = 8

    @pl.when(pl.program_id(1) == 0)
    def _():
        ct_ref[...] = jnp.zeros(ct_ref.shape, F32)
        n_ref[...] = jnp.zeros(n_ref.shape, F32)
        m_ref[...] = jnp.zeros(m_ref.shape, F32)
        xx_ref[0:tail, :] = jnp.zeros((tail, xx_ref.shape[1]), F32)

    xx_ref[tail:tail + Lc, :] = qk_ref[...]
    y = jnp.zeros((Lc, 2 * HK), F32) + cb_ref[...]
    for kk in range(CONV_W):
        y = y + cw_ref[kk:kk + 1, :] * xx_ref[pl.ds(tail - (CONV_W - 1) + kk, Lc), :]
    xx_ref[0:tail, :] = xx_ref[Lc:Lc + tail, :]
    act_ref[...] = _silu(y)

    gates = if_ref[...]
    logf = _log_sigmoid(gates)
    bcum = _dot_r2(tri_ref[...], logf)
    kw_ref[...] = jnp.zeros(kw_ref.shape, BF16)
    vt_ref[...] = v_ref[...].astype(F32).T.astype(BF16)
    eye = (_iota((L, L), 0) == _iota((L, L), 1)).astype(F32)
    causal = _iota((L, L), 0) >= _iota((L, L), 1)
    lane_hk = _iota((1, HK), 1) >> (MLSTM_DK.bit_length() - 1)
    col_l = _iota((1, Lc), 1)

    for ci in range(Lc // L):
        rows = slice(ci * L, (ci + 1) * L)
        q_all = act_ref[rows, 0:HK]
        k_all = act_ref[rows, HK:2 * HK] * (MLSTM_DK ** -0.5)
        ct = ct_ref[...]
        n_row = n_ref[...]
        inter_all = _dot_nt(q_all.astype(BF16), ct.astype(BF16))
        qn_all = _dot_l2(q_all * n_row, e4_ref[...])
        k_bf = k_all.astype(BF16)
        wk_full = jnp.zeros((L, HK), F32)
        dec_row = jnp.zeros((1, HK), F32)
        for h in range(MLSTM_H):
            icol = gates[rows, h:h + 1]
            bcol = bcum[rows, MLSTM_H + h:MLSTM_H + h + 1]
            m_st = m_ref[0:1, h:h + 1]
            r_row = jnp.sum((icol - bcol) * eye, axis=0, keepdims=True)
            dmat = jnp.where(causal, bcol + r_row, -jnp.inf)
            inter_log = bcol + m_st
            m_row = jnp.maximum(inter_log, jnp.max(dmat, axis=-1, keepdims=True))
            w_inter = jnp.exp(inter_log - m_row)
            head = lane_hk == h
            qh = jnp.where(head, q_all, 0.0).astype(BF16)
            s_qk = _dot_nt(qh, k_bf) * jnp.exp(dmat - m_row)
            vh = v_ref[rows, h * MLSTM_DV:(h + 1) * MLSTM_DV]
            num = w_inter * inter_all[:, h * MLSTM_DV:(h + 1) * MLSTM_DV] + _dot(s_qk.astype(BF16), vh.astype(BF16))
            den = w_inter * qn_all[:, h:h + 1] + jnp.sum(s_qk, axis=-1, keepdims=True)
            hh = num / jnp.maximum(jnp.abs(den), jnp.exp(-m_row))
            o_ref[rows, h * MLSTM_DV:(h + 1) * MLSTM_DV] = hh * _sigmoid(og_ref[rows, h * MLSTM_DV:(h + 1) * MLSTM_DV])
            m_new = m_row[L - 1:L, :]
            b_last = bcol[L - 1:L, :]
            w_k = jnp.exp(b_last - bcol + icol - m_new)
            decay = jnp.exp(b_last + m_st - m_new)
            wk_full = wk_full + jnp.where(head, w_k, 0.0)
            dec_row = dec_row + jnp.where(head, decay, 0.0)
            m_ref[0:1, h:h + 1] = m_new
        kw = k_all * wk_full
        kw_ref[rows, :] = kw.astype(BF16)
        in_chunk = (col_l >= ci * L) & (col_l < (ci + 1) * L)
        vtm = jnp.where(in_chunk, vt_ref[...], jnp.zeros((), BF16))
        ct_ref[...] = dec_row * ct + _dot(vtm, kw_ref[...]) * bdm_ref[...]
        n_ref[...] = dec_row * n_row + jnp.sum(kw, axis=0, keepdims=True)


def _mlstm_consts(Lc):
    r = np.arange(Lc)
    same = (r[:, None] // MLSTM_CHUNK) == (r[None, :] // MLSTM_CHUNK)
    tri = (same & (r[None, :] <= r[:, None])).astype(np.float32)
    hk = np.arange(MLSTM_H * MLSTM_DK) // MLSTM_DK
    hv = np.arange(MLSTM_H * MLSTM_DV) // MLSTM_DV
    eh = (hk[:, None] == hv[None, :]).astype(np.float32)
    e4 = (hk[:, None] == np.arange(LANE)[None, :]).astype(np.float32)
    return jnp.asarray(tri, BF16), jnp.asarray(eh.T.copy(), F32), jnp.asarray(e4, BF16)


def _mlstm(c_qk, c_v, c_o, c_if, cw, cb, B, S, Lc=256):
    HK, HV = MLSTM_H * MLSTM_DK, MLSTM_H * MLSTM_DV
    nb = S // Lc
    tri, bdm, e4 = _mlstm_consts(Lc)
    row = lambda w: pl.BlockSpec((Lc, w), lambda b, i: (b * nb + i, 0))
    return pl.pallas_call(
        _mlstm_body,
        out_shape=jax.ShapeDtypeStruct((B * S, HV), F32),
        grid=(B, nb),
        in_specs=[row(2 * HK), row(HV), row(HV), row(LANE),
                  _const_spec((CONV_W, 2 * HK)), _const_spec((1, 2 * HK)),
                  _const_spec((Lc, Lc)), _const_spec((HV, HK)), _const_spec((HK, LANE))],
        out_specs=row(HV),
        scratch_shapes=[pltpu.VMEM((Lc + 8, 2 * HK), F32), pltpu.VMEM((HV, HK), F32), pltpu.VMEM((1, HK), F32),
                        pltpu.VMEM((8, LANE), F32), pltpu.VMEM((Lc, HK), BF16), pltpu.VMEM((HV, Lc), BF16),
                        pltpu.VMEM((Lc, 2 * HK), F32)],
        compiler_params=_params(("parallel", "arbitrary")),
        name="mlstm",
    )(c_qk, c_v, c_o, c_if, cw, cb, tri, bdm, e4)


def _group_rms(x, gavg, gain):
    ms = _dot_l2(x * x, gavg)
    return x * lax.rsqrt(ms + RMS_EPS) * gain


def _rope_lanes(x, cos, sin_signed):
    half = NSA_D // 2
    first = (_iota((1, LANE), 1) & (NSA_D - 1)) < half
    swapped = jnp.where(first, pltpu.roll(x, LANE - half, 1), pltpu.roll(x, half, 1))
    return x * cos + swapped * sin_signed


def _store_vt_tiles(dst_ref, vt, width):
    ones = jnp.ones((VT_ROWS - NSA_D, width), BF16)
    for j in range(vt.shape[1] // width):
        for g in range(NSA_G):
            dst_ref[j, g, 0:NSA_D, :] = vt[g * NSA_D:(g + 1) * NSA_D, j * width:(j + 1) * width]
            dst_ref[j, g, NSA_D:VT_ROWS, :] = ones


def _nsa_prep_body(q_ref, ks_ref, kw_ref, vs_ref, vw_ref, g_ref, cos_ref, sin_ref, gq_ref, gs_ref, gw_ref, gavg_ref,
                   qt_ref, kso_ref, kwo_ref, vsl_ref, vwd_ref, gt_ref):
    cos, sin = cos_ref[...], sin_ref[...]
    gavg = gavg_ref[...]
    for cb in range(NSA_H * NSA_D // LANE):
        sl = slice(cb * LANE, (cb + 1) * LANE)
        qn = _rope_lanes(_group_rms(q_ref[:, sl], gavg, gq_ref[...]), cos, sin) * (NSA_D ** -0.5 * LOG2E)
        qt_ref[sl, :] = qn.T.astype(BF16)
    kso_ref[...] = _rope_lanes(_group_rms(ks_ref[...], gavg, gs_ref[...]), cos, sin).astype(BF16)
    kwo_ref[...] = _rope_lanes(_group_rms(kw_ref[...], gavg, gw_ref[...]), cos, sin).astype(BF16)
    vst = vs_ref[...].astype(F32).T.astype(BF16)
    _store_vt_tiles(vsl_ref, vst, SEL_KC)
    _store_vt_tiles(vwd_ref, vw_ref[...].astype(F32).T.astype(BF16), LANE)
    gt_ref[...] = _sigmoid(g_ref[...]).T


def _nsa_prep(b_q, b_ks, b_kw, b_vs, b_vw, b_g, cos, sin, gq, gs, gw, gavg, B, S, tm=SEL_KC):
    nb = S // tm
    row = lambda w: pl.BlockSpec((tm, w), lambda b, i: (b * nb + i, 0))
    tab = pl.BlockSpec((tm, LANE), lambda b, i: (i, 0))
    HD = NSA_H * NSA_D
    vt_shape = lambda width: jax.ShapeDtypeStruct((B, S // width, NSA_G, VT_ROWS, width), BF16)
    vt_spec = lambda width: pl.BlockSpec((None, tm // width, NSA_G, VT_ROWS, width), lambda b, i: (b, i, 0, 0, 0))
    return pl.pallas_call(
        _nsa_prep_body,
        out_shape=[jax.ShapeDtypeStruct((B, HD, S), BF16),
                   jax.ShapeDtypeStruct((B, S, LANE), BF16),
                   jax.ShapeDtypeStruct((B, S, LANE), BF16),
                   vt_shape(SEL_KC),
                   vt_shape(LANE),
                   jax.ShapeDtypeStruct((B, LANE, S), F32)],
        grid=(B, nb),
        in_specs=[row(HD), row(LANE), row(LANE), row(LANE), row(LANE), row(LANE), tab, tab,
                  _const_spec((1, LANE)), _const_spec((1, LANE)), _const_spec((1, LANE)), _const_spec((LANE, LANE))],
        out_specs=[pl.BlockSpec((None, HD, tm), lambda b, i: (b, 0, i)),
                   pl.BlockSpec((None, tm, LANE), lambda b, i: (b, i, 0)),
                   pl.BlockSpec((None, tm, LANE), lambda b, i: (b, i, 0)),
                   vt_spec(SEL_KC), vt_spec(LANE),
                   pl.BlockSpec((None, LANE, tm), lambda b, i: (b, 0, i))],
        compiler_params=_params(("parallel", "parallel")),
        name="nsa_prep",
    )(b_q, b_ks, b_kw, b_vs, b_vw, b_g, cos, sin, gq, gs, gw, gavg)


def _gelu_tanh(x):
    return 0.5 * x * (1.0 + jnp.tanh(0.7978845608028654 * (x + 0.044715 * x * x * x)))


def _nsa_cmp_body(xk_ref, xv_ref, pe_ref, wt_ref, wb_ref, w2_ref, cos_ref, sin_ref, gk_ref, gavg_ref,
                  kc_ref, vct_ref):
    n = xk_ref.shape[0]

    def compress(x, which):
        u = _dot((x + pe_ref[which, 0:1, :]).astype(BF16), wt_ref[which])
        v = _dot((x + pe_ref[which, 1:2, :]).astype(BF16), wb_ref[which])
        hid = u + pltpu.roll(v, n - 1, 0)
        return _dot(_gelu_tanh(hid).astype(BF16), w2_ref[which])

    ck = compress(xk_ref[...], 0)
    kc_ref[...] = _rope_lanes(_group_rms(ck, gavg_ref[...], gk_ref[...]), cos_ref[...], sin_ref[...]).astype(BF16)
    vct = compress(xv_ref[...], 1).T.astype(BF16)
    for g in range(NSA_G):
        vct_ref[g] = vct[g * NSA_D:(g + 1) * NSA_D, :]


def _nsa_cmp(xk, xv, pe, wt, wb, w2, cos, sin, gk, gavg):
    B, n, W = xk.shape
    return pl.pallas_call(
        _nsa_cmp_body,
        out_shape=[jax.ShapeDtypeStruct((B, n, LANE), BF16),
                   jax.ShapeDtypeStruct((B, NSA_G, NSA_D, n), BF16)],
        grid=(B,),
        in_specs=[pl.BlockSpec((None, n, W), lambda b: (b, 0, 0)), pl.BlockSpec((None, n, W), lambda b: (b, 0, 0)),
                  _const_spec(pe.shape), _const_spec(wt.shape), _const_spec(wb.shape), _const_spec(w2.shape),
                  _const_spec((n, LANE)), _const_spec((n, LANE)), _const_spec((1, LANE)), _const_spec((LANE, LANE))],
        out_specs=[pl.BlockSpec((None, n, LANE), lambda b: (b, 0, 0)),
                   pl.BlockSpec((None, NSA_G, NSA_D, n), lambda b: (b, 0, 0, 0))],
        compiler_params=_params(("parallel",)),
        name="nsa_compress",
    )(xk, xv, pe, wt, wb, w2, cos, sin, gk, gavg)


def _group_queries(qt_ref, g):
    HPG = NSA_H // NSA_G
    q4 = jnp.concatenate([qt_ref[(g * HPG + h) * NSA_D:(g * HPG + h + 1) * NSA_D, :] for h in range(HPG)], axis=1)
    parts = [jnp.zeros(q4.shape, BF16)] * NSA_G
    parts[g] = q4
    return jnp.concatenate(parts, axis=0)


def _nsa_select_body(qt_ref, kc_ref, vct_ref, kw_ref, vwd_ref, gt_ref, part_ref, sel_ref, selm_ref, ps_buf,
                     *, sel_tile):
    TQ = LANE
    HPG = NSA_H // NSA_G
    W = HPG * TQ
    n_cmp = kc_ref.shape[0]
    n_sel = sel_ref.shape[1]
    i = pl.program_id(1)
    s0 = i * TQ
    t_row = s0 + _iota((1, TQ), 1)
    n_win = WIN // LANE + 1
    j0 = jnp.maximum(i - (n_win - 1), 0)
    w_start = pl.multiple_of(j0 * LANE, LANE)
    own_first = (s0 // sel_tile) * (sel_tile // SLC_LEN)
    ps_buf[:, 0:PS_PAD, :] = jnp.zeros((NSA_G, PS_PAD, TQ), F32)

    for g in range(NSA_G):
        qpad = _group_queries(qt_ref, g)

        cend = _iota((n_cmp, 1), 0) * CMP_STRIDE + (CMP_LEN - 1)
        bias_c = jnp.where(cend <= t_row, 0.0, NEG)
        sc = _dot(kc_ref[...], qpad) + jnp.concatenate([bias_c] * HPG, axis=1)
        m = jnp.max(sc, axis=0, keepdims=True)
        m = jnp.where(m > 0.5 * NEG, m, 0.0)
        p = jnp.exp2(sc - m)
        p = p * (1.0 / jnp.maximum(jnp.sum(p, axis=0, keepdims=True), 1e-30))
        o_c = _dot(vct_ref[g], p.astype(BF16))
        psum = p[:, 0:TQ]
        for h in range(1, HPG):
            psum = psum + p[:, h * TQ:(h + 1) * TQ]
        ps_buf[g, PS_PAD:PS_PAD + n_cmp, :] = psum

        per = SLC_LEN // CMP_STRIDE
        imp = ps_buf[g, pl.ds(PS_PAD - 1, n_sel, stride=per), :]
        for jj in range(per):
            imp = imp + ps_buf[g, pl.ds(PS_PAD + jj, n_sel, stride=per), :]
        blk = _iota((n_sel, 1), 0)
        cur = t_row >> (SLC_LEN.bit_length() - 1)
        valid = blk * SLC_LEN <= t_row
        forced = (blk == 0) | (blk == cur) | (blk == cur - 1)
        score0 = jnp.where(valid, jnp.where(forced, FORCED_SCORE, imp), NEG)
        score = score0
        for _ in range(min(SLC_TOPK, n_sel)):
            best = jnp.max(score, axis=0, keepdims=True)
            first = jnp.min(jnp.where(score == best, blk, n_sel), axis=0, keepdims=True)
            score = jnp.where(blk == first, NEG, score)
        chosen = score < score0
        sel_ref[g] = jnp.where(chosen, 0.0, NEG)
        selm_ref[g] = jnp.where(chosen & (blk < own_first), 0.0, NEG)

        kpos = w_start + _iota((n_win * LANE, 1), 0)
        bias_w = jnp.where((kpos <= t_row) & (kpos > t_row - WIN), 0.0, NEG)
        sw = _dot(kw_ref[pl.ds(w_start, n_win * LANE), :], qpad) + jnp.concatenate([bias_w] * HPG, axis=1)
        pw = jnp.exp2(sw - jnp.max(sw, axis=0, keepdims=True)).astype(BF16)
        acc_w = jnp.zeros((VT_ROWS, W), F32)
        for r in range(n_win):
            acc_w = acc_w + _dot(vwd_ref[j0 + r, g], pw[r * LANE:(r + 1) * LANE, :])
        o_w = acc_w[0:NSA_D, :] * (1.0 / acc_w[NSA_D:NSA_D + 1, :])

        for hp in range(HPG // 2):
            tiles = []
            for h in (2 * hp, 2 * hp + 1):
                gr = (g * HPG + h) * 3
                cs = slice(h * TQ, (h + 1) * TQ)
                tiles.append(gt_ref[gr:gr + 1, :] * o_c[:, cs] + gt_ref[gr + 2:gr + 3, :] * o_w[:, cs])
            col = (g * HPG + 2 * hp) * NSA_D
            part_ref[:, col:col + 2 * NSA_D] = jnp.concatenate(tiles, axis=0).T


def _nsa_select(qt, kc, vct, kw, vwd, gt, B, S):
    HD = NSA_H * NSA_D
    TQ = LANE
    n_cmp = kc.shape[1]
    n_sel = S // SLC_LEN
    assert S >= WIN + TQ and CMP_LEN == 2 * CMP_STRIDE
    whole = lambda shape: pl.BlockSpec((None,) + shape, lambda b, i: (b,) + (0,) * len(shape),
                                       pipeline_mode=pl.Buffered(1))
    mask_shape = jax.ShapeDtypeStruct((B, NSA_G, n_sel, S), F32)
    mask_spec = pl.BlockSpec((None, NSA_G, n_sel, TQ), lambda b, i: (b, 0, 0, i))
    return pl.pallas_call(
        functools.partial(_nsa_select_body, sel_tile=SEL_TQ),
        out_shape=[jax.ShapeDtypeStruct((B * S, HD), F32), mask_shape, mask_shape],
        grid=(B, S // TQ),
        in_specs=[pl.BlockSpec((None, HD, TQ), lambda b, i: (b, 0, i)),
                  whole((n_cmp, LANE)), whole(vct.shape[1:]), whole((S, LANE)), whole(vwd.shape[1:]),
                  pl.BlockSpec((None, LANE, TQ), lambda b, i: (b, 0, i))],
        out_specs=[pl.BlockSpec((TQ, HD), lambda b, i: (b * (S // TQ) + i, 0)), mask_spec, mask_spec],
        scratch_shapes=[pltpu.VMEM((NSA_G, n_cmp + PS_PAD, TQ), F32)],
        compiler_params=_params(("parallel", "parallel")),
        name="nsa_select",
    )(qt, kc, vct, kw, vwd, gt)


def _nsa_selected_body(qt_ref, ks_ref, ksd_ref, vsl_ref, selm_ref, seld_ref, gt_ref, part_ref, oneh_ref, o_ref,
                       s_a, s_b, p_a, p_b, al_a, al_b, m_ref, acc_ref):
    TQ = qt_ref.shape[1]
    HPG = NSA_H // NSA_G
    W = HPG * TQ
    n_kc = ks_ref.shape[0] // SEL_KC
    bps = SEL_KC // SLC_LEN
    i = pl.program_id(1)
    n_main = i * (TQ // SEL_KC)
    n_pairs = (n_main + 1) // 2
    diag_ok = _iota((TQ, 1), 0) <= _iota((1, TQ), 1)
    pad_rows = jnp.zeros((LANE - 2 * bps, W), BF16)

    def q_aug(qpad, brows):
        brows = jnp.concatenate([jnp.concatenate([brows] * HPG, axis=1), jnp.zeros((bps, W), F32)], axis=0)
        return jnp.concatenate([qpad, brows.astype(BF16), pad_rows], axis=0)

    heads = [slice(h * TQ, (h + 1) * TQ) for h in range(HPG)]

    def softmax(s_buf, p_buf, al_ref, cs):
        sb = s_buf[:, cs]
        m_i = m_ref[:, cs]
        m_new = jnp.maximum(m_i, jnp.max(sb, axis=0, keepdims=True).astype(F32))
        p_buf[:, cs] = jnp.exp2(sb - m_new.astype(BF16))
        al_ref[:, cs] = jnp.exp2(m_i - m_new)
        m_ref[:, cs] = m_new

    def apply_values(p_buf, al_ref, vt, cs):
        acc_ref[:, cs] = al_ref[:, cs] * acc_ref[:, cs] + _dot(vt, p_buf[:, cs])

    for g in range(NSA_G):
        qpad = _group_queries(qt_ref, g)
        m_ref[...] = jnp.full(m_ref.shape, NEG, F32)
        acc_ref[...] = jnp.zeros(acc_ref.shape, F32)
        p_b[...] = jnp.zeros(p_b.shape, BF16)
        al_b[...] = jnp.ones(al_b.shape, F32)

        def step_operands(c):
            k0 = pl.multiple_of(c * SEL_KC, SEL_KC)
            brows = selm_ref[g, pl.ds(pl.multiple_of(c * bps, bps), bps), :]
            return jnp.concatenate([ks_ref[pl.ds(k0, SEL_KC), :], oneh_ref[...]], axis=1), q_aug(qpad, brows)

        def scores(ops, s_buf, cs):
            s_buf[:, cs] = _dot(ops[0], ops[1][:, cs]).astype(BF16)

        def pair(cc, carry):
            c0 = 2 * cc
            ops, vt = step_operands(c0 + 1), vsl_ref[jnp.maximum(c0 - 1, 0), g]
            for cs in heads:
                scores(ops, s_b, cs)
                softmax(s_a, p_a, al_a, cs)
                apply_values(p_b, al_b, vt, cs)
            ops, vt = step_operands(jnp.minimum(c0 + 2, n_kc - 1)), vsl_ref[c0, g]
            for cs in heads:
                scores(ops, s_a, cs)
                softmax(s_b, p_b, al_b, cs)
                apply_values(p_a, al_a, vt, cs)
            return carry

        ops = step_operands(0)
        for cs in heads:
            scores(ops, s_a, cs)
        lax.fori_loop(0, n_pairs, pair, 0)
        vt = vsl_ref[jnp.maximum(2 * n_pairs - 1, 0), g]
        for cs in heads:
            apply_values(p_b, al_b, vt, cs)

        for d in range(TQ // SEL_KC):
            k_aug = jnp.concatenate([ksd_ref[d * SEL_KC:(d + 1) * SEL_KC, :], oneh_ref[...]], axis=1)
            qa = q_aug(qpad, seld_ref[g, d * bps:(d + 1) * bps, :])
            vt = vsl_ref[i * (TQ // SEL_KC) + d, g]
            for cs in heads:
                sd = _dot(k_aug, qa[:, cs])
                s_a[:, cs] = jnp.where(diag_ok[d * SEL_KC:(d + 1) * SEL_KC, :], sd, NEG).astype(BF16)
                softmax(s_a, p_a, al_a, cs)
                apply_values(p_a, al_a, vt, cs)
        acc_s = acc_ref[...]
        o_s = acc_s[0:NSA_D, :] * (1.0 / acc_s[NSA_D:NSA_D + 1, :])

        for hp in range(HPG // 2):
            tiles = []
            for h in (2 * hp, 2 * hp + 1):
                gr = (g * HPG + h) * 3 + 1
                tiles.append(gt_ref[gr:gr + 1, :] * o_s[:, h * TQ:(h + 1) * TQ])
            cols = slice((g * HPG + 2 * hp) * NSA_D, (g * HPG + 2 * hp + 2) * NSA_D)
            o_ref[:, cols] = part_ref[:, cols] + jnp.concatenate(tiles, axis=0).T


def _nsa_selected(qt, ks, vsl, selm, sel, gt, part, B, S):
    HD = NSA_H * NSA_D
    TQ = SEL_TQ
    W = NSA_H // NSA_G * TQ
    n_sel = S // SLC_LEN
    assert S % (2 * SEL_KC) == 0 and TQ % SEL_KC == 0
    whole = lambda shape: pl.BlockSpec((None,) + shape, lambda b, i: (b,) + (0,) * len(shape),
                                       pipeline_mode=pl.Buffered(1))
    oneh = jnp.asarray(np.arange(SEL_KC)[:, None] // SLC_LEN == np.arange(LANE)[None, :], BF16)
    return pl.pallas_call(
        _nsa_selected_body,
        out_shape=jax.ShapeDtypeStruct((B * S, HD), F32),
        grid=(B, S // TQ),
        in_specs=[pl.BlockSpec((None, HD, TQ), lambda b, i: (b, 0, i)),
                  whole((S, LANE)), pl.BlockSpec((None, TQ, LANE), lambda b, i: (b, i, 0)),
                  whole(vsl.shape[1:]),
                  pl.BlockSpec((None, NSA_G, n_sel, TQ), lambda b, i: (b, 0, 0, i)),
                  pl.BlockSpec((None, NSA_G, TQ // SLC_LEN, TQ), lambda b, i: (b, 0, i, i)),
                  pl.BlockSpec((None, LANE, TQ), lambda b, i: (b, 0, i)),
                  pl.BlockSpec((TQ, HD), lambda b, i: (b * (S // TQ) + i, 0)),
                  _const_spec((SEL_KC, LANE))],
        out_specs=pl.BlockSpec((TQ, HD), lambda b, i: (b * (S // TQ) + i, 0)),
        scratch_shapes=[pltpu.VMEM((SEL_KC, W), BF16)] * 4 + [pltpu.VMEM((1, W), F32)] * 3
                       + [pltpu.VMEM((VT_ROWS, W), F32)],
        compiler_params=_params(("parallel", "arbitrary")),
        name="nsa_selected",
    )(qt, ks, ks, vsl, selm, sel, gt, part, oneh)


def _merge_body(x_ref, oa_ref, ob_ref, oc_ref, gates_ref, wbr_ref, wo_ref, o_ref):
    D = x_ref.shape[1]
    y = jnp.zeros(x_ref.shape, F32)
    for j, br in enumerate((oa_ref, ob_ref, oc_ref)):
        y = y + _sigmoid(gates_ref[:, j * D:(j + 1) * D].astype(F32)) * _dot(br[...].astype(BF16), wbr_ref[j])
    o_ref[...] = x_ref[...] + _dot(y.astype(BF16), wo_ref[...])


def _merge(x, o_a, o_b, o_c, gates, wbr, wo, tm=512):
    T, D = x.shape
    row = lambda w: pl.BlockSpec((tm, w), lambda i: (i, 0))
    return pl.pallas_call(
        _merge_body,
        out_shape=jax.ShapeDtypeStruct((T, D), F32),
        grid=(T // tm,),
        in_specs=[row(D), row(MIX_W), row(MIX_W), row(MIX_W), row(3 * D), _const_spec(wbr.shape), _const_spec(wo.shape)],
        out_specs=row(D),
        compiler_params=_params(("parallel",)),
        name="merge_out",
    )(x, o_a, o_b, o_c, gates, wbr, wo)


def _pack_in_proj(w_in, b_in):
    ws, bs = [], []
    for _, off, w, wp in _SEGS:
        ws.append(jnp.pad(w_in[..., off:off + w], ((0, 0), (0, 0), (0, wp - w))))
        bs.append(jnp.pad(b_in[..., off:off + w], ((0, 0), (0, wp - w))))
    return jnp.concatenate(ws, axis=-1).astype(BF16), jnp.concatenate(bs, axis=-1)[:, None, :]


def _rope_tables(pos):
    half = NSA_D // 2
    freqs = ROPE_THETA ** (-jnp.arange(half, dtype=F32) / half)
    ang = pos.astype(F32)[:, None] * freqs[None, :]
    cos, sin = jnp.cos(ang), jnp.sin(ang)
    reps = LANE // NSA_D
    return jnp.tile(jnp.concatenate([cos, cos], axis=1), (1, reps)), jnp.tile(jnp.concatenate([-sin, sin], axis=1), (1, reps))


def _cmp_weights(w1, w2):
    L = w1.shape[0]
    w1r = w1.reshape(L, 2, 2, CMP_STRIDE, NSA_D, CMP_HIDDEN)
    eye_g = jnp.eye(NSA_G, dtype=w1.dtype)
    ex = jnp.einsum('lwstdh,gk->lwstgdkh', w1r, eye_g)
    ex = ex.reshape(L, 2, 2, CMP_STRIDE * NSA_G * NSA_D, NSA_G * CMP_HIDDEN)
    w2x = jnp.einsum('lwhd,gk->lwghkd', w2, eye_g).reshape(L, 2, NSA_G * CMP_HIDDEN, NSA_G * NSA_D)
    return ex[:, :, 0].astype(BF16), ex[:, :, 1].astype(BF16), w2x.astype(BF16)


def _cmp_pos_rows(pe):
    L = pe.shape[0]
    r = pe.reshape(L, 2, 2, CMP_STRIDE, 1, NSA_D)
    return jnp.broadcast_to(r, (L, 2, 2, CMP_STRIDE, NSA_G, NSA_D)).reshape(L, 2, 2, CMP_STRIDE * NSA_G * NSA_D)


def _lane_gain(g):
    return jnp.tile(g, LANE // g.shape[-1])[None, :]


def _layer(x, lw, consts, B, S):
    cos, sin, cos_c, sin_c, gavg = consts
    T = B * S
    x = _ffn(x, lw["ffn1_norm"], lw["ffn1_wg"], lw["ffn1_wu"], lw["ffn1_wd"])
    z = dict(zip([s[0] for s in _SEGS], _proj(x, lw["mix_norm"], lw["w_in"], lw["b_in"])))
    o_a = _gla(z["a_q"], z["a_k"], z["a_v"], z["a_r"], z["a_lr"], lw["gla_wa"], lw["gla_ba"], lw["gla_gn"], B, S)
    o_c = _mlstm(z["c_qk"], z["c_v"], z["c_o"], z["c_if"], lw["conv_w"], lw["conv_b"], B, S)
    qn = lw["nsa_qk_norm"]
    qt, ks, kw, vsl, vwd, gt = _nsa_prep(z["b_q"], z["b_ks"], z["b_kw"], z["b_vs"], z["b_vw"], z["b_g"], cos, sin,
                                         _lane_gain(qn[0]), _lane_gain(qn[2]), _lane_gain(qn[3]), gavg, B, S)
    n_blk = S // CMP_STRIDE
    xk = z["b_kc"].reshape(B, n_blk, CMP_STRIDE * LANE)
    xv = z["b_vc"].reshape(B, n_blk, CMP_STRIDE * LANE)
    kc, vct = _nsa_cmp(xk, xv, lw["cmp_pe"], lw["cmp_wt"], lw["cmp_wb"], lw["cmp_w2"], cos_c, sin_c,
                       _lane_gain(qn[1]), gavg)
    part, sel, selm = _nsa_select(qt, kc, vct, kw, vwd, gt, B, S)
    o_b = _nsa_selected(qt, ks, vsl, selm, sel, gt, part, B, S)
    x = _merge(x, o_a, o_b, o_c, z["gates"], lw["w_branch"], lw["w_out"])
    return _ffn(x, lw["ffn2_norm"], lw["ffn2_wg"], lw["ffn2_wu"], lw["ffn2_wd"])


def kernel(x, ffn1_norm, ffn1_w_gate, ffn1_w_up, ffn1_w_down, mix_norm, w_in, b_in, gla_w_alpha, gla_b_alpha, gla_out_norm, nsa_qk_norm, nsa_cmp_pos, nsa_cmp_w1, nsa_cmp_w2, mlstm_conv_w, mlstm_conv_b, w_branch, w_out, ffn2_norm, ffn2_w_gate, ffn2_w_up, ffn2_w_down):
    B, S, D = x.shape
    w_in_p, b_in_p = _pack_in_proj(w_in, b_in)
    cmp_wt, cmp_wb, cmp_w2 = _cmp_weights(nsa_cmp_w1, nsa_cmp_w2)
    layers = {
        "ffn1_norm": ffn1_norm[:, None, :], "ffn1_wg": ffn1_w_gate.astype(BF16), "ffn1_wu": ffn1_w_up.astype(BF16),
        "ffn1_wd": ffn1_w_down.astype(BF16),
        "mix_norm": mix_norm[:, None, :], "w_in": w_in_p, "b_in": b_in_p,
        "gla_wa": jnp.pad(gla_w_alpha, ((0, 0), (0, LANE - GLA_RANK), (0, 0))), "gla_ba": gla_b_alpha[:, None, :],
        "gla_gn": gla_out_norm[:, None, :],
        "nsa_qk_norm": nsa_qk_norm, "cmp_pe": _cmp_pos_rows(nsa_cmp_pos), "cmp_wt": cmp_wt, "cmp_wb": cmp_wb,
        "cmp_w2": cmp_w2,
        "conv_w": mlstm_conv_w, "conv_b": mlstm_conv_b[:, None, :],
        "w_branch": w_branch.astype(BF16), "w_out": w_out.astype(BF16),
        "ffn2_norm": ffn2_norm[:, None, :], "ffn2_wg": ffn2_w_gate.astype(BF16), "ffn2_wu": ffn2_w_up.astype(BF16),
        "ffn2_wd": ffn2_w_down.astype(BF16),
    }
    n_blk = S // CMP_STRIDE
    cos, sin = _rope_tables(jnp.arange(S))
    cos_c, sin_c = _rope_tables(jnp.arange(n_blk) * CMP_STRIDE + CMP_LEN - 1)
    gavg = jnp.asarray(np.kron(np.eye(LANE // NSA_D), np.full((NSA_D, NSA_D), 1.0 / NSA_D)), BF16)
    consts = (cos, sin, cos_c, sin_c, gavg)

    def step(xc, lw):
        return _layer(xc, lw, consts, B, S), None

    out, _ = lax.scan(step, x.reshape(B * S, D), layers)
    return out.reshape(B, S, D)
```

```python
import functools

import numpy as np
import jax
import jax.numpy as jnp
from jax import lax
from jax.experimental import pallas as pl
from jax.experimental.pallas import tpu as pltpu

F32 = jnp.float32
BF16 = jnp.bfloat16

RMS_EPS = 1e-6
ROPE_THETA = 10000.0

GLA_H, GLA_DK, GLA_DV, GLA_RANK, GLA_GATE_NORM = 4, 64, 128, 16, 16.0
GLA_SUB = 16
NSA_H, NSA_G, NSA_D = 8, 2, 64
CMP_LEN, CMP_STRIDE, CMP_HIDDEN = 32, 16, 256
SLC_LEN, SLC_TOPK, WIN = 64, 16, 512
FORCED_SCORE = 1e4
MLSTM_H, MLSTM_DK, MLSTM_DV, CONV_W = 4, 64, 128, 4
MLSTM_CHUNK = 64
MIX_W = 512
LANE = 128
MXU_N = 256
NEG = -1e30
LOG2E = 1.4426950408889634
SEL_KC = 512
SEL_TQ = 512
PS_PAD = 8
VT_ROWS = 80

_SEGS = (
    ("a_q", 0, 256, 256), ("a_k", 256, 256, 256), ("a_v", 512, 512, 512), ("a_r", 1024, 512, 512),
    ("b_q", 1552, 512, 512), ("c_qk", 2856, 512, 512), ("c_v", 3368, 512, 512), ("c_o", 3880, 512, 512),
    ("gates", 4400, 3072, 3072),
    ("a_lr", 1536, 16, 128), ("b_kc", 2064, 128, 128), ("b_vc", 2192, 128, 128), ("b_ks", 2320, 128, 128),
    ("b_vs", 2448, 128, 128), ("b_kw", 2576, 128, 128), ("b_vw", 2704, 128, 128), ("b_g", 2832, 24, 128),
    ("c_if", 4392, 8, 128),
)
_N_PACK = sum(s[3] for s in _SEGS)
_BF16_SEGS = ("gates", "c_v", "b_vs", "b_vw")


def _dot(a, b):
    return jnp.dot(a, b, preferred_element_type=F32)


def _dot_nt(a, b):
    return lax.dot_general(a, b, (((1,), (1,)), ((), ())), preferred_element_type=F32)


def _split2(a):
    hi = a.astype(BF16)
    lo = (a - hi.astype(F32)).astype(BF16)
    return hi, lo


def _dot_l2(a, b):
    hi, lo = _split2(a)
    return _dot(hi, b) + _dot(lo, b)


def _dot_r2(a, b):
    hi, lo = _split2(b)
    return _dot(a, hi) + _dot(a, lo)


def _log_sigmoid(x):
    return jnp.minimum(x, 0.0) - jnp.log(1.0 + jnp.exp(-jnp.abs(x)))


def _sigmoid(x):
    return 1.0 / (1.0 + jnp.exp(-x))


def _silu(x):
    return x * _sigmoid(x)


def _iota(shape, dim):
    return lax.broadcasted_iota(jnp.int32, shape, dim)


def _const_spec(shape):
    nd = len(shape)
    return pl.BlockSpec(shape, lambda *_: (0,) * nd, pipeline_mode=pl.Buffered(1))


def _params(sem, vmem_mb=56):
    return pltpu.CompilerParams(dimension_semantics=sem, vmem_limit_bytes=vmem_mb * 1024 * 1024)


def _ffn_body(x_ref, g_ref, wg_ref, wu_ref, wd_ref, o_ref, *, bounds):
    x = x_ref[...]
    ms = jnp.mean(x * x, axis=-1, keepdims=True)
    h = (x * lax.rsqrt(ms + RMS_EPS) * g_ref[...]).astype(BF16)
    acc = jnp.zeros(x.shape, F32)
    for lo, hi in zip(bounds[:-1], bounds[1:]):
        a = _dot(h, wg_ref[:, lo:hi])
        u = _dot(h, wu_ref[:, lo:hi])
        t = (_silu(a) * u).astype(BF16)
        acc = acc + _dot(t, wd_ref[lo:hi, :])
    o_ref[...] = x + 0.5 * acc


def _ffn(x, g, wg, wu, wd, tm=512):
    T, D = x.shape
    F = wg.shape[1]
    bounds = (0, -(-F // (2 * MXU_N)) * MXU_N, F)
    return pl.pallas_call(
        functools.partial(_ffn_body, bounds=bounds),
        out_shape=jax.ShapeDtypeStruct((T, D), F32),
        grid=(T // tm,),
        in_specs=[pl.BlockSpec((tm, D), lambda i: (i, 0)), _const_spec((1, D)),
                  _const_spec((D, F)), _const_spec((D, F)), _const_spec((F, D))],
        out_specs=pl.BlockSpec((tm, D), lambda i: (i, 0)),
        compiler_params=_params(("parallel",)),
        name="ffn",
    )(x, g, wg, wu, wd)


def _proj_body(x_ref, g_ref, w_ref, b_ref, *o_refs):
    x = x_ref[...]
    ms = jnp.mean(x * x, axis=-1, keepdims=True)
    h = (x * lax.rsqrt(ms + RMS_EPS) * g_ref[...]).astype(BF16)
    off, i = 0, 0
    while i < len(o_refs):
        group, w = [], 0
        while i < len(o_refs) and (not group or w % MXU_N):
            group.append(o_refs[i])
            w += o_refs[i].shape[1]
            i += 1
        z = _dot(h, w_ref[:, off:off + w]) + b_ref[:, off:off + w]
        c = 0
        for o_ref in group:
            o_ref[...] = z[:, c:c + o_ref.shape[1]].astype(o_ref.dtype)
            c += o_ref.shape[1]
        off += w


def _proj(x, g, w, b, tm=256):
    T, D = x.shape
    return pl.pallas_call(
        _proj_body,
        out_shape=[jax.ShapeDtypeStruct((T, s[3]), BF16 if s[0] in _BF16_SEGS else F32) for s in _SEGS],
        grid=(T // tm,),
        in_specs=[pl.BlockSpec((tm, D), lambda i: (i, 0)), _const_spec((1, D)),
                  _const_spec((D, _N_PACK)), _const_spec((1, _N_PACK))],
        out_specs=[pl.BlockSpec((tm, s[3]), lambda i: (i, 0)) for s in _SEGS],
        compiler_params=_params(("parallel",)),
        name="in_proj",
    )(x, g, w, b)


def _gla_body(q_ref, k_ref, v_ref, r_ref, lr_ref, wa_ref, ba_ref, gn_ref, tri_ref, bones_ref, eh_ref,
              o_ref, st_ref, qs_ref, c_ref, tot_ref, kst_ref, vt4_ref, oi_ref):
    nB, Lc = q_ref.shape[0], q_ref.shape[1]
    n_sub = Lc // GLA_SUB
    HK = GLA_H * GLA_DK

    @pl.when(pl.program_id(0) == 0)
    def _():
        st_ref[...] = jnp.zeros(st_ref.shape, F32)

    lane_h = _iota((1, HK), 1) >> (GLA_DK.bit_length() - 1)
    row_i = _iota((GLA_SUB, 1), 0)
    col_r = _iota((1, GLA_H * Lc), 1) & (Lc - 1)
    wa_hi, wa_lo = _split2(wa_ref[...])

    for b in range(nB):
        lr_hi, lr_lo = _split2(lr_ref[b])
        u = _dot(lr_hi, wa_hi) + _dot(lr_hi, wa_lo) + _dot(lr_lo, wa_hi) + ba_ref[...]
        g = _log_sigmoid(u) * (1.0 / GLA_GATE_NORM)
        c = _dot_r2(tri_ref[...], g)
        tot = _dot_r2(bones_ref[...], g)
        qs_ref[b] = q_ref[b] * (GLA_DK ** -0.5)
        c_ref[b] = c
        tot_ref[b] = tot
        kt = k_ref[b] * jnp.exp(tot - c)
        vt = v_ref[b].T.astype(BF16)
        for h in range(GLA_H):
            kst_ref[b, h * Lc:(h + 1) * Lc, :] = jnp.where(lane_h == h, kt, 0.0).astype(BF16)
            vt4_ref[b, :, h * Lc:(h + 1) * Lc] = vt[h * GLA_DV:(h + 1) * GLA_DV, :]

    def sub(b, s):
        rows = slice(s * GLA_SUB, (s + 1) * GLA_SUB)
        qs = qs_ref[b, rows, :]
        cs = c_ref[b, rows, :]
        st = st_ref[b]
        qd = qs * jnp.exp(cs)
        q4 = jnp.concatenate([jnp.where(lane_h == h, qd, 0.0) for h in range(GLA_H)], axis=0).astype(BF16)
        inter4 = _dot_nt(q4, st.astype(BF16))
        inter = jnp.concatenate([inter4[h * GLA_SUB:(h + 1) * GLA_SUB, :] for h in range(GLA_H)], axis=1)
        xs = []
        for j in range(GLA_SUB):
            r = s * GLA_SUB + j
            x = qs * k_ref[b, r:r + 1, :] * jnp.exp(jnp.minimum(cs - c_ref[b, r:r + 1, :], 0.0))
            xs.append(jnp.where(row_i >= j, x, 0.0))
        r_all = _dot(jnp.concatenate(xs, axis=0).astype(BF16), eh_ref[...])
        intra = jnp.zeros((GLA_SUB, r_all.shape[1]), F32)
        for j in range(GLA_SUB):
            r = s * GLA_SUB + j
            intra = intra + r_all[j * GLA_SUB:(j + 1) * GLA_SUB, :] * v_ref[b, r:r + 1, :]
        oi_ref[b, rows, :] = inter + intra
        dec = jnp.exp(tot_ref[b, s * GLA_SUB:s * GLA_SUB + 1, :])
        in_sub = (col_r >= s * GLA_SUB) & (col_r < (s + 1) * GLA_SUB)
        vtm = jnp.where(in_sub, vt4_ref[b], jnp.zeros((), BF16))
        st_ref[b] = dec * st + _dot(vtm, kst_ref[b])

    for s in range(n_sub):
        for b in range(nB):
            sub(b, s)

    gn = gn_ref[...]
    for b in range(nB):
        for h in range(GLA_H):
            sl = slice(h * GLA_DV, (h + 1) * GLA_DV)
            o = oi_ref[b, :, sl]
            ms = jnp.mean(o * o, axis=-1, keepdims=True)
            o_ref[b, :, sl] = o * lax.rsqrt(ms + RMS_EPS) * gn * _silu(r_ref[b, :, sl])


def _gla_consts(Lc):
    r = np.arange(Lc)
    same = (r[:, None] // GLA_SUB) == (r[None, :] // GLA_SUB)
    tri = (same & (r[None, :] <= r[:, None])).astype(np.float32)
    bones = same.astype(np.float32)
    hk = np.arange(GLA_H * GLA_DK) // GLA_DK
    hv = np.arange(GLA_H * GLA_DV) // GLA_DV
    eh = (hk[:, None] == hv[None, :]).astype(np.float32)
    return jnp.asarray(tri, BF16), jnp.asarray(bones, BF16), jnp.asarray(eh, BF16)


def _gla(a_q, a_k, a_v, a_r, a_lr, wa, ba, gn, B, S, Lc=128):
    HK, HV = GLA_H * GLA_DK, GLA_H * GLA_DV
    nb = S // Lc
    assert Lc & (Lc - 1) == 0
    tri, bones, eh = _gla_consts(Lc)
    row = lambda w: pl.BlockSpec((B, Lc, w), lambda i: (0, i, 0))
    seq = lambda t: t.reshape(B, S, t.shape[-1])
    out = pl.pallas_call(
        _gla_body,
        out_shape=jax.ShapeDtypeStruct((B, S, HV), F32),
        grid=(nb,),
        in_specs=[row(HK), row(HK), row(HV), row(HV), row(LANE),
                  _const_spec((LANE, HK)), _const_spec((1, HK)), _const_spec((1, GLA_DV)),
                  _const_spec((Lc, Lc)), _const_spec((Lc, Lc)), _const_spec((HK, HV))],
        out_specs=row(HV),
        scratch_shapes=[pltpu.VMEM((B, GLA_DV, HK), F32), pltpu.VMEM((B, Lc, HK), F32), pltpu.VMEM((B, Lc, HK), F32),
                        pltpu.VMEM((B, Lc, HK), F32), pltpu.VMEM((B, GLA_H * Lc, HK), BF16),
                        pltpu.VMEM((B, GLA_DV, GLA_H * Lc), BF16), pltpu.VMEM((B, Lc, HV), F32)],
        compiler_params=_params(("arbitrary",)),
        name="gla",
    )(seq(a_q), seq(a_k), seq(a_v), seq(a_r), seq(a_lr), wa, ba, gn, tri, bones, eh)
    return out.reshape(B * S, HV)


def _mlstm_body(qk_ref, v_ref, og_ref, if_ref, cw_ref, cb_ref, tri_ref, bdm_ref, e4_ref,
                o_ref, xx_ref, ct_ref, n_ref, m_ref, kw_ref, vt_ref, act_ref):
    Lc = v_ref.shape[0]
    L = MLSTM_CHUNK
    HK = MLSTM_H * MLSTM_DK
    tail = 8

    @pl.when(pl.program_id(1) == 0)
    def _():
        ct_ref[...] = jnp.zeros(ct_ref.shape, F32)
        n_ref[...] = jnp.zeros(n_ref.shape, F32)
        m_ref[...] = jnp.zeros(m_ref.shape, F32)
        xx_ref[0:tail, :] = jnp.zeros((tail, xx_ref.shape[1]), F32)

    xx_ref[tail:tail + Lc, :] = qk_ref[...]
    y = jnp.zeros((Lc, 2 * HK), F32) + cb_ref[...]
    for kk in range(CONV_W):
        y = y + cw_ref[kk:kk + 1, :] * xx_ref[pl.ds(tail - (CONV_W - 1) + kk, Lc), :]
    xx_ref[0:tail, :] = xx_ref[Lc:Lc + tail, :]
    act_ref[...] = _silu(y)

    gates = if_ref[...]
    logf = _log_sigmoid(gates)
    bcum = _dot_r2(tri_ref[...], logf)
    kw_ref[...] = jnp.zeros(kw_ref.shape, BF16)
    vt_ref[...] = v_ref[...].astype(F32).T.astype(BF16)
    eye = (_iota((L, L), 0) == _iota((L, L), 1)).astype(F32)
    causal = _iota((L, L), 0) >= _iota((L, L), 1)
    lane_hk = _iota((1, HK), 1) >> (MLSTM_DK.bit_length() - 1)
    col_l = _iota((1, Lc), 1)

    for ci in range(Lc // L):
        rows = slice(ci * L, (ci + 1) * L)
        q_all = act_ref[rows, 0:HK]
        k_all = act_ref[rows, HK:2 * HK] * (MLSTM_DK ** -0.5)
        ct = ct_ref[...]
        n_row = n_ref[...]
        inter_all = _dot_nt(q_all.astype(BF16), ct.astype(BF16))
        qn_all = _dot_l2(q_all * n_row, e4_ref[...])
        k_bf = k_all.astype(BF16)
        wk_full = jnp.zeros((L, HK), F32)
        dec_row = jnp.zeros((1, HK), F32)
        for h in range(MLSTM_H):
            icol = gates[rows, h:h + 1]
            bcol = bcum[rows, MLSTM_H + h:MLSTM_H + h + 1]
            m_st = m_ref[0:1, h:h + 1]
            r_row = jnp.sum((icol - bcol) * eye, axis=0, keepdims=True)
            dmat = jnp.where(causal, bcol + r_row, -jnp.inf)
            inter_log = bcol + m_st
            m_row = jnp.maximum(inter_log, jnp.max(dmat, axis=-1, keepdims=True))
            w_inter = jnp.exp(inter_log - m_row)
            head = lane_hk == h
            qh = jnp.where(head, q_all, 0.0).astype(BF16)
            s_qk = _dot_nt(qh, k_bf) * jnp.exp(dmat - m_row)
            vh = v_ref[rows, h * MLSTM_DV:(h + 1) * MLSTM_DV]
            num = w_inter * inter_all[:, h * MLSTM_DV:(h + 1) * MLSTM_DV] + _dot(s_qk.astype(BF16), vh.astype(BF16))
            den = w_inter * qn_all[:, h:h + 1] + jnp.sum(s_qk, axis=-1, keepdims=True)
            hh = num / jnp.maximum(jnp.abs(den), jnp.exp(-m_row))
            o_ref[rows, h * MLSTM_DV:(h + 1) * MLSTM_DV] = hh * _sigmoid(og_ref[rows, h * MLSTM_DV:(h + 1) * MLSTM_DV])
            m_new = m_row[L - 1:L, :]
            b_last = bcol[L - 1:L, :]
            w_k = jnp.exp(b_last - bcol + icol - m_new)
            decay = jnp.exp(b_last + m_st - m_new)
            wk_full = wk_full + jnp.where(head, w_k, 0.0)
            dec_row = dec_row + jnp.where(head, decay, 0.0)
            m_ref[0:1, h:h + 1] = m_new
        kw = k_all * wk_full
        kw_ref[rows, :] = kw.astype(BF16)
        in_chunk = (col_l >= ci * L) & (col_l < (ci + 1) * L)
        vtm = jnp.where(in_chunk, vt_ref[...], jnp.zeros((), BF16))
        ct_ref[...] = dec_row * ct + _dot(vtm, kw_ref[...]) * bdm_ref[...]
        n_ref[...] = dec_row * n_row + jnp.sum(kw, axis=0, keepdims=True)


def _mlstm_consts(Lc):
    r = np.arange(Lc)
    same = (r[:, None] // MLSTM_CHUNK) == (r[None, :] // MLSTM_CHUNK)
    tri = (same & (r[None, :] <= r[:, None])).astype(np.float32)
    hk = np.arange(MLSTM_H * MLSTM_DK) // MLSTM_DK
    hv = np.arange(MLSTM_H * MLSTM_DV) // MLSTM_DV
    eh = (hk[:, None] == hv[None, :]).astype(np.float32)
    e4 = (hk[:, None] == np.arange(LANE)[None, :]).astype(np.float32)
    return jnp.asarray(tri, BF16), jnp.asarray(eh.T.copy(), F32), jnp.asarray(e4, BF16)


def _mlstm(c_qk, c_v, c_o, c_if, cw, cb, B, S, Lc=256):
    HK, HV = MLSTM_H * MLSTM_DK, MLSTM_H * MLSTM_DV
    nb = S // Lc
    tri, bdm, e4 = _mlstm_consts(Lc)
    row = lambda w: pl.BlockSpec((Lc, w), lambda b, i: (b * nb + i, 0))
    return pl.pallas_call(
        _mlstm_body,
        out_shape=jax.ShapeDtypeStruct((B * S, HV), F32),
        grid=(B, nb),
        in_specs=[row(2 * HK), row(HV), row(HV), row(LANE),
                  _const_spec((CONV_W, 2 * HK)), _const_spec((1, 2 * HK)),
                  _const_spec((Lc, Lc)), _const_spec((HV, HK)), _const_spec((HK, LANE))],
        out_specs=row(HV),
        scratch_shapes=[pltpu.VMEM((Lc + 8, 2 * HK), F32), pltpu.VMEM((HV, HK), F32), pltpu.VMEM((1, HK), F32),
                        pltpu.VMEM((8, LANE), F32), pltpu.VMEM((Lc, HK), BF16), pltpu.VMEM((HV, Lc), BF16),
                        pltpu.VMEM((Lc, 2 * HK), F32)],
        compiler_params=_params(("parallel", "arbitrary")),
        name="mlstm",
    )(c_qk, c_v, c_o, c_if, cw, cb, tri, bdm, e4)


def _group_rms(x, gavg, gain):
    ms = _dot_l2(x * x, gavg)
    return x * lax.rsqrt(ms + RMS_EPS) * gain


def _rope_lanes(x, cos, sin_signed):
    half = NSA_D // 2
    first = (_iota((1, LANE), 1) & (NSA_D - 1)) < half
    swapped = jnp.where(first, pltpu.roll(x, LANE - half, 1), pltpu.roll(x, half, 1))
    return x * cos + swapped * sin_signed


def _store_vt_tiles(dst_ref, vt, width):
    ones = jnp.ones((VT_ROWS - NSA_D, width), BF16)
    for j in range(vt.shape[1] // width):
        for g in range(NSA_G):
            dst_ref[j, g, 0:NSA_D, :] = vt[g * NSA_D:(g + 1) * NSA_D, j * width:(j + 1) * width]
            dst_ref[j, g, NSA_D:VT_ROWS, :] = ones


def _nsa_prep_body(q_ref, ks_ref, kw_ref, vs_ref, vw_ref, g_ref, cos_ref, sin_ref, gq_ref, gs_ref, gw_ref, gavg_ref,
                   qt_ref, kso_ref, kwo_ref, vsl_ref, vwd_ref, gt_ref):
    cos, sin = cos_ref[...], sin_ref[...]
    gavg = gavg_ref[...]
    for cb in range(NSA_H * NSA_D // LANE):
        sl = slice(cb * LANE, (cb + 1) * LANE)
        qn = _rope_lanes(_group_rms(q_ref[:, sl], gavg, gq_ref[...]), cos, sin) * (NSA_D ** -0.5 * LOG2E)
        qt_ref[sl, :] = qn.T.astype(BF16)
    kso_ref[...] = _rope_lanes(_group_rms(ks_ref[...], gavg, gs_ref[...]), cos, sin).astype(BF16)
    kwo_ref[...] = _rope_lanes(_group_rms(kw_ref[...], gavg, gw_ref[...]), cos, sin).astype(BF16)
    vst = vs_ref[...].astype(F32).T.astype(BF16)
    _store_vt_tiles(vsl_ref, vst, SEL_KC)
    _store_vt_tiles(vwd_ref, vw_ref[...].astype(F32).T.astype(BF16), LANE)
    gt_ref[...] = _sigmoid(g_ref[...]).T


def _nsa_prep(b_q, b_ks, b_kw, b_vs, b_vw, b_g, cos, sin, gq, gs, gw, gavg, B, S, tm=SEL_KC):
    nb = S // tm
    row = lambda w: pl.BlockSpec((tm, w), lambda b, i: (b * nb + i, 0))
    tab = pl.BlockSpec((tm, LANE), lambda b, i: (i, 0))
    HD = NSA_H * NSA_D
    vt_shape = lambda width: jax.ShapeDtypeStruct((B, S // width, NSA_G, VT_ROWS, width), BF16)
    vt_spec = lambda width: pl.BlockSpec((None, tm // width, NSA_G, VT_ROWS, width), lambda b, i: (b, i, 0, 0, 0))
    return pl.pallas_call(
        _nsa_prep_body,
        out_shape=[jax.ShapeDtypeStruct((B, HD, S), BF16),
                   jax.ShapeDtypeStruct((B, S, LANE), BF16),
                   jax.ShapeDtypeStruct((B, S, LANE), BF16),
                   vt_shape(SEL_KC),
                   vt_shape(LANE),
                   jax.ShapeDtypeStruct((B, LANE, S), F32)],
        grid=(B, nb),
        in_specs=[row(HD), row(LANE), row(LANE), row(LANE), row(LANE), row(LANE), tab, tab,
                  _const_spec((1, LANE)), _const_spec((1, LANE)), _const_spec((1, LANE)), _const_spec((LANE, LANE))],
        out_specs=[pl.BlockSpec((None, HD, tm), lambda b, i: (b, 0, i)),
                   pl.BlockSpec((None, tm, LANE), lambda b, i: (b, i, 0)),
                   pl.BlockSpec((None, tm, LANE), lambda b, i: (b, i, 0)),
                   vt_spec(SEL_KC), vt_spec(LANE),
                   pl.BlockSpec((None, LANE, tm), lambda b, i: (b, 0, i))],
        compiler_params=_params(("parallel", "parallel")),
        name="nsa_prep",
    )(b_q, b_ks, b_kw, b_vs, b_vw, b_g, cos, sin, gq, gs, gw, gavg)


def _gelu_tanh(x):
    return 0.5 * x * (1.0 + jnp.tanh(0.7978845608028654 * (x + 0.044715 * x * x * x)))


def _nsa_cmp_body(xk_ref, xv_ref, pe_ref, wt_ref, wb_ref, w2_ref, cos_ref, sin_ref, gk_ref, gavg_ref,
                  kc_ref, vct_ref):
    n = xk_ref.shape[0]

    def compress(x, which):
        u = _dot((x + pe_ref[which, 0:1, :]).astype(BF16), wt_ref[which])
        v = _dot((x + pe_ref[which, 1:2, :]).astype(BF16), wb_ref[which])
        hid = u + pltpu.roll(v, n - 1, 0)
        return _dot(_gelu_tanh(hid).astype(BF16), w2_ref[which])

    ck = compress(xk_ref[...], 0)
    kc_ref[...] = _rope_lanes(_group_rms(ck, gavg_ref[...], gk_ref[...]), cos_ref[...], sin_ref[...]).astype(BF16)
    vct = compress(xv_ref[...], 1).T.astype(BF16)
    for g in range(NSA_G):
        vct_ref[g] = vct[g * NSA_D:(g + 1) * NSA_D, :]


def _nsa_cmp(xk, xv, pe, wt, wb, w2, cos, sin, gk, gavg):
    B, n, W = xk.shape
    return pl.pallas_call(
        _nsa_cmp_body,
        out_shape=[jax.ShapeDtypeStruct((B, n, LANE), BF16),
                   jax.ShapeDtypeStruct((B, NSA_G, NSA_D, n), BF16)],
        grid=(B,),
        in_specs=[pl.BlockSpec((None, n, W), lambda b: (b, 0, 0)), pl.BlockSpec((None, n, W), lambda b: (b, 0, 0)),
                  _const_spec(pe.shape), _const_spec(wt.shape), _const_spec(wb.shape), _const_spec(w2.shape),
                  _const_spec((n, LANE)), _const_spec((n, LANE)), _const_spec((1, LANE)), _const_spec((LANE, LANE))],
        out_specs=[pl.BlockSpec((None, n, LANE), lambda b: (b, 0, 0)),
                   pl.BlockSpec((None, NSA_G, NSA_D, n), lambda b: (b, 0, 0, 0))],
        compiler_params=_params(("parallel",)),
        name="nsa_compress",
    )(xk, xv, pe, wt, wb, w2, cos, sin, gk, gavg)


def _group_queries(qt_ref, g):
    HPG = NSA_H // NSA_G
    q4 = jnp.concatenate([qt_ref[(g * HPG + h) * NSA_D:(g * HPG + h + 1) * NSA_D, :] for h in range(HPG)], axis=1)
    parts = [jnp.zeros(q4.shape, BF16)] * NSA_G
    parts[g] = q4
    return jnp.concatenate(parts, axis=0)


def _bitonic_merge_desc(xs):
    xs = list(xs)
    j = len(xs) // 2
    while j >= 1:
        for a in range(len(xs)):
            b = a ^ j
            if b > a:
                xs[a], xs[b] = jnp.maximum(xs[a], xs[b]), jnp.minimum(xs[a], xs[b])
        j //= 2
    return xs


def _bitonic_sort_desc(xs):
    if len(xs) == 1:
        return list(xs)
    half = len(xs) // 2
    lo = _bitonic_sort_desc(xs[:half])
    hi = _bitonic_sort_desc(xs[half:])
    return _bitonic_merge_desc(lo + hi[::-1])


def _kth_largest(score, k):
    SUBLANES = 8
    tiles = [score[SUBLANES * v:SUBLANES * (v + 1), :] for v in range(score.shape[0] // SUBLANES)]
    size = max(k, 1 << (len(tiles) - 1).bit_length())
    tiles = tiles + [jnp.full(tiles[0].shape, NEG, F32)] * (size - len(tiles))

    def top_of_two(a, b):
        return _bitonic_merge_desc([jnp.maximum(a[r], b[k - 1 - r]) for r in range(k)])

    tops = [_bitonic_sort_desc(tiles[c:c + k]) for c in range(0, size, k)]
    while len(tops) > 1:
        tops = [top_of_two(tops[c], tops[c + 1]) for c in range(0, len(tops), 2)]
    top = tops[0]
    shift = SUBLANES // 2
    while shift >= 1:
        top = top_of_two(top, [pltpu.roll(t, shift, 0) for t in top])
        shift //= 2
    return top[k - 1][0:1, :]


def _nsa_select_body(qt_ref, kc_ref, vct_ref, kw_ref, vwd_ref, gt_ref, tril_ref, part_ref, sel_ref, selm_ref, ps_buf,
                     *, sel_tile):
    TQ = LANE
    HPG = NSA_H // NSA_G
    W = HPG * TQ
    n_cmp = kc_ref.shape[0]
    n_sel = sel_ref.shape[1]
    i = pl.program_id(1)
    s0 = i * TQ
    t_row = s0 + _iota((1, TQ), 1)
    n_win = WIN // LANE + 1
    j0 = jnp.maximum(i - (n_win - 1), 0)
    w_start = pl.multiple_of(j0 * LANE, LANE)
    own_first = (s0 // sel_tile) * (sel_tile // SLC_LEN)
    ps_buf[:, 0:PS_PAD, :] = jnp.zeros((NSA_G, PS_PAD, TQ), F32)

    for g in range(NSA_G):
        qpad = _group_queries(qt_ref, g)

        cend = _iota((n_cmp, 1), 0) * CMP_STRIDE + (CMP_LEN - 1)
        bias_c = jnp.where(cend <= t_row, 0.0, NEG)
        sc = _dot(kc_ref[...], qpad) + jnp.concatenate([bias_c] * HPG, axis=1)
        m = jnp.max(sc, axis=0, keepdims=True)
        m = jnp.where(m > 0.5 * NEG, m, 0.0)
        p = jnp.exp2(sc - m)
        p = p * (1.0 / jnp.maximum(jnp.sum(p, axis=0, keepdims=True), 1e-30))
        o_c = _dot(vct_ref[g], p.astype(BF16))
        psum = p[:, 0:TQ]
        for h in range(1, HPG):
            psum = psum + p[:, h * TQ:(h + 1) * TQ]
        ps_buf[g, PS_PAD:PS_PAD + n_cmp, :] = psum

        per = SLC_LEN // CMP_STRIDE
        imp = ps_buf[g, pl.ds(PS_PAD - 1, n_sel, stride=per), :]
        for jj in range(per):
            imp = imp + ps_buf[g, pl.ds(PS_PAD + jj, n_sel, stride=per), :]
        blk = _iota((n_sel, 1), 0)
        cur = t_row >> (SLC_LEN.bit_length() - 1)
        valid = blk * SLC_LEN <= t_row
        forced = (blk == 0) | (blk == cur) | (blk == cur - 1)
        score0 = jnp.where(valid, jnp.where(forced, FORCED_SCORE, imp), NEG)
        kth = _kth_largest(score0, min(SLC_TOPK, n_sel))
        above = score0 > kth
        tied = score0 == kth
        need = min(SLC_TOPK, n_sel) - jnp.sum(jnp.where(above, 1.0, 0.0), axis=0, keepdims=True)
        rank = _dot(tril_ref[...], jnp.where(tied, 1.0, 0.0).astype(BF16))
        chosen = (above | (tied & (rank <= need))) & valid
        sel_ref[g] = jnp.where(chosen, 0.0, NEG)
        selm_ref[g] = jnp.where(chosen & (blk < own_first), 0.0, NEG)

        kpos = w_start + _iota((n_win * LANE, 1), 0)
        bias_w = jnp.where((kpos <= t_row) & (kpos > t_row - WIN), 0.0, NEG)
        sw = _dot(kw_ref[pl.ds(w_start, n_win * LANE), :], qpad) + jnp.concatenate([bias_w] * HPG, axis=1)
        pw = jnp.exp2(sw - jnp.max(sw, axis=0, keepdims=True)).astype(BF16)
        acc_w = jnp.zeros((VT_ROWS, W), F32)
        for r in range(n_win):
            acc_w = acc_w + _dot(vwd_ref[j0 + r, g], pw[r * LANE:(r + 1) * LANE, :])
        o_w = acc_w[0:NSA_D, :] * (1.0 / acc_w[NSA_D:NSA_D + 1, :])

        for hp in range(HPG // 2):
            tiles = []
            for h in (2 * hp, 2 * hp + 1):
                gr = (g * HPG + h) * 3
                cs = slice(h * TQ, (h + 1) * TQ)
                tiles.append(gt_ref[gr:gr + 1, :] * o_c[:, cs] + gt_ref[gr + 2:gr + 3, :] * o_w[:, cs])
            col = (g * HPG + 2 * hp) * NSA_D
            part_ref[:, col:col + 2 * NSA_D] = jnp.concatenate(tiles, axis=0).T


def _nsa_select(qt, kc, vct, kw, vwd, gt, B, S):
    HD = NSA_H * NSA_D
    TQ = LANE
    n_cmp = kc.shape[1]
    n_sel = S // SLC_LEN
    assert S >= WIN + TQ and CMP_LEN == 2 * CMP_STRIDE
    whole = lambda shape: pl.BlockSpec((None,) + shape, lambda b, i: (b,) + (0,) * len(shape),
                                       pipeline_mode=pl.Buffered(1))
    mask_shape = jax.ShapeDtypeStruct((B, NSA_G, n_sel, S), F32)
    mask_spec = pl.BlockSpec((None, NSA_G, n_sel, TQ), lambda b, i: (b, 0, 0, i))
    tril = jnp.asarray(np.tril(np.ones((n_sel, n_sel), np.float32)), BF16)
    return pl.pallas_call(
        functools.partial(_nsa_select_body, sel_tile=SEL_TQ),
        out_shape=[jax.ShapeDtypeStruct((B * S, HD), F32), mask_shape, mask_shape],
        grid=(B, S // TQ),
        in_specs=[pl.BlockSpec((None, HD, TQ), lambda b, i: (b, 0, i)),
                  whole((n_cmp, LANE)), whole(vct.shape[1:]), whole((S, LANE)), whole(vwd.shape[1:]),
                  pl.BlockSpec((None, LANE, TQ), lambda b, i: (b, 0, i)), _const_spec((n_sel, n_sel))],
        out_specs=[pl.BlockSpec((TQ, HD), lambda b, i: (b * (S // TQ) + i, 0)), mask_spec, mask_spec],
        scratch_shapes=[pltpu.VMEM((NSA_G, n_cmp + PS_PAD, TQ), F32)],
        compiler_params=_params(("parallel", "parallel")),
        name="nsa_select",
    )(qt, kc, vct, kw, vwd, gt, tril)


def _nsa_selected_body(qt_ref, ks_ref, ksd_ref, vsl_ref, selm_ref, seld_ref, gt_ref, part_ref, oneh_ref, o_ref,
                       s_a, s_b, p_a, p_b, al_a, al_b, m_ref, acc_ref):
    TQ = qt_ref.shape[1]
    HPG = NSA_H // NSA_G
    W = HPG * TQ
    n_kc = ks_ref.shape[0] // SEL_KC
    bps = SEL_KC // SLC_LEN
    i = pl.program_id(1)
    n_main = i * (TQ // SEL_KC)
    n_pairs = (n_main + 1) // 2
    diag_ok = _iota((TQ, 1), 0) <= _iota((1, TQ), 1)
    pad_rows = jnp.zeros((LANE - 2 * bps, W), BF16)

    def q_aug(qpad, brows):
        brows = jnp.concatenate([jnp.concatenate([brows] * HPG, axis=1), jnp.zeros((bps, W), F32)], axis=0)
        return jnp.concatenate([qpad, brows.astype(BF16), pad_rows], axis=0)

    heads = [slice(h * TQ, (h + 1) * TQ) for h in range(HPG)]

    def softmax(s_buf, p_buf, al_ref, cs):
        sb = s_buf[:, cs]
        m_i = m_ref[:, cs]
        m_new = jnp.maximum(m_i, jnp.max(sb, axis=0, keepdims=True).astype(F32))
        p_buf[:, cs] = jnp.exp2(sb - m_new.astype(BF16))
        al_ref[:, cs] = jnp.exp2(m_i - m_new)
        m_ref[:, cs] = m_new

    def apply_values(p_buf, al_ref, vt, cs):
        acc_ref[:, cs] = al_ref[:, cs] * acc_ref[:, cs] + _dot(vt, p_buf[:, cs])

    for g in range(NSA_G):
        qpad = _group_queries(qt_ref, g)
        m_ref[...] = jnp.full(m_ref.shape, NEG, F32)
        acc_ref[...] = jnp.zeros(acc_ref.shape, F32)
        p_b[...] = jnp.zeros(p_b.shape, BF16)
        al_b[...] = jnp.ones(al_b.shape, F32)

        def step_operands(c):
            k0 = pl.multiple_of(c * SEL_KC, SEL_KC)
            brows = selm_ref[g, pl.ds(pl.multiple_of(c * bps, bps), bps), :]
            return jnp.concatenate([ks_ref[pl.ds(k0, SEL_KC), :], oneh_ref[...]], axis=1), q_aug(qpad, brows)

        def scores(ops, s_buf, cs):
            s_buf[:, cs] = _dot(ops[0], ops[1][:, cs]).astype(BF16)

        def pair(cc, carry):
            c0 = 2 * cc
            ops, vt = step_operands(c0 + 1), vsl_ref[jnp.maximum(c0 - 1, 0), g]
            for cs in heads:
                scores(ops, s_b, cs)
                softmax(s_a, p_a, al_a, cs)
                apply_values(p_b, al_b, vt, cs)
            ops, vt = step_operands(jnp.minimum(c0 + 2, n_kc - 1)), vsl_ref[c0, g]
            for cs in heads:
                scores(ops, s_a, cs)
                softmax(s_b, p_b, al_b, cs)
                apply_values(p_a, al_a, vt, cs)
            return carry

        ops = step_operands(0)
        for cs in heads:
            scores(ops, s_a, cs)
        lax.fori_loop(0, n_pairs, pair, 0)
        vt = vsl_ref[jnp.maximum(2 * n_pairs - 1, 0), g]
        for cs in heads:
            apply_values(p_b, al_b, vt, cs)

        for d in range(TQ // SEL_KC):
            k_aug = jnp.concatenate([ksd_ref[d * SEL_KC:(d + 1) * SEL_KC, :], oneh_ref[...]], axis=1)
            qa = q_aug(qpad, seld_ref[g, d * bps:(d + 1) * bps, :])
            vt = vsl_ref[i * (TQ // SEL_KC) + d, g]
            for cs in heads:
                sd = _dot(k_aug, qa[:, cs])
                s_a[:, cs] = jnp.where(diag_ok[d * SEL_KC:(d + 1) * SEL_KC, :], sd, NEG).astype(BF16)
                softmax(s_a, p_a, al_a, cs)
                apply_values(p_a, al_a, vt, cs)
        acc_s = acc_ref[...]
        o_s = acc_s[0:NSA_D, :] * (1.0 / acc_s[NSA_D:NSA_D + 1, :])

        for hp in range(HPG // 2):
            tiles = []
            for h in (2 * hp, 2 * hp + 1):
                gr = (g * HPG + h) * 3 + 1
                tiles.append(gt_ref[gr:gr + 1, :] * o_s[:, h * TQ:(h + 1) * TQ])
            cols = slice((g * HPG + 2 * hp) * NSA_D, (g * HPG + 2 * hp + 2) * NSA_D)
            o_ref[:, cols] = part_ref[:, cols] + jnp.concatenate(tiles, axis=0).T


def _nsa_selected(qt, ks, vsl, selm, sel, gt, part, B, S):
    HD = NSA_H * NSA_D
    TQ = SEL_TQ
    W = NSA_H // NSA_G * TQ
    n_sel = S // SLC_LEN
    assert S % (2 * SEL_KC) == 0 and TQ % SEL_KC == 0
    whole = lambda shape: pl.BlockSpec((None,) + shape, lambda b, i: (b,) + (0,) * len(shape),
                                       pipeline_mode=pl.Buffered(1))
    oneh = jnp.asarray(np.arange(SEL_KC)[:, None] // SLC_LEN == np.arange(LANE)[None, :], BF16)
    return pl.pallas_call(
        _nsa_selected_body,
        out_shape=jax.ShapeDtypeStruct((B * S, HD), F32),
        grid=(B, S // TQ),
        in_specs=[pl.BlockSpec((None, HD, TQ), lambda b, i: (b, 0, i)),
                  whole((S, LANE)), pl.BlockSpec((None, TQ, LANE), lambda b, i: (b, i, 0)),
                  whole(vsl.shape[1:]),
                  pl.BlockSpec((None, NSA_G, n_sel, TQ), lambda b, i: (b, 0, 0, i)),
                  pl.BlockSpec((None, NSA_G, TQ // SLC_LEN, TQ), lambda b, i: (b, 0, i, i)),
                  pl.BlockSpec((None, LANE, TQ), lambda b, i: (b, 0, i)),
                  pl.BlockSpec((TQ, HD), lambda b, i: (b * (S // TQ) + i, 0)),
                  _const_spec((SEL_KC, LANE))],
        out_specs=pl.BlockSpec((TQ, HD), lambda b, i: (b * (S // TQ) + i, 0)),
        scratch_shapes=[pltpu.VMEM((SEL_KC, W), BF16)] * 4 + [pltpu.VMEM((1, W), F32)] * 3
                       + [pltpu.VMEM((VT_ROWS, W), F32)],
        compiler_params=_params(("parallel", "arbitrary")),
        name="nsa_selected",
    )(qt, ks, ks, vsl, selm, sel, gt, part, oneh)


def _merge_body(x_ref, oa_ref, ob_ref, oc_ref, gates_ref, wbr_ref, wo_ref, o_ref):
    D = x_ref.shape[1]
    y = jnp.zeros(x_ref.shape, F32)
    for j, br in enumerate((oa_ref, ob_ref, oc_ref)):
        y = y + _sigmoid(gates_ref[:, j * D:(j + 1) * D].astype(F32)) * _dot(br[...].astype(BF16), wbr_ref[j])
    o_ref[...] = x_ref[...] + _dot(y.astype(BF16), wo_ref[...])


def _merge(x, o_a, o_b, o_c, gates, wbr, wo, tm=512):
    T, D = x.shape
    row = lambda w: pl.BlockSpec((tm, w), lambda i: (i, 0))
    return pl.pallas_call(
        _merge_body,
        out_shape=jax.ShapeDtypeStruct((T, D), F32),
        grid=(T // tm,),
        in_specs=[row(D), row(MIX_W), row(MIX_W), row(MIX_W), row(3 * D), _const_spec(wbr.shape), _const_spec(wo.shape)],
        out_specs=row(D),
        compiler_params=_params(("parallel",)),
        name="merge_out",
    )(x, o_a, o_b, o_c, gates, wbr, wo)


def _pack_in_proj(w_in, b_in):
    ws, bs = [], []
    for _, off, w, wp in _SEGS:
        ws.append(jnp.pad(w_in[..., off:off + w], ((0, 0), (0, 0), (0, wp - w))))
        bs.append(jnp.pad(b_in[..., off:off + w], ((0, 0), (0, wp - w))))
    return jnp.concatenate(ws, axis=-1).astype(BF16), jnp.concatenate(bs, axis=-1)[:, None, :]


def _rope_tables(pos):
    half = NSA_D // 2
    freqs = ROPE_THETA ** (-jnp.arange(half, dtype=F32) / half)
    ang = pos.astype(F32)[:, None] * freqs[None, :]
    cos, sin = jnp.cos(ang), jnp.sin(ang)
    reps = LANE // NSA_D
    return jnp.tile(jnp.concatenate([cos, cos], axis=1), (1, reps)), jnp.tile(jnp.concatenate([-sin, sin], axis=1), (1, reps))


def _cmp_weights(w1, w2):
    L = w1.shape[0]
    w1r = w1.reshape(L, 2, 2, CMP_STRIDE, NSA_D, CMP_HIDDEN)
    eye_g = jnp.eye(NSA_G, dtype=w1.dtype)
    ex = jnp.einsum('lwstdh,gk->lwstgdkh', w1r, eye_g)
    ex = ex.reshape(L, 2, 2, CMP_STRIDE * NSA_G * NSA_D, NSA_G * CMP_HIDDEN)
    w2x = jnp.einsum('lwhd,gk->lwghkd', w2, eye_g).reshape(L, 2, NSA_G * CMP_HIDDEN, NSA_G * NSA_D)
    return ex[:, :, 0].astype(BF16), ex[:, :, 1].astype(BF16), w2x.astype(BF16)


def _cmp_pos_rows(pe):
    L = pe.shape[0]
    r = pe.reshape(L, 2, 2, CMP_STRIDE, 1, NSA_D)
    return jnp.broadcast_to(r, (L, 2, 2, CMP_STRIDE, NSA_G, NSA_D)).reshape(L, 2, 2, CMP_STRIDE * NSA_G * NSA_D)


def _lane_gain(g):
    return jnp.tile(g, LANE // g.shape[-1])[None, :]


def _layer(x, lw, consts, B, S):
    cos, sin, cos_c, sin_c, gavg = consts
    T = B * S
    x = _ffn(x, lw["ffn1_norm"], lw["ffn1_wg"], lw["ffn1_wu"], lw["ffn1_wd"])
    z = dict(zip([s[0] for s in _SEGS], _proj(x, lw["mix_norm"], lw["w_in"], lw["b_in"])))
    o_a = _gla(z["a_q"], z["a_k"], z["a_v"], z["a_r"], z["a_lr"], lw["gla_wa"], lw["gla_ba"], lw["gla_gn"], B, S)
    o_c = _mlstm(z["c_qk"], z["c_v"], z["c_o"], z["c_if"], lw["conv_w"], lw["conv_b"], B, S)
    qn = lw["nsa_qk_norm"]
    qt, ks, kw, vsl, vwd, gt = _nsa_prep(z["b_q"], z["b_ks"], z["b_kw"], z["b_vs"], z["b_vw"], z["b_g"], cos, sin,
                                         _lane_gain(qn[0]), _lane_gain(qn[2]), _lane_gain(qn[3]), gavg, B, S)
    n_blk = S // CMP_STRIDE
    xk = z["b_kc"].reshape(B, n_blk, CMP_STRIDE * LANE)
    xv = z["b_vc"].reshape(B, n_blk, CMP_STRIDE * LANE)
    kc, vct = _nsa_cmp(xk, xv, lw["cmp_pe"], lw["cmp_wt"], lw["cmp_wb"], lw["cmp_w2"], cos_c, sin_c,
                       _lane_gain(qn[1]), gavg)
    part, sel, selm = _nsa_select(qt, kc, vct, kw, vwd, gt, B, S)
    o_b = _nsa_selected(qt, ks, vsl, selm, sel, gt, part, B, S)
    x = _merge(x, o_a, o_b, o_c, z["gates"], lw["w_branch"], lw["w_out"])
    return _ffn(x, lw["ffn2_norm"], lw["ffn2_wg"], lw["ffn2_wu"], lw["ffn2_wd"])


def kernel(x, ffn1_norm, ffn1_w_gate, ffn1_w_up, ffn1_w_down, mix_norm, w_in, b_in, gla_w_alpha, gla_b_alpha, gla_out_norm, nsa_qk_norm, nsa_cmp_pos, nsa_cmp_w1, nsa_cmp_w2, mlstm_conv_w, mlstm_conv_b, w_branch, w_out, ffn2_norm, ffn2_w_gate, ffn2_w_up, ffn2_w_down):
    B, S, D = x.shape
    w_in_p, b_in_p = _pack_in_proj(w_in, b_in)
    cmp_wt, cmp_wb, cmp_w2 = _cmp_weights(nsa_cmp_w1, nsa_cmp_w2)
    layers = {
        "ffn1_norm": ffn1_norm[:, None, :], "ffn1_wg": ffn1_w_gate.astype(BF16), "ffn1_wu": ffn1_w_up.astype(BF16),
        "ffn1_wd": ffn1_w_down.astype(BF16),
        "mix_norm": mix_norm[:, None, :], "w_in": w_in_p, "b_in": b_in_p,
        "gla_wa": jnp.pad(gla_w_alpha, ((0, 0), (0, LANE - GLA_RANK), (0, 0))), "gla_ba": gla_b_alpha[:, None, :],
        "gla_gn": gla_out_norm[:, None, :],
        "nsa_qk_norm": nsa_qk_norm, "cmp_pe": _cmp_pos_rows(nsa_cmp_pos), "cmp_wt": cmp_wt, "cmp_wb": cmp_wb,
        "cmp_w2": cmp_w2,
        "conv_w": mlstm_conv_w, "conv_b": mlstm_conv_b[:, None, :],
        "w_branch": w_branch.astype(BF16), "w_out": w_out.astype(BF16),
        "ffn2_norm": ffn2_norm[:, None, :], "ffn2_wg": ffn2_w_gate.astype(BF16), "ffn2_wu": ffn2_w_up.astype(BF16),
        "ffn2_wd": ffn2_w_down.astype(BF16),
    }
    n_blk = S // CMP_STRIDE
    cos, sin = _rope_tables(jnp.arange(S))
    cos_c, sin_c = _rope_tables(jnp.arange(n_blk) * CMP_STRIDE + CMP_LEN - 1)
    gavg = jnp.asarray(np.kron(np.eye(LANE // NSA_D), np.full((NSA_D, NSA_D), 1.0 / NSA_D)), BF16)
    consts = (cos, sin, cos_c, sin_c, gavg)

    out = x.reshape(B * S, D)
    for l in range(w_in.shape[0]):
        out = _layer(out, {name: w[l] for name, w in layers.items()}, consts, B, S)
    return out.reshape(B, S, D)
```

```python
import functools

import numpy as np
import jax
import jax.numpy as jnp
from jax import lax
from jax.experimental import pallas as pl
from jax.experimental.pallas import tpu as pltpu

F32 = jnp.float32
BF16 = jnp.bfloat16

RMS_EPS = 1e-6
ROPE_THETA = 10000.0

GLA_H, GLA_DK, GLA_DV, GLA_RANK, GLA_GATE_NORM = 4, 64, 128, 16, 16.0
GLA_SUB = 16
NSA_H, NSA_G, NSA_D = 8, 2, 64
CMP_LEN, CMP_STRIDE, CMP_HIDDEN = 32, 16, 256
SLC_LEN, SLC_TOPK, WIN = 64, 16, 512
FORCED_SCORE = 1e4
MLSTM_H, MLSTM_DK, MLSTM_DV, CONV_W = 4, 64, 128, 4
MLSTM_CHUNK = 64
MIX_W = 512
LANE = 128
MXU_N = 256
NEG = -1e30
LOG2E = 1.4426950408889634
SEL_KC = 512
SEL_TQ = 512
PS_PAD = 8
VT_ROWS = 80

_SEGS = (
    ("a_q", 0, 256, 256), ("a_k", 256, 256, 256), ("a_v", 512, 512, 512), ("a_r", 1024, 512, 512),
    ("b_q", 1552, 512, 512), ("c_qk", 2856, 512, 512), ("c_v", 3368, 512, 512), ("c_o", 3880, 512, 512),
    ("gates", 4400, 3072, 3072),
    ("a_lr", 1536, 16, 128), ("b_kc", 2064, 128, 128), ("b_vc", 2192, 128, 128), ("b_ks", 2320, 128, 128),
    ("b_vs", 2448, 128, 128), ("b_kw", 2576, 128, 128), ("b_vw", 2704, 128, 128), ("b_g", 2832, 24, 128),
    ("c_if", 4392, 8, 128),
)
_N_PACK = sum(s[3] for s in _SEGS)
_BF16_SEGS = ("gates", "c_v", "b_vs", "b_vw")


def _dot(a, b):
    return jnp.dot(a, b, preferred_element_type=F32)


def _dot_nt(a, b):
    return lax.dot_general(a, b, (((1,), (1,)), ((), ())), preferred_element_type=F32)


def _split2(a):
    hi = a.astype(BF16)
    lo = (a - hi.astype(F32)).astype(BF16)
    return hi, lo


def _dot_l2(a, b):
    hi, lo = _split2(a)
    return _dot(hi, b) + _dot(lo, b)


def _dot_r2(a, b):
    hi, lo = _split2(b)
    return _dot(a, hi) + _dot(a, lo)


def _log_sigmoid(x):
    return jnp.minimum(x, 0.0) - jnp.log(1.0 + jnp.exp(-jnp.abs(x)))


def _sigmoid(x):
    return 1.0 / (1.0 + jnp.exp(-x))


def _silu(x):
    return x * _sigmoid(x)


def _iota(shape, dim):
    return lax.broadcasted_iota(jnp.int32, shape, dim)


def _const_spec(shape):
    nd = len(shape)
    return pl.BlockSpec(shape, lambda *_: (0,) * nd, pipeline_mode=pl.Buffered(1))


def _params(sem, vmem_mb=56):
    return pltpu.CompilerParams(dimension_semantics=sem, vmem_limit_bytes=vmem_mb * 1024 * 1024)


def _ffn_body(x_ref, g_ref, wg_ref, wu_ref, wd_ref, o_ref, *, bounds):
    x = x_ref[...]
    ms = jnp.mean(x * x, axis=-1, keepdims=True)
    h = (x * lax.rsqrt(ms + RMS_EPS) * g_ref[...]).astype(BF16)
    acc = jnp.zeros(x.shape, F32)
    for lo, hi in zip(bounds[:-1], bounds[1:]):
        a = _dot(h, wg_ref[:, lo:hi])
        u = _dot(h, wu_ref[:, lo:hi])
        t = (_silu(a) * u).astype(BF16)
        acc = acc + _dot(t, wd_ref[lo:hi, :])
    o_ref[...] = x + 0.5 * acc


def _ffn(x, g, wg, wu, wd, tm=512):
    T, D = x.shape
    F = wg.shape[1]
    bounds = (0, -(-F // (2 * MXU_N)) * MXU_N, F)
    return pl.pallas_call(
        functools.partial(_ffn_body, bounds=bounds),
        out_shape=jax.ShapeDtypeStruct((T, D), F32),
        grid=(T // tm,),
        in_specs=[pl.BlockSpec((tm, D), lambda i: (i, 0)), _const_spec((1, D)),
                  _const_spec((D, F)), _const_spec((D, F)), _const_spec((F, D))],
        out_specs=pl.BlockSpec((tm, D), lambda i: (i, 0)),
        compiler_params=_params(("parallel",)),
        name="ffn",
    )(x, g, wg, wu, wd)


def _proj_body(x_ref, g_ref, w_ref, b_ref, *o_refs):
    x = x_ref[...]
    ms = jnp.mean(x * x, axis=-1, keepdims=True)
    h = (x * lax.rsqrt(ms + RMS_EPS) * g_ref[...]).astype(BF16)
    off, i = 0, 0
    while i < len(o_refs):
        group, w = [], 0
        while i < len(o_refs) and (not group or w % MXU_N):
            group.append(o_refs[i])
            w += o_refs[i].shape[1]
            i += 1
        z = _dot(h, w_ref[:, off:off + w]) + b_ref[:, off:off + w]
        c = 0
        for o_ref in group:
            o_ref[...] = z[:, c:c + o_ref.shape[1]].astype(o_ref.dtype)
            c += o_ref.shape[1]
        off += w


def _proj(x, g, w, b, tm=256):
    T, D = x.shape
    return pl.pallas_call(
        _proj_body,
        out_shape=[jax.ShapeDtypeStruct((T, s[3]), BF16 if s[0] in _BF16_SEGS else F32) for s in _SEGS],
        grid=(T // tm,),
        in_specs=[pl.BlockSpec((tm, D), lambda i: (i, 0)), _const_spec((1, D)),
                  _const_spec((D, _N_PACK)), _const_spec((1, _N_PACK))],
        out_specs=[pl.BlockSpec((tm, s[3]), lambda i: (i, 0)) for s in _SEGS],
        compiler_params=_params(("parallel",)),
        name="in_proj",
    )(x, g, w, b)


def _gla_body(q_ref, k_ref, v_ref, r_ref, lr_ref, wa_ref, ba_ref, gn_ref, tri_ref, bones_ref, eh_ref,
              o_ref, st_ref, qs_ref, c_ref, tot_ref, kst_ref, vt4_ref, oi_ref):
    nB, Lc = q_ref.shape[0], q_ref.shape[1]
    n_sub = Lc // GLA_SUB
    HK = GLA_H * GLA_DK

    @pl.when(pl.program_id(0) == 0)
    def _():
        st_ref[...] = jnp.zeros(st_ref.shape, F32)

    lane_h = _iota((1, HK), 1) >> (GLA_DK.bit_length() - 1)
    row_i = _iota((GLA_SUB, 1), 0)
    col_r = _iota((1, GLA_H * Lc), 1) & (Lc - 1)
    wa_hi, wa_lo = _split2(wa_ref[...])

    for b in range(nB):
        lr_hi, lr_lo = _split2(lr_ref[b])
        u = _dot(lr_hi, wa_hi) + _dot(lr_hi, wa_lo) + _dot(lr_lo, wa_hi) + ba_ref[...]
        g = _log_sigmoid(u) * (1.0 / GLA_GATE_NORM)
        c = _dot_r2(tri_ref[...], g)
        tot = _dot_r2(bones_ref[...], g)
        qs_ref[b] = q_ref[b] * (GLA_DK ** -0.5)
        c_ref[b] = c
        tot_ref[b] = tot
        kt = k_ref[b] * jnp.exp(tot - c)
        vt = v_ref[b].T.astype(BF16)
        for h in range(GLA_H):
            kst_ref[b, h * Lc:(h + 1) * Lc, :] = jnp.where(lane_h == h, kt, 0.0).astype(BF16)
            vt4_ref[b, :, h * Lc:(h + 1) * Lc] = vt[h * GLA_DV:(h + 1) * GLA_DV, :]

    def sub(b, s):
        rows = slice(s * GLA_SUB, (s + 1) * GLA_SUB)
        qs = qs_ref[b, rows, :]
        cs = c_ref[b, rows, :]
        st = st_ref[b]
        qd = qs * jnp.exp(cs)
        q4 = jnp.concatenate([jnp.where(lane_h == h, qd, 0.0) for h in range(GLA_H)], axis=0).astype(BF16)
        inter4 = _dot_nt(q4, st.astype(BF16))
        inter = jnp.concatenate([inter4[h * GLA_SUB:(h + 1) * GLA_SUB, :] for h in range(GLA_H)], axis=1)
        xs = []
        for j in range(GLA_SUB):
            r = s * GLA_SUB + j
            x = qs * k_ref[b, r:r + 1, :] * jnp.exp(jnp.minimum(cs - c_ref[b, r:r + 1, :], 0.0))
            xs.append(jnp.where(row_i >= j, x, 0.0))
        r_all = _dot(jnp.concatenate(xs, axis=0).astype(BF16), eh_ref[...])
        intra = jnp.zeros((GLA_SUB, r_all.shape[1]), F32)
        for j in range(GLA_SUB):
            r = s * GLA_SUB + j
            intra = intra + r_all[j * GLA_SUB:(j + 1) * GLA_SUB, :] * v_ref[b, r:r + 1, :]
        oi_ref[b, rows, :] = inter + intra
        dec = jnp.exp(tot_ref[b, s * GLA_SUB:s * GLA_SUB + 1, :])
        in_sub = (col_r >= s * GLA_SUB) & (col_r < (s + 1) * GLA_SUB)
        vtm = jnp.where(in_sub, vt4_ref[b], jnp.zeros((), BF16))
        st_ref[b] = dec * st + _dot(vtm, kst_ref[b])

    for s in range(n_sub):
        for b in range(nB):
            sub(b, s)

    gn = gn_ref[...]
    for b in range(nB):
        for h in range(GLA_H):
            sl = slice(h * GLA_DV, (h + 1) * GLA_DV)
            o = oi_ref[b, :, sl]
            ms = jnp.mean(o * o, axis=-1, keepdims=True)
            o_ref[b, :, sl] = o * lax.rsqrt(ms + RMS_EPS) * gn * _silu(r_ref[b, :, sl])


def _gla_consts(Lc):
    r = np.arange(Lc)
    same = (r[:, None] // GLA_SUB) == (r[None, :] // GLA_SUB)
    tri = (same & (r[None, :] <= r[:, None])).astype(np.float32)
    bones = same.astype(np.float32)
    hk = np.arange(GLA_H * GLA_DK) // GLA_DK
    hv = np.arange(GLA_H * GLA_DV) // GLA_DV
    eh = (hk[:, None] == hv[None, :]).astype(np.float32)
    return jnp.asarray(tri, BF16), jnp.asarray(bones, BF16), jnp.asarray(eh, BF16)


def _gla(a_q, a_k, a_v, a_r, a_lr, wa, ba, gn, B, S, Lc=128):
    HK, HV = GLA_H * GLA_DK, GLA_H * GLA_DV
    nb = S // Lc
    assert Lc & (Lc - 1) == 0
    tri, bones, eh = _gla_consts(Lc)
    row = lambda w: pl.BlockSpec((B, Lc, w), lambda i: (0, i, 0))
    seq = lambda t: t.reshape(B, S, t.shape[-1])
    out = pl.pallas_call(
        _gla_body,
        out_shape=jax.ShapeDtypeStruct((B, S, HV), F32),
        grid=(nb,),
        in_specs=[row(HK), row(HK), row(HV), row(HV), row(LANE),
                  _const_spec((LANE, HK)), _const_spec((1, HK)), _const_spec((1, GLA_DV)),
                  _const_spec((Lc, Lc)), _const_spec((Lc, Lc)), _const_spec((HK, HV))],
        out_specs=row(HV),
        scratch_shapes=[pltpu.VMEM((B, GLA_DV, HK), F32), pltpu.VMEM((B, Lc, HK), F32), pltpu.VMEM((B, Lc, HK), F32),
                        pltpu.VMEM((B, Lc, HK), F32), pltpu.VMEM((B, GLA_H * Lc, HK), BF16),
                        pltpu.VMEM((B, GLA_DV, GLA_H * Lc), BF16), pltpu.VMEM((B, Lc, HV), F32)],
        compiler_params=_params(("arbitrary",)),
        name="gla",
    )(seq(a_q), seq(a_k), seq(a_v), seq(a_r), seq(a_lr), wa, ba, gn, tri, bones, eh)
    return out.reshape(B * S, HV)


def _mlstm_body(qk_ref, v_ref, og_ref, if_ref, cw_ref, cb_ref, tri_ref, bdm_ref,
                o_ref, xx_ref, ct_ref, n_ref, m_ref, kw_ref, vt_ref, act_ref):
    Lc = v_ref.shape[0]
    L = MLSTM_CHUNK
    HK = MLSTM_H * MLSTM_DK
    tail = 8

    @pl.when(pl.program_id(1) == 0)
    def _():
        ct_ref[...] = jnp.zeros(ct_ref.shape, F32)
        n_ref[...] = jnp.zeros(n_ref.shape, F32)
        m_ref[...] = jnp.zeros(m_ref.shape, F32)
        xx_ref[0:tail, :] = jnp.zeros((tail, xx_ref.shape[1]), F32)

    xx_ref[tail:tail + Lc, :] = qk_ref[...]
    y = jnp.zeros((Lc, 2 * HK), F32) + cb_ref[...]
    for kk in range(CONV_W):
        y = y + cw_ref[kk:kk + 1, :] * xx_ref[pl.ds(tail - (CONV_W - 1) + kk, Lc), :]
    xx_ref[0:tail, :] = xx_ref[Lc:Lc + tail, :]
    act_ref[...] = _silu(y)

    gates = if_ref[...]
    logf = _log_sigmoid(gates)
    bcum = _dot_r2(tri_ref[...], logf)
    b_t = bcum.T
    q_t = act_ref[:, 0:HK].T
    qt_hi, qt_lo = _split2(q_t)
    kw_ref[...] = jnp.zeros(kw_ref.shape, BF16)
    vt_ref[...] = v_ref[...].astype(F32).T.astype(BF16)
    lane_hk = _iota((1, HK), 1) >> (MLSTM_DK.bit_length() - 1)
    col_l = _iota((1, Lc), 1)
    lane_w = _iota((1, LANE), 1)
    row_l = _iota((L, 1), 0)
    per_win = LANE // L

    for ci in range(Lc // L):
        p, c = divmod(ci, per_win)
        rows = slice(ci * L, (ci + 1) * L)
        win = slice(p * LANE, (p + 1) * LANE)
        causal_t = ((c * L + row_l) <= lane_w) & ((lane_w >> (L.bit_length() - 1)) == c)
        k_all = act_ref[rows, HK:2 * HK] * (MLSTM_DK ** -0.5)
        k_bf = k_all.astype(BF16)
        ct = ct_ref[...]
        ct_bf = ct.astype(BF16)
        n_row = n_ref[...]
        n_hi, n_lo = _split2(n_row)
        n8 = jnp.concatenate([jnp.where(lane_hk == h, part, jnp.zeros((), BF16))
                              for part in (n_hi, n_lo) for h in range(MLSTM_H)], axis=0)
        qn8 = _dot(n8, qt_hi[:, win]) + _dot(n8, qt_lo[:, win])
        wk_full = jnp.zeros((L, HK), F32)
        dec_row = jnp.zeros((1, HK), F32)
        for h in range(MLSTM_H):
            icol = gates[rows, h:h + 1]
            bcol = bcum[rows, MLSTM_H + h:MLSTM_H + h + 1]
            brow = b_t[MLSTM_H + h:MLSTM_H + h + 1, win]
            m_st = m_ref[0:1, h:h + 1]
            dmat_t = jnp.where(causal_t, brow + (icol - bcol), -jnp.inf)
            inter_log = brow + m_st
            m_row = jnp.maximum(inter_log, jnp.max(dmat_t, axis=0, keepdims=True))
            w_inter = jnp.exp(inter_log - m_row)
            head_rows = slice(h * MLSTM_DK, (h + 1) * MLSTM_DK)
            zero_q = jnp.zeros((MLSTM_DK, LANE), BF16)
            qh = jnp.concatenate([qt_hi[head_rows, win] if hh == h else zero_q for hh in range(MLSTM_H)], axis=0)
            s_qk = _dot(k_bf, qh) * jnp.exp(dmat_t - m_row)
            den = w_inter * (qn8[h:h + 1, :] + qn8[MLSTM_H + h:MLSTM_H + h + 1, :]) + jnp.sum(s_qk, axis=0, keepdims=True)
            s_bf = s_qk.astype(BF16)
            pads = [jnp.zeros((L, LANE), BF16)] * per_win
            pads[c] = s_bf
            dv = slice(h * MLSTM_DV, (h + 1) * MLSTM_DV)
            num_t = w_inter * _dot(ct_bf[dv, :], qt_hi[:, win]) + _dot(vt_ref[dv, win], jnp.concatenate(pads, axis=0))
            h_t = num_t / jnp.maximum(jnp.abs(den), jnp.exp(-m_row))
            o_ref[rows, dv] = h_t.T[c * L:(c + 1) * L, :] * _sigmoid(og_ref[rows, dv])
            m_new = m_row[:, c * L + L - 1:c * L + L]
            b_last = bcol[L - 1:L, :]
            w_k = jnp.exp(b_last - bcol + icol - m_new)
            decay = jnp.exp(b_last + m_st - m_new)
            head = lane_hk == h
            wk_full = wk_full + jnp.where(head, w_k, 0.0)
            dec_row = dec_row + jnp.where(head, decay, 0.0)
            m_ref[0:1, h:h + 1] = m_new
        kw = k_all * wk_full
        kw_ref[rows, :] = kw.astype(BF16)
        in_chunk = (col_l >= ci * L) & (col_l < (ci + 1) * L)
        vtm = jnp.where(in_chunk, vt_ref[...], jnp.zeros((), BF16))
        ct_ref[...] = dec_row * ct + _dot(vtm, kw_ref[...]) * bdm_ref[...]
        n_ref[...] = dec_row * n_row + jnp.sum(kw, axis=0, keepdims=True)


def _mlstm_consts(Lc):
    r = np.arange(Lc)
    same = (r[:, None] // MLSTM_CHUNK) == (r[None, :] // MLSTM_CHUNK)
    tri = (same & (r[None, :] <= r[:, None])).astype(np.float32)
    hk = np.arange(MLSTM_H * MLSTM_DK) // MLSTM_DK
    hv = np.arange(MLSTM_H * MLSTM_DV) // MLSTM_DV
    eh = (hk[:, None] == hv[None, :]).astype(np.float32)
    return jnp.asarray(tri, BF16), jnp.asarray(eh.T.copy(), F32)


def _mlstm(c_qk, c_v, c_o, c_if, cw, cb, B, S, Lc=256):
    HK, HV = MLSTM_H * MLSTM_DK, MLSTM_H * MLSTM_DV
    nb = S // Lc
    assert Lc % LANE == 0 and LANE % MLSTM_CHUNK == 0
    tri, bdm = _mlstm_consts(Lc)
    row = lambda w: pl.BlockSpec((Lc, w), lambda b, i: (b * nb + i, 0))
    return pl.pallas_call(
        _mlstm_body,
        out_shape=jax.ShapeDtypeStruct((B * S, HV), F32),
        grid=(B, nb),
        in_specs=[row(2 * HK), row(HV), row(HV), row(LANE),
                  _const_spec((CONV_W, 2 * HK)), _const_spec((1, 2 * HK)),
                  _const_spec((Lc, Lc)), _const_spec((HV, HK))],
        out_specs=row(HV),
        scratch_shapes=[pltpu.VMEM((Lc + 8, 2 * HK), F32), pltpu.VMEM((HV, HK), F32), pltpu.VMEM((1, HK), F32),
                        pltpu.VMEM((8, LANE), F32), pltpu.VMEM((Lc, HK), BF16), pltpu.VMEM((HV, Lc), BF16),
                        pltpu.VMEM((Lc, 2 * HK), F32)],
        compiler_params=_params(("parallel", "arbitrary")),
        name="mlstm",
    )(c_qk, c_v, c_o, c_if, cw, cb, tri, bdm)


def _group_rms(x, gavg, gain):
    ms = _dot_l2(x * x, gavg)
    return x * lax.rsqrt(ms + RMS_EPS) * gain


def _rope_lanes(x, cos, sin_signed):
    half = NSA_D // 2
    first = (_iota((1, LANE), 1) & (NSA_D - 1)) < half
    swapped = jnp.where(first, pltpu.roll(x, LANE - half, 1), pltpu.roll(x, half, 1))
    return x * cos + swapped * sin_signed


def _store_vt_tiles(dst_ref, vt, width):
    ones = jnp.ones((VT_ROWS - NSA_D, width), BF16)
    for j in range(vt.shape[1] // width):
        for g in range(NSA_G):
            dst_ref[j, g, 0:NSA_D, :] = vt[g * NSA_D:(g + 1) * NSA_D, j * width:(j + 1) * width]
            dst_ref[j, g, NSA_D:VT_ROWS, :] = ones


def _nsa_prep_body(q_ref, ks_ref, kw_ref, vs_ref, vw_ref, g_ref, cos_ref, sin_ref, gq_ref, gs_ref, gw_ref, gavg_ref,
                   qt_ref, kso_ref, kwo_ref, vsl_ref, vwd_ref, gt_ref):
    cos, sin = cos_ref[...], sin_ref[...]
    gavg = gavg_ref[...]
    for cb in range(NSA_H * NSA_D // LANE):
        sl = slice(cb * LANE, (cb + 1) * LANE)
        qn = _rope_lanes(_group_rms(q_ref[:, sl], gavg, gq_ref[...]), cos, sin) * (NSA_D ** -0.5 * LOG2E)
        qt_ref[sl, :] = qn.T.astype(BF16)
    kso_ref[...] = _rope_lanes(_group_rms(ks_ref[...], gavg, gs_ref[...]), cos, sin).astype(BF16)
    kwo_ref[...] = _rope_lanes(_group_rms(kw_ref[...], gavg, gw_ref[...]), cos, sin).astype(BF16)
    vst = vs_ref[...].astype(F32).T.astype(BF16)
    _store_vt_tiles(vsl_ref, vst, SEL_KC)
    _store_vt_tiles(vwd_ref, vw_ref[...].astype(F32).T.astype(BF16), LANE)
    gt_ref[...] = _sigmoid(g_ref[...]).T


def _nsa_prep(b_q, b_ks, b_kw, b_vs, b_vw, b_g, cos, sin, gq, gs, gw, gavg, B, S, tm=SEL_KC):
    nb = S // tm
    row = lambda w: pl.BlockSpec((tm, w), lambda b, i: (b * nb + i, 0))
    tab = pl.BlockSpec((tm, LANE), lambda b, i: (i, 0))
    HD = NSA_H * NSA_D
    vt_shape = lambda width: jax.ShapeDtypeStruct((B, S // width, NSA_G, VT_ROWS, width), BF16)
    vt_spec = lambda width: pl.BlockSpec((None, tm // width, NSA_G, VT_ROWS, width), lambda b, i: (b, i, 0, 0, 0))
    return pl.pallas_call(
        _nsa_prep_body,
        out_shape=[jax.ShapeDtypeStruct((B, HD, S), BF16),
                   jax.ShapeDtypeStruct((B, S, LANE), BF16),
                   jax.ShapeDtypeStruct((B, S, LANE), BF16),
                   vt_shape(SEL_KC),
                   vt_shape(LANE),
                   jax.ShapeDtypeStruct((B, LANE, S), F32)],
        grid=(B, nb),
        in_specs=[row(HD), row(LANE), row(LANE), row(LANE), row(LANE), row(LANE), tab, tab,
                  _const_spec((1, LANE)), _const_spec((1, LANE)), _const_spec((1, LANE)), _const_spec((LANE, LANE))],
        out_specs=[pl.BlockSpec((None, HD, tm), lambda b, i: (b, 0, i)),
                   pl.BlockSpec((None, tm, LANE), lambda b, i: (b, i, 0)),
                   pl.BlockSpec((None, tm, LANE), lambda b, i: (b, i, 0)),
                   vt_spec(SEL_KC), vt_spec(LANE),
                   pl.BlockSpec((None, LANE, tm), lambda b, i: (b, 0, i))],
        compiler_params=_params(("parallel", "parallel")),
        name="nsa_prep",
    )(b_q, b_ks, b_kw, b_vs, b_vw, b_g, cos, sin, gq, gs, gw, gavg)


def _gelu_tanh(x):
    return 0.5 * x * (1.0 + jnp.tanh(0.7978845608028654 * (x + 0.044715 * x * x * x)))


def _nsa_cmp_body(xk_ref, xv_ref, pe_ref, wt_ref, wb_ref, w2_ref, cos_ref, sin_ref, gk_ref, gavg_ref,
                  kc_ref, vct_ref):
    n = xk_ref.shape[0]

    def compress(x, which):
        u = _dot((x + pe_ref[which, 0:1, :]).astype(BF16), wt_ref[which])
        v = _dot((x + pe_ref[which, 1:2, :]).astype(BF16), wb_ref[which])
        hid = u + pltpu.roll(v, n - 1, 0)
        return _dot(_gelu_tanh(hid).astype(BF16), w2_ref[which])

    ck = compress(xk_ref[...], 0)
    kc_ref[...] = _rope_lanes(_group_rms(ck, gavg_ref[...], gk_ref[...]), cos_ref[...], sin_ref[...]).astype(BF16)
    vct = compress(xv_ref[...], 1).T.astype(BF16)
    for g in range(NSA_G):
        vct_ref[g] = vct[g * NSA_D:(g + 1) * NSA_D, :]


def _nsa_cmp(xk, xv, pe, wt, wb, w2, cos, sin, gk, gavg):
    B, n, W = xk.shape
    return pl.pallas_call(
        _nsa_cmp_body,
        out_shape=[jax.ShapeDtypeStruct((B, n, LANE), BF16),
                   jax.ShapeDtypeStruct((B, NSA_G, NSA_D, n), BF16)],
        grid=(B,),
        in_specs=[pl.BlockSpec((None, n, W), lambda b: (b, 0, 0)), pl.BlockSpec((None, n, W), lambda b: (b, 0, 0)),
                  _const_spec(pe.shape), _const_spec(wt.shape), _const_spec(wb.shape), _const_spec(w2.shape),
                  _const_spec((n, LANE)), _const_spec((n, LANE)), _const_spec((1, LANE)), _const_spec((LANE, LANE))],
        out_specs=[pl.BlockSpec((None, n, LANE), lambda b: (b, 0, 0)),
                   pl.BlockSpec((None, NSA_G, NSA_D, n), lambda b: (b, 0, 0, 0))],
        compiler_params=_params(("parallel",)),
        name="nsa_compress",
    )(xk, xv, pe, wt, wb, w2, cos, sin, gk, gavg)


def _group_queries(qt_ref, g):
    HPG = NSA_H // NSA_G
    q4 = jnp.concatenate([qt_ref[(g * HPG + h) * NSA_D:(g * HPG + h + 1) * NSA_D, :] for h in range(HPG)], axis=1)
    parts = [jnp.zeros(q4.shape, BF16)] * NSA_G
    parts[g] = q4
    return jnp.concatenate(parts, axis=0)


def _bitonic_merge_desc(xs):
    xs = list(xs)
    j = len(xs) // 2
    while j >= 1:
        for a in range(len(xs)):
            b = a ^ j
            if b > a:
                xs[a], xs[b] = jnp.maximum(xs[a], xs[b]), jnp.minimum(xs[a], xs[b])
        j //= 2
    return xs


def _bitonic_sort_desc(xs):
    if len(xs) == 1:
        return list(xs)
    half = len(xs) // 2
    lo = _bitonic_sort_desc(xs[:half])
    hi = _bitonic_sort_desc(xs[half:])
    return _bitonic_merge_desc(lo + hi[::-1])


def _kth_largest(score, k):
    SUBLANES = 8
    tiles = [score[SUBLANES * v:SUBLANES * (v + 1), :] for v in range(score.shape[0] // SUBLANES)]
    size = max(k, 1 << (len(tiles) - 1).bit_length())
    tiles = tiles + [jnp.full(tiles[0].shape, NEG, F32)] * (size - len(tiles))

    def top_of_two(a, b):
        return _bitonic_merge_desc([jnp.maximum(a[r], b[k - 1 - r]) for r in range(k)])

    tops = [_bitonic_sort_desc(tiles[c:c + k]) for c in range(0, size, k)]
    while len(tops) > 1:
        tops = [top_of_two(tops[c], tops[c + 1]) for c in range(0, len(tops), 2)]
    top = tops[0]
    shift = SUBLANES // 2
    while shift >= 1:
        top = top_of_two(top, [pltpu.roll(t, shift, 0) for t in top])
        shift //= 2
    return top[k - 1][0:1, :]


def _nsa_select_body(qt_ref, kc_ref, vct_ref, kw_ref, vwd_ref, gt_ref, tril_ref, part_ref, sel_ref, selm_ref, ps_buf,
                     *, sel_tile):
    TQ = LANE
    HPG = NSA_H // NSA_G
    W = HPG * TQ
    n_cmp = kc_ref.shape[0]
    n_sel = sel_ref.shape[1]
    i = pl.program_id(1)
    s0 = i * TQ
    t_row = s0 + _iota((1, TQ), 1)
    n_win = WIN // LANE + 1
    j0 = jnp.maximum(i - (n_win - 1), 0)
    w_start = pl.multiple_of(j0 * LANE, LANE)
    own_first = (s0 // sel_tile) * (sel_tile // SLC_LEN)
    ps_buf[:, 0:PS_PAD, :] = jnp.zeros((NSA_G, PS_PAD, TQ), F32)

    for g in range(NSA_G):
        qpad = _group_queries(qt_ref, g)

        cend = _iota((n_cmp, 1), 0) * CMP_STRIDE + (CMP_LEN - 1)
        bias_c = jnp.where(cend <= t_row, 0.0, NEG)
        sc = _dot(kc_ref[...], qpad) + jnp.concatenate([bias_c] * HPG, axis=1)
        m = jnp.max(sc, axis=0, keepdims=True)
        m = jnp.where(m > 0.5 * NEG, m, 0.0)
        p = jnp.exp2(sc - m)
        p = p * (1.0 / jnp.maximum(jnp.sum(p, axis=0, keepdims=True), 1e-30))
        o_c = _dot(vct_ref[g], p.astype(BF16))
        psum = p[:, 0:TQ]
        for h in range(1, HPG):
            psum = psum + p[:, h * TQ:(h + 1) * TQ]
        ps_buf[g, PS_PAD:PS_PAD + n_cmp, :] = psum

        per = SLC_LEN // CMP_STRIDE
        imp = ps_buf[g, pl.ds(PS_PAD - 1, n_sel, stride=per), :]
        for jj in range(per):
            imp = imp + ps_buf[g, pl.ds(PS_PAD + jj, n_sel, stride=per), :]
        blk = _iota((n_sel, 1), 0)
        cur = t_row >> (SLC_LEN.bit_length() - 1)
        valid = blk * SLC_LEN <= t_row
        forced = (blk == 0) | (blk == cur) | (blk == cur - 1)
        score0 = jnp.where(valid, jnp.where(forced, FORCED_SCORE, imp), NEG)
        kth = _kth_largest(score0, min(SLC_TOPK, n_sel))
        above = score0 > kth
        tied = score0 == kth
        need = min(SLC_TOPK, n_sel) - jnp.sum(jnp.where(above, 1.0, 0.0), axis=0, keepdims=True)
        rank = _dot(tril_ref[...], jnp.where(tied, 1.0, 0.0).astype(BF16))
        chosen = (above | (tied & (rank <= need))) & valid
        sel_ref[g] = jnp.where(chosen, 0.0, NEG)
        selm_ref[g] = jnp.where(chosen & (blk < own_first), 0.0, NEG)

        kpos = w_start + _iota((n_win * LANE, 1), 0)
        bias_w = jnp.where((kpos <= t_row) & (kpos > t_row - WIN), 0.0, NEG)
        sw = _dot(kw_ref[pl.ds(w_start, n_win * LANE), :], qpad) + jnp.concatenate([bias_w] * HPG, axis=1)
        pw = jnp.exp2(sw - jnp.max(sw, axis=0, keepdims=True)).astype(BF16)
        acc_w = jnp.zeros((VT_ROWS, W), F32)
        for r in range(n_win):
            acc_w = acc_w + _dot(vwd_ref[j0 + r, g], pw[r * LANE:(r + 1) * LANE, :])
        o_w = acc_w[0:NSA_D, :] * (1.0 / acc_w[NSA_D:NSA_D + 1, :])

        for hp in range(HPG // 2):
            tiles = []
            for h in (2 * hp, 2 * hp + 1):
                gr = (g * HPG + h) * 3
                cs = slice(h * TQ, (h + 1) * TQ)
                tiles.append(gt_ref[gr:gr + 1, :] * o_c[:, cs] + gt_ref[gr + 2:gr + 3, :] * o_w[:, cs])
            col = (g * HPG + 2 * hp) * NSA_D
            part_ref[:, col:col + 2 * NSA_D] = jnp.concatenate(tiles, axis=0).T


def _nsa_select(qt, kc, vct, kw, vwd, gt, B, S):
    HD = NSA_H * NSA_D
    TQ = LANE
    n_cmp = kc.shape[1]
    n_sel = S // SLC_LEN
    assert S >= WIN + TQ and CMP_LEN == 2 * CMP_STRIDE
    whole = lambda shape: pl.BlockSpec((None,) + shape, lambda b, i: (b,) + (0,) * len(shape),
                                       pipeline_mode=pl.Buffered(1))
    mask_shape = jax.ShapeDtypeStruct((B, NSA_G, n_sel, S), F32)
    mask_spec = pl.BlockSpec((None, NSA_G, n_sel, TQ), lambda b, i: (b, 0, 0, i))
    tril = jnp.asarray(np.tril(np.ones((n_sel, n_sel), np.float32)), BF16)
    return pl.pallas_call(
        functools.partial(_nsa_select_body, sel_tile=SEL_TQ),
        out_shape=[jax.ShapeDtypeStruct((B * S, HD), F32), mask_shape, mask_shape],
        grid=(B, S // TQ),
        in_specs=[pl.BlockSpec((None, HD, TQ), lambda b, i: (b, 0, i)),
                  whole((n_cmp, LANE)), whole(vct.shape[1:]), whole((S, LANE)), whole(vwd.shape[1:]),
                  pl.BlockSpec((None, LANE, TQ), lambda b, i: (b, 0, i)), _const_spec((n_sel, n_sel))],
        out_specs=[pl.BlockSpec((TQ, HD), lambda b, i: (b * (S // TQ) + i, 0)), mask_spec, mask_spec],
        scratch_shapes=[pltpu.VMEM((NSA_G, n_cmp + PS_PAD, TQ), F32)],
        compiler_params=_params(("parallel", "parallel")),
        name="nsa_select",
    )(qt, kc, vct, kw, vwd, gt, tril)


def _nsa_selected_body(qt_ref, ks_ref, ksd_ref, vsl_ref, selm_ref, seld_ref, gt_ref, part_ref, oneh_ref, o_ref,
                       s_a, s_b, p_a, p_b, al_a, al_b, m_ref, acc_ref):
    TQ = qt_ref.shape[1]
    HPG = NSA_H // NSA_G
    W = HPG * TQ
    n_kc = ks_ref.shape[0] // SEL_KC
    bps = SEL_KC // SLC_LEN
    i = pl.program_id(1)
    n_main = i * (TQ // SEL_KC)
    n_pairs = (n_main + 1) // 2
    diag_ok = _iota((TQ, 1), 0) <= _iota((1, TQ), 1)
    pad_rows = jnp.zeros((LANE - 2 * bps, W), BF16)

    def q_aug(qpad, brows):
        brows = jnp.concatenate([jnp.concatenate([brows] * HPG, axis=1), jnp.zeros((bps, W), F32)], axis=0)
        return jnp.concatenate([qpad, brows.astype(BF16), pad_rows], axis=0)

    heads = [slice(h * TQ, (h + 1) * TQ) for h in range(HPG)]

    def softmax(s_buf, p_buf, al_ref, cs):
        sb = s_buf[:, cs]
        m_i = m_ref[:, cs]
        m_new = jnp.maximum(m_i, jnp.max(sb, axis=0, keepdims=True).astype(F32))
        p_buf[:, cs] = jnp.exp2(sb - m_new.astype(BF16))
        al_ref[:, cs] = jnp.exp2(m_i - m_new)
        m_ref[:, cs] = m_new

    def apply_values(p_buf, al_ref, vt, cs):
        acc_ref[:, cs] = al_ref[:, cs] * acc_ref[:, cs] + _dot(vt, p_buf[:, cs])

    for g in range(NSA_G):
        qpad = _group_queries(qt_ref, g)
        m_ref[...] = jnp.full(m_ref.shape, NEG, F32)
        acc_ref[...] = jnp.zeros(acc_ref.shape, F32)
        p_b[...] = jnp.zeros(p_b.shape, BF16)
        al_b[...] = jnp.ones(al_b.shape, F32)

        def step_operands(c):
            k0 = pl.multiple_of(c * SEL_KC, SEL_KC)
            brows = selm_ref[g, pl.ds(pl.multiple_of(c * bps, bps), bps), :]
            return jnp.concatenate([ks_ref[pl.ds(k0, SEL_KC), :], oneh_ref[...]], axis=1), q_aug(qpad, brows)

        def scores(ops, s_buf, cs):
            s_buf[:, cs] = _dot(ops[0], ops[1][:, cs]).astype(BF16)

        def pair(cc, carry):
            c0 = 2 * cc
            ops, vt = step_operands(c0 + 1), vsl_ref[jnp.maximum(c0 - 1, 0), g]
            for cs in heads:
                scores(ops, s_b, cs)
                softmax(s_a, p_a, al_a, cs)
                apply_values(p_b, al_b, vt, cs)
            ops, vt = step_operands(jnp.minimum(c0 + 2, n_kc - 1)), vsl_ref[c0, g]
            for cs in heads:
                scores(ops, s_a, cs)
                softmax(s_b, p_b, al_b, cs)
                apply_values(p_a, al_a, vt, cs)
            return carry

        ops = step_operands(0)
        for cs in heads:
            scores(ops, s_a, cs)
        lax.fori_loop(0, n_pairs, pair, 0)
        vt = vsl_ref[jnp.maximum(2 * n_pairs - 1, 0), g]
        for cs in heads:
            apply_values(p_b, al_b, vt, cs)

        for d in range(TQ // SEL_KC):
            k_aug = jnp.concatenate([ksd_ref[d * SEL_KC:(d + 1) * SEL_KC, :], oneh_ref[...]], axis=1)
            qa = q_aug(qpad, seld_ref[g, d * bps:(d + 1) * bps, :])
            vt = vsl_ref[i * (TQ // SEL_KC) + d, g]
            for cs in heads:
                sd = _dot(k_aug, qa[:, cs])
                s_a[:, cs] = jnp.where(diag_ok[d * SEL_KC:(d + 1) * SEL_KC, :], sd, NEG).astype(BF16)
                softmax(s_a, p_a, al_a, cs)
                apply_values(p_a, al_a, vt, cs)
        acc_s = acc_ref[...]
        o_s = acc_s[0:NSA_D, :] * (1.0 / acc_s[NSA_D:NSA_D + 1, :])

        for hp in range(HPG // 2):
            tiles = []
            for h in (2 * hp, 2 * hp + 1):
                gr = (g * HPG + h) * 3 + 1
                tiles.append(gt_ref[gr:gr + 1, :] * o_s[:, h * TQ:(h + 1) * TQ])
            cols = slice((g * HPG + 2 * hp) * NSA_D, (g * HPG + 2 * hp + 2) * NSA_D)
            o_ref[:, cols] = part_ref[:, cols] + jnp.concatenate(tiles, axis=0).T


def _nsa_selected(qt, ks, vsl, selm, sel, gt, part, B, S):
    HD = NSA_H * NSA_D
    TQ = SEL_TQ
    W = NSA_H // NSA_G * TQ
    n_sel = S // SLC_LEN
    assert S % (2 * SEL_KC) == 0 and TQ % SEL_KC == 0
    whole = lambda shape: pl.BlockSpec((None,) + shape, lambda b, i: (b,) + (0,) * len(shape),
                                       pipeline_mode=pl.Buffered(1))
    oneh = jnp.asarray(np.arange(SEL_KC)[:, None] // SLC_LEN == np.arange(LANE)[None, :], BF16)
    return pl.pallas_call(
        _nsa_selected_body,
        out_shape=jax.ShapeDtypeStruct((B * S, HD), F32),
        grid=(B, S // TQ),
        in_specs=[pl.BlockSpec((None, HD, TQ), lambda b, i: (b, 0, i)),
                  whole((S, LANE)), pl.BlockSpec((None, TQ, LANE), lambda b, i: (b, i, 0)),
                  whole(vsl.shape[1:]),
                  pl.BlockSpec((None, NSA_G, n_sel, TQ), lambda b, i: (b, 0, 0, i)),
                  pl.BlockSpec((None, NSA_G, TQ // SLC_LEN, TQ), lambda b, i: (b, 0, i, i)),
                  pl.BlockSpec((None, LANE, TQ), lambda b, i: (b, 0, i)),
                  pl.BlockSpec((TQ, HD), lambda b, i: (b * (S // TQ) + i, 0)),
                  _const_spec((SEL_KC, LANE))],
        out_specs=pl.BlockSpec((TQ, HD), lambda b, i: (b * (S // TQ) + i, 0)),
        scratch_shapes=[pltpu.VMEM((SEL_KC, W), BF16)] * 4 + [pltpu.VMEM((1, W), F32)] * 3
                       + [pltpu.VMEM((VT_ROWS, W), F32)],
        compiler_params=_params(("parallel", "arbitrary")),
        name="nsa_selected",
    )(qt, ks, ks, vsl, selm, sel, gt, part, oneh)


def _merge_body(x_ref, oa_ref, ob_ref, oc_ref, gates_ref, wbr_ref, wo_ref, o_ref):
    D = x_ref.shape[1]
    y = jnp.zeros(x_ref.shape, F32)
    for j, br in enumerate((oa_ref, ob_ref, oc_ref)):
        y = y + _sigmoid(gates_ref[:, j * D:(j + 1) * D].astype(F32)) * _dot(br[...].astype(BF16), wbr_ref[j])
    o_ref[...] = x_ref[...] + _dot(y.astype(BF16), wo_ref[...])


def _merge(x, o_a, o_b, o_c, gates, wbr, wo, tm=512):
    T, D = x.shape
    row = lambda w: pl.BlockSpec((tm, w), lambda i: (i, 0))
    return pl.pallas_call(
        _merge_body,
        out_shape=jax.ShapeDtypeStruct((T, D), F32),
        grid=(T // tm,),
        in_specs=[row(D), row(MIX_W), row(MIX_W), row(MIX_W), row(3 * D), _const_spec(wbr.shape), _const_spec(wo.shape)],
        out_specs=row(D),
        compiler_params=_params(("parallel",)),
        name="merge_out",
    )(x, o_a, o_b, o_c, gates, wbr, wo)


def _pack_in_proj(w_in, b_in):
    ws, bs = [], []
    for _, off, w, wp in _SEGS:
        ws.append(jnp.pad(w_in[..., off:off + w], ((0, 0), (0, 0), (0, wp - w))))
        bs.append(jnp.pad(b_in[..., off:off + w], ((0, 0), (0, wp - w))))
    return jnp.concatenate(ws, axis=-1).astype(BF16), jnp.concatenate(bs, axis=-1)[:, None, :]


def _rope_tables(pos):
    half = NSA_D // 2
    freqs = ROPE_THETA ** (-jnp.arange(half, dtype=F32) / half)
    ang = pos.astype(F32)[:, None] * freqs[None, :]
    cos, sin = jnp.cos(ang), jnp.sin(ang)
    reps = LANE // NSA_D
    return jnp.tile(jnp.concatenate([cos, cos], axis=1), (1, reps)), jnp.tile(jnp.concatenate([-sin, sin], axis=1), (1, reps))


def _cmp_weights(w1, w2):
    L = w1.shape[0]
    w1r = w1.reshape(L, 2, 2, CMP_STRIDE, NSA_D, CMP_HIDDEN)
    eye_g = jnp.eye(NSA_G, dtype=w1.dtype)
    ex = jnp.einsum('lwstdh,gk->lwstgdkh', w1r, eye_g)
    ex = ex.reshape(L, 2, 2, CMP_STRIDE * NSA_G * NSA_D, NSA_G * CMP_HIDDEN)
    w2x = jnp.einsum('lwhd,gk->lwghkd', w2, eye_g).reshape(L, 2, NSA_G * CMP_HIDDEN, NSA_G * NSA_D)
    return ex[:, :, 0].astype(BF16), ex[:, :, 1].astype(BF16), w2x.astype(BF16)


def _cmp_pos_rows(pe):
    L = pe.shape[0]
    r = pe.reshape(L, 2, 2, CMP_STRIDE, 1, NSA_D)
    return jnp.broadcast_to(r, (L, 2, 2, CMP_STRIDE, NSA_G, NSA_D)).reshape(L, 2, 2, CMP_STRIDE * NSA_G * NSA_D)


def _lane_gain(g):
    return jnp.tile(g, LANE // g.shape[-1])[None, :]


def _layer(x, lw, consts, B, S):
    cos, sin, cos_c, sin_c, gavg = consts
    T = B * S
    x = _ffn(x, lw["ffn1_norm"], lw["ffn1_wg"], lw["ffn1_wu"], lw["ffn1_wd"])
    z = dict(zip([s[0] for s in _SEGS], _proj(x, lw["mix_norm"], lw["w_in"], lw["b_in"])))
    o_a = _gla(z["a_q"], z["a_k"], z["a_v"], z["a_r"], z["a_lr"], lw["gla_wa"], lw["gla_ba"], lw["gla_gn"], B, S)
    o_c = _mlstm(z["c_qk"], z["c_v"], z["c_o"], z["c_if"], lw["conv_w"], lw["conv_b"], B, S)
    qn = lw["nsa_qk_norm"]
    qt, ks, kw, vsl, vwd, gt = _nsa_prep(z["b_q"], z["b_ks"], z["b_kw"], z["b_vs"], z["b_vw"], z["b_g"], cos, sin,
                                         _lane_gain(qn[0]), _lane_gain(qn[2]), _lane_gain(qn[3]), gavg, B, S)
    n_blk = S // CMP_STRIDE
    xk = z["b_kc"].reshape(B, n_blk, CMP_STRIDE * LANE)
    xv = z["b_vc"].reshape(B, n_blk, CMP_STRIDE * LANE)
    kc, vct = _nsa_cmp(xk, xv, lw["cmp_pe"], lw["cmp_wt"], lw["cmp_wb"], lw["cmp_w2"], cos_c, sin_c,
                       _lane_gain(qn[1]), gavg)
    part, sel, selm = _nsa_select(qt, kc, vct, kw, vwd, gt, B, S)
    o_b = _nsa_selected(qt, ks, vsl, selm, sel, gt, part, B, S)
    x = _merge(x, o_a, o_b, o_c, z["gates"], lw["w_branch"], lw["w_out"])
    return _ffn(x, lw["ffn2_norm"], lw["ffn2_wg"], lw["ffn2_wu"], lw["ffn2_wd"])


def kernel(x, ffn1_norm, ffn1_w_gate, ffn1_w_up, ffn1_w_down, mix_norm, w_in, b_in, gla_w_alpha, gla_b_alpha, gla_out_norm, nsa_qk_norm, nsa_cmp_pos, nsa_cmp_w1, nsa_cmp_w2, mlstm_conv_w, mlstm_conv_b, w_branch, w_out, ffn2_norm, ffn2_w_gate, ffn2_w_up, ffn2_w_down):
    B, S, D = x.shape
    w_in_p, b_in_p = _pack_in_proj(w_in, b_in)
    cmp_wt, cmp_wb, cmp_w2 = _cmp_weights(nsa_cmp_w1, nsa_cmp_w2)
    layers = {
        "ffn1_norm": ffn1_norm[:, None, :], "ffn1_wg": ffn1_w_gate.astype(BF16), "ffn1_wu": ffn1_w_up.astype(BF16),
        "ffn1_wd": ffn1_w_down.astype(BF16),
        "mix_norm": mix_norm[:, None, :], "w_in": w_in_p, "b_in": b_in_p,
        "gla_wa": jnp.pad(gla_w_alpha, ((0, 0), (0, LANE - GLA_RANK), (0, 0))), "gla_ba": gla_b_alpha[:, None, :],
        "gla_gn": gla_out_norm[:, None, :],
        "nsa_qk_norm": nsa_qk_norm, "cmp_pe": _cmp_pos_rows(nsa_cmp_pos), "cmp_wt": cmp_wt, "cmp_wb": cmp_wb,
        "cmp_w2": cmp_w2,
        "conv_w": mlstm_conv_w, "conv_b": mlstm_conv_b[:, None, :],
        "w_branch": w_branch.astype(BF16), "w_out": w_out.astype(BF16),
        "ffn2_norm": ffn2_norm[:, None, :], "ffn2_wg": ffn2_w_gate.astype(BF16), "ffn2_wu": ffn2_w_up.astype(BF16),
        "ffn2_wd": ffn2_w_down.astype(BF16),
    }
    n_blk = S // CMP_STRIDE
    cos, sin = _rope_tables(jnp.arange(S))
    cos_c, sin_c = _rope_tables(jnp.arange(n_blk) * CMP_STRIDE + CMP_LEN - 1)
    gavg = jnp.asarray(np.kron(np.eye(LANE // NSA_D), np.full((NSA_D, NSA_D), 1.0 / NSA_D)), BF16)
    consts = (cos, sin, cos_c, sin_c, gavg)

    out = x.reshape(B * S, D)
    for l in range(w_in.shape[0]):
        out = _layer(out, {name: w[l] for name, w in layers.items()}, consts, B, S)
    return out.reshape(B, S, D)
```

```python
import functools

import numpy as np
import jax
import jax.numpy as jnp
from jax import lax
from jax.experimental import pallas as pl
from jax.experimental.pallas import tpu as pltpu

F32 = jnp.float32
BF16 = jnp.bfloat16

RMS_EPS = 1e-6
ROPE_THETA = 10000.0

GLA_H, GLA_DK, GLA_DV, GLA_RANK, GLA_GATE_NORM = 4, 64, 128, 16, 16.0
GLA_SUB = 16
NSA_H, NSA_G, NSA_D = 8, 2, 64
CMP_LEN, CMP_STRIDE, CMP_HIDDEN = 32, 16, 256
SLC_LEN, SLC_TOPK, WIN = 64, 16, 512
FORCED_SCORE = 1e4
MLSTM_H, MLSTM_DK, MLSTM_DV, CONV_W = 4, 64, 128, 4
MLSTM_CHUNK = 64
MIX_W = 512
LANE = 128
MXU_N = 256
NEG = -1e30
LOG2E = 1.4426950408889634
SEL_KC = 512
SEL_TQ = 512
PS_PAD = 8
VT_ROWS = 80

_SEGS = (
    ("a_q", 0, 256, 256), ("a_k", 256, 256, 256), ("a_v", 512, 512, 512), ("a_r", 1024, 512, 512),
    ("b_q", 1552, 512, 512), ("c_qk", 2856, 512, 512), ("c_v", 3368, 512, 512), ("c_o", 3880, 512, 512),
    ("gates", 4400, 3072, 3072),
    ("a_lr", 1536, 16, 128), ("b_kc", 2064, 128, 128), ("b_vc", 2192, 128, 128), ("b_ks", 2320, 128, 128),
    ("b_vs", 2448, 128, 128), ("b_kw", 2576, 128, 128), ("b_vw", 2704, 128, 128), ("b_g", 2832, 24, 128),
    ("c_if", 4392, 8, 128),
)
_N_PACK = sum(s[3] for s in _SEGS)
_BF16_SEGS = ("gates", "c_v", "b_vs", "b_vw")


def _dot(a, b):
    return jnp.dot(a, b, preferred_element_type=F32)


def _dot_nt(a, b):
    return lax.dot_general(a, b, (((1,), (1,)), ((), ())), preferred_element_type=F32)


def _split2(a):
    hi = a.astype(BF16)
    lo = (a - hi.astype(F32)).astype(BF16)
    return hi, lo


def _dot_l2(a, b):
    hi, lo = _split2(a)
    return _dot(hi, b) + _dot(lo, b)


def _dot_r2(a, b):
    hi, lo = _split2(b)
    return _dot(a, hi) + _dot(a, lo)


def _log_sigmoid(x):
    return jnp.minimum(x, 0.0) - jnp.log(1.0 + jnp.exp(-jnp.abs(x)))


def _sigmoid(x):
    return 1.0 / (1.0 + jnp.exp(-x))


def _silu(x):
    return x * _sigmoid(x)


def _iota(shape, dim):
    return lax.broadcasted_iota(jnp.int32, shape, dim)


def _const_spec(shape):
    nd = len(shape)
    return pl.BlockSpec(shape, lambda *_: (0,) * nd, pipeline_mode=pl.Buffered(1))


def _params(sem, vmem_mb=56):
    return pltpu.CompilerParams(dimension_semantics=sem, vmem_limit_bytes=vmem_mb * 1024 * 1024)


def _ffn_body(x_ref, g_ref, wg_ref, wu_ref, wd_ref, o_ref, *, bounds):
    x = x_ref[...]
    ms = jnp.mean(x * x, axis=-1, keepdims=True)
    h = (x * lax.rsqrt(ms + RMS_EPS) * g_ref[...]).astype(BF16)
    acc = jnp.zeros(x.shape, F32)
    for lo, hi in zip(bounds[:-1], bounds[1:]):
        a = _dot(h, wg_ref[:, lo:hi])
        u = _dot(h, wu_ref[:, lo:hi])
        t = (_silu(a) * u).astype(BF16)
        acc = acc + _dot(t, wd_ref[lo:hi, :])
    o_ref[...] = x + 0.5 * acc


def _ffn(x, g, wg, wu, wd, tm=512):
    T, D = x.shape
    F = wg.shape[1]
    bounds = (0, -(-F // (2 * MXU_N)) * MXU_N, F)
    return pl.pallas_call(
        functools.partial(_ffn_body, bounds=bounds),
        out_shape=jax.ShapeDtypeStruct((T, D), F32),
        grid=(T // tm,),
        in_specs=[pl.BlockSpec((tm, D), lambda i: (i, 0)), _const_spec((1, D)),
                  _const_spec((D, F)), _const_spec((D, F)), _const_spec((F, D))],
        out_specs=pl.BlockSpec((tm, D), lambda i: (i, 0)),
        compiler_params=_params(("parallel",)),
        name="ffn",
    )(x, g, wg, wu, wd)


def _proj_body(x_ref, g_ref, w_ref, b_ref, *o_refs):
    x = x_ref[...]
    ms = jnp.mean(x * x, axis=-1, keepdims=True)
    h = (x * lax.rsqrt(ms + RMS_EPS) * g_ref[...]).astype(BF16)
    off, i = 0, 0
    while i < len(o_refs):
        group, w = [], 0
        while i < len(o_refs) and (not group or w % MXU_N):
            group.append(o_refs[i])
            w += o_refs[i].shape[1]
            i += 1
        z = _dot(h, w_ref[:, off:off + w]) + b_ref[:, off:off + w]
        c = 0
        for o_ref in group:
            o_ref[...] = z[:, c:c + o_ref.shape[1]].astype(o_ref.dtype)
            c += o_ref.shape[1]
        off += w


def _proj(x, g, w, b, tm=256):
    T, D = x.shape
    return pl.pallas_call(
        _proj_body,
        out_shape=[jax.ShapeDtypeStruct((T, s[3]), BF16 if s[0] in _BF16_SEGS else F32) for s in _SEGS],
        grid=(T // tm,),
        in_specs=[pl.BlockSpec((tm, D), lambda i: (i, 0)), _const_spec((1, D)),
                  _const_spec((D, _N_PACK)), _const_spec((1, _N_PACK))],
        out_specs=[pl.BlockSpec((tm, s[3]), lambda i: (i, 0)) for s in _SEGS],
        compiler_params=_params(("parallel",)),
        name="in_proj",
    )(x, g, w, b)


def _gla_body(q_ref, k_ref, v_ref, r_ref, lr_ref, wa_ref, ba_ref, gn_ref, tri_ref, bones_ref, eh_ref,
              o_ref, st_ref, qs_ref, c_ref, tot_ref, kst_ref, vt4_ref, oi_ref):
    nB, Lc = q_ref.shape[0], q_ref.shape[1]
    n_sub = Lc // GLA_SUB
    HK = GLA_H * GLA_DK

    @pl.when(pl.program_id(0) == 0)
    def _():
        st_ref[...] = jnp.zeros(st_ref.shape, F32)

    lane_h = _iota((1, HK), 1) >> (GLA_DK.bit_length() - 1)
    row_i = _iota((GLA_SUB, 1), 0)
    col_r = _iota((1, GLA_H * Lc), 1) & (Lc - 1)
    wa_hi, wa_lo = _split2(wa_ref[...])

    for b in range(nB):
        lr_hi, lr_lo = _split2(lr_ref[b])
        u = _dot(lr_hi, wa_hi) + _dot(lr_hi, wa_lo) + _dot(lr_lo, wa_hi) + ba_ref[...]
        g = _log_sigmoid(u) * (1.0 / GLA_GATE_NORM)
        c = _dot_r2(tri_ref[...], g)
        tot = _dot_r2(bones_ref[...], g)
        qs_ref[b] = q_ref[b] * (GLA_DK ** -0.5)
        c_ref[b] = c
        tot_ref[b] = tot
        kt = k_ref[b] * jnp.exp(tot - c)
        vt = v_ref[b].T.astype(BF16)
        for h in range(GLA_H):
            kst_ref[b, h * Lc:(h + 1) * Lc, :] = jnp.where(lane_h == h, kt, 0.0).astype(BF16)
            vt4_ref[b, :, h * Lc:(h + 1) * Lc] = vt[h * GLA_DV:(h + 1) * GLA_DV, :]

    def sub(b, s):
        rows = slice(s * GLA_SUB, (s + 1) * GLA_SUB)
        qs = qs_ref[b, rows, :]
        cs = c_ref[b, rows, :]
        st = st_ref[b]
        qd = qs * jnp.exp(cs)
        q4 = jnp.concatenate([jnp.where(lane_h == h, qd, 0.0) for h in range(GLA_H)], axis=0).astype(BF16)
        inter4 = _dot_nt(q4, st.astype(BF16))
        inter = jnp.concatenate([inter4[h * GLA_SUB:(h + 1) * GLA_SUB, :] for h in range(GLA_H)], axis=1)
        xs = []
        for j in range(GLA_SUB):
            r = s * GLA_SUB + j
            x = qs * k_ref[b, r:r + 1, :] * jnp.exp(jnp.minimum(cs - c_ref[b, r:r + 1, :], 0.0))
            xs.append(jnp.where(row_i >= j, x, 0.0))
        r_all = _dot(jnp.concatenate(xs, axis=0).astype(BF16), eh_ref[...])
        intra = jnp.zeros((GLA_SUB, r_all.shape[1]), F32)
        for j in range(GLA_SUB):
            r = s * GLA_SUB + j
            intra = intra + r_all[j * GLA_SUB:(j + 1) * GLA_SUB, :] * v_ref[b, r:r + 1, :]
        oi_ref[b, rows, :] = inter + intra
        dec = jnp.exp(tot_ref[b, s * GLA_SUB:s * GLA_SUB + 1, :])
        in_sub = (col_r >= s * GLA_SUB) & (col_r < (s + 1) * GLA_SUB)
        vtm = jnp.where(in_sub, vt4_ref[b], jnp.zeros((), BF16))
        st_ref[b] = dec * st + _dot(vtm, kst_ref[b])

    for s in range(n_sub):
        for b in range(nB):
            sub(b, s)

    gn = gn_ref[...]
    for b in range(nB):
        for h in range(GLA_H):
            sl = slice(h * GLA_DV, (h + 1) * GLA_DV)
            o = oi_ref[b, :, sl]
            ms = jnp.mean(o * o, axis=-1, keepdims=True)
            o_ref[b, :, sl] = o * lax.rsqrt(ms + RMS_EPS) * gn * _silu(r_ref[b, :, sl])


def _gla_consts(Lc):
    r = np.arange(Lc)
    same = (r[:, None] // GLA_SUB) == (r[None, :] // GLA_SUB)
    tri = (same & (r[None, :] <= r[:, None])).astype(np.float32)
    bones = same.astype(np.float32)
    hk = np.arange(GLA_H * GLA_DK) // GLA_DK
    hv = np.arange(GLA_H * GLA_DV) // GLA_DV
    eh = (hk[:, None] == hv[None, :]).astype(np.float32)
    return jnp.asarray(tri, BF16), jnp.asarray(bones, BF16), jnp.asarray(eh, BF16)


def _gla(a_q, a_k, a_v, a_r, a_lr, wa, ba, gn, B, S, Lc=128):
    HK, HV = GLA_H * GLA_DK, GLA_H * GLA_DV
    nb = S // Lc
    assert Lc & (Lc - 1) == 0
    tri, bones, eh = _gla_consts(Lc)
    row = lambda w: pl.BlockSpec((B, Lc, w), lambda i: (0, i, 0))
    seq = lambda t: t.reshape(B, S, t.shape[-1])
    out = pl.pallas_call(
        _gla_body,
        out_shape=jax.ShapeDtypeStruct((B, S, HV), F32),
        grid=(nb,),
        in_specs=[row(HK), row(HK), row(HV), row(HV), row(LANE),
                  _const_spec((LANE, HK)), _const_spec((1, HK)), _const_spec((1, GLA_DV)),
                  _const_spec((Lc, Lc)), _const_spec((Lc, Lc)), _const_spec((HK, HV))],
        out_specs=row(HV),
        scratch_shapes=[pltpu.VMEM((B, GLA_DV, HK), F32), pltpu.VMEM((B, Lc, HK), F32), pltpu.VMEM((B, Lc, HK), F32),
                        pltpu.VMEM((B, Lc, HK), F32), pltpu.VMEM((B, GLA_H * Lc, HK), BF16),
                        pltpu.VMEM((B, GLA_DV, GLA_H * Lc), BF16), pltpu.VMEM((B, Lc, HV), F32)],
        compiler_params=_params(("arbitrary",)),
        name="gla",
    )(seq(a_q), seq(a_k), seq(a_v), seq(a_r), seq(a_lr), wa, ba, gn, tri, bones, eh)
    return out.reshape(B * S, HV)


def _mlstm_body(qk_ref, v_ref, og_ref, if_ref, cw_ref, cb_ref, tri_ref, bdm_ref,
                o_ref, xx_ref, ct_ref, n_ref, m_ref, kw_ref, vt_ref, act_ref):
    nB, Lc = v_ref.shape[0], v_ref.shape[1]
    L = MLSTM_CHUNK
    HK = MLSTM_H * MLSTM_DK
    tail = 8
    lane_hk = _iota((1, HK), 1) >> (MLSTM_DK.bit_length() - 1)
    col_l = _iota((1, Lc), 1)
    lane_w = _iota((1, LANE), 1)
    row_l = _iota((L, 1), 0)
    per_win = LANE // L

    @pl.when(pl.program_id(0) == 0)
    def _():
        ct_ref[...] = jnp.zeros(ct_ref.shape, F32)
        n_ref[...] = jnp.zeros(n_ref.shape, F32)
        m_ref[...] = jnp.zeros(m_ref.shape, F32)
        xx_ref[:, 0:tail, :] = jnp.zeros((nB, tail, xx_ref.shape[2]), F32)

    def block_setup(b):
        xx, act = xx_ref.at[b], act_ref.at[b]
        xx[tail:tail + Lc, :] = qk_ref[b]
        y = jnp.zeros((Lc, 2 * HK), F32) + cb_ref[...]
        for kk in range(CONV_W):
            y = y + cw_ref[kk:kk + 1, :] * xx[pl.ds(tail - (CONV_W - 1) + kk, Lc), :]
        xx[0:tail, :] = xx[Lc:Lc + tail, :]
        act[...] = _silu(y)
        gates = if_ref[b]
        bcum = _dot_r2(tri_ref[...], _log_sigmoid(gates))
        qt_hi, qt_lo = _split2(act[:, 0:HK].T)
        kw_ref[b] = jnp.zeros(kw_ref.shape[1:], BF16)
        vt_ref[b] = v_ref[b].astype(F32).T.astype(BF16)
        return gates, bcum, bcum.T, qt_hi, qt_lo

    setups = [block_setup(b) for b in range(nB)]
    batched = (act_ref, ct_ref, n_ref, m_ref, kw_ref, vt_ref, og_ref, o_ref)

    def chunk(b, ci):
        gates, bcum, b_t, qt_hi, qt_lo = setups[b]
        act_ref, ct_ref, n_ref, m_ref, kw_ref, vt_ref, og_ref, o_ref = (r.at[b] for r in batched)
        p, c = divmod(ci, per_win)
        rows = slice(ci * L, (ci + 1) * L)
        win = slice(p * LANE, (p + 1) * LANE)
        causal_t = ((c * L + row_l) <= lane_w) & ((lane_w >> (L.bit_length() - 1)) == c)
        k_all = act_ref[rows, HK:2 * HK] * (MLSTM_DK ** -0.5)
        k_bf = k_all.astype(BF16)
        ct = ct_ref[...]
        ct_bf = ct.astype(BF16)
        n_row = n_ref[...]
        n_hi, n_lo = _split2(n_row)
        n8 = jnp.concatenate([jnp.where(lane_hk == h, part, jnp.zeros((), BF16))
                              for part in (n_hi, n_lo) for h in range(MLSTM_H)], axis=0)
        qn8 = _dot(n8, qt_hi[:, win]) + _dot(n8, qt_lo[:, win])
        wk_full = jnp.zeros((L, HK), F32)
        dec_row = jnp.zeros((1, HK), F32)
        for h in range(MLSTM_H):
            icol = gates[rows, h:h + 1]
            bcol = bcum[rows, MLSTM_H + h:MLSTM_H + h + 1]
            brow = b_t[MLSTM_H + h:MLSTM_H + h + 1, win]
            m_st = m_ref[0:1, h:h + 1]
            dmat_t = jnp.where(causal_t, brow + (icol - bcol), -jnp.inf)
            inter_log = brow + m_st
            m_row = jnp.maximum(inter_log, jnp.max(dmat_t, axis=0, keepdims=True))
            w_inter = jnp.exp(inter_log - m_row)
            head_rows = slice(h * MLSTM_DK, (h + 1) * MLSTM_DK)
            zero_q = jnp.zeros((MLSTM_DK, LANE), BF16)
            qh = jnp.concatenate([qt_hi[head_rows, win] if hh == h else zero_q for hh in range(MLSTM_H)], axis=0)
            s_qk = _dot(k_bf, qh) * jnp.exp(dmat_t - m_row)
            den = w_inter * (qn8[h:h + 1, :] + qn8[MLSTM_H + h:MLSTM_H + h + 1, :]) + jnp.sum(s_qk, axis=0, keepdims=True)
            s_bf = s_qk.astype(BF16)
            pads = [jnp.zeros((L, LANE), BF16)] * per_win
            pads[c] = s_bf
            dv = slice(h * MLSTM_DV, (h + 1) * MLSTM_DV)
            num_t = w_inter * _dot(ct_bf[dv, :], qt_hi[:, win]) + _dot(vt_ref[dv, win], jnp.concatenate(pads, axis=0))
            h_t = num_t / jnp.maximum(jnp.abs(den), jnp.exp(-m_row))
            o_ref[rows, dv] = h_t.T[c * L:(c + 1) * L, :] * _sigmoid(og_ref[rows, dv])
            m_new = m_row[:, c * L + L - 1:c * L + L]
            b_last = bcol[L - 1:L, :]
            w_k = jnp.exp(b_last - bcol + icol - m_new)
            decay = jnp.exp(b_last + m_st - m_new)
            head = lane_hk == h
            wk_full = wk_full + jnp.where(head, w_k, 0.0)
            dec_row = dec_row + jnp.where(head, decay, 0.0)
            m_ref[0:1, h:h + 1] = m_new
        kw = k_all * wk_full
        kw_ref[rows, :] = kw.astype(BF16)
        in_chunk = (col_l >= ci * L) & (col_l < (ci + 1) * L)
        vtm = jnp.where(in_chunk, vt_ref[...], jnp.zeros((), BF16))
        ct_ref[...] = dec_row * ct + _dot(vtm, kw_ref[...]) * bdm_ref[...]
        n_ref[...] = dec_row * n_row + jnp.sum(kw, axis=0, keepdims=True)

    for ci in range(Lc // L):
        for b in range(nB):
            chunk(b, ci)


def _mlstm_consts(Lc):
    r = np.arange(Lc)
    same = (r[:, None] // MLSTM_CHUNK) == (r[None, :] // MLSTM_CHUNK)
    tri = (same & (r[None, :] <= r[:, None])).astype(np.float32)
    hk = np.arange(MLSTM_H * MLSTM_DK) // MLSTM_DK
    hv = np.arange(MLSTM_H * MLSTM_DV) // MLSTM_DV
    eh = (hk[:, None] == hv[None, :]).astype(np.float32)
    return jnp.asarray(tri, BF16), jnp.asarray(eh.T.copy(), F32)


def _mlstm(c_qk, c_v, c_o, c_if, cw, cb, B, S, Lc=256):
    HK, HV = MLSTM_H * MLSTM_DK, MLSTM_H * MLSTM_DV
    nb = S // Lc
    assert Lc % LANE == 0 and LANE % MLSTM_CHUNK == 0
    tri, bdm = _mlstm_consts(Lc)
    row = lambda w: pl.BlockSpec((B, Lc, w), lambda i: (0, i, 0))
    seq = lambda t: t.reshape(B, S, t.shape[-1])
    out = pl.pallas_call(
        _mlstm_body,
        out_shape=jax.ShapeDtypeStruct((B, S, HV), F32),
        grid=(nb,),
        in_specs=[row(2 * HK), row(HV), row(HV), row(LANE),
                  _const_spec((CONV_W, 2 * HK)), _const_spec((1, 2 * HK)),
                  _const_spec((Lc, Lc)), _const_spec((HV, HK))],
        out_specs=row(HV),
        scratch_shapes=[pltpu.VMEM((B, Lc + 8, 2 * HK), F32), pltpu.VMEM((B, HV, HK), F32),
                        pltpu.VMEM((B, 1, HK), F32), pltpu.VMEM((B, 8, LANE), F32), pltpu.VMEM((B, Lc, HK), BF16),
                        pltpu.VMEM((B, HV, Lc), BF16), pltpu.VMEM((B, Lc, 2 * HK), F32)],
        compiler_params=_params(("arbitrary",)),
        name="mlstm",
    )(seq(c_qk), seq(c_v), seq(c_o), seq(c_if), cw, cb, tri, bdm)
    return out.reshape(B * S, HV)


def _group_rms(x, gavg, gain):
    ms = _dot_l2(x * x, gavg)
    return x * lax.rsqrt(ms + RMS_EPS) * gain


def _rope_lanes(x, cos, sin_signed):
    half = NSA_D // 2
    first = (_iota((1, LANE), 1) & (NSA_D - 1)) < half
    swapped = jnp.where(first, pltpu.roll(x, LANE - half, 1), pltpu.roll(x, half, 1))
    return x * cos + swapped * sin_signed


def _store_vt_tiles(dst_ref, vt, width):
    ones = jnp.ones((VT_ROWS - NSA_D, width), BF16)
    for j in range(vt.shape[1] // width):
        for g in range(NSA_G):
            dst_ref[j, g, 0:NSA_D, :] = vt[g * NSA_D:(g + 1) * NSA_D, j * width:(j + 1) * width]
            dst_ref[j, g, NSA_D:VT_ROWS, :] = ones


def _nsa_prep_body(q_ref, ks_ref, kw_ref, vs_ref, vw_ref, g_ref, cos_ref, sin_ref, gq_ref, gs_ref, gw_ref, gavg_ref,
                   qt_ref, kso_ref, kwo_ref, vsl_ref, vwd_ref, gt_ref):
    cos, sin = cos_ref[...], sin_ref[...]
    gavg = gavg_ref[...]
    for cb in range(NSA_H * NSA_D // LANE):
        sl = slice(cb * LANE, (cb + 1) * LANE)
        qn = _rope_lanes(_group_rms(q_ref[:, sl], gavg, gq_ref[...]), cos, sin) * (NSA_D ** -0.5 * LOG2E)
        qt_ref[sl, :] = qn.T.astype(BF16)
    kso_ref[...] = _rope_lanes(_group_rms(ks_ref[...], gavg, gs_ref[...]), cos, sin).astype(BF16)
    kwo_ref[...] = _rope_lanes(_group_rms(kw_ref[...], gavg, gw_ref[...]), cos, sin).astype(BF16)
    vst = vs_ref[...].astype(F32).T.astype(BF16)
    _store_vt_tiles(vsl_ref, vst, SEL_KC)
    _store_vt_tiles(vwd_ref, vw_ref[...].astype(F32).T.astype(BF16), LANE)
    gt_ref[...] = _sigmoid(g_ref[...]).T


def _nsa_prep(b_q, b_ks, b_kw, b_vs, b_vw, b_g, cos, sin, gq, gs, gw, gavg, B, S, tm=SEL_KC):
    nb = S // tm
    row = lambda w: pl.BlockSpec((tm, w), lambda b, i: (b * nb + i, 0))
    tab = pl.BlockSpec((tm, LANE), lambda b, i: (i, 0))
    HD = NSA_H * NSA_D
    vt_shape = lambda width: jax.ShapeDtypeStruct((B, S // width, NSA_G, VT_ROWS, width), BF16)
    vt_spec = lambda width: pl.BlockSpec((None, tm // width, NSA_G, VT_ROWS, width), lambda b, i: (b, i, 0, 0, 0))
    return pl.pallas_call(
        _nsa_prep_body,
        out_shape=[jax.ShapeDtypeStruct((B, HD, S), BF16),
                   jax.ShapeDtypeStruct((B, S, LANE), BF16),
                   jax.ShapeDtypeStruct((B, S, LANE), BF16),
                   vt_shape(SEL_KC),
                   vt_shape(LANE),
                   jax.ShapeDtypeStruct((B, LANE, S), F32)],
        grid=(B, nb),
        in_specs=[row(HD), row(LANE), row(LANE), row(LANE), row(LANE), row(LANE), tab, tab,
                  _const_spec((1, LANE)), _const_spec((1, LANE)), _const_spec((1, LANE)), _const_spec((LANE, LANE))],
        out_specs=[pl.BlockSpec((None, HD, tm), lambda b, i: (b, 0, i)),
                   pl.BlockSpec((None, tm, LANE), lambda b, i: (b, i, 0)),
                   pl.BlockSpec((None, tm, LANE), lambda b, i: (b, i, 0)),
                   vt_spec(SEL_KC), vt_spec(LANE),
                   pl.BlockSpec((None, LANE, tm), lambda b, i: (b, 0, i))],
        compiler_params=_params(("parallel", "parallel")),
        name="nsa_prep",
    )(b_q, b_ks, b_kw, b_vs, b_vw, b_g, cos, sin, gq, gs, gw, gavg)


def _gelu_tanh(x):
    return 0.5 * x * (1.0 + jnp.tanh(0.7978845608028654 * (x + 0.044715 * x * x * x)))


def _nsa_cmp_body(xk_ref, xv_ref, pe_ref, wt_ref, wb_ref, w2_ref, cos_ref, sin_ref, gk_ref, gavg_ref,
                  kc_ref, vct_ref):
    n = xk_ref.shape[0]

    def compress(x, which):
        u = _dot((x + pe_ref[which, 0:1, :]).astype(BF16), wt_ref[which])
        v = _dot((x + pe_ref[which, 1:2, :]).astype(BF16), wb_ref[which])
        hid = u + pltpu.roll(v, n - 1, 0)
        return _dot(_gelu_tanh(hid).astype(BF16), w2_ref[which])

    ck = compress(xk_ref[...], 0)
    kc_ref[...] = _rope_lanes(_group_rms(ck, gavg_ref[...], gk_ref[...]), cos_ref[...], sin_ref[...]).astype(BF16)
    vct = compress(xv_ref[...], 1).T.astype(BF16)
    for g in range(NSA_G):
        vct_ref[g] = vct[g * NSA_D:(g + 1) * NSA_D, :]


def _nsa_cmp(xk, xv, pe, wt, wb, w2, cos, sin, gk, gavg):
    B, n, W = xk.shape
    return pl.pallas_call(
        _nsa_cmp_body,
        out_shape=[jax.ShapeDtypeStruct((B, n, LANE), BF16),
                   jax.ShapeDtypeStruct((B, NSA_G, NSA_D, n), BF16)],
        grid=(B,),
        in_specs=[pl.BlockSpec((None, n, W), lambda b: (b, 0, 0)), pl.BlockSpec((None, n, W), lambda b: (b, 0, 0)),
                  _const_spec(pe.shape), _const_spec(wt.shape), _const_spec(wb.shape), _const_spec(w2.shape),
                  _const_spec((n, LANE)), _const_spec((n, LANE)), _const_spec((1, LANE)), _const_spec((LANE, LANE))],
        out_specs=[pl.BlockSpec((None, n, LANE), lambda b: (b, 0, 0)),
                   pl.BlockSpec((None, NSA_G, NSA_D, n), lambda b: (b, 0, 0, 0))],
        compiler_params=_params(("parallel",)),
        name="nsa_compress",
    )(xk, xv, pe, wt, wb, w2, cos, sin, gk, gavg)


def _group_queries(qt_ref, g):
    HPG = NSA_H // NSA_G
    q4 = jnp.concatenate([qt_ref[(g * HPG + h) * NSA_D:(g * HPG + h + 1) * NSA_D, :] for h in range(HPG)], axis=1)
    parts = [jnp.zeros(q4.shape, BF16)] * NSA_G
    parts[g] = q4
    return jnp.concatenate(parts, axis=0)


def _bitonic_merge_desc(xs):
    xs = list(xs)
    j = len(xs) // 2
    while j >= 1:
        for a in range(len(xs)):
            b = a ^ j
            if b > a:
                xs[a], xs[b] = jnp.maximum(xs[a], xs[b]), jnp.minimum(xs[a], xs[b])
        j //= 2
    return xs


def _bitonic_sort_desc(xs):
    if len(xs) == 1:
        return list(xs)
    half = len(xs) // 2
    lo = _bitonic_sort_desc(xs[:half])
    hi = _bitonic_sort_desc(xs[half:])
    return _bitonic_merge_desc(lo + hi[::-1])


def _kth_largest(score, k):
    SUBLANES = 8
    tiles = [score[SUBLANES * v:SUBLANES * (v + 1), :] for v in range(score.shape[0] // SUBLANES)]
    size = max(k, 1 << (len(tiles) - 1).bit_length())
    tiles = tiles + [jnp.full(tiles[0].shape, NEG, F32)] * (size - len(tiles))

    def top_of_two(a, b):
        return _bitonic_merge_desc([jnp.maximum(a[r], b[k - 1 - r]) for r in range(k)])

    tops = [_bitonic_sort_desc(tiles[c:c + k]) for c in range(0, size, k)]
    while len(tops) > 1:
        tops = [top_of_two(tops[c], tops[c + 1]) for c in range(0, len(tops), 2)]
    top = tops[0]
    shift = SUBLANES // 2
    while shift >= 1:
        top = top_of_two(top, [pltpu.roll(t, shift, 0) for t in top])
        shift //= 2
    return top[k - 1][0:1, :]


def _nsa_select_body(qt_ref, kc_ref, vct_ref, kw_ref, vwd_ref, gt_ref, tril_ref, part_ref, sel_ref, selm_ref, ps_buf,
                     *, sel_tile):
    TQ = LANE
    HPG = NSA_H // NSA_G
    W = HPG * TQ
    n_cmp = kc_ref.shape[0]
    n_sel = sel_ref.shape[1]
    i = pl.program_id(1)
    s0 = i * TQ
    t_row = s0 + _iota((1, TQ), 1)
    n_win = WIN // LANE + 1
    j0 = jnp.maximum(i - (n_win - 1), 0)
    w_start = pl.multiple_of(j0 * LANE, LANE)
    own_first = (s0 // sel_tile) * (sel_tile // SLC_LEN)
    ps_buf[:, 0:PS_PAD, :] = jnp.zeros((NSA_G, PS_PAD, TQ), F32)

    for g in range(NSA_G):
        qpad = _group_queries(qt_ref, g)

        cend = _iota((n_cmp, 1), 0) * CMP_STRIDE + (CMP_LEN - 1)
        bias_c = jnp.where(cend <= t_row, 0.0, NEG)
        sc = _dot(kc_ref[...], qpad) + jnp.concatenate([bias_c] * HPG, axis=1)
        m = jnp.max(sc, axis=0, keepdims=True)
        m = jnp.where(m > 0.5 * NEG, m, 0.0)
        p = jnp.exp2(sc - m)
        p = p * (1.0 / jnp.maximum(jnp.sum(p, axis=0, keepdims=True), 1e-30))
        o_c = _dot(vct_ref[g], p.astype(BF16))
        psum = p[:, 0:TQ]
        for h in range(1, HPG):
            psum = psum + p[:, h * TQ:(h + 1) * TQ]
        ps_buf[g, PS_PAD:PS_PAD + n_cmp, :] = psum

        per = SLC_LEN // CMP_STRIDE
        imp = ps_buf[g, pl.ds(PS_PAD - 1, n_sel, stride=per), :]
        for jj in range(per):
            imp = imp + ps_buf[g, pl.ds(PS_PAD + jj, n_sel, stride=per), :]
        blk = _iota((n_sel, 1), 0)
        cur = t_row >> (SLC_LEN.bit_length() - 1)
        valid = blk * SLC_LEN <= t_row
        forced = (blk == 0) | (blk == cur) | (blk == cur - 1)
        score0 = jnp.where(valid, jnp.where(forced, FORCED_SCORE, imp), NEG)
        kth = _kth_largest(score0, min(SLC_TOPK, n_sel))
        above = score0 > kth
        tied = score0 == kth
        need = min(SLC_TOPK, n_sel) - jnp.sum(jnp.where(above, 1.0, 0.0), axis=0, keepdims=True)
        rank = _dot(tril_ref[...], jnp.where(tied, 1.0, 0.0).astype(BF16))
        chosen = (above | (tied & (rank <= need))) & valid
        sel_ref[g] = jnp.where(chosen, 0.0, NEG)
        selm_ref[g] = jnp.where(chosen & (blk < own_first), 0.0, NEG)

        kpos = w_start + _iota((n_win * LANE, 1), 0)
        bias_w = jnp.where((kpos <= t_row) & (kpos > t_row - WIN), 0.0, NEG)
        sw = _dot(kw_ref[pl.ds(w_start, n_win * LANE), :], qpad) + jnp.concatenate([bias_w] * HPG, axis=1)
        pw = jnp.exp2(sw - jnp.max(sw, axis=0, keepdims=True)).astype(BF16)
        acc_w = jnp.zeros((VT_ROWS, W), F32)
        for r in range(n_win):
            acc_w = acc_w + _dot(vwd_ref[j0 + r, g], pw[r * LANE:(r + 1) * LANE, :])
        o_w = acc_w[0:NSA_D, :] * (1.0 / acc_w[NSA_D:NSA_D + 1, :])

        for hp in range(HPG // 2):
            tiles = []
            for h in (2 * hp, 2 * hp + 1):
                gr = (g * HPG + h) * 3
                cs = slice(h * TQ, (h + 1) * TQ)
                tiles.append(gt_ref[gr:gr + 1, :] * o_c[:, cs] + gt_ref[gr + 2:gr + 3, :] * o_w[:, cs])
            col = (g * HPG + 2 * hp) * NSA_D
            part_ref[:, col:col + 2 * NSA_D] = jnp.concatenate(tiles, axis=0).T


def _nsa_select(qt, kc, vct, kw, vwd, gt, B, S):
    HD = NSA_H * NSA_D
    TQ = LANE
    n_cmp = kc.shape[1]
    n_sel = S // SLC_LEN
    assert S >= WIN + TQ and CMP_LEN == 2 * CMP_STRIDE
    whole = lambda shape: pl.BlockSpec((None,) + shape, lambda b, i: (b,) + (0,) * len(shape),
                                       pipeline_mode=pl.Buffered(1))
    mask_shape = jax.ShapeDtypeStruct((B, NSA_G, n_sel, S), F32)
    mask_spec = pl.BlockSpec((None, NSA_G, n_sel, TQ), lambda b, i: (b, 0, 0, i))
    tril = jnp.asarray(np.tril(np.ones((n_sel, n_sel), np.float32)), BF16)
    return pl.pallas_call(
        functools.partial(_nsa_select_body, sel_tile=SEL_TQ),
        out_shape=[jax.ShapeDtypeStruct((B * S, HD), F32), mask_shape, mask_shape],
        grid=(B, S // TQ),
        in_specs=[pl.BlockSpec((None, HD, TQ), lambda b, i: (b, 0, i)),
                  whole((n_cmp, LANE)), whole(vct.shape[1:]), whole((S, LANE)), whole(vwd.shape[1:]),
                  pl.BlockSpec((None, LANE, TQ), lambda b, i: (b, 0, i)), _const_spec((n_sel, n_sel))],
        out_specs=[pl.BlockSpec((TQ, HD), lambda b, i: (b * (S // TQ) + i, 0)), mask_spec, mask_spec],
        scratch_shapes=[pltpu.VMEM((NSA_G, n_cmp + PS_PAD, TQ), F32)],
        compiler_params=_params(("parallel", "parallel")),
        name="nsa_select",
    )(qt, kc, vct, kw, vwd, gt, tril)


def _nsa_selected_body(qt_ref, ks_ref, ksd_ref, vsl_ref, selm_ref, seld_ref, gt_ref, part_ref, oneh_ref, o_ref,
                       s_a, s_b, p_a, p_b, al_a, al_b, m_ref, acc_ref):
    TQ = qt_ref.shape[1]
    HPG = NSA_H // NSA_G
    W = HPG * TQ
    n_kc = ks_ref.shape[0] // SEL_KC
    bps = SEL_KC // SLC_LEN
    i = pl.program_id(1)
    n_main = i * (TQ // SEL_KC)
    n_pairs = (n_main + 1) // 2
    diag_ok = _iota((TQ, 1), 0) <= _iota((1, TQ), 1)
    pad_rows = jnp.zeros((LANE - 2 * bps, W), BF16)

    def q_aug(qpad, brows):
        brows = jnp.concatenate([jnp.concatenate([brows] * HPG, axis=1), jnp.zeros((bps, W), F32)], axis=0)
        return jnp.concatenate([qpad, brows.astype(BF16), pad_rows], axis=0)

    heads = [slice(h * TQ, (h + 1) * TQ) for h in range(HPG)]

    def softmax(s_buf, p_buf, al_ref, cs):
        sb = s_buf[:, cs]
        m_i = m_ref[:, cs]
        m_new = jnp.maximum(m_i, jnp.max(sb, axis=0, keepdims=True).astype(F32))
        p_buf[:, cs] = jnp.exp2(sb - m_new.astype(BF16))
        al_ref[:, cs] = jnp.exp2(m_i - m_new)
        m_ref[:, cs] = m_new

    def apply_values(p_buf, al_ref, vt, cs):
        acc_ref[:, cs] = al_ref[:, cs] * acc_ref[:, cs] + _dot(vt, p_buf[:, cs])

    for g in range(NSA_G):
        qpad = _group_queries(qt_ref, g)
        m_ref[...] = jnp.full(m_ref.shape, NEG, F32)
        acc_ref[...] = jnp.zeros(acc_ref.shape, F32)
        p_b[...] = jnp.zeros(p_b.shape, BF16)
        al_b[...] = jnp.ones(al_b.shape, F32)

        def step_operands(c):
            k0 = pl.multiple_of(c * SEL_KC, SEL_KC)
            brows = selm_ref[g, pl.ds(pl.multiple_of(c * bps, bps), bps), :]
            return jnp.concatenate([ks_ref[pl.ds(k0, SEL_KC), :], oneh_ref[...]], axis=1), q_aug(qpad, brows)

        def scores(ops, s_buf, cs):
            s_buf[:, cs] = _dot(ops[0], ops[1][:, cs]).astype(BF16)

        def pair(cc, carry):
            c0 = 2 * cc
            ops, vt = step_operands(c0 + 1), vsl_ref[jnp.maximum(c0 - 1, 0), g]
            for cs in heads:
                scores(ops, s_b, cs)
                softmax(s_a, p_a, al_a, cs)
                apply_values(p_b, al_b, vt, cs)
            ops, vt = step_operands(jnp.minimum(c0 + 2, n_kc - 1)), vsl_ref[c0, g]
            for cs in heads:
                scores(ops, s_a, cs)
                softmax(s_b, p_b, al_b, cs)
                apply_values(p_a, al_a, vt, cs)
            return carry

        ops = step_operands(0)
        for cs in heads:
            scores(ops, s_a, cs)
        lax.fori_loop(0, n_pairs, pair, 0)
        vt = vsl_ref[jnp.maximum(2 * n_pairs - 1, 0), g]
        for cs in heads:
            apply_values(p_b, al_b, vt, cs)

        for d in range(TQ // SEL_KC):
            k_aug = jnp.concatenate([ksd_ref[d * SEL_KC:(d + 1) * SEL_KC, :], oneh_ref[...]], axis=1)
            qa = q_aug(qpad, seld_ref[g, d * bps:(d + 1) * bps, :])
            vt = vsl_ref[i * (TQ // SEL_KC) + d, g]
            for cs in heads:
                sd = _dot(k_aug, qa[:, cs])
                s_a[:, cs] = jnp.where(diag_ok[d * SEL_KC:(d + 1) * SEL_KC, :], sd, NEG).astype(BF16)
                softmax(s_a, p_a, al_a, cs)
                apply_values(p_a, al_a, vt, cs)
        acc_s = acc_ref[...]
        o_s = acc_s[0:NSA_D, :] * (1.0 / acc_s[NSA_D:NSA_D + 1, :])

        for hp in range(HPG // 2):
            tiles = []
            for h in (2 * hp, 2 * hp + 1):
                gr = (g * HPG + h) * 3 + 1
                tiles.append(gt_ref[gr:gr + 1, :] * o_s[:, h * TQ:(h + 1) * TQ])
            cols = slice((g * HPG + 2 * hp) * NSA_D, (g * HPG + 2 * hp + 2) * NSA_D)
            o_ref[:, cols] = part_ref[:, cols] + jnp.concatenate(tiles, axis=0).T


def _nsa_selected(qt, ks, vsl, selm, sel, gt, part, B, S):
    HD = NSA_H * NSA_D
    TQ = SEL_TQ
    W = NSA_H // NSA_G * TQ
    n_sel = S // SLC_LEN
    assert S % (2 * SEL_KC) == 0 and TQ % SEL_KC == 0
    whole = lambda shape: pl.BlockSpec((None,) + shape, lambda b, i: (b,) + (0,) * len(shape),
                                       pipeline_mode=pl.Buffered(1))
    oneh = jnp.asarray(np.arange(SEL_KC)[:, None] // SLC_LEN == np.arange(LANE)[None, :], BF16)
    return pl.pallas_call(
        _nsa_selected_body,
        out_shape=jax.ShapeDtypeStruct((B * S, HD), F32),
        grid=(B, S // TQ),
        in_specs=[pl.BlockSpec((None, HD, TQ), lambda b, i: (b, 0, i)),
                  whole((S, LANE)), pl.BlockSpec((None, TQ, LANE), lambda b, i: (b, i, 0)),
                  whole(vsl.shape[1:]),
                  pl.BlockSpec((None, NSA_G, n_sel, TQ), lambda b, i: (b, 0, 0, i)),
                  pl.BlockSpec((None, NSA_G, TQ // SLC_LEN, TQ), lambda b, i: (b, 0, i, i)),
                  pl.BlockSpec((None, LANE, TQ), lambda b, i: (b, 0, i)),
                  pl.BlockSpec((TQ, HD), lambda b, i: (b * (S // TQ) + i, 0)),
                  _const_spec((SEL_KC, LANE))],
        out_specs=pl.BlockSpec((TQ, HD), lambda b, i: (b * (S // TQ) + i, 0)),
        scratch_shapes=[pltpu.VMEM((SEL_KC, W), BF16)] * 4 + [pltpu.VMEM((1, W), F32)] * 3
                       + [pltpu.VMEM((VT_ROWS, W), F32)],
        compiler_params=_params(("parallel", "arbitrary")),
        name="nsa_selected",
    )(qt, ks, ks, vsl, selm, sel, gt, part, oneh)


def _merge_body(x_ref, oa_ref, ob_ref, oc_ref, gates_ref, wbr_ref, wo_ref, o_ref):
    D = x_ref.shape[1]
    y = jnp.zeros(x_ref.shape, F32)
    for j, br in enumerate((oa_ref, ob_ref, oc_ref)):
        y = y + _sigmoid(gates_ref[:, j * D:(j + 1) * D].astype(F32)) * _dot(br[...].astype(BF16), wbr_ref[j])
    o_ref[...] = x_ref[...] + _dot(y.astype(BF16), wo_ref[...])


def _merge(x, o_a, o_b, o_c, gates, wbr, wo, tm=512):
    T, D = x.shape
    row = lambda w: pl.BlockSpec((tm, w), lambda i: (i, 0))
    return pl.pallas_call(
        _merge_body,
        out_shape=jax.ShapeDtypeStruct((T, D), F32),
        grid=(T // tm,),
        in_specs=[row(D), row(MIX_W), row(MIX_W), row(MIX_W), row(3 * D), _const_spec(wbr.shape), _const_spec(wo.shape)],
        out_specs=row(D),
        compiler_params=_params(("parallel",)),
        name="merge_out",
    )(x, o_a, o_b, o_c, gates, wbr, wo)


def _pack_in_proj(w_in, b_in):
    ws, bs = [], []
    for _, off, w, wp in _SEGS:
        ws.append(jnp.pad(w_in[..., off:off + w], ((0, 0), (0, 0), (0, wp - w))))
        bs.append(jnp.pad(b_in[..., off:off + w], ((0, 0), (0, wp - w))))
    return jnp.concatenate(ws, axis=-1).astype(BF16), jnp.concatenate(bs, axis=-1)[:, None, :]


def _rope_tables(pos):
    half = NSA_D // 2
    freqs = ROPE_THETA ** (-jnp.arange(half, dtype=F32) / half)
    ang = pos.astype(F32)[:, None] * freqs[None, :]
    cos, sin = jnp.cos(ang), jnp.sin(ang)
    reps = LANE // NSA_D
    return jnp.tile(jnp.concatenate([cos, cos], axis=1), (1, reps)), jnp.tile(jnp.concatenate([-sin, sin], axis=1), (1, reps))


def _cmp_weights(w1, w2):
    L = w1.shape[0]
    w1r = w1.reshape(L, 2, 2, CMP_STRIDE, NSA_D, CMP_HIDDEN)
    eye_g = jnp.eye(NSA_G, dtype=w1.dtype)
    ex = jnp.einsum('lwstdh,gk->lwstgdkh', w1r, eye_g)
    ex = ex.reshape(L, 2, 2, CMP_STRIDE * NSA_G * NSA_D, NSA_G * CMP_HIDDEN)
    w2x = jnp.einsum('lwhd,gk->lwghkd', w2, eye_g).reshape(L, 2, NSA_G * CMP_HIDDEN, NSA_G * NSA_D)
    return ex[:, :, 0].astype(BF16), ex[:, :, 1].astype(BF16), w2x.astype(BF16)


def _cmp_pos_rows(pe):
    L = pe.shape[0]
    r = pe.reshape(L, 2, 2, CMP_STRIDE, 1, NSA_D)
    return jnp.broadcast_to(r, (L, 2, 2, CMP_STRIDE, NSA_G, NSA_D)).reshape(L, 2, 2, CMP_STRIDE * NSA_G * NSA_D)


def _lane_gain(g):
    return jnp.tile(g, LANE // g.shape[-1])[None, :]


def _layer(x, lw, consts, B, S):
    cos, sin, cos_c, sin_c, gavg = consts
    T = B * S
    x = _ffn(x, lw["ffn1_norm"], lw["ffn1_wg"], lw["ffn1_wu"], lw["ffn1_wd"])
    z = dict(zip([s[0] for s in _SEGS], _proj(x, lw["mix_norm"], lw["w_in"], lw["b_in"])))
    o_a = _gla(z["a_q"], z["a_k"], z["a_v"], z["a_r"], z["a_lr"], lw["gla_wa"], lw["gla_ba"], lw["gla_gn"], B, S)
    o_c = _mlstm(z["c_qk"], z["c_v"], z["c_o"], z["c_if"], lw["conv_w"], lw["conv_b"], B, S)
    qn = lw["nsa_qk_norm"]
    qt, ks, kw, vsl, vwd, gt = _nsa_prep(z["b_q"], z["b_ks"], z["b_kw"], z["b_vs"], z["b_vw"], z["b_g"], cos, sin,
                                         _lane_gain(qn[0]), _lane_gain(qn[2]), _lane_gain(qn[3]), gavg, B, S)
    n_blk = S // CMP_STRIDE
    xk = z["b_kc"].reshape(B, n_blk, CMP_STRIDE * LANE)
    xv = z["b_vc"].reshape(B, n_blk, CMP_STRIDE * LANE)
    kc, vct = _nsa_cmp(xk, xv, lw["cmp_pe"], lw["cmp_wt"], lw["cmp_wb"], lw["cmp_w2"], cos_c, sin_c,
                       _lane_gain(qn[1]), gavg)
    part, sel, selm = _nsa_select(qt, kc, vct, kw, vwd, gt, B, S)
    o_b = _nsa_selected(qt, ks, vsl, selm, sel, gt, part, B, S)
    x = _merge(x, o_a, o_b, o_c, z["gates"], lw["w_branch"], lw["w_out"])
    return _ffn(x, lw["ffn2_norm"], lw["ffn2_wg"], lw["ffn2_wu"], lw["ffn2_wd"])


def kernel(x, ffn1_norm, ffn1_w_gate, ffn1_w_up, ffn1_w_down, mix_norm, w_in, b_in, gla_w_alpha, gla_b_alpha, gla_out_norm, nsa_qk_norm, nsa_cmp_pos, nsa_cmp_w1, nsa_cmp_w2, mlstm_conv_w, mlstm_conv_b, w_branch, w_out, ffn2_norm, ffn2_w_gate, ffn2_w_up, ffn2_w_down):
    B, S, D = x.shape
    w_in_p, b_in_p = _pack_in_proj(w_in, b_in)
    cmp_wt, cmp_wb, cmp_w2 = _cmp_weights(nsa_cmp_w1, nsa_cmp_w2)
    layers = {
        "ffn1_norm": ffn1_norm[:, None, :], "ffn1_wg": ffn1_w_gate.astype(BF16), "ffn1_wu": ffn1_w_up.astype(BF16),
        "ffn1_wd": ffn1_w_down.astype(BF16),
        "mix_norm": mix_norm[:, None, :], "w_in": w_in_p, "b_in": b_in_p,
        "gla_wa": jnp.pad(gla_w_alpha, ((0, 0), (0, LANE - GLA_RANK), (0, 0))), "gla_ba": gla_b_alpha[:, None, :],
        "gla_gn": gla_out_norm[:, None, :],
        "nsa_qk_norm": nsa_qk_norm, "cmp_pe": _cmp_pos_rows(nsa_cmp_pos), "cmp_wt": cmp_wt, "cmp_wb": cmp_wb,
        "cmp_w2": cmp_w2,
        "conv_w": mlstm_conv_w, "conv_b": mlstm_conv_b[:, None, :],
        "w_branch": w_branch.astype(BF16), "w_out": w_out.astype(BF16),
        "ffn2_norm": ffn2_norm[:, None, :], "ffn2_wg": ffn2_w_gate.astype(BF16), "ffn2_wu": ffn2_w_up.astype(BF16),
        "ffn2_wd": ffn2_w_down.astype(BF16),
    }
    n_blk = S // CMP_STRIDE
    cos, sin = _rope_tables(jnp.arange(S))
    cos_c, sin_c = _rope_tables(jnp.arange(n_blk) * CMP_STRIDE + CMP_LEN - 1)
    gavg = jnp.asarray(np.kron(np.eye(LANE // NSA_D), np.full((NSA_D, NSA_D), 1.0 / NSA_D)), BF16)
    consts = (cos, sin, cos_c, sin_c, gavg)

    out = x.reshape(B * S, D)
    for l in range(w_in.shape[0]):
        out = _layer(out, {name: w[l] for name, w in layers.items()}, consts, B, S)
    return out.reshape(B, S, D)
```

```python
import functools

import numpy as np
import jax
import jax.numpy as jnp
from jax import lax
from jax.experimental import pallas as pl
from jax.experimental.pallas import tpu as pltpu

F32 = jnp.float32
BF16 = jnp.bfloat16

RMS_EPS = 1e-6
ROPE_THETA = 10000.0

GLA_H, GLA_DK, GLA_DV, GLA_RANK, GLA_GATE_NORM = 4, 64, 128, 16, 16.0
GLA_SUB = 16
NSA_H, NSA_G, NSA_D = 8, 2, 64
CMP_LEN, CMP_STRIDE, CMP_HIDDEN = 32, 16, 256
SLC_LEN, SLC_TOPK, WIN = 64, 16, 512
FORCED_SCORE = 1e4
MLSTM_H, MLSTM_DK, MLSTM_DV, CONV_W = 4, 64, 128, 4
MLSTM_CHUNK = 64
MIX_W = 512
LANE = 128
MXU_N = 256
NEG = -1e30
LOG2E = 1.4426950408889634
SEL_KC = 512
SEL_TQ = 512
PS_PAD = 8
CMP_VARIANTS = 4
VT_ROWS = 80

_SEGS = (
    ("a_q", 0, 256, 256), ("a_k", 256, 256, 256), ("a_v", 512, 512, 512), ("a_r", 1024, 512, 512),
    ("b_q", 1552, 512, 512), ("c_qk", 2856, 512, 512), ("c_v", 3368, 512, 512), ("c_o", 3880, 512, 512),
    ("gates", 4400, 3072, 3072),
    ("a_lr", 1536, 16, 128), ("b_kc", 2064, 128, 128), ("b_vc", 2192, 128, 128), ("b_ks", 2320, 128, 128),
    ("b_vs", 2448, 128, 128), ("b_kw", 2576, 128, 128), ("b_vw", 2704, 128, 128), ("b_g", 2832, 24, 128),
    ("c_if", 4392, 8, 128),
)
_N_PACK = sum(s[3] for s in _SEGS)
_BF16_SEGS = ("gates", "c_v", "b_vs", "b_vw")


def _dot(a, b):
    return jnp.dot(a, b, preferred_element_type=F32)


def _dot_nt(a, b):
    return lax.dot_general(a, b, (((1,), (1,)), ((), ())), preferred_element_type=F32)


def _split2(a):
    hi = a.astype(BF16)
    lo = (a - hi.astype(F32)).astype(BF16)
    return hi, lo


def _dot_l2(a, b):
    hi, lo = _split2(a)
    return _dot(hi, b) + _dot(lo, b)


def _dot_r2(a, b):
    hi, lo = _split2(b)
    return _dot(a, hi) + _dot(a, lo)


def _log_sigmoid(x):
    return jnp.minimum(x, 0.0) - jnp.log(1.0 + jnp.exp(-jnp.abs(x)))


def _sigmoid(x):
    return 1.0 / (1.0 + jnp.exp(-x))


def _silu(x):
    return x * _sigmoid(x)


def _iota(shape, dim):
    return lax.broadcasted_iota(jnp.int32, shape, dim)


def _const_spec(shape):
    nd = len(shape)
    return pl.BlockSpec(shape, lambda *_: (0,) * nd, pipeline_mode=pl.Buffered(1))


def _params(sem, vmem_mb=56):
    return pltpu.CompilerParams(dimension_semantics=sem, vmem_limit_bytes=vmem_mb * 1024 * 1024)


def _ffn_body(x_ref, g_ref, wg_ref, wu_ref, wd_ref, o_ref, *, bounds):
    x = x_ref[...]
    ms = jnp.mean(x * x, axis=-1, keepdims=True)
    h = (x * lax.rsqrt(ms + RMS_EPS) * g_ref[...]).astype(BF16)
    acc = jnp.zeros(x.shape, F32)
    for lo, hi in zip(bounds[:-1], bounds[1:]):
        a = _dot(h, wg_ref[:, lo:hi])
        u = _dot(h, wu_ref[:, lo:hi])
        t = (_silu(a) * u).astype(BF16)
        acc = acc + _dot(t, wd_ref[lo:hi, :])
    o_ref[...] = x + 0.5 * acc


def _ffn(x, g, wg, wu, wd, tm=512):
    T, D = x.shape
    F = wg.shape[1]
    bounds = (0, -(-F // (2 * MXU_N)) * MXU_N, F)
    return pl.pallas_call(
        functools.partial(_ffn_body, bounds=bounds),
        out_shape=jax.ShapeDtypeStruct((T, D), F32),
        grid=(T // tm,),
        in_specs=[pl.BlockSpec((tm, D), lambda i: (i, 0)), _const_spec((1, D)),
                  _const_spec((D, F)), _const_spec((D, F)), _const_spec((F, D))],
        out_specs=pl.BlockSpec((tm, D), lambda i: (i, 0)),
        compiler_params=_params(("parallel",)),
        name="ffn",
    )(x, g, wg, wu, wd)


def _proj_body(x_ref, g_ref, w_ref, b_ref, *o_refs):
    x = x_ref[...]
    ms = jnp.mean(x * x, axis=-1, keepdims=True)
    h = (x * lax.rsqrt(ms + RMS_EPS) * g_ref[...]).astype(BF16)
    off, i = 0, 0
    while i < len(o_refs):
        group, w = [], 0
        while i < len(o_refs) and (not group or w % MXU_N):
            group.append(o_refs[i])
            w += o_refs[i].shape[1]
            i += 1
        z = _dot(h, w_ref[:, off:off + w]) + b_ref[:, off:off + w]
        c = 0
        for o_ref in group:
            o_ref[...] = z[:, c:c + o_ref.shape[1]].astype(o_ref.dtype)
            c += o_ref.shape[1]
        off += w


def _proj(x, g, w, b, tm=256):
    T, D = x.shape
    return pl.pallas_call(
        _proj_body,
        out_shape=[jax.ShapeDtypeStruct((T, s[3]), BF16 if s[0] in _BF16_SEGS else F32) for s in _SEGS],
        grid=(T // tm,),
        in_specs=[pl.BlockSpec((tm, D), lambda i: (i, 0)), _const_spec((1, D)),
                  _const_spec((D, _N_PACK)), _const_spec((1, _N_PACK))],
        out_specs=[pl.BlockSpec((tm, s[3]), lambda i: (i, 0)) for s in _SEGS],
        compiler_params=_params(("parallel",)),
        name="in_proj",
    )(x, g, w, b)


def _gla_body(q_ref, k_ref, v_ref, r_ref, lr_ref, wa_ref, ba_ref, gn_ref, tri_ref, bones_ref, eh_ref,
              o_ref, st_ref, qs_ref, c_ref, tot_ref, kst_ref, vt4_ref, oi_ref):
    nB, Lc = q_ref.shape[0], q_ref.shape[1]
    n_sub = Lc // GLA_SUB
    HK = GLA_H * GLA_DK

    @pl.when(pl.program_id(0) == 0)
    def _():
        st_ref[...] = jnp.zeros(st_ref.shape, F32)

    lane_h = _iota((1, HK), 1) >> (GLA_DK.bit_length() - 1)
    row_i = _iota((GLA_SUB, 1), 0)
    col_r = _iota((1, GLA_H * Lc), 1) & (Lc - 1)
    wa_hi, wa_lo = _split2(wa_ref[...])

    for b in range(nB):
        lr_hi, lr_lo = _split2(lr_ref[b])
        u = _dot(lr_hi, wa_hi) + _dot(lr_hi, wa_lo) + _dot(lr_lo, wa_hi) + ba_ref[...]
        g = _log_sigmoid(u) * (1.0 / GLA_GATE_NORM)
        c = _dot_r2(tri_ref[...], g)
        tot = _dot_r2(bones_ref[...], g)
        qs_ref[b] = q_ref[b] * (GLA_DK ** -0.5)
        c_ref[b] = c
        tot_ref[b] = tot
        kt = k_ref[b] * jnp.exp(tot - c)
        vt = v_ref[b].T.astype(BF16)
        for h in range(GLA_H):
            kst_ref[b, h * Lc:(h + 1) * Lc, :] = jnp.where(lane_h == h, kt, 0.0).astype(BF16)
            vt4_ref[b, :, h * Lc:(h + 1) * Lc] = vt[h * GLA_DV:(h + 1) * GLA_DV, :]

    def sub(b, s):
        rows = slice(s * GLA_SUB, (s + 1) * GLA_SUB)
        qs = qs_ref[b, rows, :]
        cs = c_ref[b, rows, :]
        st = st_ref[b]
        qd = qs * jnp.exp(cs)
        q4 = jnp.concatenate([jnp.where(lane_h == h, qd, 0.0) for h in range(GLA_H)], axis=0).astype(BF16)
        inter4 = _dot_nt(q4, st.astype(BF16))
        inter = jnp.concatenate([inter4[h * GLA_SUB:(h + 1) * GLA_SUB, :] for h in range(GLA_H)], axis=1)
        xs = []
        for j in range(GLA_SUB):
            r = s * GLA_SUB + j
            x = qs * k_ref[b, r:r + 1, :] * jnp.exp(jnp.minimum(cs - c_ref[b, r:r + 1, :], 0.0))
            xs.append(jnp.where(row_i >= j, x, 0.0))
        r_all = _dot(jnp.concatenate(xs, axis=0).astype(BF16), eh_ref[...])
        intra = jnp.zeros((GLA_SUB, r_all.shape[1]), F32)
        for j in range(GLA_SUB):
            r = s * GLA_SUB + j
            intra = intra + r_all[j * GLA_SUB:(j + 1) * GLA_SUB, :] * v_ref[b, r:r + 1, :]
        oi_ref[b, rows, :] = inter + intra
        dec = jnp.exp(tot_ref[b, s * GLA_SUB:s * GLA_SUB + 1, :])
        in_sub = (col_r >= s * GLA_SUB) & (col_r < (s + 1) * GLA_SUB)
        vtm = jnp.where(in_sub, vt4_ref[b], jnp.zeros((), BF16))
        st_ref[b] = dec * st + _dot(vtm, kst_ref[b])

    for s in range(n_sub):
        for b in range(nB):
            sub(b, s)

    gn = gn_ref[...]
    for b in range(nB):
        for h in range(GLA_H):
            sl = slice(h * GLA_DV, (h + 1) * GLA_DV)
            o = oi_ref[b, :, sl]
            ms = jnp.mean(o * o, axis=-1, keepdims=True)
            o_ref[b, :, sl] = o * lax.rsqrt(ms + RMS_EPS) * gn * _silu(r_ref[b, :, sl])


def _gla_consts(Lc):
    r = np.arange(Lc)
    same = (r[:, None] // GLA_SUB) == (r[None, :] // GLA_SUB)
    tri = (same & (r[None, :] <= r[:, None])).astype(np.float32)
    bones = same.astype(np.float32)
    hk = np.arange(GLA_H * GLA_DK) // GLA_DK
    hv = np.arange(GLA_H * GLA_DV) // GLA_DV
    eh = (hk[:, None] == hv[None, :]).astype(np.float32)
    return jnp.asarray(tri, BF16), jnp.asarray(bones, BF16), jnp.asarray(eh, BF16)


def _gla(a_q, a_k, a_v, a_r, a_lr, wa, ba, gn, B, S, Lc=128):
    HK, HV = GLA_H * GLA_DK, GLA_H * GLA_DV
    nb = S // Lc
    assert Lc & (Lc - 1) == 0
    tri, bones, eh = _gla_consts(Lc)
    row = lambda w: pl.BlockSpec((B, Lc, w), lambda i: (0, i, 0))
    seq = lambda t: t.reshape(B, S, t.shape[-1])
    out = pl.pallas_call(
        _gla_body,
        out_shape=jax.ShapeDtypeStruct((B, S, HV), F32),
        grid=(nb,),
        in_specs=[row(HK), row(HK), row(HV), row(HV), row(LANE),
                  _const_spec((LANE, HK)), _const_spec((1, HK)), _const_spec((1, GLA_DV)),
                  _const_spec((Lc, Lc)), _const_spec((Lc, Lc)), _const_spec((HK, HV))],
        out_specs=row(HV),
        scratch_shapes=[pltpu.VMEM((B, GLA_DV, HK), F32), pltpu.VMEM((B, Lc, HK), F32), pltpu.VMEM((B, Lc, HK), F32),
                        pltpu.VMEM((B, Lc, HK), F32), pltpu.VMEM((B, GLA_H * Lc, HK), BF16),
                        pltpu.VMEM((B, GLA_DV, GLA_H * Lc), BF16), pltpu.VMEM((B, Lc, HV), F32)],
        compiler_params=_params(("arbitrary",)),
        name="gla",
    )(seq(a_q), seq(a_k), seq(a_v), seq(a_r), seq(a_lr), wa, ba, gn, tri, bones, eh)
    return out.reshape(B * S, HV)


def _mlstm_body(qk_ref, v_ref, og_ref, if_ref, cw_ref, cb_ref, tri_ref, bdm_ref,
                o_ref, xx_ref, ct_ref, n_ref, m_ref, kw_ref, vt_ref, act_ref):
    Lc = v_ref.shape[0]
    L = MLSTM_CHUNK
    HK = MLSTM_H * MLSTM_DK
    tail = 8

    @pl.when(pl.program_id(1) == 0)
    def _():
        ct_ref[...] = jnp.zeros(ct_ref.shape, F32)
        n_ref[...] = jnp.zeros(n_ref.shape, F32)
        m_ref[...] = jnp.zeros(m_ref.shape, F32)
        xx_ref[0:tail, :] = jnp.zeros((tail, xx_ref.shape[1]), F32)

    xx_ref[tail:tail + Lc, :] = qk_ref[...]
    y = jnp.zeros((Lc, 2 * HK), F32) + cb_ref[...]
    for kk in range(CONV_W):
        y = y + cw_ref[kk:kk + 1, :] * xx_ref[pl.ds(tail - (CONV_W - 1) + kk, Lc), :]
    xx_ref[0:tail, :] = xx_ref[Lc:Lc + tail, :]
    act_ref[...] = _silu(y)

    gates = if_ref[...]
    logf = _log_sigmoid(gates)
    bcum = _dot_r2(tri_ref[...], logf)
    b_t = bcum.T
    q_t = act_ref[:, 0:HK].T
    qt_hi, qt_lo = _split2(q_t)
    kw_ref[...] = jnp.zeros(kw_ref.shape, BF16)
    vt_ref[...] = v_ref[...].astype(F32).T.astype(BF16)
    lane_hk = _iota((1, HK), 1) >> (MLSTM_DK.bit_length() - 1)
    col_l = _iota((1, Lc), 1)
    lane_w = _iota((1, LANE), 1)
    row_l = _iota((L, 1), 0)
    per_win = LANE // L

    for ci in range(Lc // L):
        p, c = divmod(ci, per_win)
        rows = slice(ci * L, (ci + 1) * L)
        win = slice(p * LANE, (p + 1) * LANE)
        causal_t = ((c * L + row_l) <= lane_w) & ((lane_w >> (L.bit_length() - 1)) == c)
        k_all = act_ref[rows, HK:2 * HK] * (MLSTM_DK ** -0.5)
        k_bf = k_all.astype(BF16)
        ct = ct_ref[...]
        ct_bf = ct.astype(BF16)
        n_row = n_ref[...]
        n_hi, n_lo = _split2(n_row)
        n8 = jnp.concatenate([jnp.where(lane_hk == h, part, jnp.zeros((), BF16))
                              for part in (n_hi, n_lo) for h in range(MLSTM_H)], axis=0)
        qn8 = _dot(n8, qt_hi[:, win]) + _dot(n8, qt_lo[:, win])
        wk_full = jnp.zeros((L, HK), F32)
        dec_row = jnp.zeros((1, HK), F32)
        for h in range(MLSTM_H):
            icol = gates[rows, h:h + 1]
            bcol = bcum[rows, MLSTM_H + h:MLSTM_H + h + 1]
            brow = b_t[MLSTM_H + h:MLSTM_H + h + 1, win]
            m_st = m_ref[0:1, h:h + 1]
            dmat_t = jnp.where(causal_t, brow + (icol - bcol), -jnp.inf)
            inter_log = brow + m_st
            m_row = jnp.maximum(inter_log, jnp.max(dmat_t, axis=0, keepdims=True))
            w_inter = jnp.exp(inter_log - m_row)
            head_rows = slice(h * MLSTM_DK, (h + 1) * MLSTM_DK)
            zero_q = jnp.zeros((MLSTM_DK, LANE), BF16)
            qh = jnp.concatenate([qt_hi[head_rows, win] if hh == h else zero_q for hh in range(MLSTM_H)], axis=0)
            s_qk = _dot(k_bf, qh) * jnp.exp(dmat_t - m_row)
            den = w_inter * (qn8[h:h + 1, :] + qn8[MLSTM_H + h:MLSTM_H + h + 1, :]) + jnp.sum(s_qk, axis=0, keepdims=True)
            s_bf = s_qk.astype(BF16)
            pads = [jnp.zeros((L, LANE), BF16)] * per_win
            pads[c] = s_bf
            dv = slice(h * MLSTM_DV, (h + 1) * MLSTM_DV)
            num_t = w_inter * _dot(ct_bf[dv, :], qt_hi[:, win]) + _dot(vt_ref[dv, win], jnp.concatenate(pads, axis=0))
            h_t = num_t / jnp.maximum(jnp.abs(den), jnp.exp(-m_row))
            o_ref[rows, dv] = h_t.T[c * L:(c + 1) * L, :] * _sigmoid(og_ref[rows, dv])
            m_new = m_row[:, c * L + L - 1:c * L + L]
            b_last = bcol[L - 1:L, :]
            w_k = jnp.exp(b_last - bcol + icol - m_new)
            decay = jnp.exp(b_last + m_st - m_new)
            head = lane_hk == h
            wk_full = wk_full + jnp.where(head, w_k, 0.0)
            dec_row = dec_row + jnp.where(head, decay, 0.0)
            m_ref[0:1, h:h + 1] = m_new
        kw = k_all * wk_full
        kw_ref[rows, :] = kw.astype(BF16)
        in_chunk = (col_l >= ci * L) & (col_l < (ci + 1) * L)
        vtm = jnp.where(in_chunk, vt_ref[...], jnp.zeros((), BF16))
        ct_ref[...] = dec_row * ct + _dot(vtm, kw_ref[...]) * bdm_ref[...]
        n_ref[...] = dec_row * n_row + jnp.sum(kw, axis=0, keepdims=True)


def _mlstm_consts(Lc):
    r = np.arange(Lc)
    same = (r[:, None] // MLSTM_CHUNK) == (r[None, :] // MLSTM_CHUNK)
    tri = (same & (r[None, :] <= r[:, None])).astype(np.float32)
    hk = np.arange(MLSTM_H * MLSTM_DK) // MLSTM_DK
    hv = np.arange(MLSTM_H * MLSTM_DV) // MLSTM_DV
    eh = (hk[:, None] == hv[None, :]).astype(np.float32)
    return jnp.asarray(tri, BF16), jnp.asarray(eh.T.copy(), F32)


def _mlstm(c_qk, c_v, c_o, c_if, cw, cb, B, S, Lc=256):
    HK, HV = MLSTM_H * MLSTM_DK, MLSTM_H * MLSTM_DV
    nb = S // Lc
    assert Lc % LANE == 0 and LANE % MLSTM_CHUNK == 0
    tri, bdm = _mlstm_consts(Lc)
    row = lambda w: pl.BlockSpec((Lc, w), lambda b, i: (b * nb + i, 0))
    return pl.pallas_call(
        _mlstm_body,
        out_shape=jax.ShapeDtypeStruct((B * S, HV), F32),
        grid=(B, nb),
        in_specs=[row(2 * HK), row(HV), row(HV), row(LANE),
                  _const_spec((CONV_W, 2 * HK)), _const_spec((1, 2 * HK)),
                  _const_spec((Lc, Lc)), _const_spec((HV, HK))],
        out_specs=row(HV),
        scratch_shapes=[pltpu.VMEM((Lc + 8, 2 * HK), F32), pltpu.VMEM((HV, HK), F32), pltpu.VMEM((1, HK), F32),
                        pltpu.VMEM((8, LANE), F32), pltpu.VMEM((Lc, HK), BF16), pltpu.VMEM((HV, Lc), BF16),
                        pltpu.VMEM((Lc, 2 * HK), F32)],
        compiler_params=_params(("parallel", "arbitrary")),
        name="mlstm",
    )(c_qk, c_v, c_o, c_if, cw, cb, tri, bdm)


def _group_rms(x, gavg, gain):
    ms = _dot_l2(x * x, gavg)
    return x * lax.rsqrt(ms + RMS_EPS) * gain


def _rope_lanes(x, cos, sin_signed):
    half = NSA_D // 2
    first = (_iota((1, LANE), 1) & (NSA_D - 1)) < half
    swapped = jnp.where(first, pltpu.roll(x, LANE - half, 1), pltpu.roll(x, half, 1))
    return x * cos + swapped * sin_signed


def _store_vt_tiles(dst_ref, vt, width):
    ones = jnp.ones((VT_ROWS - NSA_D, width), BF16)
    for j in range(vt.shape[1] // width):
        for g in range(NSA_G):
            dst_ref[j, g, 0:NSA_D, :] = vt[g * NSA_D:(g + 1) * NSA_D, j * width:(j + 1) * width]
            dst_ref[j, g, NSA_D:VT_ROWS, :] = ones


def _nsa_prep_body(q_ref, ks_ref, kw_ref, vs_ref, vw_ref, g_ref, cos_ref, sin_ref, gq_ref, gs_ref, gw_ref, gavg_ref,
                   qt_ref, kso_ref, kwo_ref, vsl_ref, vwd_ref, gt_ref):
    cos, sin = cos_ref[...], sin_ref[...]
    gavg = gavg_ref[...]
    for cb in range(NSA_H * NSA_D // LANE):
        sl = slice(cb * LANE, (cb + 1) * LANE)
        qn = _rope_lanes(_group_rms(q_ref[:, sl], gavg, gq_ref[...]), cos, sin) * (NSA_D ** -0.5 * LOG2E)
        qt_ref[sl, :] = qn.T.astype(BF16)
    kso_ref[...] = _rope_lanes(_group_rms(ks_ref[...], gavg, gs_ref[...]), cos, sin).astype(BF16)
    kwo_ref[...] = _rope_lanes(_group_rms(kw_ref[...], gavg, gw_ref[...]), cos, sin).astype(BF16)
    vst = vs_ref[...].astype(F32).T.astype(BF16)
    _store_vt_tiles(vsl_ref, vst, SEL_KC)
    _store_vt_tiles(vwd_ref, vw_ref[...].astype(F32).T.astype(BF16), LANE)
    gt_ref[...] = _sigmoid(g_ref[...]).T


def _nsa_prep(b_q, b_ks, b_kw, b_vs, b_vw, b_g, cos, sin, gq, gs, gw, gavg, B, S, tm=SEL_KC):
    nb = S // tm
    row = lambda w: pl.BlockSpec((tm, w), lambda b, i: (b * nb + i, 0))
    tab = pl.BlockSpec((tm, LANE), lambda b, i: (i, 0))
    HD = NSA_H * NSA_D
    vt_shape = lambda width: jax.ShapeDtypeStruct((B, S // width, NSA_G, VT_ROWS, width), BF16)
    vt_spec = lambda width: pl.BlockSpec((None, tm // width, NSA_G, VT_ROWS, width), lambda b, i: (b, i, 0, 0, 0))
    return pl.pallas_call(
        _nsa_prep_body,
        out_shape=[jax.ShapeDtypeStruct((B, HD, S), BF16),
                   jax.ShapeDtypeStruct((B, S, LANE), BF16),
                   jax.ShapeDtypeStruct((B, S, LANE), BF16),
                   vt_shape(SEL_KC),
                   vt_shape(LANE),
                   jax.ShapeDtypeStruct((B, LANE, S), F32)],
        grid=(B, nb),
        in_specs=[row(HD), row(LANE), row(LANE), row(LANE), row(LANE), row(LANE), tab, tab,
                  _const_spec((1, LANE)), _const_spec((1, LANE)), _const_spec((1, LANE)), _const_spec((LANE, LANE))],
        out_specs=[pl.BlockSpec((None, HD, tm), lambda b, i: (b, 0, i)),
                   pl.BlockSpec((None, tm, LANE), lambda b, i: (b, i, 0)),
                   pl.BlockSpec((None, tm, LANE), lambda b, i: (b, i, 0)),
                   vt_spec(SEL_KC), vt_spec(LANE),
                   pl.BlockSpec((None, LANE, tm), lambda b, i: (b, 0, i))],
        compiler_params=_params(("parallel", "parallel")),
        name="nsa_prep",
    )(b_q, b_ks, b_kw, b_vs, b_vw, b_g, cos, sin, gq, gs, gw, gavg)


def _gelu_tanh(x):
    return 0.5 * x * (1.0 + jnp.tanh(0.7978845608028654 * (x + 0.044715 * x * x * x)))


def _nsa_cmp_body(xk_ref, xv_ref, pe_ref, wt_ref, wb_ref, w2_ref, cos_ref, sin_ref, gk_ref, gavg_ref,
                  kc_ref, vct_ref):
    n = xk_ref.shape[0]

    def compress(x, which):
        u = _dot((x + pe_ref[which, 0:1, :]).astype(BF16), wt_ref[which])
        v = _dot((x + pe_ref[which, 1:2, :]).astype(BF16), wb_ref[which])
        hid = u + pltpu.roll(v, n - 1, 0)
        return _dot(_gelu_tanh(hid).astype(BF16), w2_ref[which])

    ck = compress(xk_ref[...], 0)
    kc_ref[...] = _rope_lanes(_group_rms(ck, gavg_ref[...], gk_ref[...]), cos_ref[...], sin_ref[...]).astype(BF16)
    vct = compress(xv_ref[...], 1).T.astype(BF16)
    for g in range(NSA_G):
        vct_ref[g] = vct[g * NSA_D:(g + 1) * NSA_D, :]


def _nsa_cmp(xk, xv, pe, wt, wb, w2, cos, sin, gk, gavg):
    B, n, W = xk.shape
    return pl.pallas_call(
        _nsa_cmp_body,
        out_shape=[jax.ShapeDtypeStruct((B, n, LANE), BF16),
                   jax.ShapeDtypeStruct((B, NSA_G, NSA_D, n), BF16)],
        grid=(B,),
        in_specs=[pl.BlockSpec((None, n, W), lambda b: (b, 0, 0)), pl.BlockSpec((None, n, W), lambda b: (b, 0, 0)),
                  _const_spec(pe.shape), _const_spec(wt.shape), _const_spec(wb.shape), _const_spec(w2.shape),
                  _const_spec((n, LANE)), _const_spec((n, LANE)), _const_spec((1, LANE)), _const_spec((LANE, LANE))],
        out_specs=[pl.BlockSpec((None, n, LANE), lambda b: (b, 0, 0)),
                   pl.BlockSpec((None, NSA_G, NSA_D, n), lambda b: (b, 0, 0, 0))],
        compiler_params=_params(("parallel",)),
        name="nsa_compress",
    )(xk, xv, pe, wt, wb, w2, cos, sin, gk, gavg)


def _group_queries(qt_ref, g):
    HPG = NSA_H // NSA_G
    q4 = jnp.concatenate([qt_ref[(g * HPG + h) * NSA_D:(g * HPG + h + 1) * NSA_D, :] for h in range(HPG)], axis=1)
    parts = [jnp.zeros(q4.shape, BF16)] * NSA_G
    parts[g] = q4
    return jnp.concatenate(parts, axis=0)


def _bitonic_merge_desc(xs):
    xs = list(xs)
    j = len(xs) // 2
    while j >= 1:
        for a in range(len(xs)):
            b = a ^ j
            if b > a:
                xs[a], xs[b] = jnp.maximum(xs[a], xs[b]), jnp.minimum(xs[a], xs[b])
        j //= 2
    return xs


def _bitonic_sort_desc(xs):
    if len(xs) == 1:
        return list(xs)
    half = len(xs) // 2
    lo = _bitonic_sort_desc(xs[:half])
    hi = _bitonic_sort_desc(xs[half:])
    return _bitonic_merge_desc(lo + hi[::-1])


def _kth_largest(score, k):
    SUBLANES = 8
    tiles = [score[SUBLANES * v:SUBLANES * (v + 1), :] for v in range(score.shape[0] // SUBLANES)]
    size = max(k, 1 << (len(tiles) - 1).bit_length())
    tiles = tiles + [jnp.full(tiles[0].shape, NEG, F32)] * (size - len(tiles))

    def top_of_two(a, b):
        return _bitonic_merge_desc([jnp.maximum(a[r], b[k - 1 - r]) for r in range(k)])

    tops = [_bitonic_sort_desc(tiles[c:c + k]) for c in range(0, size, k)]
    while len(tops) > 1:
        tops = [top_of_two(tops[c], tops[c + 1]) for c in range(0, len(tops), 2)]
    top = tops[0]
    shift = SUBLANES // 2
    while shift >= 1:
        top = top_of_two(top, [pltpu.roll(t, shift, 0) for t in top])
        shift //= 2
    return top[k - 1][0:1, :]


def _nsa_select_dispatch(*refs, sel_tile, n_variants):
    n_cmp = refs[1].shape[0]
    need = (pl.program_id(1) + 1) * LANE // CMP_STRIDE
    variant = (need - 1) * n_variants // n_cmp
    for k in range(n_variants):
        pl.when(variant == k)(functools.partial(_nsa_select_body, *refs, sel_tile=sel_tile,
                                                n_rows=n_cmp * (k + 1) // n_variants))


def _nsa_select_body(qt_ref, kc_ref, vct_ref, kw_ref, vwd_ref, gt_ref, tril_ref, part_ref, sel_ref, selm_ref, ps_buf,
                     *, sel_tile, n_rows):
    TQ = LANE
    HPG = NSA_H // NSA_G
    W = HPG * TQ
    n_cmp = kc_ref.shape[0]
    n_sel = sel_ref.shape[1]
    i = pl.program_id(1)
    s0 = i * TQ
    t_row = s0 + _iota((1, TQ), 1)
    n_win = WIN // LANE + 1
    j0 = jnp.maximum(i - (n_win - 1), 0)
    w_start = pl.multiple_of(j0 * LANE, LANE)
    own_first = (s0 // sel_tile) * (sel_tile // SLC_LEN)
    ps_buf[:, 0:PS_PAD, :] = jnp.zeros((NSA_G, PS_PAD, TQ), F32)

    for g in range(NSA_G):
        qpad = _group_queries(qt_ref, g)

        cend = _iota((n_rows, 1), 0) * CMP_STRIDE + (CMP_LEN - 1)
        bias_c = jnp.where(cend <= t_row, 0.0, NEG)
        sc = _dot(kc_ref[0:n_rows, :], qpad) + jnp.concatenate([bias_c] * HPG, axis=1)
        m = jnp.max(sc, axis=0, keepdims=True)
        m = jnp.where(m > 0.5 * NEG, m, 0.0)
        p = jnp.exp2(sc - m)
        p = p * (1.0 / jnp.maximum(jnp.sum(p, axis=0, keepdims=True), 1e-30))
        o_c = _dot(vct_ref[g, :, 0:n_rows], p.astype(BF16))
        psum = p[:, 0:TQ]
        for h in range(1, HPG):
            psum = psum + p[:, h * TQ:(h + 1) * TQ]
        ps_buf[g, PS_PAD:PS_PAD + n_rows, :] = psum
        if n_rows < n_cmp:
            ps_buf[g, PS_PAD + n_rows:PS_PAD + n_cmp, :] = jnp.zeros((n_cmp - n_rows, TQ), F32)

        per = SLC_LEN // CMP_STRIDE
        imp = ps_buf[g, pl.ds(PS_PAD - 1, n_sel, stride=per), :]
        for jj in range(per):
            imp = imp + ps_buf[g, pl.ds(PS_PAD + jj, n_sel, stride=per), :]
        blk = _iota((n_sel, 1), 0)
        cur = t_row >> (SLC_LEN.bit_length() - 1)
        valid = blk * SLC_LEN <= t_row
        forced = (blk == 0) | (blk == cur) | (blk == cur - 1)
        score0 = jnp.where(valid, jnp.where(forced, FORCED_SCORE, imp), NEG)
        kth = _kth_largest(score0, min(SLC_TOPK, n_sel))
        above = score0 > kth
        tied = score0 == kth
        need = min(SLC_TOPK, n_sel) - jnp.sum(jnp.where(above, 1.0, 0.0), axis=0, keepdims=True)
        rank = _dot(tril_ref[...], jnp.where(tied, 1.0, 0.0).astype(BF16))
        chosen = (above | (tied & (rank <= need))) & valid
        sel_ref[g] = jnp.where(chosen, 0.0, NEG)
        selm_ref[g] = jnp.where(chosen & (blk < own_first), 0.0, NEG)

        kpos = w_start + _iota((n_win * LANE, 1), 0)
        bias_w = jnp.where((kpos <= t_row) & (kpos > t_row - WIN), 0.0, NEG)
        sw = _dot(kw_ref[pl.ds(w_start, n_win * LANE), :], qpad) + jnp.concatenate([bias_w] * HPG, axis=1)
        pw = jnp.exp2(sw - jnp.max(sw, axis=0, keepdims=True)).astype(BF16)
        acc_w = jnp.zeros((VT_ROWS, W), F32)
        for r in range(n_win):
            acc_w = acc_w + _dot(vwd_ref[j0 + r, g], pw[r * LANE:(r + 1) * LANE, :])
        o_w = acc_w[0:NSA_D, :] * (1.0 / acc_w[NSA_D:NSA_D + 1, :])

        for hp in range(HPG // 2):
            tiles = []
            for h in (2 * hp, 2 * hp + 1):
                gr = (g * HPG + h) * 3
                cs = slice(h * TQ, (h + 1) * TQ)
                tiles.append(gt_ref[gr:gr + 1, :] * o_c[:, cs] + gt_ref[gr + 2:gr + 3, :] * o_w[:, cs])
            col = (g * HPG + 2 * hp) * NSA_D
            part_ref[:, col:col + 2 * NSA_D] = jnp.concatenate(tiles, axis=0).T


def _nsa_select(qt, kc, vct, kw, vwd, gt, B, S):
    HD = NSA_H * NSA_D
    TQ = LANE
    n_cmp = kc.shape[1]
    n_sel = S // SLC_LEN
    assert S >= WIN + TQ and CMP_LEN == 2 * CMP_STRIDE
    whole = lambda shape: pl.BlockSpec((None,) + shape, lambda b, i: (b,) + (0,) * len(shape),
                                       pipeline_mode=pl.Buffered(1))
    mask_shape = jax.ShapeDtypeStruct((B, NSA_G, n_sel, S), F32)
    mask_spec = pl.BlockSpec((None, NSA_G, n_sel, TQ), lambda b, i: (b, 0, 0, i))
    tril = jnp.asarray(np.tril(np.ones((n_sel, n_sel), np.float32)), BF16)
    return pl.pallas_call(
        functools.partial(_nsa_select_dispatch, sel_tile=SEL_TQ, n_variants=CMP_VARIANTS),
        out_shape=[jax.ShapeDtypeStruct((B * S, HD), F32), mask_shape, mask_shape],
        grid=(B, S // TQ),
        in_specs=[pl.BlockSpec((None, HD, TQ), lambda b, i: (b, 0, i)),
                  whole((n_cmp, LANE)), whole(vct.shape[1:]), whole((S, LANE)), whole(vwd.shape[1:]),
                  pl.BlockSpec((None, LANE, TQ), lambda b, i: (b, 0, i)), _const_spec((n_sel, n_sel))],
        out_specs=[pl.BlockSpec((TQ, HD), lambda b, i: (b * (S // TQ) + i, 0)), mask_spec, mask_spec],
        scratch_shapes=[pltpu.VMEM((NSA_G, n_cmp + PS_PAD, TQ), F32)],
        compiler_params=_params(("parallel", "parallel")),
        name="nsa_select",
    )(qt, kc, vct, kw, vwd, gt, tril)


def _nsa_selected_body(qt_ref, ks_ref, ksd_ref, vsl_ref, selm_ref, seld_ref, gt_ref, part_ref, oneh_ref, o_ref,
                       s_a, s_b, p_a, p_b, al_a, al_b, m_ref, acc_ref):
    TQ = qt_ref.shape[1]
    HPG = NSA_H // NSA_G
    W = HPG * TQ
    n_kc = ks_ref.shape[0] // SEL_KC
    bps = SEL_KC // SLC_LEN
    i = pl.program_id(1)
    n_main = i * (TQ // SEL_KC)
    n_pairs = (n_main + 1) // 2
    diag_ok = _iota((TQ, 1), 0) <= _iota((1, TQ), 1)
    pad_rows = jnp.zeros((LANE - 2 * bps, W), BF16)

    def q_aug(qpad, brows):
        brows = jnp.concatenate([jnp.concatenate([brows] * HPG, axis=1), jnp.zeros((bps, W), F32)], axis=0)
        return jnp.concatenate([qpad, brows.astype(BF16), pad_rows], axis=0)

    heads = [slice(h * TQ, (h + 1) * TQ) for h in range(HPG)]

    def softmax(s_buf, p_buf, al_ref, cs):
        sb = s_buf[:, cs]
        m_i = m_ref[:, cs]
        m_new = jnp.maximum(m_i, jnp.max(sb, axis=0, keepdims=True).astype(F32))
        p_buf[:, cs] = jnp.exp2(sb - m_new.astype(BF16))
        al_ref[:, cs] = jnp.exp2(m_i - m_new)
        m_ref[:, cs] = m_new

    def apply_values(p_buf, al_ref, vt, cs):
        acc_ref[:, cs] = al_ref[:, cs] * acc_ref[:, cs] + _dot(vt, p_buf[:, cs])

    for g in range(NSA_G):
        qpad = _group_queries(qt_ref, g)
        m_ref[...] = jnp.full(m_ref.shape, NEG, F32)
        acc_ref[...] = jnp.zeros(acc_ref.shape, F32)
        p_b[...] = jnp.zeros(p_b.shape, BF16)
        al_b[...] = jnp.ones(al_b.shape, F32)

        def step_operands(c):
            k0 = pl.multiple_of(c * SEL_KC, SEL_KC)
            brows = selm_ref[g, pl.ds(pl.multiple_of(c * bps, bps), bps), :]
            return jnp.concatenate([ks_ref[pl.ds(k0, SEL_KC), :], oneh_ref[...]], axis=1), q_aug(qpad, brows)

        def scores(ops, s_buf, cs):
            s_buf[:, cs] = _dot(ops[0], ops[1][:, cs]).astype(BF16)

        def pair(cc, carry):
            c0 = 2 * cc
            ops, vt = step_operands(c0 + 1), vsl_ref[jnp.maximum(c0 - 1, 0), g]
            for cs in heads:
                scores(ops, s_b, cs)
                softmax(s_a, p_a, al_a, cs)
                apply_values(p_b, al_b, vt, cs)
            ops, vt = step_operands(jnp.minimum(c0 + 2, n_kc - 1)), vsl_ref[c0, g]
            for cs in heads:
                scores(ops, s_a, cs)
                softmax(s_b, p_b, al_b, cs)
                apply_values(p_a, al_a, vt, cs)
            return carry

        ops = step_operands(0)
        for cs in heads:
            scores(ops, s_a, cs)
        lax.fori_loop(0, n_pairs, pair, 0)
        vt = vsl_ref[jnp.maximum(2 * n_pairs - 1, 0), g]
        for cs in heads:
            apply_values(p_b, al_b, vt, cs)

        for d in range(TQ // SEL_KC):
            k_aug = jnp.concatenate([ksd_ref[d * SEL_KC:(d + 1) * SEL_KC, :], oneh_ref[...]], axis=1)
            qa = q_aug(qpad, seld_ref[g, d * bps:(d + 1) * bps, :])
            vt = vsl_ref[i * (TQ // SEL_KC) + d, g]
            for cs in heads:
                sd = _dot(k_aug, qa[:, cs])
                s_a[:, cs] = jnp.where(diag_ok[d * SEL_KC:(d + 1) * SEL_KC, :], sd, NEG).astype(BF16)
                softmax(s_a, p_a, al_a, cs)
                apply_values(p_a, al_a, vt, cs)
        acc_s = acc_ref[...]
        o_s = acc_s[0:NSA_D, :] * (1.0 / acc_s[NSA_D:NSA_D + 1, :])

        for hp in range(HPG // 2):
            tiles = []
            for h in (2 * hp, 2 * hp + 1):
                gr = (g * HPG + h) * 3 + 1
                tiles.append(gt_ref[gr:gr + 1, :] * o_s[:, h * TQ:(h + 1) * TQ])
            cols = slice((g * HPG + 2 * hp) * NSA_D, (g * HPG + 2 * hp + 2) * NSA_D)
            o_ref[:, cols] = part_ref[:, cols] + jnp.concatenate(tiles, axis=0).T


def _nsa_selected(qt, ks, vsl, selm, sel, gt, part, B, S):
    HD = NSA_H * NSA_D
    TQ = SEL_TQ
    W = NSA_H // NSA_G * TQ
    n_sel = S // SLC_LEN
    assert S % (2 * SEL_KC) == 0 and TQ % SEL_KC == 0
    whole = lambda shape: pl.BlockSpec((None,) + shape, lambda b, i: (b,) + (0,) * len(shape),
                                       pipeline_mode=pl.Buffered(1))
    oneh = jnp.asarray(np.arange(SEL_KC)[:, None] // SLC_LEN == np.arange(LANE)[None, :], BF16)
    return pl.pallas_call(
        _nsa_selected_body,
        out_shape=jax.ShapeDtypeStruct((B * S, HD), F32),
        grid=(B, S // TQ),
        in_specs=[pl.BlockSpec((None, HD, TQ), lambda b, i: (b, 0, i)),
                  whole((S, LANE)), pl.BlockSpec((None, TQ, LANE), lambda b, i: (b, i, 0)),
                  whole(vsl.shape[1:]),
                  pl.BlockSpec((None, NSA_G, n_sel, TQ), lambda b, i: (b, 0, 0, i)),
                  pl.BlockSpec((None, NSA_G, TQ // SLC_LEN, TQ), lambda b, i: (b, 0, i, i)),
                  pl.BlockSpec((None, LANE, TQ), lambda b, i: (b, 0, i)),
                  pl.BlockSpec((TQ, HD), lambda b, i: (b * (S // TQ) + i, 0)),
                  _const_spec((SEL_KC, LANE))],
        out_specs=pl.BlockSpec((TQ, HD), lambda b, i: (b * (S // TQ) + i, 0)),
        scratch_shapes=[pltpu.VMEM((SEL_KC, W), BF16)] * 4 + [pltpu.VMEM((1, W), F32)] * 3
                       + [pltpu.VMEM((VT_ROWS, W), F32)],
        compiler_params=_params(("parallel", "arbitrary")),
        name="nsa_selected",
    )(qt, ks, ks, vsl, selm, sel, gt, part, oneh)


def _merge_body(x_ref, oa_ref, ob_ref, oc_ref, gates_ref, wbr_ref, wo_ref, o_ref):
    D = x_ref.shape[1]
    y = jnp.zeros(x_ref.shape, F32)
    for j, br in enumerate((oa_ref, ob_ref, oc_ref)):
        y = y + _sigmoid(gates_ref[:, j * D:(j + 1) * D].astype(F32)) * _dot(br[...].astype(BF16), wbr_ref[j])
    o_ref[...] = x_ref[...] + _dot(y.astype(BF16), wo_ref[...])


def _merge(x, o_a, o_b, o_c, gates, wbr, wo, tm=512):
    T, D = x.shape
    row = lambda w: pl.BlockSpec((tm, w), lambda i: (i, 0))
    return pl.pallas_call(
        _merge_body,
        out_shape=jax.ShapeDtypeStruct((T, D), F32),
        grid=(T // tm,),
        in_specs=[row(D), row(MIX_W), row(MIX_W), row(MIX_W), row(3 * D), _const_spec(wbr.shape), _const_spec(wo.shape)],
        out_specs=row(D),
        compiler_params=_params(("parallel",)),
        name="merge_out",
    )(x, o_a, o_b, o_c, gates, wbr, wo)


def _pack_in_proj(w_in, b_in):
    ws, bs = [], []
    for _, off, w, wp in _SEGS:
        ws.append(jnp.pad(w_in[..., off:off + w], ((0, 0), (0, 0), (0, wp - w))))
        bs.append(jnp.pad(b_in[..., off:off + w], ((0, 0), (0, wp - w))))
    return jnp.concatenate(ws, axis=-1).astype(BF16), jnp.concatenate(bs, axis=-1)[:, None, :]


def _rope_tables(pos):
    half = NSA_D // 2
    freqs = ROPE_THETA ** (-jnp.arange(half, dtype=F32) / half)
    ang = pos.astype(F32)[:, None] * freqs[None, :]
    cos, sin = jnp.cos(ang), jnp.sin(ang)
    reps = LANE // NSA_D
    return jnp.tile(jnp.concatenate([cos, cos], axis=1), (1, reps)), jnp.tile(jnp.concatenate([-sin, sin], axis=1), (1, reps))


def _cmp_weights(w1, w2):
    L = w1.shape[0]
    w1r = w1.reshape(L, 2, 2, CMP_STRIDE, NSA_D, CMP_HIDDEN)
    eye_g = jnp.eye(NSA_G, dtype=w1.dtype)
    ex = jnp.einsum('lwstdh,gk->lwstgdkh', w1r, eye_g)
    ex = ex.reshape(L, 2, 2, CMP_STRIDE * NSA_G * NSA_D, NSA_G * CMP_HIDDEN)
    w2x = jnp.einsum('lwhd,gk->lwghkd', w2, eye_g).reshape(L, 2, NSA_G * CMP_HIDDEN, NSA_G * NSA_D)
    return ex[:, :, 0].astype(BF16), ex[:, :, 1].astype(BF16), w2x.astype(BF16)


def _cmp_pos_rows(pe):
    L = pe.shape[0]
    r = pe.reshape(L, 2, 2, CMP_STRIDE, 1, NSA_D)
    return jnp.broadcast_to(r, (L, 2, 2, CMP_STRIDE, NSA_G, NSA_D)).reshape(L, 2, 2, CMP_STRIDE * NSA_G * NSA_D)


def _lane_gain(g):
    return jnp.tile(g, LANE // g.shape[-1])[None, :]


def _layer(x, lw, consts, B, S):
    cos, sin, cos_c, sin_c, gavg = consts
    T = B * S
    x = _ffn(x, lw["ffn1_norm"], lw["ffn1_wg"], lw["ffn1_wu"], lw["ffn1_wd"])
    z = dict(zip([s[0] for s in _SEGS], _proj(x, lw["mix_norm"], lw["w_in"], lw["b_in"])))
    o_a = _gla(z["a_q"], z["a_k"], z["a_v"], z["a_r"], z["a_lr"], lw["gla_wa"], lw["gla_ba"], lw["gla_gn"], B, S)
    o_c = _mlstm(z["c_qk"], z["c_v"], z["c_o"], z["c_if"], lw["conv_w"], lw["conv_b"], B, S)
    qn = lw["nsa_qk_norm"]
    qt, ks, kw, vsl, vwd, gt = _nsa_prep(z["b_q"], z["b_ks"], z["b_kw"], z["b_vs"], z["b_vw"], z["b_g"], cos, sin,
                                         _lane_gain(qn[0]), _lane_gain(qn[2]), _lane_gain(qn[3]), gavg, B, S)
    n_blk = S // CMP_STRIDE
    xk = z["b_kc"].reshape(B, n_blk, CMP_STRIDE * LANE)
    xv = z["b_vc"].reshape(B, n_blk, CMP_STRIDE * LANE)
    kc, vct = _nsa_cmp(xk, xv, lw["cmp_pe"], lw["cmp_wt"], lw["cmp_wb"], lw["cmp_w2"], cos_c, sin_c,
                       _lane_gain(qn[1]), gavg)
    part, sel, selm = _nsa_select(qt, kc, vct, kw, vwd, gt, B, S)
    o_b = _nsa_selected(qt, ks, vsl, selm, sel, gt, part, B, S)
    x = _merge(x, o_a, o_b, o_c, z["gates"], lw["w_branch"], lw["w_out"])
    return _ffn(x, lw["ffn2_norm"], lw["ffn2_wg"], lw["ffn2_wu"], lw["ffn2_wd"])


def kernel(x, ffn1_norm, ffn1_w_gate, ffn1_w_up, ffn1_w_down, mix_norm, w_in, b_in, gla_w_alpha, gla_b_alpha, gla_out_norm, nsa_qk_norm, nsa_cmp_pos, nsa_cmp_w1, nsa_cmp_w2, mlstm_conv_w, mlstm_conv_b, w_branch, w_out, ffn2_norm, ffn2_w_gate, ffn2_w_up, ffn2_w_down):
    B, S, D = x.shape
    w_in_p, b_in_p = _pack_in_proj(w_in, b_in)
    cmp_wt, cmp_wb, cmp_w2 = _cmp_weights(nsa_cmp_w1, nsa_cmp_w2)
    layers = {
        "ffn1_norm": ffn1_norm[:, None, :], "ffn1_wg": ffn1_w_gate.astype(BF16), "ffn1_wu": ffn1_w_up.astype(BF16),
        "ffn1_wd": ffn1_w_down.astype(BF16),
        "mix_norm": mix_norm[:, None, :], "w_in": w_in_p, "b_in": b_in_p,
        "gla_wa": jnp.pad(gla_w_alpha, ((0, 0), (0, LANE - GLA_RANK), (0, 0))), "gla_ba": gla_b_alpha[:, None, :],
        "gla_gn": gla_out_norm[:, None, :],
        "nsa_qk_norm": nsa_qk_norm, "cmp_pe": _cmp_pos_rows(nsa_cmp_pos), "cmp_wt": cmp_wt, "cmp_wb": cmp_wb,
        "cmp_w2": cmp_w2,
        "conv_w": mlstm_conv_w, "conv_b": mlstm_conv_b[:, None, :],
        "w_branch": w_branch.astype(BF16), "w_out": w_out.astype(BF16),
        "ffn2_norm": ffn2_norm[:, None, :], "ffn2_wg": ffn2_w_gate.astype(BF16), "ffn2_wu": ffn2_w_up.astype(BF16),
        "ffn2_wd": ffn2_w_down.astype(BF16),
    }
    n_blk = S // CMP_STRIDE
    cos, sin = _rope_tables(jnp.arange(S))
    cos_c, sin_c = _rope_tables(jnp.arange(n_blk) * CMP_STRIDE + CMP_LEN - 1)
    gavg = jnp.asarray(np.kron(np.eye(LANE // NSA_D), np.full((NSA_D, NSA_D), 1.0 / NSA_D)), BF16)
    consts = (cos, sin, cos_c, sin_c, gavg)

    out = x.reshape(B * S, D)
    for l in range(w_in.shape[0]):
        out = _layer(out, {name: w[l] for name, w in layers.items()}, consts, B, S)
    return out.reshape(B, S, D)
```

```python
import functools

import numpy as np
import jax
import jax.numpy as jnp
from jax import lax
from jax.experimental import pallas as pl
from jax.experimental.pallas import tpu as pltpu

F32 = jnp.float32
BF16 = jnp.bfloat16

RMS_EPS = 1e-6
ROPE_THETA = 10000.0

GLA_H, GLA_DK, GLA_DV, GLA_RANK, GLA_GATE_NORM = 4, 64, 128, 16, 16.0
GLA_SUB = 16
NSA_H, NSA_G, NSA_D = 8, 2, 64
CMP_LEN, CMP_STRIDE, CMP_HIDDEN = 32, 16, 256
SLC_LEN, SLC_TOPK, WIN = 64, 16, 512
FORCED_SCORE = 1e4
MLSTM_H, MLSTM_DK, MLSTM_DV, CONV_W = 4, 64, 128, 4
MLSTM_CHUNK = 64
MIX_W = 512
LANE = 128
MXU_N = 256
NEG = -1e30
LOG2E = 1.4426950408889634
SEL_KC = 512
SEL_TQ = 512
PS_PAD = 8
CMP_VARIANTS = 4
VT_ROWS = 80

_SEGS = (
    ("a_q", 0, 256, 256), ("a_k", 256, 256, 256), ("a_v", 512, 512, 512), ("a_r", 1024, 512, 512),
    ("b_q", 1552, 512, 512), ("c_qk", 2856, 512, 512), ("c_v", 3368, 512, 512), ("c_o", 3880, 512, 512),
    ("gates", 4400, 3072, 3072),
    ("a_lr", 1536, 16, 128), ("b_kc", 2064, 128, 128), ("b_vc", 2192, 128, 128), ("b_ks", 2320, 128, 128),
    ("b_vs", 2448, 128, 128), ("b_kw", 2576, 128, 128), ("b_vw", 2704, 128, 128), ("b_g", 2832, 24, 128),
    ("c_if", 4392, 8, 128),
)
_N_PACK = sum(s[3] for s in _SEGS)
_BF16_SEGS = ("gates", "c_v", "b_vs", "b_vw")


def _dot(a, b):
    return jnp.dot(a, b, preferred_element_type=F32)


def _dot_nt(a, b):
    return lax.dot_general(a, b, (((1,), (1,)), ((), ())), preferred_element_type=F32)


def _split2(a):
    hi = a.astype(BF16)
    lo = (a - hi.astype(F32)).astype(BF16)
    return hi, lo


def _dot_l2(a, b):
    hi, lo = _split2(a)
    return _dot(hi, b) + _dot(lo, b)


def _dot_r2(a, b):
    hi, lo = _split2(b)
    return _dot(a, hi) + _dot(a, lo)


def _log_sigmoid(x):
    return jnp.minimum(x, 0.0) - jnp.log(1.0 + jnp.exp(-jnp.abs(x)))


def _sigmoid(x):
    return 1.0 / (1.0 + jnp.exp(-x))


def _silu(x):
    return x * _sigmoid(x)


def _iota(shape, dim):
    return lax.broadcasted_iota(jnp.int32, shape, dim)


def _const_spec(shape):
    nd = len(shape)
    return pl.BlockSpec(shape, lambda *_: (0,) * nd, pipeline_mode=pl.Buffered(1))


def _params(sem, vmem_mb=56):
    return pltpu.CompilerParams(dimension_semantics=sem, vmem_limit_bytes=vmem_mb * 1024 * 1024)


def _ffn_body(x_ref, g_ref, wg_ref, wu_ref, wd_ref, o_ref, *, bounds):
    x = x_ref[...]
    ms = jnp.mean(x * x, axis=-1, keepdims=True)
    h = (x * lax.rsqrt(ms + RMS_EPS) * g_ref[...]).astype(BF16)
    acc = jnp.zeros(x.shape, F32)
    for lo, hi in zip(bounds[:-1], bounds[1:]):
        a = _dot(h, wg_ref[:, lo:hi])
        u = _dot(h, wu_ref[:, lo:hi])
        t = (_silu(a) * u).astype(BF16)
        acc = acc + _dot(t, wd_ref[lo:hi, :])
    o_ref[...] = x + 0.5 * acc


def _ffn(x, g, wg, wu, wd, tm=512):
    T, D = x.shape
    F = wg.shape[1]
    bounds = (0, -(-F // (2 * MXU_N)) * MXU_N, F)
    return pl.pallas_call(
        functools.partial(_ffn_body, bounds=bounds),
        out_shape=jax.ShapeDtypeStruct((T, D), F32),
        grid=(T // tm,),
        in_specs=[pl.BlockSpec((tm, D), lambda i: (i, 0)), _const_spec((1, D)),
                  _const_spec((D, F)), _const_spec((D, F)), _const_spec((F, D))],
        out_specs=pl.BlockSpec((tm, D), lambda i: (i, 0)),
        compiler_params=_params(("parallel",)),
        name="ffn",
    )(x, g, wg, wu, wd)


def _proj_body(x_ref, g_ref, w_ref, b_ref, *o_refs):
    x = x_ref[...]
    ms = jnp.mean(x * x, axis=-1, keepdims=True)
    h = (x * lax.rsqrt(ms + RMS_EPS) * g_ref[...]).astype(BF16)
    off, i = 0, 0
    while i < len(o_refs):
        group, w = [], 0
        while i < len(o_refs) and (not group or w % MXU_N):
            group.append(o_refs[i])
            w += o_refs[i].shape[1]
            i += 1
        z = _dot(h, w_ref[:, off:off + w]) + b_ref[:, off:off + w]
        c = 0
        for o_ref in group:
            o_ref[...] = z[:, c:c + o_ref.shape[1]].astype(o_ref.dtype)
            c += o_ref.shape[1]
        off += w


def _proj(x, g, w, b, tm=256):
    T, D = x.shape
    return pl.pallas_call(
        _proj_body,
        out_shape=[jax.ShapeDtypeStruct((T, s[3]), BF16 if s[0] in _BF16_SEGS else F32) for s in _SEGS],
        grid=(T // tm,),
        in_specs=[pl.BlockSpec((tm, D), lambda i: (i, 0)), _const_spec((1, D)),
                  _const_spec((D, _N_PACK)), _const_spec((1, _N_PACK))],
        out_specs=[pl.BlockSpec((tm, s[3]), lambda i: (i, 0)) for s in _SEGS],
        compiler_params=_params(("parallel",)),
        name="in_proj",
    )(x, g, w, b)


def _gla_body(q_ref, k_ref, v_ref, r_ref, lr_ref, wa_ref, ba_ref, gn_ref, tri_ref, bones_ref, eh_ref,
              o_ref, st_ref, qs_ref, c_ref, tot_ref, kst_ref, vt4_ref, oi_ref):
    nB, Lc = q_ref.shape[0], q_ref.shape[1]
    n_sub = Lc // GLA_SUB
    HK = GLA_H * GLA_DK

    @pl.when(pl.program_id(0) == 0)
    def _():
        st_ref[...] = jnp.zeros(st_ref.shape, F32)

    lane_h = _iota((1, HK), 1) >> (GLA_DK.bit_length() - 1)
    row_i = _iota((GLA_SUB, 1), 0)
    col_r = _iota((1, GLA_H * Lc), 1) & (Lc - 1)
    wa_hi, wa_lo = _split2(wa_ref[...])

    for b in range(nB):
        lr_hi, lr_lo = _split2(lr_ref[b])
        u = _dot(lr_hi, wa_hi) + _dot(lr_hi, wa_lo) + _dot(lr_lo, wa_hi) + ba_ref[...]
        g = _log_sigmoid(u) * (1.0 / GLA_GATE_NORM)
        c = _dot_r2(tri_ref[...], g)
        tot = _dot_r2(bones_ref[...], g)
        qs_ref[b] = q_ref[b] * (GLA_DK ** -0.5)
        c_ref[b] = c
        tot_ref[b] = tot
        kt = k_ref[b] * jnp.exp(tot - c)
        vt = v_ref[b].T.astype(BF16)
        for h in range(GLA_H):
            kst_ref[b, h * Lc:(h + 1) * Lc, :] = jnp.where(lane_h == h, kt, 0.0).astype(BF16)
            vt4_ref[b, :, h * Lc:(h + 1) * Lc] = vt[h * GLA_DV:(h + 1) * GLA_DV, :]

    def sub(b, s):
        rows = slice(s * GLA_SUB, (s + 1) * GLA_SUB)
        qs = qs_ref[b, rows, :]
        cs = c_ref[b, rows, :]
        st = st_ref[b]
        qd = qs * jnp.exp(cs)
        q4 = jnp.concatenate([jnp.where(lane_h == h, qd, 0.0) for h in range(GLA_H)], axis=0).astype(BF16)
        inter4 = _dot_nt(q4, st.astype(BF16))
        inter = jnp.concatenate([inter4[h * GLA_SUB:(h + 1) * GLA_SUB, :] for h in range(GLA_H)], axis=1)
        xs = []
        for j in range(GLA_SUB):
            r = s * GLA_SUB + j
            x = qs * k_ref[b, r:r + 1, :] * jnp.exp(jnp.minimum(cs - c_ref[b, r:r + 1, :], 0.0))
            xs.append(jnp.where(row_i >= j, x, 0.0))
        r_all = _dot(jnp.concatenate(xs, axis=0).astype(BF16), eh_ref[...])
        intra = jnp.zeros((GLA_SUB, r_all.shape[1]), F32)
        for j in range(GLA_SUB):
            r = s * GLA_SUB + j
            intra = intra + r_all[j * GLA_SUB:(j + 1) * GLA_SUB, :] * v_ref[b, r:r + 1, :]
        oi_ref[b, rows, :] = inter + intra
        dec = jnp.exp(tot_ref[b, s * GLA_SUB:s * GLA_SUB + 1, :])
        in_sub = (col_r >= s * GLA_SUB) & (col_r < (s + 1) * GLA_SUB)
        vtm = jnp.where(in_sub, vt4_ref[b], jnp.zeros((), BF16))
        st_ref[b] = dec * st + _dot(vtm, kst_ref[b])

    for s in range(n_sub):
        for b in range(nB):
            sub(b, s)

    gn = gn_ref[...]
    for b in range(nB):
        for h in range(GLA_H):
            sl = slice(h * GLA_DV, (h + 1) * GLA_DV)
            o = oi_ref[b, :, sl]
            ms = jnp.mean(o * o, axis=-1, keepdims=True)
            o_ref[b, :, sl] = (o * lax.rsqrt(ms + RMS_EPS) * gn * _silu(r_ref[b, :, sl])).astype(o_ref.dtype)


def _gla_consts(Lc):
    r = np.arange(Lc)
    same = (r[:, None] // GLA_SUB) == (r[None, :] // GLA_SUB)
    tri = (same & (r[None, :] <= r[:, None])).astype(np.float32)
    bones = same.astype(np.float32)
    hk = np.arange(GLA_H * GLA_DK) // GLA_DK
    hv = np.arange(GLA_H * GLA_DV) // GLA_DV
    eh = (hk[:, None] == hv[None, :]).astype(np.float32)
    return jnp.asarray(tri, BF16), jnp.asarray(bones, BF16), jnp.asarray(eh, BF16)


def _gla(a_q, a_k, a_v, a_r, a_lr, wa, ba, gn, B, S, Lc=128):
    HK, HV = GLA_H * GLA_DK, GLA_H * GLA_DV
    nb = S // Lc
    assert Lc & (Lc - 1) == 0
    tri, bones, eh = _gla_consts(Lc)
    row = lambda w: pl.BlockSpec((B, Lc, w), lambda i: (0, i, 0))
    seq = lambda t: t.reshape(B, S, t.shape[-1])
    out = pl.pallas_call(
        _gla_body,
        out_shape=jax.ShapeDtypeStruct((B, S, HV), BF16),
        grid=(nb,),
        in_specs=[row(HK), row(HK), row(HV), row(HV), row(LANE),
                  _const_spec((LANE, HK)), _const_spec((1, HK)), _const_spec((1, GLA_DV)),
                  _const_spec((Lc, Lc)), _const_spec((Lc, Lc)), _const_spec((HK, HV))],
        out_specs=row(HV),
        scratch_shapes=[pltpu.VMEM((B, GLA_DV, HK), F32), pltpu.VMEM((B, Lc, HK), F32), pltpu.VMEM((B, Lc, HK), F32),
                        pltpu.VMEM((B, Lc, HK), F32), pltpu.VMEM((B, GLA_H * Lc, HK), BF16),
                        pltpu.VMEM((B, GLA_DV, GLA_H * Lc), BF16), pltpu.VMEM((B, Lc, HV), F32)],
        compiler_params=_params(("arbitrary",)),
        name="gla",
    )(seq(a_q), seq(a_k), seq(a_v), seq(a_r), seq(a_lr), wa, ba, gn, tri, bones, eh)
    return out.reshape(B * S, HV)


def _mlstm_body(qk_ref, v_ref, og_ref, if_ref, cw_ref, cb_ref, tri_ref, bdm_ref,
                o_ref, xx_ref, ct_ref, n_ref, m_ref, kw_ref, vt_ref, act_ref):
    Lc = v_ref.shape[0]
    L = MLSTM_CHUNK
    HK = MLSTM_H * MLSTM_DK
    tail = 8

    @pl.when(pl.program_id(1) == 0)
    def _():
        ct_ref[...] = jnp.zeros(ct_ref.shape, F32)
        n_ref[...] = jnp.zeros(n_ref.shape, F32)
        m_ref[...] = jnp.zeros(m_ref.shape, F32)
        xx_ref[0:tail, :] = jnp.zeros((tail, xx_ref.shape[1]), F32)

    xx_ref[tail:tail + Lc, :] = qk_ref[...]
    y = jnp.zeros((Lc, 2 * HK), F32) + cb_ref[...]
    for kk in range(CONV_W):
        y = y + cw_ref[kk:kk + 1, :] * xx_ref[pl.ds(tail - (CONV_W - 1) + kk, Lc), :]
    xx_ref[0:tail, :] = xx_ref[Lc:Lc + tail, :]
    act_ref[...] = _silu(y)

    gates = if_ref[...]
    logf = _log_sigmoid(gates)
    bcum = _dot_r2(tri_ref[...], logf)
    b_t = bcum.T
    q_t = act_ref[:, 0:HK].T
    qt_hi, qt_lo = _split2(q_t)
    kw_ref[...] = jnp.zeros(kw_ref.shape, BF16)
    vt_ref[...] = v_ref[...].astype(F32).T.astype(BF16)
    lane_hk = _iota((1, HK), 1) >> (MLSTM_DK.bit_length() - 1)
    col_l = _iota((1, Lc), 1)
    lane_w = _iota((1, LANE), 1)
    row_l = _iota((L, 1), 0)
    per_win = LANE // L

    for ci in range(Lc // L):
        p, c = divmod(ci, per_win)
        rows = slice(ci * L, (ci + 1) * L)
        win = slice(p * LANE, (p + 1) * LANE)
        causal_t = ((c * L + row_l) <= lane_w) & ((lane_w >> (L.bit_length() - 1)) == c)
        k_all = act_ref[rows, HK:2 * HK] * (MLSTM_DK ** -0.5)
        k_bf = k_all.astype(BF16)
        ct = ct_ref[...]
        ct_bf = ct.astype(BF16)
        n_row = n_ref[...]
        n_hi, n_lo = _split2(n_row)
        n8 = jnp.concatenate([jnp.where(lane_hk == h, part, jnp.zeros((), BF16))
                              for part in (n_hi, n_lo) for h in range(MLSTM_H)], axis=0)
        qn8 = _dot(n8, qt_hi[:, win]) + _dot(n8, qt_lo[:, win])
        wk_full = jnp.zeros((L, HK), F32)
        dec_row = jnp.zeros((1, HK), F32)
        for h in range(MLSTM_H):
            icol = gates[rows, h:h + 1]
            bcol = bcum[rows, MLSTM_H + h:MLSTM_H + h + 1]
            brow = b_t[MLSTM_H + h:MLSTM_H + h + 1, win]
            m_st = m_ref[0:1, h:h + 1]
            dmat_t = jnp.where(causal_t, brow + (icol - bcol), -jnp.inf)
            inter_log = brow + m_st
            m_row = jnp.maximum(inter_log, jnp.max(dmat_t, axis=0, keepdims=True))
            w_inter = jnp.exp(inter_log - m_row)
            head_rows = slice(h * MLSTM_DK, (h + 1) * MLSTM_DK)
            zero_q = jnp.zeros((MLSTM_DK, LANE), BF16)
            qh = jnp.concatenate([qt_hi[head_rows, win] if hh == h else zero_q for hh in range(MLSTM_H)], axis=0)
            s_qk = _dot(k_bf, qh) * jnp.exp(dmat_t - m_row)
            den = w_inter * (qn8[h:h + 1, :] + qn8[MLSTM_H + h:MLSTM_H + h + 1, :]) + jnp.sum(s_qk, axis=0, keepdims=True)
            s_bf = s_qk.astype(BF16)
            pads = [jnp.zeros((L, LANE), BF16)] * per_win
            pads[c] = s_bf
            dv = slice(h * MLSTM_DV, (h + 1) * MLSTM_DV)
            num_t = w_inter * _dot(ct_bf[dv, :], qt_hi[:, win]) + _dot(vt_ref[dv, win], jnp.concatenate(pads, axis=0))
            h_t = num_t / jnp.maximum(jnp.abs(den), jnp.exp(-m_row))
            o_ref[rows, dv] = (h_t.T[c * L:(c + 1) * L, :] * _sigmoid(og_ref[rows, dv])).astype(o_ref.dtype)
            m_new = m_row[:, c * L + L - 1:c * L + L]
            b_last = bcol[L - 1:L, :]
            w_k = jnp.exp(b_last - bcol + icol - m_new)
            decay = jnp.exp(b_last + m_st - m_new)
            head = lane_hk == h
            wk_full = wk_full + jnp.where(head, w_k, 0.0)
            dec_row = dec_row + jnp.where(head, decay, 0.0)
            m_ref[0:1, h:h + 1] = m_new
        kw = k_all * wk_full
        kw_ref[rows, :] = kw.astype(BF16)
        in_chunk = (col_l >= ci * L) & (col_l < (ci + 1) * L)
        vtm = jnp.where(in_chunk, vt_ref[...], jnp.zeros((), BF16))
        ct_ref[...] = dec_row * ct + _dot(vtm, kw_ref[...]) * bdm_ref[...]
        n_ref[...] = dec_row * n_row + jnp.sum(kw, axis=0, keepdims=True)


def _mlstm_consts(Lc):
    r = np.arange(Lc)
    same = (r[:, None] // MLSTM_CHUNK) == (r[None, :] // MLSTM_CHUNK)
    tri = (same & (r[None, :] <= r[:, None])).astype(np.float32)
    hk = np.arange(MLSTM_H * MLSTM_DK) // MLSTM_DK
    hv = np.arange(MLSTM_H * MLSTM_DV) // MLSTM_DV
    eh = (hk[:, None] == hv[None, :]).astype(np.float32)
    return jnp.asarray(tri, BF16), jnp.asarray(eh.T.copy(), F32)


def _mlstm(c_qk, c_v, c_o, c_if, cw, cb, B, S, Lc=256):
    HK, HV = MLSTM_H * MLSTM_DK, MLSTM_H * MLSTM_DV
    nb = S // Lc
    assert Lc % LANE == 0 and LANE % MLSTM_CHUNK == 0
    tri, bdm = _mlstm_consts(Lc)
    row = lambda w: pl.BlockSpec((Lc, w), lambda b, i: (b * nb + i, 0))
    return pl.pallas_call(
        _mlstm_body,
        out_shape=jax.ShapeDtypeStruct((B * S, HV), BF16),
        grid=(B, nb),
        in_specs=[row(2 * HK), row(HV), row(HV), row(LANE),
                  _const_spec((CONV_W, 2 * HK)), _const_spec((1, 2 * HK)),
                  _const_spec((Lc, Lc)), _const_spec((HV, HK))],
        out_specs=row(HV),
        scratch_shapes=[pltpu.VMEM((Lc + 8, 2 * HK), F32), pltpu.VMEM((HV, HK), F32), pltpu.VMEM((1, HK), F32),
                        pltpu.VMEM((8, LANE), F32), pltpu.VMEM((Lc, HK), BF16), pltpu.VMEM((HV, Lc), BF16),
                        pltpu.VMEM((Lc, 2 * HK), F32)],
        compiler_params=_params(("parallel", "arbitrary")),
        name="mlstm",
    )(c_qk, c_v, c_o, c_if, cw, cb, tri, bdm)


def _group_rms(x, gavg, gain):
    ms = _dot_l2(x * x, gavg)
    return x * lax.rsqrt(ms + RMS_EPS) * gain


def _rope_lanes(x, cos, sin_signed):
    half = NSA_D // 2
    first = (_iota((1, LANE), 1) & (NSA_D - 1)) < half
    swapped = jnp.where(first, pltpu.roll(x, LANE - half, 1), pltpu.roll(x, half, 1))
    return x * cos + swapped * sin_signed


def _store_vt_tiles(dst_ref, vt, width):
    ones = jnp.ones((VT_ROWS - NSA_D, width), BF16)
    for j in range(vt.shape[1] // width):
        for g in range(NSA_G):
            dst_ref[j, g, 0:NSA_D, :] = vt[g * NSA_D:(g + 1) * NSA_D, j * width:(j + 1) * width]
            dst_ref[j, g, NSA_D:VT_ROWS, :] = ones


def _nsa_prep_body(q_ref, ks_ref, kw_ref, vs_ref, vw_ref, g_ref, cos_ref, sin_ref, gq_ref, gs_ref, gw_ref, gavg_ref,
                   qt_ref, kso_ref, kwo_ref, vsl_ref, vwd_ref, gt_ref):
    cos, sin = cos_ref[...], sin_ref[...]
    gavg = gavg_ref[...]
    for cb in range(NSA_H * NSA_D // LANE):
        sl = slice(cb * LANE, (cb + 1) * LANE)
        qn = _rope_lanes(_group_rms(q_ref[:, sl], gavg, gq_ref[...]), cos, sin) * (NSA_D ** -0.5 * LOG2E)
        qt_ref[sl, :] = qn.T.astype(BF16)
    kso_ref[...] = _rope_lanes(_group_rms(ks_ref[...], gavg, gs_ref[...]), cos, sin).astype(BF16)
    kwo_ref[...] = _rope_lanes(_group_rms(kw_ref[...], gavg, gw_ref[...]), cos, sin).astype(BF16)
    vst = vs_ref[...].astype(F32).T.astype(BF16)
    _store_vt_tiles(vsl_ref, vst, SEL_KC)
    _store_vt_tiles(vwd_ref, vw_ref[...].astype(F32).T.astype(BF16), LANE)
    gt_ref[...] = _sigmoid(g_ref[...]).T


def _nsa_prep(b_q, b_ks, b_kw, b_vs, b_vw, b_g, cos, sin, gq, gs, gw, gavg, B, S, tm=SEL_KC):
    nb = S // tm
    row = lambda w: pl.BlockSpec((tm, w), lambda b, i: (b * nb + i, 0))
    tab = pl.BlockSpec((tm, LANE), lambda b, i: (i, 0))
    HD = NSA_H * NSA_D
    vt_shape = lambda width: jax.ShapeDtypeStruct((B, S // width, NSA_G, VT_ROWS, width), BF16)
    vt_spec = lambda width: pl.BlockSpec((None, tm // width, NSA_G, VT_ROWS, width), lambda b, i: (b, i, 0, 0, 0))
    return pl.pallas_call(
        _nsa_prep_body,
        out_shape=[jax.ShapeDtypeStruct((B, HD, S), BF16),
                   jax.ShapeDtypeStruct((B, S, LANE), BF16),
                   jax.ShapeDtypeStruct((B, S, LANE), BF16),
                   vt_shape(SEL_KC),
                   vt_shape(LANE),
                   jax.ShapeDtypeStruct((B, LANE, S), F32)],
        grid=(B, nb),
        in_specs=[row(HD), row(LANE), row(LANE), row(LANE), row(LANE), row(LANE), tab, tab,
                  _const_spec((1, LANE)), _const_spec((1, LANE)), _const_spec((1, LANE)), _const_spec((LANE, LANE))],
        out_specs=[pl.BlockSpec((None, HD, tm), lambda b, i: (b, 0, i)),
                   pl.BlockSpec((None, tm, LANE), lambda b, i: (b, i, 0)),
                   pl.BlockSpec((None, tm, LANE), lambda b, i: (b, i, 0)),
                   vt_spec(SEL_KC), vt_spec(LANE),
                   pl.BlockSpec((None, LANE, tm), lambda b, i: (b, 0, i))],
        compiler_params=_params(("parallel", "parallel")),
        name="nsa_prep",
    )(b_q, b_ks, b_kw, b_vs, b_vw, b_g, cos, sin, gq, gs, gw, gavg)


def _gelu_tanh(x):
    return 0.5 * x * (1.0 + jnp.tanh(0.7978845608028654 * (x + 0.044715 * x * x * x)))


def _nsa_cmp_body(xk_ref, xv_ref, pe_ref, wt_ref, wb_ref, w2_ref, cos_ref, sin_ref, gk_ref, gavg_ref,
                  kc_ref, vct_ref):
    n = xk_ref.shape[0]

    def compress(x, which):
        u = _dot((x + pe_ref[which, 0:1, :]).astype(BF16), wt_ref[which])
        v = _dot((x + pe_ref[which, 1:2, :]).astype(BF16), wb_ref[which])
        hid = u + pltpu.roll(v, n - 1, 0)
        return _dot(_gelu_tanh(hid).astype(BF16), w2_ref[which])

    ck = compress(xk_ref[...], 0)
    kc_ref[...] = _rope_lanes(_group_rms(ck, gavg_ref[...], gk_ref[...]), cos_ref[...], sin_ref[...]).astype(BF16)
    vct = compress(xv_ref[...], 1).T.astype(BF16)
    for g in range(NSA_G):
        vct_ref[g] = vct[g * NSA_D:(g + 1) * NSA_D, :]


def _nsa_cmp(xk, xv, pe, wt, wb, w2, cos, sin, gk, gavg):
    B, n, W = xk.shape
    return pl.pallas_call(
        _nsa_cmp_body,
        out_shape=[jax.ShapeDtypeStruct((B, n, LANE), BF16),
                   jax.ShapeDtypeStruct((B, NSA_G, NSA_D, n), BF16)],
        grid=(B,),
        in_specs=[pl.BlockSpec((None, n, W), lambda b: (b, 0, 0)), pl.BlockSpec((None, n, W), lambda b: (b, 0, 0)),
                  _const_spec(pe.shape), _const_spec(wt.shape), _const_spec(wb.shape), _const_spec(w2.shape),
                  _const_spec((n, LANE)), _const_spec((n, LANE)), _const_spec((1, LANE)), _const_spec((LANE, LANE))],
        out_specs=[pl.BlockSpec((None, n, LANE), lambda b: (b, 0, 0)),
                   pl.BlockSpec((None, NSA_G, NSA_D, n), lambda b: (b, 0, 0, 0))],
        compiler_params=_params(("parallel",)),
        name="nsa_compress",
    )(xk, xv, pe, wt, wb, w2, cos, sin, gk, gavg)


def _group_queries(qt_ref, g):
    HPG = NSA_H // NSA_G
    q4 = jnp.concatenate([qt_ref[(g * HPG + h) * NSA_D:(g * HPG + h + 1) * NSA_D, :] for h in range(HPG)], axis=1)
    parts = [jnp.zeros(q4.shape, BF16)] * NSA_G
    parts[g] = q4
    return jnp.concatenate(parts, axis=0)


def _bitonic_merge_desc(xs):
    xs = list(xs)
    j = len(xs) // 2
    while j >= 1:
        for a in range(len(xs)):
            b = a ^ j
            if b > a:
                xs[a], xs[b] = jnp.maximum(xs[a], xs[b]), jnp.minimum(xs[a], xs[b])
        j //= 2
    return xs


def _bitonic_sort_desc(xs):
    if len(xs) == 1:
        return list(xs)
    half = len(xs) // 2
    lo = _bitonic_sort_desc(xs[:half])
    hi = _bitonic_sort_desc(xs[half:])
    return _bitonic_merge_desc(lo + hi[::-1])


def _kth_largest(score, k):
    SUBLANES = 8
    tiles = [score[SUBLANES * v:SUBLANES * (v + 1), :] for v in range(score.shape[0] // SUBLANES)]
    size = max(k, 1 << (len(tiles) - 1).bit_length())
    tiles = tiles + [jnp.full(tiles[0].shape, NEG, F32)] * (size - len(tiles))

    def top_of_two(a, b):
        return _bitonic_merge_desc([jnp.maximum(a[r], b[k - 1 - r]) for r in range(k)])

    tops = [_bitonic_sort_desc(tiles[c:c + k]) for c in range(0, size, k)]
    while len(tops) > 1:
        tops = [top_of_two(tops[c], tops[c + 1]) for c in range(0, len(tops), 2)]
    top = tops[0]
    shift = SUBLANES // 2
    while shift >= 1:
        top = top_of_two(top, [pltpu.roll(t, shift, 0) for t in top])
        shift //= 2
    return top[k - 1][0:1, :]


def _nsa_select_dispatch(*refs, sel_tile, n_variants):
    n_cmp = refs[1].shape[0]
    need = (pl.program_id(1) + 1) * LANE // CMP_STRIDE
    variant = (need - 1) * n_variants // n_cmp
    for k in range(n_variants):
        pl.when(variant == k)(functools.partial(_nsa_select_body, *refs, sel_tile=sel_tile,
                                                n_rows=n_cmp * (k + 1) // n_variants))


def _nsa_select_body(qt_ref, kc_ref, vct_ref, kw_ref, vwd_ref, gt_ref, tril_ref, part_ref, sel_ref, selm_ref, ps_buf,
                     *, sel_tile, n_rows):
    TQ = LANE
    HPG = NSA_H // NSA_G
    W = HPG * TQ
    n_cmp = kc_ref.shape[0]
    n_sel = sel_ref.shape[1]
    i = pl.program_id(1)
    s0 = i * TQ
    t_row = s0 + _iota((1, TQ), 1)
    n_win = WIN // LANE + 1
    j0 = jnp.maximum(i - (n_win - 1), 0)
    w_start = pl.multiple_of(j0 * LANE, LANE)
    own_first = (s0 // sel_tile) * (sel_tile // SLC_LEN)
    ps_buf[:, 0:PS_PAD, :] = jnp.zeros((NSA_G, PS_PAD, TQ), F32)

    for g in range(NSA_G):
        qpad = _group_queries(qt_ref, g)

        cend = _iota((n_rows, 1), 0) * CMP_STRIDE + (CMP_LEN - 1)
        bias_c = jnp.where(cend <= t_row, 0.0, NEG)
        sc = _dot(kc_ref[0:n_rows, :], qpad) + jnp.concatenate([bias_c] * HPG, axis=1)
        m = jnp.max(sc, axis=0, keepdims=True)
        m = jnp.where(m > 0.5 * NEG, m, 0.0)
        p = jnp.exp2(sc - m)
        p = p * (1.0 / jnp.maximum(jnp.sum(p, axis=0, keepdims=True), 1e-30))
        o_c = _dot(vct_ref[g, :, 0:n_rows], p.astype(BF16))
        psum = p[:, 0:TQ]
        for h in range(1, HPG):
            psum = psum + p[:, h * TQ:(h + 1) * TQ]
        ps_buf[g, PS_PAD:PS_PAD + n_rows, :] = psum

        per = SLC_LEN // CMP_STRIDE
        n_blk = n_rows // per
        imp = ps_buf[g, pl.ds(PS_PAD - 1, n_blk, stride=per), :]
        for jj in range(per):
            imp = imp + ps_buf[g, pl.ds(PS_PAD + jj, n_blk, stride=per), :]
        blk = _iota((n_blk, 1), 0)
        cur = t_row >> (SLC_LEN.bit_length() - 1)
        valid = blk * SLC_LEN <= t_row
        forced = (blk == 0) | (blk == cur) | (blk == cur - 1)
        score0 = jnp.where(valid, jnp.where(forced, FORCED_SCORE, imp), NEG)
        kth = _kth_largest(score0, min(SLC_TOPK, n_sel))
        above = score0 > kth
        tied = jnp.where(score0 == kth, 1.0, 0.0)
        need = min(SLC_TOPK, n_sel) - jnp.sum(jnp.where(above, 1.0, 0.0), axis=0, keepdims=True)
        tied_all = jnp.concatenate([tied, jnp.zeros((n_sel - n_blk, TQ), F32)], axis=0) if n_blk < n_sel else tied
        rank = _dot(tril_ref[0:n_blk, :], tied_all.astype(BF16))
        chosen = (above | ((tied > 0.5) & (rank <= need))) & valid
        sel_ref[g, 0:n_blk, :] = jnp.where(chosen, 0.0, NEG)
        selm_ref[g, 0:n_blk, :] = jnp.where(chosen & (blk < own_first), 0.0, NEG)
        if n_blk < n_sel:
            sel_ref[g, n_blk:n_sel, :] = jnp.full((n_sel - n_blk, TQ), NEG, F32)
            selm_ref[g, n_blk:n_sel, :] = jnp.full((n_sel - n_blk, TQ), NEG, F32)

        kpos = w_start + _iota((n_win * LANE, 1), 0)
        bias_w = jnp.where((kpos <= t_row) & (kpos > t_row - WIN), 0.0, NEG)
        sw = _dot(kw_ref[pl.ds(w_start, n_win * LANE), :], qpad) + jnp.concatenate([bias_w] * HPG, axis=1)
        pw = jnp.exp2(sw - jnp.max(sw, axis=0, keepdims=True)).astype(BF16)
        acc_w = jnp.zeros((VT_ROWS, W), F32)
        for r in range(n_win):
            acc_w = acc_w + _dot(vwd_ref[j0 + r, g], pw[r * LANE:(r + 1) * LANE, :])
        o_w = acc_w[0:NSA_D, :] * (1.0 / acc_w[NSA_D:NSA_D + 1, :])

        for hp in range(HPG // 2):
            tiles = []
            for h in (2 * hp, 2 * hp + 1):
                gr = (g * HPG + h) * 3
                cs = slice(h * TQ, (h + 1) * TQ)
                tiles.append(gt_ref[gr:gr + 1, :] * o_c[:, cs] + gt_ref[gr + 2:gr + 3, :] * o_w[:, cs])
            col = (g * HPG + 2 * hp) * NSA_D
            part_ref[:, col:col + 2 * NSA_D] = jnp.concatenate(tiles, axis=0).T


def _nsa_select(qt, kc, vct, kw, vwd, gt, B, S):
    HD = NSA_H * NSA_D
    TQ = LANE
    n_cmp = kc.shape[1]
    n_sel = S // SLC_LEN
    assert S >= WIN + TQ and CMP_LEN == 2 * CMP_STRIDE
    whole = lambda shape: pl.BlockSpec((None,) + shape, lambda b, i: (b,) + (0,) * len(shape),
                                       pipeline_mode=pl.Buffered(1))
    mask_shape = jax.ShapeDtypeStruct((B, NSA_G, n_sel, S), F32)
    mask_spec = pl.BlockSpec((None, NSA_G, n_sel, TQ), lambda b, i: (b, 0, 0, i))
    tril = jnp.asarray(np.tril(np.ones((n_sel, n_sel), np.float32)), BF16)
    return pl.pallas_call(
        functools.partial(_nsa_select_dispatch, sel_tile=SEL_TQ, n_variants=CMP_VARIANTS),
        out_shape=[jax.ShapeDtypeStruct((B * S, HD), F32), mask_shape, mask_shape],
        grid=(B, S // TQ),
        in_specs=[pl.BlockSpec((None, HD, TQ), lambda b, i: (b, 0, i)),
                  whole((n_cmp, LANE)), whole(vct.shape[1:]), whole((S, LANE)), whole(vwd.shape[1:]),
                  pl.BlockSpec((None, LANE, TQ), lambda b, i: (b, 0, i)), _const_spec((n_sel, n_sel))],
        out_specs=[pl.BlockSpec((TQ, HD), lambda b, i: (b * (S // TQ) + i, 0)), mask_spec, mask_spec],
        scratch_shapes=[pltpu.VMEM((NSA_G, n_cmp + PS_PAD, TQ), F32)],
        compiler_params=_params(("parallel", "parallel")),
        name="nsa_select",
    )(qt, kc, vct, kw, vwd, gt, tril)


def _nsa_selected_body(qt_ref, ks_ref, ksd_ref, vsl_ref, selm_ref, seld_ref, gt_ref, part_ref, oneh_ref, o_ref,
                       s_a, s_b, p_a, p_b, al_a, al_b, m_ref, acc_ref):
    TQ = qt_ref.shape[1]
    HPG = NSA_H // NSA_G
    W = HPG * TQ
    n_kc = ks_ref.shape[0] // SEL_KC
    bps = SEL_KC // SLC_LEN
    i = pl.program_id(1)
    n_main = i * (TQ // SEL_KC)
    n_pairs = (n_main + 1) // 2
    diag_ok = _iota((TQ, 1), 0) <= _iota((1, TQ), 1)
    pad_rows = jnp.zeros((LANE - 2 * bps, W), BF16)

    def q_aug(qpad, brows):
        brows = jnp.concatenate([jnp.concatenate([brows] * HPG, axis=1), jnp.zeros((bps, W), F32)], axis=0)
        return jnp.concatenate([qpad, brows.astype(BF16), pad_rows], axis=0)

    heads = [slice(h * TQ, (h + 1) * TQ) for h in range(HPG)]

    def softmax(s_buf, p_buf, al_ref, cs):
        sb = s_buf[:, cs]
        m_i = m_ref[:, cs]
        m_new = jnp.maximum(m_i, jnp.max(sb, axis=0, keepdims=True).astype(F32))
        p_buf[:, cs] = jnp.exp2(sb - m_new.astype(BF16))
        al_ref[:, cs] = jnp.exp2(m_i - m_new)
        m_ref[:, cs] = m_new

    def apply_values(p_buf, al_ref, vt, cs):
        acc_ref[:, cs] = al_ref[:, cs] * acc_ref[:, cs] + _dot(vt, p_buf[:, cs])

    for g in range(NSA_G):
        qpad = _group_queries(qt_ref, g)
        m_ref[...] = jnp.full(m_ref.shape, NEG, F32)
        acc_ref[...] = jnp.zeros(acc_ref.shape, F32)
        p_b[...] = jnp.zeros(p_b.shape, BF16)
        al_b[...] = jnp.ones(al_b.shape, F32)

        def step_operands(c):
            k0 = pl.multiple_of(c * SEL_KC, SEL_KC)
            brows = selm_ref[g, pl.ds(pl.multiple_of(c * bps, bps), bps), :]
            return jnp.concatenate([ks_ref[pl.ds(k0, SEL_KC), :], oneh_ref[...]], axis=1), q_aug(qpad, brows)

        def scores(ops, s_buf, cs):
            s_buf[:, cs] = _dot(ops[0], ops[1][:, cs]).astype(BF16)

        def pair(cc, carry):
            c0 = 2 * cc
            ops, vt = step_operands(c0 + 1), vsl_ref[jnp.maximum(c0 - 1, 0), g]
            for cs in heads:
                scores(ops, s_b, cs)
                softmax(s_a, p_a, al_a, cs)
                apply_values(p_b, al_b, vt, cs)
            ops, vt = step_operands(jnp.minimum(c0 + 2, n_kc - 1)), vsl_ref[c0, g]
            for cs in heads:
                scores(ops, s_a, cs)
                softmax(s_b, p_b, al_b, cs)
                apply_values(p_a, al_a, vt, cs)
            return carry

        ops = step_operands(0)
        for cs in heads:
            scores(ops, s_a, cs)
        lax.fori_loop(0, n_pairs, pair, 0)
        vt = vsl_ref[jnp.maximum(2 * n_pairs - 1, 0), g]
        for cs in heads:
            apply_values(p_b, al_b, vt, cs)

        for d in range(TQ // SEL_KC):
            k_aug = jnp.concatenate([ksd_ref[d * SEL_KC:(d + 1) * SEL_KC, :], oneh_ref[...]], axis=1)
            qa = q_aug(qpad, seld_ref[g, d * bps:(d + 1) * bps, :])
            vt = vsl_ref[i * (TQ // SEL_KC) + d, g]
            for cs in heads:
                sd = _dot(k_aug, qa[:, cs])
                s_a[:, cs] = jnp.where(diag_ok[d * SEL_KC:(d + 1) * SEL_KC, :], sd, NEG).astype(BF16)
                softmax(s_a, p_a, al_a, cs)
                apply_values(p_a, al_a, vt, cs)
        acc_s = acc_ref[...]
        o_s = acc_s[0:NSA_D, :] * (1.0 / acc_s[NSA_D:NSA_D + 1, :])

        for hp in range(HPG // 2):
            tiles = []
            for h in (2 * hp, 2 * hp + 1):
                gr = (g * HPG + h) * 3 + 1
                tiles.append(gt_ref[gr:gr + 1, :] * o_s[:, h * TQ:(h + 1) * TQ])
            cols = slice((g * HPG + 2 * hp) * NSA_D, (g * HPG + 2 * hp + 2) * NSA_D)
            o_ref[:, cols] = (part_ref[:, cols] + jnp.concatenate(tiles, axis=0).T).astype(o_ref.dtype)


def _nsa_selected(qt, ks, vsl, selm, sel, gt, part, B, S):
    HD = NSA_H * NSA_D
    TQ = SEL_TQ
    W = NSA_H // NSA_G * TQ
    n_sel = S // SLC_LEN
    assert S % (2 * SEL_KC) == 0 and TQ % SEL_KC == 0
    whole = lambda shape: pl.BlockSpec((None,) + shape, lambda b, i: (b,) + (0,) * len(shape),
                                       pipeline_mode=pl.Buffered(1))
    oneh = jnp.asarray(np.arange(SEL_KC)[:, None] // SLC_LEN == np.arange(LANE)[None, :], BF16)
    return pl.pallas_call(
        _nsa_selected_body,
        out_shape=jax.ShapeDtypeStruct((B * S, HD), BF16),
        grid=(B, S // TQ),
        in_specs=[pl.BlockSpec((None, HD, TQ), lambda b, i: (b, 0, i)),
                  whole((S, LANE)), pl.BlockSpec((None, TQ, LANE), lambda b, i: (b, i, 0)),
                  whole(vsl.shape[1:]),
                  pl.BlockSpec((None, NSA_G, n_sel, TQ), lambda b, i: (b, 0, 0, i)),
                  pl.BlockSpec((None, NSA_G, TQ // SLC_LEN, TQ), lambda b, i: (b, 0, i, i)),
                  pl.BlockSpec((None, LANE, TQ), lambda b, i: (b, 0, i)),
                  pl.BlockSpec((TQ, HD), lambda b, i: (b * (S // TQ) + i, 0)),
                  _const_spec((SEL_KC, LANE))],
        out_specs=pl.BlockSpec((TQ, HD), lambda b, i: (b * (S // TQ) + i, 0)),
        scratch_shapes=[pltpu.VMEM((SEL_KC, W), BF16)] * 4 + [pltpu.VMEM((1, W), F32)] * 3
                       + [pltpu.VMEM((VT_ROWS, W), F32)],
        compiler_params=_params(("parallel", "arbitrary")),
        name="nsa_selected",
    )(qt, ks, ks, vsl, selm, sel, gt, part, oneh)


def _merge_body(x_ref, oa_ref, ob_ref, oc_ref, gates_ref, wbr_ref, wo_ref, o_ref):
    D = x_ref.shape[1]
    y = jnp.zeros(x_ref.shape, F32)
    for j, br in enumerate((oa_ref, ob_ref, oc_ref)):
        y = y + _sigmoid(gates_ref[:, j * D:(j + 1) * D].astype(F32)) * _dot(br[...].astype(BF16), wbr_ref[j])
    o_ref[...] = x_ref[...] + _dot(y.astype(BF16), wo_ref[...])


def _merge(x, o_a, o_b, o_c, gates, wbr, wo, tm=512):
    T, D = x.shape
    row = lambda w: pl.BlockSpec((tm, w), lambda i: (i, 0))
    return pl.pallas_call(
        _merge_body,
        out_shape=jax.ShapeDtypeStruct((T, D), F32),
        grid=(T // tm,),
        in_specs=[row(D), row(MIX_W), row(MIX_W), row(MIX_W), row(3 * D), _const_spec(wbr.shape), _const_spec(wo.shape)],
        out_specs=row(D),
        compiler_params=_params(("parallel",)),
        name="merge_out",
    )(x, o_a, o_b, o_c, gates, wbr, wo)


def _pack_in_proj(w_in, b_in):
    ws, bs = [], []
    for _, off, w, wp in _SEGS:
        ws.append(jnp.pad(w_in[..., off:off + w], ((0, 0), (0, 0), (0, wp - w))))
        bs.append(jnp.pad(b_in[..., off:off + w], ((0, 0), (0, wp - w))))
    return jnp.concatenate(ws, axis=-1).astype(BF16), jnp.concatenate(bs, axis=-1)[:, None, :]


def _rope_tables(pos):
    half = NSA_D // 2
    freqs = ROPE_THETA ** (-jnp.arange(half, dtype=F32) / half)
    ang = pos.astype(F32)[:, None] * freqs[None, :]
    cos, sin = jnp.cos(ang), jnp.sin(ang)
    reps = LANE // NSA_D
    return jnp.tile(jnp.concatenate([cos, cos], axis=1), (1, reps)), jnp.tile(jnp.concatenate([-sin, sin], axis=1), (1, reps))


def _cmp_weights(w1, w2):
    L = w1.shape[0]
    w1r = w1.reshape(L, 2, 2, CMP_STRIDE, NSA_D, CMP_HIDDEN)
    eye_g = jnp.eye(NSA_G, dtype=w1.dtype)
    ex = jnp.einsum('lwstdh,gk->lwstgdkh', w1r, eye_g)
    ex = ex.reshape(L, 2, 2, CMP_STRIDE * NSA_G * NSA_D, NSA_G * CMP_HIDDEN)
    w2x = jnp.einsum('lwhd,gk->lwghkd', w2, eye_g).reshape(L, 2, NSA_G * CMP_HIDDEN, NSA_G * NSA_D)
    return ex[:, :, 0].astype(BF16), ex[:, :, 1].astype(BF16), w2x.astype(BF16)


def _cmp_pos_rows(pe):
    L = pe.shape[0]
    r = pe.reshape(L, 2, 2, CMP_STRIDE, 1, NSA_D)
    return jnp.broadcast_to(r, (L, 2, 2, CMP_STRIDE, NSA_G, NSA_D)).reshape(L, 2, 2, CMP_STRIDE * NSA_G * NSA_D)


def _lane_gain(g):
    return jnp.tile(g, LANE // g.shape[-1])[None, :]


def _layer(x, lw, consts, B, S):
    cos, sin, cos_c, sin_c, gavg = consts
    T = B * S
    x = _ffn(x, lw["ffn1_norm"], lw["ffn1_wg"], lw["ffn1_wu"], lw["ffn1_wd"])
    z = dict(zip([s[0] for s in _SEGS], _proj(x, lw["mix_norm"], lw["w_in"], lw["b_in"])))
    o_a = _gla(z["a_q"], z["a_k"], z["a_v"], z["a_r"], z["a_lr"], lw["gla_wa"], lw["gla_ba"], lw["gla_gn"], B, S)
    o_c = _mlstm(z["c_qk"], z["c_v"], z["c_o"], z["c_if"], lw["conv_w"], lw["conv_b"], B, S)
    qn = lw["nsa_qk_norm"]
    qt, ks, kw, vsl, vwd, gt = _nsa_prep(z["b_q"], z["b_ks"], z["b_kw"], z["b_vs"], z["b_vw"], z["b_g"], cos, sin,
                                         _lane_gain(qn[0]), _lane_gain(qn[2]), _lane_gain(qn[3]), gavg, B, S)
    n_blk = S // CMP_STRIDE
    xk = z["b_kc"].reshape(B, n_blk, CMP_STRIDE * LANE)
    xv = z["b_vc"].reshape(B, n_blk, CMP_STRIDE * LANE)
    kc, vct = _nsa_cmp(xk, xv, lw["cmp_pe"], lw["cmp_wt"], lw["cmp_wb"], lw["cmp_w2"], cos_c, sin_c,
                       _lane_gain(qn[1]), gavg)
    part, sel, selm = _nsa_select(qt, kc, vct, kw, vwd, gt, B, S)
    o_b = _nsa_selected(qt, ks, vsl, selm, sel, gt, part, B, S)
    x = _merge(x, o_a, o_b, o_c, z["gates"], lw["w_branch"], lw["w_out"])
    return _ffn(x, lw["ffn2_norm"], lw["ffn2_wg"], lw["ffn2_wu"], lw["ffn2_wd"])


def kernel(x, ffn1_norm, ffn1_w_gate, ffn1_w_up, ffn1_w_down, mix_norm, w_in, b_in, gla_w_alpha, gla_b_alpha, gla_out_norm, nsa_qk_norm, nsa_cmp_pos, nsa_cmp_w1, nsa_cmp_w2, mlstm_conv_w, mlstm_conv_b, w_branch, w_out, ffn2_norm, ffn2_w_gate, ffn2_w_up, ffn2_w_down):
    B, S, D = x.shape
    w_in_p, b_in_p = _pack_in_proj(w_in, b_in)
    cmp_wt, cmp_wb, cmp_w2 = _cmp_weights(nsa_cmp_w1, nsa_cmp_w2)
    layers = {
        "ffn1_norm": ffn1_norm[:, None, :], "ffn1_wg": ffn1_w_gate.astype(BF16), "ffn1_wu": ffn1_w_up.astype(BF16),
        "ffn1_wd": ffn1_w_down.astype(BF16),
        "mix_norm": mix_norm[:, None, :], "w_in": w_in_p, "b_in": b_in_p,
        "gla_wa": jnp.pad(gla_w_alpha, ((0, 0), (0, LANE - GLA_RANK), (0, 0))), "gla_ba": gla_b_alpha[:, None, :],
        "gla_gn": gla_out_norm[:, None, :],
        "nsa_qk_norm": nsa_qk_norm, "cmp_pe": _cmp_pos_rows(nsa_cmp_pos), "cmp_wt": cmp_wt, "cmp_wb": cmp_wb,
        "cmp_w2": cmp_w2,
        "conv_w": mlstm_conv_w, "conv_b": mlstm_conv_b[:, None, :],
        "w_branch": w_branch.astype(BF16), "w_out": w_out.astype(BF16),
        "ffn2_norm": ffn2_norm[:, None, :], "ffn2_wg": ffn2_w_gate.astype(BF16), "ffn2_wu": ffn2_w_up.astype(BF16),
        "ffn2_wd": ffn2_w_down.astype(BF16),
    }
    n_blk = S // CMP_STRIDE
    cos, sin = _rope_tables(jnp.arange(S))
    cos_c, sin_c = _rope_tables(jnp.arange(n_blk) * CMP_STRIDE + CMP_LEN - 1)
    gavg = jnp.asarray(np.kron(np.eye(LANE // NSA_D), np.full((NSA_D, NSA_D), 1.0 / NSA_D)), BF16)
    consts = (cos, sin, cos_c, sin_c, gavg)

    out = x.reshape(B * S, D)
    for l in range(w_in.shape[0]):
        out = _layer(out, {name: w[l] for name, w in layers.items()}, consts, B, S)
    return out.reshape(B, S, D)
```

```python
import functools

import numpy as np
import jax
import jax.numpy as jnp
from jax import lax
from jax.experimental import pallas as pl
from jax.experimental.pallas import tpu as pltpu

F32 = jnp.float32
BF16 = jnp.bfloat16

RMS_EPS = 1e-6
ROPE_THETA = 10000.0

GLA_H, GLA_DK, GLA_DV, GLA_RANK, GLA_GATE_NORM = 4, 64, 128, 16, 16.0
GLA_SUB = 16
NSA_H, NSA_G, NSA_D = 8, 2, 64
CMP_LEN, CMP_STRIDE, CMP_HIDDEN = 32, 16, 256
SLC_LEN, SLC_TOPK, WIN = 64, 16, 512
FORCED_SCORE = 1e4
MLSTM_H, MLSTM_DK, MLSTM_DV, CONV_W = 4, 64, 128, 4
MLSTM_CHUNK = 64
MIX_W = 512
LANE = 128
MXU_N = 256
NEG = -1e30
LOG2E = 1.4426950408889634
SEL_KC = 512
SEL_TQ = 512
PS_PAD = 8
CMP_VARIANTS = 4
VT_ROWS = 80

_SEGS = (
    ("a_q", 0, 256, 256), ("a_k", 256, 256, 256), ("a_v", 512, 512, 512), ("a_r", 1024, 512, 512),
    ("b_q", 1552, 512, 512), ("c_qk", 2856, 512, 512), ("c_v", 3368, 512, 512), ("c_o", 3880, 512, 512),
    ("gates", 4400, 3072, 3072),
    ("a_lr", 1536, 16, 128), ("b_kc", 2064, 128, 128), ("b_vc", 2192, 128, 128), ("b_ks", 2320, 128, 128),
    ("b_vs", 2448, 128, 128), ("b_kw", 2576, 128, 128), ("b_vw", 2704, 128, 128), ("b_g", 2832, 24, 128),
    ("c_if", 4392, 8, 128),
)
_N_PACK = sum(s[3] for s in _SEGS)
_BF16_SEGS = ("gates", "c_v", "b_vs", "b_vw")


def _dot(a, b):
    return jnp.dot(a, b, preferred_element_type=F32)


def _dot_nt(a, b):
    return lax.dot_general(a, b, (((1,), (1,)), ((), ())), preferred_element_type=F32)


def _split2(a):
    hi = a.astype(BF16)
    lo = (a - hi.astype(F32)).astype(BF16)
    return hi, lo


def _dot_l2(a, b):
    hi, lo = _split2(a)
    return _dot(hi, b) + _dot(lo, b)


def _dot_r2(a, b):
    hi, lo = _split2(b)
    return _dot(a, hi) + _dot(a, lo)


def _log_sigmoid(x):
    return jnp.minimum(x, 0.0) - jnp.log(1.0 + jnp.exp(-jnp.abs(x)))


def _sigmoid(x):
    return 1.0 / (1.0 + jnp.exp(-x))


def _silu(x):
    return x * _sigmoid(x)


def _iota(shape, dim):
    return lax.broadcasted_iota(jnp.int32, shape, dim)


def _const_spec(shape):
    nd = len(shape)
    return pl.BlockSpec(shape, lambda *_: (0,) * nd, pipeline_mode=pl.Buffered(1))


def _params(sem, vmem_mb=56):
    return pltpu.CompilerParams(dimension_semantics=sem, vmem_limit_bytes=vmem_mb * 1024 * 1024)


def _ffn_body(x_ref, g_ref, wg_ref, wu_ref, wd_ref, o_ref, *, bounds):
    x = x_ref[...]
    ms = jnp.mean(x * x, axis=-1, keepdims=True)
    h = (x * lax.rsqrt(ms + RMS_EPS) * g_ref[...]).astype(BF16)
    acc = jnp.zeros(x.shape, F32)
    for lo, hi in zip(bounds[:-1], bounds[1:]):
        a = _dot(h, wg_ref[:, lo:hi])
        u = _dot(h, wu_ref[:, lo:hi])
        t = (_silu(a) * u).astype(BF16)
        acc = acc + _dot(t, wd_ref[lo:hi, :])
    o_ref[...] = x + 0.5 * acc


def _ffn(x, g, wg, wu, wd, tm=512):
    T, D = x.shape
    F = wg.shape[1]
    bounds = (0, -(-F // (2 * MXU_N)) * MXU_N, F)
    return pl.pallas_call(
        functools.partial(_ffn_body, bounds=bounds),
        out_shape=jax.ShapeDtypeStruct((T, D), F32),
        grid=(T // tm,),
        in_specs=[pl.BlockSpec((tm, D), lambda i: (i, 0)), _const_spec((1, D)),
                  _const_spec((D, F)), _const_spec((D, F)), _const_spec((F, D))],
        out_specs=pl.BlockSpec((tm, D), lambda i: (i, 0)),
        compiler_params=_params(("parallel",)),
        name="ffn",
    )(x, g, wg, wu, wd)


def _proj_body(x_ref, g_ref, w_ref, b_ref, *o_refs):
    x = x_ref[...]
    ms = jnp.mean(x * x, axis=-1, keepdims=True)
    h = (x * lax.rsqrt(ms + RMS_EPS) * g_ref[...]).astype(BF16)
    off, i = 0, 0
    while i < len(o_refs):
        group, w = [], 0
        while i < len(o_refs) and (not group or w % MXU_N):
            group.append(o_refs[i])
            w += o_refs[i].shape[1]
            i += 1
        z = _dot(h, w_ref[:, off:off + w]) + b_ref[:, off:off + w]
        c = 0
        for o_ref in group:
            o_ref[...] = z[:, c:c + o_ref.shape[1]].astype(o_ref.dtype)
            c += o_ref.shape[1]
        off += w


def _proj(x, g, w, b, tm=256):
    T, D = x.shape
    return pl.pallas_call(
        _proj_body,
        out_shape=[jax.ShapeDtypeStruct((T, s[3]), BF16 if s[0] in _BF16_SEGS else F32) for s in _SEGS],
        grid=(T // tm,),
        in_specs=[pl.BlockSpec((tm, D), lambda i: (i, 0)), _const_spec((1, D)),
                  _const_spec((D, _N_PACK)), _const_spec((1, _N_PACK))],
        out_specs=[pl.BlockSpec((tm, s[3]), lambda i: (i, 0)) for s in _SEGS],
        compiler_params=_params(("parallel",)),
        name="in_proj",
    )(x, g, w, b)


def _gla_body(q_ref, k_ref, v_ref, r_ref, lr_ref, wa_ref, ba_ref, gn_ref, tri_ref, bones_ref, eh_ref,
              o_ref, st_ref, qs_ref, c_ref, tot_ref, kst_ref, vt4_ref, oi_ref):
    nB, Lc = q_ref.shape[0], q_ref.shape[1]
    n_sub = Lc // GLA_SUB
    HK = GLA_H * GLA_DK

    @pl.when(pl.program_id(0) == 0)
    def _():
        st_ref[...] = jnp.zeros(st_ref.shape, F32)

    lane_h = _iota((1, HK), 1) >> (GLA_DK.bit_length() - 1)
    row_i = _iota((GLA_SUB, 1), 0)
    col_r = _iota((1, GLA_H * Lc), 1) & (Lc - 1)
    wa_hi, wa_lo = _split2(wa_ref[...])

    for b in range(nB):
        lr_hi, lr_lo = _split2(lr_ref[b])
        u = _dot(lr_hi, wa_hi) + _dot(lr_hi, wa_lo) + _dot(lr_lo, wa_hi) + ba_ref[...]
        g = _log_sigmoid(u) * (1.0 / GLA_GATE_NORM)
        c = _dot_r2(tri_ref[...], g)
        tot = _dot_r2(bones_ref[...], g)
        qs_ref[b] = q_ref[b] * (GLA_DK ** -0.5)
        c_ref[b] = c
        tot_ref[b] = tot
        kt = k_ref[b] * jnp.exp(tot - c)
        vt = v_ref[b].T.astype(BF16)
        for h in range(GLA_H):
            kst_ref[b, h * Lc:(h + 1) * Lc, :] = jnp.where(lane_h == h, kt, 0.0).astype(BF16)
            vt4_ref[b, :, h * Lc:(h + 1) * Lc] = vt[h * GLA_DV:(h + 1) * GLA_DV, :]

    def sub(b, s):
        rows = slice(s * GLA_SUB, (s + 1) * GLA_SUB)
        qs = qs_ref[b, rows, :]
        cs = c_ref[b, rows, :]
        st = st_ref[b]
        qd = qs * jnp.exp(cs)
        q4 = jnp.concatenate([jnp.where(lane_h == h, qd, 0.0) for h in range(GLA_H)], axis=0).astype(BF16)
        inter4 = _dot_nt(q4, st.astype(BF16))
        inter = jnp.concatenate([inter4[h * GLA_SUB:(h + 1) * GLA_SUB, :] for h in range(GLA_H)], axis=1)
        xs = []
        for j in range(GLA_SUB):
            r = s * GLA_SUB + j
            x = qs * k_ref[b, r:r + 1, :] * jnp.exp(jnp.minimum(cs - c_ref[b, r:r + 1, :], 0.0))
            xs.append(jnp.where(row_i >= j, x, 0.0))
        r_all = _dot(jnp.concatenate(xs, axis=0).astype(BF16), eh_ref[...])
        intra = jnp.zeros((GLA_SUB, r_all.shape[1]), F32)
        for j in range(GLA_SUB):
            r = s * GLA_SUB + j
            intra = intra + r_all[j * GLA_SUB:(j + 1) * GLA_SUB, :] * v_ref[b, r:r + 1, :]
        oi_ref[b, rows, :] = inter + intra
        dec = jnp.exp(tot_ref[b, s * GLA_SUB:s * GLA_SUB + 1, :])
        in_sub = (col_r >= s * GLA_SUB) & (col_r < (s + 1) * GLA_SUB)
        vtm = jnp.where(in_sub, vt4_ref[b], jnp.zeros((), BF16))
        st_ref[b] = dec * st + _dot(vtm, kst_ref[b])

    for s in range(n_sub):
        for b in range(nB):
            sub(b, s)

    gn = gn_ref[...]
    for b in range(nB):
        for h in range(GLA_H):
            sl = slice(h * GLA_DV, (h + 1) * GLA_DV)
            o = oi_ref[b, :, sl]
            ms = jnp.mean(o * o, axis=-1, keepdims=True)
            o_ref[b, :, sl] = (o * lax.rsqrt(ms + RMS_EPS) * gn * _silu(r_ref[b, :, sl])).astype(o_ref.dtype)


def _gla_consts(Lc):
    r = np.arange(Lc)
    same = (r[:, None] // GLA_SUB) == (r[None, :] // GLA_SUB)
    tri = (same & (r[None, :] <= r[:, None])).astype(np.float32)
    bones = same.astype(np.float32)
    hk = np.arange(GLA_H * GLA_DK) // GLA_DK
    hv = np.arange(GLA_H * GLA_DV) // GLA_DV
    eh = (hk[:, None] == hv[None, :]).astype(np.float32)
    return jnp.asarray(tri, BF16), jnp.asarray(bones, BF16), jnp.asarray(eh, BF16)


def _gla(a_q, a_k, a_v, a_r, a_lr, wa, ba, gn, B, S, Lc=128):
    HK, HV = GLA_H * GLA_DK, GLA_H * GLA_DV
    nb = S // Lc
    assert Lc & (Lc - 1) == 0
    tri, bones, eh = _gla_consts(Lc)
    row = lambda w: pl.BlockSpec((B, Lc, w), lambda i: (0, i, 0))
    seq = lambda t: t.reshape(B, S, t.shape[-1])
    out = pl.pallas_call(
        _gla_body,
        out_shape=jax.ShapeDtypeStruct((B, S, HV), BF16),
        grid=(nb,),
        in_specs=[row(HK), row(HK), row(HV), row(HV), row(LANE),
                  _const_spec((LANE, HK)), _const_spec((1, HK)), _const_spec((1, GLA_DV)),
                  _const_spec((Lc, Lc)), _const_spec((Lc, Lc)), _const_spec((HK, HV))],
        out_specs=row(HV),
        scratch_shapes=[pltpu.VMEM((B, GLA_DV, HK), F32), pltpu.VMEM((B, Lc, HK), F32), pltpu.VMEM((B, Lc, HK), F32),
                        pltpu.VMEM((B, Lc, HK), F32), pltpu.VMEM((B, GLA_H * Lc, HK), BF16),
                        pltpu.VMEM((B, GLA_DV, GLA_H * Lc), BF16), pltpu.VMEM((B, Lc, HV), F32)],
        compiler_params=_params(("arbitrary",)),
        name="gla",
    )(seq(a_q), seq(a_k), seq(a_v), seq(a_r), seq(a_lr), wa, ba, gn, tri, bones, eh)
    return out.reshape(B * S, HV)


def _mlstm_body(qk_ref, v_ref, og_ref, if_ref, cw_ref, cb_ref, tri_ref, bdm_ref,
                o_ref, xx_ref, ct_ref, n_ref, m_ref, kw_ref, vt_ref, act_ref):
    Lc = v_ref.shape[0]
    L = MLSTM_CHUNK
    HK = MLSTM_H * MLSTM_DK
    tail = 8

    @pl.when(pl.program_id(1) == 0)
    def _():
        ct_ref[...] = jnp.zeros(ct_ref.shape, F32)
        n_ref[...] = jnp.zeros(n_ref.shape, F32)
        m_ref[...] = jnp.zeros(m_ref.shape, F32)
        xx_ref[0:tail, :] = jnp.zeros((tail, xx_ref.shape[1]), F32)

    xx_ref[tail:tail + Lc, :] = qk_ref[...]
    y = jnp.zeros((Lc, 2 * HK), F32) + cb_ref[...]
    for kk in range(CONV_W):
        y = y + cw_ref[kk:kk + 1, :] * xx_ref[pl.ds(tail - (CONV_W - 1) + kk, Lc), :]
    xx_ref[0:tail, :] = xx_ref[Lc:Lc + tail, :]
    act_ref[...] = _silu(y)

    gates = if_ref[...]
    logf = _log_sigmoid(gates)
    bcum = _dot_r2(tri_ref[...], logf)
    b_t = bcum.T
    q_t = act_ref[:, 0:HK].T
    qt_hi, qt_lo = _split2(q_t)
    kw_ref[...] = jnp.zeros(kw_ref.shape, BF16)
    vt_ref[...] = v_ref[...].astype(F32).T.astype(BF16)
    lane_hk = _iota((1, HK), 1) >> (MLSTM_DK.bit_length() - 1)
    col_l = _iota((1, Lc), 1)
    lane_w = _iota((1, LANE), 1)
    row_l = _iota((L, 1), 0)
    per_win = LANE // L

    for ci in range(Lc // L):
        p, c = divmod(ci, per_win)
        rows = slice(ci * L, (ci + 1) * L)
        win = slice(p * LANE, (p + 1) * LANE)
        causal_t = ((c * L + row_l) <= lane_w) & ((lane_w >> (L.bit_length() - 1)) == c)
        k_all = act_ref[rows, HK:2 * HK] * (MLSTM_DK ** -0.5)
        k_bf = k_all.astype(BF16)
        ct = ct_ref[...]
        ct_bf = ct.astype(BF16)
        n_row = n_ref[...]
        n_hi, n_lo = _split2(n_row)
        n8 = jnp.concatenate([jnp.where(lane_hk == h, part, jnp.zeros((), BF16))
                              for part in (n_hi, n_lo) for h in range(MLSTM_H)], axis=0)
        qn8 = _dot(n8, qt_hi[:, win]) + _dot(n8, qt_lo[:, win])
        wk_full = jnp.zeros((L, HK), F32)
        dec_row = jnp.zeros((1, HK), F32)
        for h in range(MLSTM_H):
            icol = gates[rows, h:h + 1]
            bcol = bcum[rows, MLSTM_H + h:MLSTM_H + h + 1]
            brow = b_t[MLSTM_H + h:MLSTM_H + h + 1, win]
            m_st = m_ref[0:1, h:h + 1]
            dmat_t = jnp.where(causal_t, brow + (icol - bcol), -jnp.inf)
            inter_log = brow + m_st
            m_row = jnp.maximum(inter_log, jnp.max(dmat_t, axis=0, keepdims=True))
            w_inter = jnp.exp(inter_log - m_row)
            head_rows = slice(h * MLSTM_DK, (h + 1) * MLSTM_DK)
            zero_q = jnp.zeros((MLSTM_DK, LANE), BF16)
            qh = jnp.concatenate([qt_hi[head_rows, win] if hh == h else zero_q for hh in range(MLSTM_H)], axis=0)
            s_qk = _dot(k_bf, qh) * jnp.exp(dmat_t - m_row)
            den = w_inter * (qn8[h:h + 1, :] + qn8[MLSTM_H + h:MLSTM_H + h + 1, :]) + jnp.sum(s_qk, axis=0, keepdims=True)
            s_bf = s_qk.astype(BF16)
            pads = [jnp.zeros((L, LANE), BF16)] * per_win
            pads[c] = s_bf
            dv = slice(h * MLSTM_DV, (h + 1) * MLSTM_DV)
            num_t = w_inter * _dot(ct_bf[dv, :], qt_hi[:, win]) + _dot(vt_ref[dv, win], jnp.concatenate(pads, axis=0))
            h_t = num_t / jnp.maximum(jnp.abs(den), jnp.exp(-m_row))
            o_ref[rows, dv] = (h_t.T[c * L:(c + 1) * L, :] * _sigmoid(og_ref[rows, dv])).astype(o_ref.dtype)
            m_new = m_row[:, c * L + L - 1:c * L + L]
            b_last = bcol[L - 1:L, :]
            w_k = jnp.exp(b_last - bcol + icol - m_new)
            decay = jnp.exp(b_last + m_st - m_new)
            head = lane_hk == h
            wk_full = wk_full + jnp.where(head, w_k, 0.0)
            dec_row = dec_row + jnp.where(head, decay, 0.0)
            m_ref[0:1, h:h + 1] = m_new
        kw = k_all * wk_full
        kw_ref[rows, :] = kw.astype(BF16)
        in_chunk = (col_l >= ci * L) & (col_l < (ci + 1) * L)
        vtm = jnp.where(in_chunk, vt_ref[...], jnp.zeros((), BF16))
        ct_ref[...] = dec_row * ct + _dot(vtm, kw_ref[...]) * bdm_ref[...]
        n_ref[...] = dec_row * n_row + jnp.sum(kw, axis=0, keepdims=True)


def _mlstm_consts(Lc):
    r = np.arange(Lc)
    same = (r[:, None] // MLSTM_CHUNK) == (r[None, :] // MLSTM_CHUNK)
    tri = (same & (r[None, :] <= r[:, None])).astype(np.float32)
    hk = np.arange(MLSTM_H * MLSTM_DK) // MLSTM_DK
    hv = np.arange(MLSTM_H * MLSTM_DV) // MLSTM_DV
    eh = (hk[:, None] == hv[None, :]).astype(np.float32)
    return jnp.asarray(tri, BF16), jnp.asarray(eh.T.copy(), F32)


def _mlstm(c_qk, c_v, c_o, c_if, cw, cb, B, S, Lc=256):
    HK, HV = MLSTM_H * MLSTM_DK, MLSTM_H * MLSTM_DV
    nb = S // Lc
    assert Lc % LANE == 0 and LANE % MLSTM_CHUNK == 0
    tri, bdm = _mlstm_consts(Lc)
    row = lambda w: pl.BlockSpec((Lc, w), lambda b, i: (b * nb + i, 0))
    return pl.pallas_call(
        _mlstm_body,
        out_shape=jax.ShapeDtypeStruct((B * S, HV), BF16),
        grid=(B, nb),
        in_specs=[row(2 * HK), row(HV), row(HV), row(LANE),
                  _const_spec((CONV_W, 2 * HK)), _const_spec((1, 2 * HK)),
                  _const_spec((Lc, Lc)), _const_spec((HV, HK))],
        out_specs=row(HV),
        scratch_shapes=[pltpu.VMEM((Lc + 8, 2 * HK), F32), pltpu.VMEM((HV, HK), F32), pltpu.VMEM((1, HK), F32),
                        pltpu.VMEM((8, LANE), F32), pltpu.VMEM((Lc, HK), BF16), pltpu.VMEM((HV, Lc), BF16),
                        pltpu.VMEM((Lc, 2 * HK), F32)],
        compiler_params=_params(("parallel", "arbitrary")),
        name="mlstm",
    )(c_qk, c_v, c_o, c_if, cw, cb, tri, bdm)


def _group_rms(x, gavg, gain):
    ms = _dot_l2(x * x, gavg)
    return x * lax.rsqrt(ms + RMS_EPS) * gain


def _rope_lanes(x, cos, sin_signed):
    half = NSA_D // 2
    first = (_iota((1, LANE), 1) & (NSA_D - 1)) < half
    swapped = jnp.where(first, pltpu.roll(x, LANE - half, 1), pltpu.roll(x, half, 1))
    return x * cos + swapped * sin_signed


def _store_vt_tiles(dst_ref, vt, width):
    ones = jnp.ones((VT_ROWS - NSA_D, width), BF16)
    for j in range(vt.shape[1] // width):
        for g in range(NSA_G):
            dst_ref[j, g, 0:NSA_D, :] = vt[g * NSA_D:(g + 1) * NSA_D, j * width:(j + 1) * width]
            dst_ref[j, g, NSA_D:VT_ROWS, :] = ones


def _nsa_prep_body(q_ref, ks_ref, kw_ref, vs_ref, vw_ref, g_ref, cos_ref, sin_ref, gq_ref, gs_ref, gw_ref, gavg_ref,
                   qt_ref, kso_ref, kwo_ref, vsl_ref, vwd_ref, gt_ref):
    cos, sin = cos_ref[...], sin_ref[...]
    gavg = gavg_ref[...]
    for cb in range(NSA_H * NSA_D // LANE):
        sl = slice(cb * LANE, (cb + 1) * LANE)
        qn = _rope_lanes(_group_rms(q_ref[:, sl], gavg, gq_ref[...]), cos, sin) * (NSA_D ** -0.5 * LOG2E)
        qt_ref[sl, :] = qn.T.astype(BF16)
    kso_ref[...] = _rope_lanes(_group_rms(ks_ref[...], gavg, gs_ref[...]), cos, sin).astype(BF16)
    kwo_ref[...] = _rope_lanes(_group_rms(kw_ref[...], gavg, gw_ref[...]), cos, sin).astype(BF16)
    vst = vs_ref[...].astype(F32).T.astype(BF16)
    _store_vt_tiles(vsl_ref, vst, SEL_KC)
    _store_vt_tiles(vwd_ref, vw_ref[...].astype(F32).T.astype(BF16), LANE)
    gt_ref[...] = _sigmoid(g_ref[...]).T


def _nsa_prep(b_q, b_ks, b_kw, b_vs, b_vw, b_g, cos, sin, gq, gs, gw, gavg, B, S, tm=SEL_KC):
    nb = S // tm
    row = lambda w: pl.BlockSpec((tm, w), lambda b, i: (b * nb + i, 0))
    tab = pl.BlockSpec((tm, LANE), lambda b, i: (i, 0))
    HD = NSA_H * NSA_D
    vt_shape = lambda width: jax.ShapeDtypeStruct((B, S // width, NSA_G, VT_ROWS, width), BF16)
    vt_spec = lambda width: pl.BlockSpec((None, tm // width, NSA_G, VT_ROWS, width), lambda b, i: (b, i, 0, 0, 0))
    return pl.pallas_call(
        _nsa_prep_body,
        out_shape=[jax.ShapeDtypeStruct((B, HD, S), BF16),
                   jax.ShapeDtypeStruct((B, S, LANE), BF16),
                   jax.ShapeDtypeStruct((B, S, LANE), BF16),
                   vt_shape(SEL_KC),
                   vt_shape(LANE),
                   jax.ShapeDtypeStruct((B, LANE, S), F32)],
        grid=(B, nb),
        in_specs=[row(HD), row(LANE), row(LANE), row(LANE), row(LANE), row(LANE), tab, tab,
                  _const_spec((1, LANE)), _const_spec((1, LANE)), _const_spec((1, LANE)), _const_spec((LANE, LANE))],
        out_specs=[pl.BlockSpec((None, HD, tm), lambda b, i: (b, 0, i)),
                   pl.BlockSpec((None, tm, LANE), lambda b, i: (b, i, 0)),
                   pl.BlockSpec((None, tm, LANE), lambda b, i: (b, i, 0)),
                   vt_spec(SEL_KC), vt_spec(LANE),
                   pl.BlockSpec((None, LANE, tm), lambda b, i: (b, 0, i))],
        compiler_params=_params(("parallel", "parallel")),
        name="nsa_prep",
    )(b_q, b_ks, b_kw, b_vs, b_vw, b_g, cos, sin, gq, gs, gw, gavg)


def _gelu_tanh(x):
    return 0.5 * x * (1.0 + jnp.tanh(0.7978845608028654 * (x + 0.044715 * x * x * x)))


def _nsa_cmp_body(xk_ref, xv_ref, pe_ref, wt_ref, wb_ref, w2_ref, cos_ref, sin_ref, gk_ref, gavg_ref,
                  kc_ref, vct_ref):
    n = xk_ref.shape[0]

    def compress(x, which):
        u = _dot((x + pe_ref[which, 0:1, :]).astype(BF16), wt_ref[which])
        v = _dot((x + pe_ref[which, 1:2, :]).astype(BF16), wb_ref[which])
        hid = u + pltpu.roll(v, n - 1, 0)
        return _dot(_gelu_tanh(hid).astype(BF16), w2_ref[which])

    ck = compress(xk_ref[...], 0)
    kc_ref[...] = _rope_lanes(_group_rms(ck, gavg_ref[...], gk_ref[...]), cos_ref[...], sin_ref[...]).astype(BF16)
    vct = compress(xv_ref[...], 1).T.astype(BF16)
    for g in range(NSA_G):
        vct_ref[g] = vct[g * NSA_D:(g + 1) * NSA_D, :]


def _nsa_cmp(xk, xv, pe, wt, wb, w2, cos, sin, gk, gavg):
    B, n, W = xk.shape
    return pl.pallas_call(
        _nsa_cmp_body,
        out_shape=[jax.ShapeDtypeStruct((B, n, LANE), BF16),
                   jax.ShapeDtypeStruct((B, NSA_G, NSA_D, n), BF16)],
        grid=(B,),
        in_specs=[pl.BlockSpec((None, n, W), lambda b: (b, 0, 0)), pl.BlockSpec((None, n, W), lambda b: (b, 0, 0)),
                  _const_spec(pe.shape), _const_spec(wt.shape), _const_spec(wb.shape), _const_spec(w2.shape),
                  _const_spec((n, LANE)), _const_spec((n, LANE)), _const_spec((1, LANE)), _const_spec((LANE, LANE))],
        out_specs=[pl.BlockSpec((None, n, LANE), lambda b: (b, 0, 0)),
                   pl.BlockSpec((None, NSA_G, NSA_D, n), lambda b: (b, 0, 0, 0))],
        compiler_params=_params(("parallel",)),
        name="nsa_compress",
    )(xk, xv, pe, wt, wb, w2, cos, sin, gk, gavg)


def _group_queries(qt_ref, g):
    HPG = NSA_H // NSA_G
    q4 = jnp.concatenate([qt_ref[(g * HPG + h) * NSA_D:(g * HPG + h + 1) * NSA_D, :] for h in range(HPG)], axis=1)
    parts = [jnp.zeros(q4.shape, BF16)] * NSA_G
    parts[g] = q4
    return jnp.concatenate(parts, axis=0)


def _bitonic_merge_desc(xs):
    xs = list(xs)
    j = len(xs) // 2
    while j >= 1:
        for a in range(len(xs)):
            b = a ^ j
            if b > a:
                xs[a], xs[b] = jnp.maximum(xs[a], xs[b]), jnp.minimum(xs[a], xs[b])
        j //= 2
    return xs


def _bitonic_sort_desc(xs):
    if len(xs) == 1:
        return list(xs)
    half = len(xs) // 2
    lo = _bitonic_sort_desc(xs[:half])
    hi = _bitonic_sort_desc(xs[half:])
    return _bitonic_merge_desc(lo + hi[::-1])


def _kth_largest(score, k):
    SUBLANES = 8
    tiles = [score[SUBLANES * v:SUBLANES * (v + 1), :] for v in range(score.shape[0] // SUBLANES)]
    size = max(k, 1 << (len(tiles) - 1).bit_length())
    tiles = tiles + [jnp.full(tiles[0].shape, NEG, F32)] * (size - len(tiles))

    def top_of_two(a, b):
        return _bitonic_merge_desc([jnp.maximum(a[r], b[k - 1 - r]) for r in range(k)])

    tops = [_bitonic_sort_desc(tiles[c:c + k]) for c in range(0, size, k)]
    while len(tops) > 1:
        tops = [top_of_two(tops[c], tops[c + 1]) for c in range(0, len(tops), 2)]
    top = tops[0]
    shift = SUBLANES // 2
    while shift >= 1:
        top = top_of_two(top, [pltpu.roll(t, shift, 0) for t in top])
        shift //= 2
    return top[k - 1][0:1, :]


def _nsa_select_dispatch(*refs, sel_tile, n_variants):
    n_cmp = refs[1].shape[0]
    need = (pl.program_id(1) + 1) * LANE // CMP_STRIDE
    variant = (need - 1) * n_variants // n_cmp
    for k in range(n_variants):
        pl.when(variant == k)(functools.partial(_nsa_select_body, *refs, sel_tile=sel_tile,
                                                n_rows=n_cmp * (k + 1) // n_variants))


def _nsa_select_body(qt_ref, kc_ref, vct_ref, kw_ref, vwd_ref, gt_ref, tril_ref, part_ref, sel_ref, selm_ref, ps_buf,
                     *, sel_tile, n_rows):
    TQ = LANE
    HPG = NSA_H // NSA_G
    W = HPG * TQ
    n_cmp = kc_ref.shape[0]
    n_sel = sel_ref.shape[1]
    i = pl.program_id(1)
    s0 = i * TQ
    t_row = s0 + _iota((1, TQ), 1)
    n_win = WIN // LANE + 1
    j0 = jnp.maximum(i - (n_win - 1), 0)
    w_start = pl.multiple_of(j0 * LANE, LANE)
    own_first = (s0 // sel_tile) * (sel_tile // SLC_LEN)
    ps_buf[:, 0:PS_PAD, :] = jnp.zeros((NSA_G, PS_PAD, TQ), F32)

    for g in range(NSA_G):
        qpad = _group_queries(qt_ref, g)

        cend = _iota((n_rows, 1), 0) * CMP_STRIDE + (CMP_LEN - 1)
        bias_c = jnp.where(cend <= t_row, 0.0, NEG)
        sc = _dot(kc_ref[0:n_rows, :], qpad) + jnp.concatenate([bias_c] * HPG, axis=1)
        m = jnp.max(sc, axis=0, keepdims=True)
        m = jnp.where(m > 0.5 * NEG, m, 0.0)
        p = jnp.exp2(sc - m)
        p = p * (1.0 / jnp.maximum(jnp.sum(p, axis=0, keepdims=True), 1e-30))
        o_c = _dot(vct_ref[g, :, 0:n_rows], p.astype(BF16))
        psum = p[:, 0:TQ]
        for h in range(1, HPG):
            psum = psum + p[:, h * TQ:(h + 1) * TQ]
        ps_buf[g, PS_PAD:PS_PAD + n_rows, :] = psum

        per = SLC_LEN // CMP_STRIDE
        n_blk = n_rows // per
        imp = ps_buf[g, pl.ds(PS_PAD - 1, n_blk, stride=per), :]
        for jj in range(per):
            imp = imp + ps_buf[g, pl.ds(PS_PAD + jj, n_blk, stride=per), :]
        blk = _iota((n_blk, 1), 0)
        cur = t_row >> (SLC_LEN.bit_length() - 1)
        valid = blk * SLC_LEN <= t_row
        forced = (blk == 0) | (blk == cur) | (blk == cur - 1)
        score0 = jnp.where(valid, jnp.where(forced, FORCED_SCORE, imp), NEG)
        kth = _kth_largest(score0, min(SLC_TOPK, n_sel))
        above = score0 > kth
        tied = jnp.where(score0 == kth, 1.0, 0.0)
        need = min(SLC_TOPK, n_sel) - jnp.sum(jnp.where(above, 1.0, 0.0), axis=0, keepdims=True)
        tied_all = jnp.concatenate([tied, jnp.zeros((n_sel - n_blk, TQ), F32)], axis=0) if n_blk < n_sel else tied
        rank = _dot(tril_ref[0:n_blk, :], tied_all.astype(BF16))
        chosen = (above | ((tied > 0.5) & (rank <= need))) & valid
        sel_ref[g, 0:n_blk, :] = jnp.where(chosen, 0.0, NEG)
        selm_ref[g, 0:n_blk, :] = jnp.where(chosen & (blk < own_first), 0.0, NEG)
        if n_blk < n_sel:
            sel_ref[g, n_blk:n_sel, :] = jnp.full((n_sel - n_blk, TQ), NEG, F32)
            selm_ref[g, n_blk:n_sel, :] = jnp.full((n_sel - n_blk, TQ), NEG, F32)

        kpos = w_start + _iota((n_win * LANE, 1), 0)
        bias_w = jnp.where((kpos <= t_row) & (kpos > t_row - WIN), 0.0, NEG)
        sw = _dot(kw_ref[pl.ds(w_start, n_win * LANE), :], qpad) + jnp.concatenate([bias_w] * HPG, axis=1)
        pw = jnp.exp2(sw - jnp.max(sw, axis=0, keepdims=True)).astype(BF16)
        acc_w = jnp.zeros((VT_ROWS, W), F32)
        for r in range(n_win):
            acc_w = acc_w + _dot(vwd_ref[j0 + r, g], pw[r * LANE:(r + 1) * LANE, :])
        o_w = acc_w[0:NSA_D, :] * (1.0 / acc_w[NSA_D:NSA_D + 1, :])

        for hp in range(HPG // 2):
            tiles = []
            for h in (2 * hp, 2 * hp + 1):
                gr = (g * HPG + h) * 3
                cs = slice(h * TQ, (h + 1) * TQ)
                tiles.append(gt_ref[gr:gr + 1, :] * o_c[:, cs] + gt_ref[gr + 2:gr + 3, :] * o_w[:, cs])
            col = (g * HPG + 2 * hp) * NSA_D
            part_ref[:, col:col + 2 * NSA_D] = jnp.concatenate(tiles, axis=0).T


def _nsa_select(qt, kc, vct, kw, vwd, gt, B, S):
    HD = NSA_H * NSA_D
    TQ = LANE
    n_cmp = kc.shape[1]
    n_sel = S // SLC_LEN
    assert S >= WIN + TQ and CMP_LEN == 2 * CMP_STRIDE
    whole = lambda shape: pl.BlockSpec((None,) + shape, lambda b, i: (b,) + (0,) * len(shape),
                                       pipeline_mode=pl.Buffered(1))
    mask_shape = jax.ShapeDtypeStruct((B, NSA_G, n_sel, S), F32)
    mask_spec = pl.BlockSpec((None, NSA_G, n_sel, TQ), lambda b, i: (b, 0, 0, i))
    tril = jnp.asarray(np.tril(np.ones((n_sel, n_sel), np.float32)), BF16)
    return pl.pallas_call(
        functools.partial(_nsa_select_dispatch, sel_tile=SEL_TQ, n_variants=CMP_VARIANTS),
        out_shape=[jax.ShapeDtypeStruct((B * S, HD), F32), mask_shape, mask_shape],
        grid=(B, S // TQ),
        in_specs=[pl.BlockSpec((None, HD, TQ), lambda b, i: (b, 0, i)),
                  whole((n_cmp, LANE)), whole(vct.shape[1:]), whole((S, LANE)), whole(vwd.shape[1:]),
                  pl.BlockSpec((None, LANE, TQ), lambda b, i: (b, 0, i)), _const_spec((n_sel, n_sel))],
        out_specs=[pl.BlockSpec((TQ, HD), lambda b, i: (b * (S // TQ) + i, 0)), mask_spec, mask_spec],
        scratch_shapes=[pltpu.VMEM((NSA_G, n_cmp + PS_PAD, TQ), F32)],
        compiler_params=_params(("parallel", "parallel")),
        name="nsa_select",
    )(qt, kc, vct, kw, vwd, gt, tril)


def _nsa_selected_body(qt_ref, ks_ref, ksd_ref, vsl_ref, selm_ref, seld_ref, gt_ref, part_ref, oneh_ref, o_ref,
                       s_a, s_b, p_a, p_b, al_a, al_b, m_ref, acc_ref):
    TQ = qt_ref.shape[1]
    HPG = NSA_H // NSA_G
    W = HPG * TQ
    n_kc = ks_ref.shape[0] // SEL_KC
    bps = SEL_KC // SLC_LEN
    i = pl.program_id(1)
    n_main = i * (TQ // SEL_KC)
    n_pairs = (n_main + 1) // 2
    diag_ok = _iota((TQ, 1), 0) <= _iota((1, TQ), 1)
    pad_rows = jnp.zeros((LANE - 2 * bps, W), BF16)

    def q_aug(qpad, brows):
        brows = jnp.concatenate([jnp.concatenate([brows] * HPG, axis=1), jnp.zeros((bps, W), F32)], axis=0)
        return jnp.concatenate([qpad, brows.astype(BF16), pad_rows], axis=0)

    heads = [slice(h * TQ, (h + 1) * TQ) for h in range(HPG)]

    groups = range(NSA_G)

    def softmax(g, s_buf, p_buf, al_ref, cs):
        sb = s_buf[g, :, cs]
        m_i = m_ref[g, :, cs]
        m_new = jnp.maximum(m_i, jnp.max(sb, axis=0, keepdims=True).astype(F32))
        p_buf[g, :, cs] = jnp.exp2(sb - m_new.astype(BF16))
        al_ref[g, :, cs] = jnp.exp2(m_i - m_new)
        m_ref[g, :, cs] = m_new

    def apply_values(g, p_buf, al_ref, vt, cs):
        acc_ref[g, :, cs] = al_ref[g, :, cs] * acc_ref[g, :, cs] + _dot(vt, p_buf[g, :, cs])

    qpads = [_group_queries(qt_ref, g) for g in groups]
    m_ref[...] = jnp.full(m_ref.shape, NEG, F32)
    acc_ref[...] = jnp.zeros(acc_ref.shape, F32)
    p_b[...] = jnp.zeros(p_b.shape, BF16)
    al_b[...] = jnp.ones(al_b.shape, F32)

    def step_operands(c):
        k0 = pl.multiple_of(c * SEL_KC, SEL_KC)
        rows = pl.ds(pl.multiple_of(c * bps, bps), bps)
        k_aug = jnp.concatenate([ks_ref[pl.ds(k0, SEL_KC), :], oneh_ref[...]], axis=1)
        return k_aug, [q_aug(qpads[g], selm_ref[g, rows, :]) for g in groups]

    def scores(ops, g, s_buf, cs):
        s_buf[g, :, cs] = _dot(ops[0], ops[1][g][:, cs]).astype(BF16)

    def stages(ops, c_prev, s_new, s_cur, p_cur, al_cur, p_prev, al_prev):
        for g in groups:
            vt = vsl_ref[c_prev, g]
            for cs in heads:
                scores(ops, g, s_new, cs)
                softmax(g, s_cur, p_cur, al_cur, cs)
                apply_values(g, p_prev, al_prev, vt, cs)

    def pair(cc, carry):
        c0 = 2 * cc
        stages(step_operands(c0 + 1), jnp.maximum(c0 - 1, 0), s_b, s_a, p_a, al_a, p_b, al_b)
        stages(step_operands(jnp.minimum(c0 + 2, n_kc - 1)), c0, s_a, s_b, p_b, al_b, p_a, al_a)
        return carry

    ops = step_operands(0)
    for g in groups:
        for cs in heads:
            scores(ops, g, s_a, cs)
    lax.fori_loop(0, n_pairs, pair, 0)
    for g in groups:
        vt = vsl_ref[jnp.maximum(2 * n_pairs - 1, 0), g]
        for cs in heads:
            apply_values(g, p_b, al_b, vt, cs)

    for d in range(TQ // SEL_KC):
        k_aug = jnp.concatenate([ksd_ref[d * SEL_KC:(d + 1) * SEL_KC, :], oneh_ref[...]], axis=1)
        for g in groups:
            qa = q_aug(qpads[g], seld_ref[g, d * bps:(d + 1) * bps, :])
            vt = vsl_ref[i * (TQ // SEL_KC) + d, g]
            for cs in heads:
                sd = _dot(k_aug, qa[:, cs])
                s_a[g, :, cs] = jnp.where(diag_ok[d * SEL_KC:(d + 1) * SEL_KC, :], sd, NEG).astype(BF16)
                softmax(g, s_a, p_a, al_a, cs)
                apply_values(g, p_a, al_a, vt, cs)

    for g in groups:
        acc_s = acc_ref[g]
        o_s = acc_s[0:NSA_D, :] * (1.0 / acc_s[NSA_D:NSA_D + 1, :])

        for hp in range(HPG // 2):
            tiles = []
            for h in (2 * hp, 2 * hp + 1):
                gr = (g * HPG + h) * 3 + 1
                tiles.append(gt_ref[gr:gr + 1, :] * o_s[:, h * TQ:(h + 1) * TQ])
            cols = slice((g * HPG + 2 * hp) * NSA_D, (g * HPG + 2 * hp + 2) * NSA_D)
            o_ref[:, cols] = (part_ref[:, cols] + jnp.concatenate(tiles, axis=0).T).astype(o_ref.dtype)


def _nsa_selected(qt, ks, vsl, selm, sel, gt, part, B, S):
    HD = NSA_H * NSA_D
    TQ = SEL_TQ
    W = NSA_H // NSA_G * TQ
    n_sel = S // SLC_LEN
    assert S % (2 * SEL_KC) == 0 and TQ % SEL_KC == 0
    whole = lambda shape: pl.BlockSpec((None,) + shape, lambda b, i: (b,) + (0,) * len(shape),
                                       pipeline_mode=pl.Buffered(1))
    oneh = jnp.asarray(np.arange(SEL_KC)[:, None] // SLC_LEN == np.arange(LANE)[None, :], BF16)
    return pl.pallas_call(
        _nsa_selected_body,
        out_shape=jax.ShapeDtypeStruct((B * S, HD), BF16),
        grid=(B, S // TQ),
        in_specs=[pl.BlockSpec((None, HD, TQ), lambda b, i: (b, 0, i)),
                  whole((S, LANE)), pl.BlockSpec((None, TQ, LANE), lambda b, i: (b, i, 0)),
                  whole(vsl.shape[1:]),
                  pl.BlockSpec((None, NSA_G, n_sel, TQ), lambda b, i: (b, 0, 0, i)),
                  pl.BlockSpec((None, NSA_G, TQ // SLC_LEN, TQ), lambda b, i: (b, 0, i, i)),
                  pl.BlockSpec((None, LANE, TQ), lambda b, i: (b, 0, i)),
                  pl.BlockSpec((TQ, HD), lambda b, i: (b * (S // TQ) + i, 0)),
                  _const_spec((SEL_KC, LANE))],
        out_specs=pl.BlockSpec((TQ, HD), lambda b, i: (b * (S // TQ) + i, 0)),
        scratch_shapes=[pltpu.VMEM((NSA_G, SEL_KC, W), BF16)] * 4 + [pltpu.VMEM((NSA_G, 1, W), F32)] * 3
                       + [pltpu.VMEM((NSA_G, VT_ROWS, W), F32)],
        compiler_params=_params(("parallel", "arbitrary")),
        name="nsa_selected",
    )(qt, ks, ks, vsl, selm, sel, gt, part, oneh)


def _merge_body(x_ref, oa_ref, ob_ref, oc_ref, gates_ref, wbr_ref, wo_ref, o_ref):
    D = x_ref.shape[1]
    y = jnp.zeros(x_ref.shape, F32)
    for j, br in enumerate((oa_ref, ob_ref, oc_ref)):
        y = y + _sigmoid(gates_ref[:, j * D:(j + 1) * D].astype(F32)) * _dot(br[...].astype(BF16), wbr_ref[j])
    o_ref[...] = x_ref[...] + _dot(y.astype(BF16), wo_ref[...])


def _merge(x, o_a, o_b, o_c, gates, wbr, wo, tm=512):
    T, D = x.shape
    row = lambda w: pl.BlockSpec((tm, w), lambda i: (i, 0))
    return pl.pallas_call(
        _merge_body,
        out_shape=jax.ShapeDtypeStruct((T, D), F32),
        grid=(T // tm,),
        in_specs=[row(D), row(MIX_W), row(MIX_W), row(MIX_W), row(3 * D), _const_spec(wbr.shape), _const_spec(wo.shape)],
        out_specs=row(D),
        compiler_params=_params(("parallel",)),
        name="merge_out",
    )(x, o_a, o_b, o_c, gates, wbr, wo)


def _pack_in_proj(w_in, b_in):
    ws, bs = [], []
    for _, off, w, wp in _SEGS:
        ws.append(jnp.pad(w_in[..., off:off + w], ((0, 0), (0, 0), (0, wp - w))))
        bs.append(jnp.pad(b_in[..., off:off + w], ((0, 0), (0, wp - w))))
    return jnp.concatenate(ws, axis=-1).astype(BF16), jnp.concatenate(bs, axis=-1)[:, None, :]


def _rope_tables(pos):
    half = NSA_D // 2
    freqs = ROPE_THETA ** (-jnp.arange(half, dtype=F32) / half)
    ang = pos.astype(F32)[:, None] * freqs[None, :]
    cos, sin = jnp.cos(ang), jnp.sin(ang)
    reps = LANE // NSA_D
    return jnp.tile(jnp.concatenate([cos, cos], axis=1), (1, reps)), jnp.tile(jnp.concatenate([-sin, sin], axis=1), (1, reps))


def _cmp_weights(w1, w2):
    L = w1.shape[0]
    w1r = w1.reshape(L, 2, 2, CMP_STRIDE, NSA_D, CMP_HIDDEN)
    eye_g = jnp.eye(NSA_G, dtype=w1.dtype)
    ex = jnp.einsum('lwstdh,gk->lwstgdkh', w1r, eye_g)
    ex = ex.reshape(L, 2, 2, CMP_STRIDE * NSA_G * NSA_D, NSA_G * CMP_HIDDEN)
    w2x = jnp.einsum('lwhd,gk->lwghkd', w2, eye_g).reshape(L, 2, NSA_G * CMP_HIDDEN, NSA_G * NSA_D)
    return ex[:, :, 0].astype(BF16), ex[:, :, 1].astype(BF16), w2x.astype(BF16)


def _cmp_pos_rows(pe):
    L = pe.shape[0]
    r = pe.reshape(L, 2, 2, CMP_STRIDE, 1, NSA_D)
    return jnp.broadcast_to(r, (L, 2, 2, CMP_STRIDE, NSA_G, NSA_D)).reshape(L, 2, 2, CMP_STRIDE * NSA_G * NSA_D)


def _lane_gain(g):
    return jnp.tile(g, LANE // g.shape[-1])[None, :]


def _layer(x, lw, consts, B, S):
    cos, sin, cos_c, sin_c, gavg = consts
    T = B * S
    x = _ffn(x, lw["ffn1_norm"], lw["ffn1_wg"], lw["ffn1_wu"], lw["ffn1_wd"])
    z = dict(zip([s[0] for s in _SEGS], _proj(x, lw["mix_norm"], lw["w_in"], lw["b_in"])))
    o_a = _gla(z["a_q"], z["a_k"], z["a_v"], z["a_r"], z["a_lr"], lw["gla_wa"], lw["gla_ba"], lw["gla_gn"], B, S)
    o_c = _mlstm(z["c_qk"], z["c_v"], z["c_o"], z["c_if"], lw["conv_w"], lw["conv_b"], B, S)
    qn = lw["nsa_qk_norm"]
    qt, ks, kw, vsl, vwd, gt = _nsa_prep(z["b_q"], z["b_ks"], z["b_kw"], z["b_vs"], z["b_vw"], z["b_g"], cos, sin,
                                         _lane_gain(qn[0]), _lane_gain(qn[2]), _lane_gain(qn[3]), gavg, B, S)
    n_blk = S // CMP_STRIDE
    xk = z["b_kc"].reshape(B, n_blk, CMP_STRIDE * LANE)
    xv = z["b_vc"].reshape(B, n_blk, CMP_STRIDE * LANE)
    kc, vct = _nsa_cmp(xk, xv, lw["cmp_pe"], lw["cmp_wt"], lw["cmp_wb"], lw["cmp_w2"], cos_c, sin_c,
                       _lane_gain(qn[1]), gavg)
    part, sel, selm = _nsa_select(qt, kc, vct, kw, vwd, gt, B, S)
    o_b = _nsa_selected(qt, ks, vsl, selm, sel, gt, part, B, S)
    x = _merge(x, o_a, o_b, o_c, z["gates"], lw["w_branch"], lw["w_out"])
    return _ffn(x, lw["ffn2_norm"], lw["ffn2_wg"], lw["ffn2_wu"], lw["ffn2_wd"])


def kernel(x, ffn1_norm, ffn1_w_gate, ffn1_w_up, ffn1_w_down, mix_norm, w_in, b_in, gla_w_alpha, gla_b_alpha, gla_out_norm, nsa_qk_norm, nsa_cmp_pos, nsa_cmp_w1, nsa_cmp_w2, mlstm_conv_w, mlstm_conv_b, w_branch, w_out, ffn2_norm, ffn2_w_gate, ffn2_w_up, ffn2_w_down):
    B, S, D = x.shape
    w_in_p, b_in_p = _pack_in_proj(w_in, b_in)
    cmp_wt, cmp_wb, cmp_w2 = _cmp_weights(nsa_cmp_w1, nsa_cmp_w2)
    layers = {
        "ffn1_norm": ffn1_norm[:, None, :], "ffn1_wg": ffn1_w_gate.astype(BF16), "ffn1_wu": ffn1_w_up.astype(BF16),
        "ffn1_wd": ffn1_w_down.astype(BF16),
        "mix_norm": mix_norm[:, None, :], "w_in": w_in_p, "b_in": b_in_p,
        "gla_wa": jnp.pad(gla_w_alpha, ((0, 0), (0, LANE - GLA_RANK), (0, 0))), "gla_ba": gla_b_alpha[:, None, :],
        "gla_gn": gla_out_norm[:, None, :],
        "nsa_qk_norm": nsa_qk_norm, "cmp_pe": _cmp_pos_rows(nsa_cmp_pos), "cmp_wt": cmp_wt, "cmp_wb": cmp_wb,
        "cmp_w2": cmp_w2,
        "conv_w": mlstm_conv_w, "conv_b": mlstm_conv_b[:, None, :],
        "w_branch": w_branch.astype(BF16), "w_out": w_out.astype(BF16),
        "ffn2_norm": ffn2_norm[:, None, :], "ffn2_wg": ffn2_w_gate.astype(BF16), "ffn2_wu": ffn2_w_up.astype(BF16),
        "ffn2_wd": ffn2_w_down.astype(BF16),
    }
    n_blk = S // CMP_STRIDE
    cos, sin = _rope_tables(jnp.arange(S))
    cos_c, sin_c = _rope_tables(jnp.arange(n_blk) * CMP_STRIDE + CMP_LEN - 1)
    gavg = jnp.asarray(np.kron(np.eye(LANE // NSA_D), np.full((NSA_D, NSA_D), 1.0 / NSA_D)), BF16)
    consts = (cos, sin, cos_c, sin_c, gavg)

    out = x.reshape(B * S, D)
    for l in range(w_in.shape[0]):
        out = _layer(out, {name: w[l] for name, w in layers.items()}, consts, B, S)
    return out.reshape(B, S, D)
```

```python
import functools

import numpy as np
import jax
import jax.numpy as jnp
from jax import lax
from jax.experimental import pallas as pl
from jax.experimental.pallas import tpu as pltpu

F32 = jnp.float32
BF16 = jnp.bfloat16

RMS_EPS = 1e-6
ROPE_THETA = 10000.0

GLA_H, GLA_DK, GLA_DV, GLA_RANK, GLA_GATE_NORM = 4, 64, 128, 16, 16.0
GLA_SUB = 16
NSA_H, NSA_G, NSA_D = 8, 2, 64
CMP_LEN, CMP_STRIDE, CMP_HIDDEN = 32, 16, 256
SLC_LEN, SLC_TOPK, WIN = 64, 16, 512
FORCED_SCORE = 1e4
MLSTM_H, MLSTM_DK, MLSTM_DV, CONV_W = 4, 64, 128, 4
MLSTM_CHUNK = 64
MIX_W = 512
LANE = 128
MXU_N = 256
NEG = -1e30
LOG2E = 1.4426950408889634
SEL_KC = 512
SEL_TQ = 512
PS_PAD = 8
CMP_VARIANTS = 4
VT_ROWS = 80

_SEGS = (
    ("a_q", 0, 256, 256), ("a_k", 256, 256, 256), ("a_v", 512, 512, 512), ("a_r", 1024, 512, 512),
    ("b_q", 1552, 512, 512), ("c_qk", 2856, 512, 512), ("c_v", 3368, 512, 512), ("c_o", 3880, 512, 512),
    ("gates", 4400, 3072, 3072),
    ("a_lr", 1536, 16, 128), ("b_kc", 2064, 128, 128), ("b_vc", 2192, 128, 128), ("b_ks", 2320, 128, 128),
    ("b_vs", 2448, 128, 128), ("b_kw", 2576, 128, 128), ("b_vw", 2704, 128, 128), ("b_g", 2832, 24, 128),
    ("c_if", 4392, 8, 128),
)
_N_PACK = sum(s[3] for s in _SEGS)
_BF16_SEGS = ("gates", "c_v", "b_vs", "b_vw")


def _dot(a, b):
    return jnp.dot(a, b, preferred_element_type=F32)


def _dot_nt(a, b):
    return lax.dot_general(a, b, (((1,), (1,)), ((), ())), preferred_element_type=F32)


def _split2(a):
    hi = a.astype(BF16)
    lo = (a - hi.astype(F32)).astype(BF16)
    return hi, lo


def _dot_l2(a, b):
    hi, lo = _split2(a)
    return _dot(hi, b) + _dot(lo, b)


def _dot_r2(a, b):
    hi, lo = _split2(b)
    return _dot(a, hi) + _dot(a, lo)


def _log_sigmoid(x):
    return jnp.minimum(x, 0.0) - jnp.log(1.0 + jnp.exp(-jnp.abs(x)))


def _sigmoid(x):
    return 1.0 / (1.0 + jnp.exp(-x))


def _silu(x):
    return x * _sigmoid(x)


def _iota(shape, dim):
    return lax.broadcasted_iota(jnp.int32, shape, dim)


def _const_spec(shape):
    nd = len(shape)
    return pl.BlockSpec(shape, lambda *_: (0,) * nd, pipeline_mode=pl.Buffered(1))


def _params(sem, vmem_mb=56):
    return pltpu.CompilerParams(dimension_semantics=sem, vmem_limit_bytes=vmem_mb * 1024 * 1024)


def _ffn_body(x_ref, g_ref, wg_ref, wu_ref, wd_ref, o_ref, *, bounds):
    x = x_ref[...]
    ms = jnp.mean(x * x, axis=-1, keepdims=True)
    h = (x * lax.rsqrt(ms + RMS_EPS) * g_ref[...]).astype(BF16)
    acc = jnp.zeros(x.shape, F32)
    for lo, hi in zip(bounds[:-1], bounds[1:]):
        a = _dot(h, wg_ref[:, lo:hi])
        u = _dot(h, wu_ref[:, lo:hi])
        t = (_silu(a) * u).astype(BF16)
        acc = acc + _dot(t, wd_ref[lo:hi, :])
    o_ref[...] = x + 0.5 * acc


def _ffn(x, g, wg, wu, wd, tm=512):
    T, D = x.shape
    F = wg.shape[1]
    bounds = (0, -(-F // (2 * MXU_N)) * MXU_N, F)
    return pl.pallas_call(
        functools.partial(_ffn_body, bounds=bounds),
        out_shape=jax.ShapeDtypeStruct((T, D), F32),
        grid=(T // tm,),
        in_specs=[pl.BlockSpec((tm, D), lambda i: (i, 0)), _const_spec((1, D)),
                  _const_spec((D, F)), _const_spec((D, F)), _const_spec((F, D))],
        out_specs=pl.BlockSpec((tm, D), lambda i: (i, 0)),
        compiler_params=_params(("parallel",)),
        name="ffn",
    )(x, g, wg, wu, wd)


def _proj_body(x_ref, g_ref, w_ref, b_ref, *o_refs):
    x = x_ref[...]
    ms = jnp.mean(x * x, axis=-1, keepdims=True)
    h = (x * lax.rsqrt(ms + RMS_EPS) * g_ref[...]).astype(BF16)
    off, i = 0, 0
    while i < len(o_refs):
        group, w = [], 0
        while i < len(o_refs) and (not group or w % MXU_N):
            group.append(o_refs[i])
            w += o_refs[i].shape[1]
            i += 1
        z = _dot(h, w_ref[:, off:off + w]) + b_ref[:, off:off + w]
        c = 0
        for o_ref in group:
            o_ref[...] = z[:, c:c + o_ref.shape[1]].astype(o_ref.dtype)
            c += o_ref.shape[1]
        off += w


def _proj(x, g, w, b, tm=256):
    T, D = x.shape
    return pl.pallas_call(
        _proj_body,
        out_shape=[jax.ShapeDtypeStruct((T, s[3]), BF16 if s[0] in _BF16_SEGS else F32) for s in _SEGS],
        grid=(T // tm,),
        in_specs=[pl.BlockSpec((tm, D), lambda i: (i, 0)), _const_spec((1, D)),
                  _const_spec((D, _N_PACK)), _const_spec((1, _N_PACK))],
        out_specs=[pl.BlockSpec((tm, s[3]), lambda i: (i, 0)) for s in _SEGS],
        compiler_params=_params(("parallel",)),
        name="in_proj",
    )(x, g, w, b)


def _gla_body(q_ref, k_ref, v_ref, r_ref, lr_ref, wa_ref, ba_ref, gn_ref, tri_ref, bones_ref, eh_ref,
              o_ref, st_ref, qs_ref, c_ref, tot_ref, kst_ref, vt4_ref, oi_ref):
    nB, Lc = q_ref.shape[0], q_ref.shape[1]
    n_sub = Lc // GLA_SUB
    HK = GLA_H * GLA_DK

    @pl.when(pl.program_id(0) == 0)
    def _():
        st_ref[...] = jnp.zeros(st_ref.shape, F32)

    lane_h = _iota((1, HK), 1) >> (GLA_DK.bit_length() - 1)
    row_i = _iota((GLA_SUB, 1), 0)
    col_r = _iota((1, GLA_H * Lc), 1) & (Lc - 1)
    wa_hi, wa_lo = _split2(wa_ref[...])

    for b in range(nB):
        lr_hi, lr_lo = _split2(lr_ref[b])
        u = _dot(lr_hi, wa_hi) + _dot(lr_hi, wa_lo) + _dot(lr_lo, wa_hi) + ba_ref[...]
        g = _log_sigmoid(u) * (1.0 / GLA_GATE_NORM)
        c = _dot_r2(tri_ref[...], g)
        tot = _dot_r2(bones_ref[...], g)
        qs_ref[b] = q_ref[b] * (GLA_DK ** -0.5)
        c_ref[b] = c
        tot_ref[b] = tot
        kt = k_ref[b] * jnp.exp(tot - c)
        vt = v_ref[b].T.astype(BF16)
        for h in range(GLA_H):
            kst_ref[b, h * Lc:(h + 1) * Lc, :] = jnp.where(lane_h == h, kt, 0.0).astype(BF16)
            vt4_ref[b, :, h * Lc:(h + 1) * Lc] = vt[h * GLA_DV:(h + 1) * GLA_DV, :]

    def sub(b, s):
        rows = slice(s * GLA_SUB, (s + 1) * GLA_SUB)
        qs = qs_ref[b, rows, :]
        cs = c_ref[b, rows, :]
        st = st_ref[b]
        qd = qs * jnp.exp(cs)
        q4 = jnp.concatenate([jnp.where(lane_h == h, qd, 0.0) for h in range(GLA_H)], axis=0).astype(BF16)
        inter4 = _dot_nt(q4, st.astype(BF16))
        inter = jnp.concatenate([inter4[h * GLA_SUB:(h + 1) * GLA_SUB, :] for h in range(GLA_H)], axis=1)
        xs = []
        for j in range(GLA_SUB):
            r = s * GLA_SUB + j
            x = qs * k_ref[b, r:r + 1, :] * jnp.exp(jnp.minimum(cs - c_ref[b, r:r + 1, :], 0.0))
            xs.append(jnp.where(row_i >= j, x, 0.0))
        r_all = _dot(jnp.concatenate(xs, axis=0).astype(BF16), eh_ref[...])
        intra = jnp.zeros((GLA_SUB, r_all.shape[1]), F32)
        for j in range(GLA_SUB):
            r = s * GLA_SUB + j
            intra = intra + r_all[j * GLA_SUB:(j + 1) * GLA_SUB, :] * v_ref[b, r:r + 1, :]
        oi_ref[b, rows, :] = inter + intra
        dec = jnp.exp(tot_ref[b, s * GLA_SUB:s * GLA_SUB + 1, :])
        in_sub = (col_r >= s * GLA_SUB) & (col_r < (s + 1) * GLA_SUB)
        vtm = jnp.where(in_sub, vt4_ref[b], jnp.zeros((), BF16))
        st_ref[b] = dec * st + _dot(vtm, kst_ref[b])

    for s in range(n_sub):
        for b in range(nB):
            sub(b, s)

    gn = gn_ref[...]
    for b in range(nB):
        for h in range(GLA_H):
            sl = slice(h * GLA_DV, (h + 1) * GLA_DV)
            o = oi_ref[b, :, sl]
            ms = jnp.mean(o * o, axis=-1, keepdims=True)
            o_ref[b, :, sl] = (o * lax.rsqrt(ms + RMS_EPS) * gn * _silu(r_ref[b, :, sl])).astype(o_ref.dtype)


def _gla_consts(Lc):
    r = np.arange(Lc)
    same = (r[:, None] // GLA_SUB) == (r[None, :] // GLA_SUB)
    tri = (same & (r[None, :] <= r[:, None])).astype(np.float32)
    bones = same.astype(np.float32)
    hk = np.arange(GLA_H * GLA_DK) // GLA_DK
    hv = np.arange(GLA_H * GLA_DV) // GLA_DV
    eh = (hk[:, None] == hv[None, :]).astype(np.float32)
    return jnp.asarray(tri, BF16), jnp.asarray(bones, BF16), jnp.asarray(eh, BF16)


def _gla(a_q, a_k, a_v, a_r, a_lr, wa, ba, gn, B, S, Lc=128):
    HK, HV = GLA_H * GLA_DK, GLA_H * GLA_DV
    nb = S // Lc
    assert Lc & (Lc - 1) == 0
    tri, bones, eh = _gla_consts(Lc)
    row = lambda w: pl.BlockSpec((B, Lc, w), lambda i: (0, i, 0))
    seq = lambda t: t.reshape(B, S, t.shape[-1])
    out = pl.pallas_call(
        _gla_body,
        out_shape=jax.ShapeDtypeStruct((B, S, HV), BF16),
        grid=(nb,),
        in_specs=[row(HK), row(HK), row(HV), row(HV), row(LANE),
                  _const_spec((LANE, HK)), _const_spec((1, HK)), _const_spec((1, GLA_DV)),
                  _const_spec((Lc, Lc)), _const_spec((Lc, Lc)), _const_spec((HK, HV))],
        out_specs=row(HV),
        scratch_shapes=[pltpu.VMEM((B, GLA_DV, HK), F32), pltpu.VMEM((B, Lc, HK), F32), pltpu.VMEM((B, Lc, HK), F32),
                        pltpu.VMEM((B, Lc, HK), F32), pltpu.VMEM((B, GLA_H * Lc, HK), BF16),
                        pltpu.VMEM((B, GLA_DV, GLA_H * Lc), BF16), pltpu.VMEM((B, Lc, HV), F32)],
        compiler_params=_params(("arbitrary",)),
        name="gla",
    )(seq(a_q), seq(a_k), seq(a_v), seq(a_r), seq(a_lr), wa, ba, gn, tri, bones, eh)
    return out.reshape(B * S, HV)


def _mlstm_body(qk_ref, v_ref, og_ref, if_ref, cw_ref, cb_ref, tri_ref, bdm_ref,
                o_ref, xx_ref, ct_ref, n_ref, m_ref, kw_ref, vt_ref, act_ref):
    Lc = v_ref.shape[0]
    L = MLSTM_CHUNK
    HK = MLSTM_H * MLSTM_DK
    tail = 8

    @pl.when(pl.program_id(1) == 0)
    def _():
        ct_ref[...] = jnp.zeros(ct_ref.shape, F32)
        n_ref[...] = jnp.zeros(n_ref.shape, F32)
        m_ref[...] = jnp.zeros(m_ref.shape, F32)
        xx_ref[0:tail, :] = jnp.zeros((tail, xx_ref.shape[1]), F32)

    xx_ref[tail:tail + Lc, :] = qk_ref[...]
    y = jnp.zeros((Lc, 2 * HK), F32) + cb_ref[...]
    for kk in range(CONV_W):
        y = y + cw_ref[kk:kk + 1, :] * xx_ref[pl.ds(tail - (CONV_W - 1) + kk, Lc), :]
    xx_ref[0:tail, :] = xx_ref[Lc:Lc + tail, :]
    act_ref[...] = _silu(y)

    gates = if_ref[...]
    logf = _log_sigmoid(gates)
    bcum = _dot_r2(tri_ref[...], logf)
    b_t = bcum.T
    q_t = act_ref[:, 0:HK].T
    qt_hi, qt_lo = _split2(q_t)
    kw_ref[...] = jnp.zeros(kw_ref.shape, BF16)
    vt_ref[...] = v_ref[...].astype(F32).T.astype(BF16)
    lane_hk = _iota((1, HK), 1) >> (MLSTM_DK.bit_length() - 1)
    col_l = _iota((1, Lc), 1)
    lane_w = _iota((1, LANE), 1)
    row_l = _iota((L, 1), 0)
    per_win = LANE // L

    for ci in range(Lc // L):
        p, c = divmod(ci, per_win)
        rows = slice(ci * L, (ci + 1) * L)
        win = slice(p * LANE, (p + 1) * LANE)
        causal_t = ((c * L + row_l) <= lane_w) & ((lane_w >> (L.bit_length() - 1)) == c)
        k_all = act_ref[rows, HK:2 * HK] * (MLSTM_DK ** -0.5)
        k_bf = k_all.astype(BF16)
        ct = ct_ref[...]
        ct_bf = ct.astype(BF16)
        n_row = n_ref[...]
        n_hi, n_lo = _split2(n_row)
        n8 = jnp.concatenate([jnp.where(lane_hk == h, part, jnp.zeros((), BF16))
                              for part in (n_hi, n_lo) for h in range(MLSTM_H)], axis=0)
        qn8 = _dot(n8, qt_hi[:, win]) + _dot(n8, qt_lo[:, win])
        wk_full = jnp.zeros((L, HK), F32)
        dec_row = jnp.zeros((1, HK), F32)
        for h in range(MLSTM_H):
            icol = gates[rows, h:h + 1]
            bcol = bcum[rows, MLSTM_H + h:MLSTM_H + h + 1]
            brow = b_t[MLSTM_H + h:MLSTM_H + h + 1, win]
            m_st = m_ref[0:1, h:h + 1]
            dmat_t = jnp.where(causal_t, brow + (icol - bcol), -jnp.inf)
            inter_log = brow + m_st
            m_row = jnp.maximum(inter_log, jnp.max(dmat_t, axis=0, keepdims=True))
            w_inter = jnp.exp(inter_log - m_row)
            head_rows = slice(h * MLSTM_DK, (h + 1) * MLSTM_DK)
            zero_q = jnp.zeros((MLSTM_DK, LANE), BF16)
            qh = jnp.concatenate([qt_hi[head_rows, win] if hh == h else zero_q for hh in range(MLSTM_H)], axis=0)
            s_qk = _dot(k_bf, qh) * jnp.exp(dmat_t - m_row)
            den = w_inter * (qn8[h:h + 1, :] + qn8[MLSTM_H + h:MLSTM_H + h + 1, :]) + jnp.sum(s_qk, axis=0, keepdims=True)
            s_bf = s_qk.astype(BF16)
            pads = [jnp.zeros((L, LANE), BF16)] * per_win
            pads[c] = s_bf
            dv = slice(h * MLSTM_DV, (h + 1) * MLSTM_DV)
            num_t = w_inter * _dot(ct_bf[dv, :], qt_hi[:, win]) + _dot(vt_ref[dv, win], jnp.concatenate(pads, axis=0))
            h_t = num_t / jnp.maximum(jnp.abs(den), jnp.exp(-m_row))
            o_ref[rows, dv] = (h_t.T[c * L:(c + 1) * L, :] * _sigmoid(og_ref[rows, dv])).astype(o_ref.dtype)
            m_new = m_row[:, c * L + L - 1:c * L + L]
            b_last = bcol[L - 1:L, :]
            w_k = jnp.exp(b_last - bcol + icol - m_new)
            decay = jnp.exp(b_last + m_st - m_new)
            head = lane_hk == h
            wk_full = wk_full + jnp.where(head, w_k, 0.0)
            dec_row = dec_row + jnp.where(head, decay, 0.0)
            m_ref[0:1, h:h + 1] = m_new
        kw = k_all * wk_full
        kw_ref[rows, :] = kw.astype(BF16)
        in_chunk = (col_l >= ci * L) & (col_l < (ci + 1) * L)
        vtm = jnp.where(in_chunk, vt_ref[...], jnp.zeros((), BF16))
        ct_ref[...] = dec_row * ct + _dot(vtm, kw_ref[...]) * bdm_ref[...]
        n_ref[...] = dec_row * n_row + jnp.sum(kw, axis=0, keepdims=True)


def _mlstm_consts(Lc):
    r = np.arange(Lc)
    same = (r[:, None] // MLSTM_CHUNK) == (r[None, :] // MLSTM_CHUNK)
    tri = (same & (r[None, :] <= r[:, None])).astype(np.float32)
    hk = np.arange(MLSTM_H * MLSTM_DK) // MLSTM_DK
    hv = np.arange(MLSTM_H * MLSTM_DV) // MLSTM_DV
    eh = (hk[:, None] == hv[None, :]).astype(np.float32)
    return jnp.asarray(tri, BF16), jnp.asarray(eh.T.copy(), F32)


def _mlstm(c_qk, c_v, c_o, c_if, cw, cb, B, S, Lc=256):
    HK, HV = MLSTM_H * MLSTM_DK, MLSTM_H * MLSTM_DV
    nb = S // Lc
    assert Lc % LANE == 0 and LANE % MLSTM_CHUNK == 0
    tri, bdm = _mlstm_consts(Lc)
    row = lambda w: pl.BlockSpec((Lc, w), lambda b, i: (b * nb + i, 0))
    return pl.pallas_call(
        _mlstm_body,
        out_shape=jax.ShapeDtypeStruct((B * S, HV), BF16),
        grid=(B, nb),
        in_specs=[row(2 * HK), row(HV), row(HV), row(LANE),
                  _const_spec((CONV_W, 2 * HK)), _const_spec((1, 2 * HK)),
                  _const_spec((Lc, Lc)), _const_spec((HV, HK))],
        out_specs=row(HV),
        scratch_shapes=[pltpu.VMEM((Lc + 8, 2 * HK), F32), pltpu.VMEM((HV, HK), F32), pltpu.VMEM((1, HK), F32),
                        pltpu.VMEM((8, LANE), F32), pltpu.VMEM((Lc, HK), BF16), pltpu.VMEM((HV, Lc), BF16),
                        pltpu.VMEM((Lc, 2 * HK), F32)],
        compiler_params=_params(("parallel", "arbitrary")),
        name="mlstm",
    )(c_qk, c_v, c_o, c_if, cw, cb, tri, bdm)


def _group_rms(x, gavg, gain):
    ms = _dot_l2(x * x, gavg)
    return x * lax.rsqrt(ms + RMS_EPS) * gain


def _rope_lanes(x, cos, sin_signed):
    half = NSA_D // 2
    first = (_iota((1, LANE), 1) & (NSA_D - 1)) < half
    swapped = jnp.where(first, pltpu.roll(x, LANE - half, 1), pltpu.roll(x, half, 1))
    return x * cos + swapped * sin_signed


def _store_vt_tiles(dst_ref, vt, width):
    ones = jnp.ones((VT_ROWS - NSA_D, width), BF16)
    for j in range(vt.shape[1] // width):
        for g in range(NSA_G):
            dst_ref[j, g, 0:NSA_D, :] = vt[g * NSA_D:(g + 1) * NSA_D, j * width:(j + 1) * width]
            dst_ref[j, g, NSA_D:VT_ROWS, :] = ones


def _nsa_prep_body(q_ref, ks_ref, kw_ref, vs_ref, vw_ref, g_ref, cos_ref, sin_ref, gq_ref, gs_ref, gw_ref, gavg_ref,
                   qt_ref, kso_ref, kwo_ref, vsl_ref, vwd_ref, gt_ref):
    cos, sin = cos_ref[...], sin_ref[...]
    gavg = gavg_ref[...]
    for cb in range(NSA_H * NSA_D // LANE):
        sl = slice(cb * LANE, (cb + 1) * LANE)
        qn = _rope_lanes(_group_rms(q_ref[:, sl], gavg, gq_ref[...]), cos, sin) * (NSA_D ** -0.5 * LOG2E)
        qt_ref[sl, :] = qn.T.astype(BF16)
    kso_ref[...] = _rope_lanes(_group_rms(ks_ref[...], gavg, gs_ref[...]), cos, sin).astype(BF16)
    kwo_ref[...] = _rope_lanes(_group_rms(kw_ref[...], gavg, gw_ref[...]), cos, sin).astype(BF16)
    vst = vs_ref[...].astype(F32).T.astype(BF16)
    _store_vt_tiles(vsl_ref, vst, SEL_KC)
    _store_vt_tiles(vwd_ref, vw_ref[...].astype(F32).T.astype(BF16), LANE)
    gt_ref[...] = _sigmoid(g_ref[...]).T


def _nsa_prep(b_q, b_ks, b_kw, b_vs, b_vw, b_g, cos, sin, gq, gs, gw, gavg, B, S, tm=SEL_KC):
    nb = S // tm
    row = lambda w: pl.BlockSpec((tm, w), lambda b, i: (b * nb + i, 0))
    tab = pl.BlockSpec((tm, LANE), lambda b, i: (i, 0))
    HD = NSA_H * NSA_D
    vt_shape = lambda width: jax.ShapeDtypeStruct((B, S // width, NSA_G, VT_ROWS, width), BF16)
    vt_spec = lambda width: pl.BlockSpec((None, tm // width, NSA_G, VT_ROWS, width), lambda b, i: (b, i, 0, 0, 0))
    return pl.pallas_call(
        _nsa_prep_body,
        out_shape=[jax.ShapeDtypeStruct((B, HD, S), BF16),
                   jax.ShapeDtypeStruct((B, S, LANE), BF16),
                   jax.ShapeDtypeStruct((B, S, LANE), BF16),
                   vt_shape(SEL_KC),
                   vt_shape(LANE),
                   jax.ShapeDtypeStruct((B, LANE, S), F32)],
        grid=(B, nb),
        in_specs=[row(HD), row(LANE), row(LANE), row(LANE), row(LANE), row(LANE), tab, tab,
                  _const_spec((1, LANE)), _const_spec((1, LANE)), _const_spec((1, LANE)), _const_spec((LANE, LANE))],
        out_specs=[pl.BlockSpec((None, HD, tm), lambda b, i: (b, 0, i)),
                   pl.BlockSpec((None, tm, LANE), lambda b, i: (b, i, 0)),
                   pl.BlockSpec((None, tm, LANE), lambda b, i: (b, i, 0)),
                   vt_spec(SEL_KC), vt_spec(LANE),
                   pl.BlockSpec((None, LANE, tm), lambda b, i: (b, 0, i))],
        compiler_params=_params(("parallel", "parallel")),
        name="nsa_prep",
    )(b_q, b_ks, b_kw, b_vs, b_vw, b_g, cos, sin, gq, gs, gw, gavg)


def _gelu_tanh(x):
    return 0.5 * x * (1.0 + jnp.tanh(0.7978845608028654 * (x + 0.044715 * x * x * x)))


def _nsa_cmp_body(xk_ref, xv_ref, pe_ref, wt_ref, wb_ref, w2_ref, cos_ref, sin_ref, gk_ref, gavg_ref,
                  kc_ref, vct_ref):
    n = xk_ref.shape[0]

    def compress(x, which):
        u = _dot((x + pe_ref[which, 0:1, :]).astype(BF16), wt_ref[which])
        v = _dot((x + pe_ref[which, 1:2, :]).astype(BF16), wb_ref[which])
        hid = u + pltpu.roll(v, n - 1, 0)
        return _dot(_gelu_tanh(hid).astype(BF16), w2_ref[which])

    ck = compress(xk_ref[...], 0)
    kc_ref[...] = _rope_lanes(_group_rms(ck, gavg_ref[...], gk_ref[...]), cos_ref[...], sin_ref[...]).astype(BF16)
    vct = compress(xv_ref[...], 1).T.astype(BF16)
    for g in range(NSA_G):
        vct_ref[g] = vct[g * NSA_D:(g + 1) * NSA_D, :]


def _nsa_cmp(xk, xv, pe, wt, wb, w2, cos, sin, gk, gavg):
    B, n, W = xk.shape
    return pl.pallas_call(
        _nsa_cmp_body,
        out_shape=[jax.ShapeDtypeStruct((B, n, LANE), BF16),
                   jax.ShapeDtypeStruct((B, NSA_G, NSA_D, n), BF16)],
        grid=(B,),
        in_specs=[pl.BlockSpec((None, n, W), lambda b: (b, 0, 0)), pl.BlockSpec((None, n, W), lambda b: (b, 0, 0)),
                  _const_spec(pe.shape), _const_spec(wt.shape), _const_spec(wb.shape), _const_spec(w2.shape),
                  _const_spec((n, LANE)), _const_spec((n, LANE)), _const_spec((1, LANE)), _const_spec((LANE, LANE))],
        out_specs=[pl.BlockSpec((None, n, LANE), lambda b: (b, 0, 0)),
                   pl.BlockSpec((None, NSA_G, NSA_D, n), lambda b: (b, 0, 0, 0))],
        compiler_params=_params(("parallel",)),
        name="nsa_compress",
    )(xk, xv, pe, wt, wb, w2, cos, sin, gk, gavg)


def _group_queries(qt_ref, g):
    HPG = NSA_H // NSA_G
    q4 = jnp.concatenate([qt_ref[(g * HPG + h) * NSA_D:(g * HPG + h + 1) * NSA_D, :] for h in range(HPG)], axis=1)
    parts = [jnp.zeros(q4.shape, BF16)] * NSA_G
    parts[g] = q4
    return jnp.concatenate(parts, axis=0)


def _bitonic_merge_desc(xs):
    xs = list(xs)
    j = len(xs) // 2
    while j >= 1:
        for a in range(len(xs)):
            b = a ^ j
            if b > a:
                xs[a], xs[b] = jnp.maximum(xs[a], xs[b]), jnp.minimum(xs[a], xs[b])
        j //= 2
    return xs


def _bitonic_sort_desc(xs):
    if len(xs) == 1:
        return list(xs)
    half = len(xs) // 2
    lo = _bitonic_sort_desc(xs[:half])
    hi = _bitonic_sort_desc(xs[half:])
    return _bitonic_merge_desc(lo + hi[::-1])


def _kth_largest(score, k):
    SUBLANES = 8
    tiles = [score[SUBLANES * v:SUBLANES * (v + 1), :] for v in range(score.shape[0] // SUBLANES)]
    size = max(k, 1 << (len(tiles) - 1).bit_length())
    tiles = tiles + [jnp.full(tiles[0].shape, NEG, F32)] * (size - len(tiles))

    def top_of_two(a, b):
        return _bitonic_merge_desc([jnp.maximum(a[r], b[k - 1 - r]) for r in range(k)])

    tops = [_bitonic_sort_desc(tiles[c:c + k]) for c in range(0, size, k)]
    while len(tops) > 1:
        tops = [top_of_two(tops[c], tops[c + 1]) for c in range(0, len(tops), 2)]
    top = tops[0]
    shift = SUBLANES // 2
    while shift >= 1:
        top = top_of_two(top, [pltpu.roll(t, shift, 0) for t in top])
        shift //= 2
    return top[k - 1][0:1, :]


def _nsa_select_dispatch(*refs, sel_tile, n_variants):
    n_cmp = refs[1].shape[0]
    need = (pl.program_id(1) + 1) * LANE // CMP_STRIDE
    variant = (need - 1) * n_variants // n_cmp
    for k in range(n_variants):
        pl.when(variant == k)(functools.partial(_nsa_select_body, *refs, sel_tile=sel_tile,
                                                n_rows=n_cmp * (k + 1) // n_variants))


def _nsa_select_body(qt_ref, kc_ref, vct_ref, kw_ref, vwd_ref, gt_ref, tril_ref, part_ref, sel_ref, selm_ref, ps_buf,
                     *, sel_tile, n_rows):
    TQ = LANE
    HPG = NSA_H // NSA_G
    W = HPG * TQ
    n_cmp = kc_ref.shape[0]
    n_sel = sel_ref.shape[1]
    i = pl.program_id(1)
    s0 = i * TQ
    t_row = s0 + _iota((1, TQ), 1)
    n_win = WIN // LANE + 1
    j0 = jnp.maximum(i - (n_win - 1), 0)
    w_start = pl.multiple_of(j0 * LANE, LANE)
    own_first = (s0 // sel_tile) * (sel_tile // SLC_LEN)
    ps_buf[:, 0:PS_PAD, :] = jnp.zeros((NSA_G, PS_PAD, TQ), F32)

    for g in range(NSA_G):
        qpad = _group_queries(qt_ref, g)

        cend = _iota((n_rows, 1), 0) * CMP_STRIDE + (CMP_LEN - 1)
        bias_c = jnp.where(cend <= t_row, 0.0, NEG)
        sc = _dot(kc_ref[0:n_rows, :], qpad) + jnp.concatenate([bias_c] * HPG, axis=1)
        m = jnp.max(sc, axis=0, keepdims=True)
        m = jnp.where(m > 0.5 * NEG, m, 0.0)
        p = jnp.exp2(sc - m)
        p = p * (1.0 / jnp.maximum(jnp.sum(p, axis=0, keepdims=True), 1e-30))
        o_c = _dot(vct_ref[g, :, 0:n_rows], p.astype(BF16))
        psum = p[:, 0:TQ]
        for h in range(1, HPG):
            psum = psum + p[:, h * TQ:(h + 1) * TQ]
        ps_buf[g, PS_PAD:PS_PAD + n_rows, :] = psum

        per = SLC_LEN // CMP_STRIDE
        n_blk = n_rows // per
        imp = ps_buf[g, pl.ds(PS_PAD - 1, n_blk, stride=per), :]
        for jj in range(per):
            imp = imp + ps_buf[g, pl.ds(PS_PAD + jj, n_blk, stride=per), :]
        blk = _iota((n_blk, 1), 0)
        cur = t_row >> (SLC_LEN.bit_length() - 1)
        valid = blk * SLC_LEN <= t_row
        forced = (blk == 0) | (blk == cur) | (blk == cur - 1)
        score0 = jnp.where(valid, jnp.where(forced, FORCED_SCORE, imp), NEG)
        kth = _kth_largest(score0, min(SLC_TOPK, n_sel))
        above = score0 > kth
        tied = jnp.where(score0 == kth, 1.0, 0.0)
        need = min(SLC_TOPK, n_sel) - jnp.sum(jnp.where(above, 1.0, 0.0), axis=0, keepdims=True)
        tied_all = jnp.concatenate([tied, jnp.zeros((n_sel - n_blk, TQ), F32)], axis=0) if n_blk < n_sel else tied
        rank = _dot(tril_ref[0:n_blk, :], tied_all.astype(BF16))
        chosen = (above | ((tied > 0.5) & (rank <= need))) & valid
        sel_ref[g, 0:n_blk, :] = jnp.where(chosen, 0.0, NEG)
        selm_ref[g, 0:n_blk, :] = jnp.where(chosen & (blk < own_first), 0.0, NEG)
        if n_blk < n_sel:
            sel_ref[g, n_blk:n_sel, :] = jnp.full((n_sel - n_blk, TQ), NEG, F32)
            selm_ref[g, n_blk:n_sel, :] = jnp.full((n_sel - n_blk, TQ), NEG, F32)

        kpos = w_start + _iota((n_win * LANE, 1), 0)
        bias_w = jnp.where((kpos <= t_row) & (kpos > t_row - WIN), 0.0, NEG)
        sw = _dot(kw_ref[pl.ds(w_start, n_win * LANE), :], qpad) + jnp.concatenate([bias_w] * HPG, axis=1)
        pw = jnp.exp2(sw - jnp.max(sw, axis=0, keepdims=True)).astype(BF16)
        acc_w = jnp.zeros((VT_ROWS, W), F32)
        for r in range(n_win):
            acc_w = acc_w + _dot(vwd_ref[j0 + r, g], pw[r * LANE:(r + 1) * LANE, :])
        o_w = acc_w[0:NSA_D, :] * (1.0 / acc_w[NSA_D:NSA_D + 1, :])

        for hp in range(HPG // 2):
            tiles = []
            for h in (2 * hp, 2 * hp + 1):
                gr = (g * HPG + h) * 3
                cs = slice(h * TQ, (h + 1) * TQ)
                tiles.append(gt_ref[gr:gr + 1, :] * o_c[:, cs] + gt_ref[gr + 2:gr + 3, :] * o_w[:, cs])
            col = (g * HPG + 2 * hp) * NSA_D
            part_ref[:, col:col + 2 * NSA_D] = jnp.concatenate(tiles, axis=0).T


def _nsa_select(qt, kc, vct, kw, vwd, gt, B, S):
    HD = NSA_H * NSA_D
    TQ = LANE
    n_cmp = kc.shape[1]
    n_sel = S // SLC_LEN
    assert S >= WIN + TQ and CMP_LEN == 2 * CMP_STRIDE
    whole = lambda shape: pl.BlockSpec((None,) + shape, lambda b, i: (b,) + (0,) * len(shape),
                                       pipeline_mode=pl.Buffered(1))
    mask_shape = jax.ShapeDtypeStruct((B, NSA_G, n_sel, S), F32)
    mask_spec = pl.BlockSpec((None, NSA_G, n_sel, TQ), lambda b, i: (b, 0, 0, i))
    tril = jnp.asarray(np.tril(np.ones((n_sel, n_sel), np.float32)), BF16)
    return pl.pallas_call(
        functools.partial(_nsa_select_dispatch, sel_tile=SEL_TQ, n_variants=CMP_VARIANTS),
        out_shape=[jax.ShapeDtypeStruct((B * S, HD), F32), mask_shape, mask_shape],
        grid=(B, S // TQ),
        in_specs=[pl.BlockSpec((None, HD, TQ), lambda b, i: (b, 0, i)),
                  whole((n_cmp, LANE)), whole(vct.shape[1:]), whole((S, LANE)), whole(vwd.shape[1:]),
                  pl.BlockSpec((None, LANE, TQ), lambda b, i: (b, 0, i)), _const_spec((n_sel, n_sel))],
        out_specs=[pl.BlockSpec((TQ, HD), lambda b, i: (b * (S // TQ) + i, 0)), mask_spec, mask_spec],
        scratch_shapes=[pltpu.VMEM((NSA_G, n_cmp + PS_PAD, TQ), F32)],
        compiler_params=_params(("parallel", "parallel")),
        name="nsa_select",
    )(qt, kc, vct, kw, vwd, gt, tril)


def _nsa_selected_body(qt_ref, ks_ref, ksd_ref, vsl_ref, selm_ref, seld_ref, gt_ref, part_ref, oneh_ref, o_ref,
                       s_a, s_b, p_a, p_b, al_a, al_b, m_ref, acc_ref):
    TQ = qt_ref.shape[1]
    HPG = NSA_H // NSA_G
    W = HPG * TQ
    n_kc = ks_ref.shape[0] // SEL_KC
    bps = SEL_KC // SLC_LEN
    i = pl.program_id(1)
    n_main = i * (TQ // SEL_KC)
    n_pairs = (n_main + 1) // 2
    diag_ok = _iota((TQ, 1), 0) <= _iota((1, TQ), 1)
    pad_rows = jnp.zeros((LANE - 2 * bps, W), BF16)

    def q_aug(qpad, brows):
        brows = jnp.concatenate([jnp.concatenate([brows] * HPG, axis=1), jnp.zeros((bps, W), F32)], axis=0)
        return jnp.concatenate([qpad, brows.astype(BF16), pad_rows], axis=0)

    heads = [slice(h * TQ, (h + 1) * TQ) for h in range(HPG)]

    groups = range(NSA_G)

    def softmax(g, s_buf, p_buf, al_ref, cs):
        sb = s_buf[g, :, cs]
        m_i = m_ref[g, :, cs]
        m_new = jnp.maximum(m_i, jnp.max(sb, axis=0, keepdims=True).astype(F32))
        p_buf[g, :, cs] = jnp.exp2((sb - m_new.astype(BF16)).astype(F32)).astype(BF16)
        al_ref[g, :, cs] = jnp.exp2(m_i - m_new)
        m_ref[g, :, cs] = m_new

    def apply_values(g, p_buf, al_ref, vt, cs):
        acc_ref[g, :, cs] = al_ref[g, :, cs] * acc_ref[g, :, cs] + _dot(vt, p_buf[g, :, cs])

    qpads = [_group_queries(qt_ref, g) for g in groups]
    m_ref[...] = jnp.full(m_ref.shape, NEG, F32)
    acc_ref[...] = jnp.zeros(acc_ref.shape, F32)
    p_b[...] = jnp.zeros(p_b.shape, BF16)
    al_b[...] = jnp.ones(al_b.shape, F32)

    def step_operands(c):
        k0 = pl.multiple_of(c * SEL_KC, SEL_KC)
        rows = pl.ds(pl.multiple_of(c * bps, bps), bps)
        k_aug = jnp.concatenate([ks_ref[pl.ds(k0, SEL_KC), :], oneh_ref[...]], axis=1)
        return k_aug, [q_aug(qpads[g], selm_ref[g, rows, :]) for g in groups]

    def scores(ops, g, s_buf, cs):
        s_buf[g, :, cs] = _dot(ops[0], ops[1][g][:, cs]).astype(BF16)

    def stages(ops, c_prev, s_new, s_cur, p_cur, al_cur, p_prev, al_prev):
        for g in groups:
            vt = vsl_ref[c_prev, g]
            for cs in heads:
                scores(ops, g, s_new, cs)
                softmax(g, s_cur, p_cur, al_cur, cs)
                apply_values(g, p_prev, al_prev, vt, cs)

    def pair(cc, carry):
        c0 = 2 * cc
        stages(step_operands(c0 + 1), jnp.maximum(c0 - 1, 0), s_b, s_a, p_a, al_a, p_b, al_b)
        stages(step_operands(jnp.minimum(c0 + 2, n_kc - 1)), c0, s_a, s_b, p_b, al_b, p_a, al_a)
        return carry

    ops = step_operands(0)
    for g in groups:
        for cs in heads:
            scores(ops, g, s_a, cs)
    lax.fori_loop(0, n_pairs, pair, 0)
    for g in groups:
        vt = vsl_ref[jnp.maximum(2 * n_pairs - 1, 0), g]
        for cs in heads:
            apply_values(g, p_b, al_b, vt, cs)

    for d in range(TQ // SEL_KC):
        k_aug = jnp.concatenate([ksd_ref[d * SEL_KC:(d + 1) * SEL_KC, :], oneh_ref[...]], axis=1)
        for g in groups:
            qa = q_aug(qpads[g], seld_ref[g, d * bps:(d + 1) * bps, :])
            vt = vsl_ref[i * (TQ // SEL_KC) + d, g]
            for cs in heads:
                sd = _dot(k_aug, qa[:, cs])
                s_a[g, :, cs] = jnp.where(diag_ok[d * SEL_KC:(d + 1) * SEL_KC, :], sd, NEG).astype(BF16)
                softmax(g, s_a, p_a, al_a, cs)
                apply_values(g, p_a, al_a, vt, cs)

    for g in groups:
        acc_s = acc_ref[g]
        o_s = acc_s[0:NSA_D, :] * (1.0 / acc_s[NSA_D:NSA_D + 1, :])

        for hp in range(HPG // 2):
            tiles = []
            for h in (2 * hp, 2 * hp + 1):
                gr = (g * HPG + h) * 3 + 1
                tiles.append(gt_ref[gr:gr + 1, :] * o_s[:, h * TQ:(h + 1) * TQ])
            cols = slice((g * HPG + 2 * hp) * NSA_D, (g * HPG + 2 * hp + 2) * NSA_D)
            o_ref[:, cols] = (part_ref[:, cols] + jnp.concatenate(tiles, axis=0).T).astype(o_ref.dtype)


def _nsa_selected(qt, ks, vsl, selm, sel, gt, part, B, S):
    HD = NSA_H * NSA_D
    TQ = SEL_TQ
    W = NSA_H // NSA_G * TQ
    n_sel = S // SLC_LEN
    assert S % (2 * SEL_KC) == 0 and TQ % SEL_KC == 0
    whole = lambda shape: pl.BlockSpec((None,) + shape, lambda b, i: (b,) + (0,) * len(shape),
                                       pipeline_mode=pl.Buffered(1))
    oneh = jnp.asarray(np.arange(SEL_KC)[:, None] // SLC_LEN == np.arange(LANE)[None, :], BF16)
    return pl.pallas_call(
        _nsa_selected_body,
        out_shape=jax.ShapeDtypeStruct((B * S, HD), BF16),
        grid=(B, S // TQ),
        in_specs=[pl.BlockSpec((None, HD, TQ), lambda b, i: (b, 0, i)),
                  whole((S, LANE)), pl.BlockSpec((None, TQ, LANE), lambda b, i: (b, i, 0)),
                  whole(vsl.shape[1:]),
                  pl.BlockSpec((None, NSA_G, n_sel, TQ), lambda b, i: (b, 0, 0, i)),
                  pl.BlockSpec((None, NSA_G, TQ // SLC_LEN, TQ), lambda b, i: (b, 0, i, i)),
                  pl.BlockSpec((None, LANE, TQ), lambda b, i: (b, 0, i)),
                  pl.BlockSpec((TQ, HD), lambda b, i: (b * (S // TQ) + i, 0)),
                  _const_spec((SEL_KC, LANE))],
        out_specs=pl.BlockSpec((TQ, HD), lambda b, i: (b * (S // TQ) + i, 0)),
        scratch_shapes=[pltpu.VMEM((NSA_G, SEL_KC, W), BF16)] * 4 + [pltpu.VMEM((NSA_G, 1, W), F32)] * 3
                       + [pltpu.VMEM((NSA_G, VT_ROWS, W), F32)],
        compiler_params=_params(("parallel", "arbitrary")),
        name="nsa_selected",
    )(qt, ks, ks, vsl, selm, sel, gt, part, oneh)


def _merge_body(x_ref, oa_ref, ob_ref, oc_ref, gates_ref, wbr_ref, wo_ref, o_ref):
    D = x_ref.shape[1]
    y = jnp.zeros(x_ref.shape, F32)
    for j, br in enumerate((oa_ref, ob_ref, oc_ref)):
        y = y + _sigmoid(gates_ref[:, j * D:(j + 1) * D].astype(F32)) * _dot(br[...].astype(BF16), wbr_ref[j])
    o_ref[...] = x_ref[...] + _dot(y.astype(BF16), wo_ref[...])


def _merge(x, o_a, o_b, o_c, gates, wbr, wo, tm=512):
    T, D = x.shape
    row = lambda w: pl.BlockSpec((tm, w), lambda i: (i, 0))
    return pl.pallas_call(
        _merge_body,
        out_shape=jax.ShapeDtypeStruct((T, D), F32),
        grid=(T // tm,),
        in_specs=[row(D), row(MIX_W), row(MIX_W), row(MIX_W), row(3 * D), _const_spec(wbr.shape), _const_spec(wo.shape)],
        out_specs=row(D),
        compiler_params=_params(("parallel",)),
        name="merge_out",
    )(x, o_a, o_b, o_c, gates, wbr, wo)


def _pack_in_proj(w_in, b_in):
    ws, bs = [], []
    for _, off, w, wp in _SEGS:
        ws.append(jnp.pad(w_in[..., off:off + w], ((0, 0), (0, 0), (0, wp - w))))
        bs.append(jnp.pad(b_in[..., off:off + w], ((0, 0), (0, wp - w))))
    return jnp.concatenate(ws, axis=-1).astype(BF16), jnp.concatenate(bs, axis=-1)[:, None, :]


def _rope_tables(pos):
    half = NSA_D // 2
    freqs = ROPE_THETA ** (-jnp.arange(half, dtype=F32) / half)
    ang = pos.astype(F32)[:, None] * freqs[None, :]
    cos, sin = jnp.cos(ang), jnp.sin(ang)
    reps = LANE // NSA_D
    return jnp.tile(jnp.concatenate([cos, cos], axis=1), (1, reps)), jnp.tile(jnp.concatenate([-sin, sin], axis=1), (1, reps))


def _cmp_weights(w1, w2):
    L = w1.shape[0]
    w1r = w1.reshape(L, 2, 2, CMP_STRIDE, NSA_D, CMP_HIDDEN)
    eye_g = jnp.eye(NSA_G, dtype=w1.dtype)
    ex = jnp.einsum('lwstdh,gk->lwstgdkh', w1r, eye_g)
    ex = ex.reshape(L, 2, 2, CMP_STRIDE * NSA_G * NSA_D, NSA_G * CMP_HIDDEN)
    w2x = jnp.einsum('lwhd,gk->lwghkd', w2, eye_g).reshape(L, 2, NSA_G * CMP_HIDDEN, NSA_G * NSA_D)
    return ex[:, :, 0].astype(BF16), ex[:, :, 1].astype(BF16), w2x.astype(BF16)


def _cmp_pos_rows(pe):
    L = pe.shape[0]
    r = pe.reshape(L, 2, 2, CMP_STRIDE, 1, NSA_D)
    return jnp.broadcast_to(r, (L, 2, 2, CMP_STRIDE, NSA_G, NSA_D)).reshape(L, 2, 2, CMP_STRIDE * NSA_G * NSA_D)


def _lane_gain(g):
    return jnp.tile(g, LANE // g.shape[-1])[None, :]


def _layer(x, lw, consts, B, S):
    cos, sin, cos_c, sin_c, gavg = consts
    T = B * S
    x = _ffn(x, lw["ffn1_norm"], lw["ffn1_wg"], lw["ffn1_wu"], lw["ffn1_wd"])
    z = dict(zip([s[0] for s in _SEGS], _proj(x, lw["mix_norm"], lw["w_in"], lw["b_in"])))
    o_a = _gla(z["a_q"], z["a_k"], z["a_v"], z["a_r"], z["a_lr"], lw["gla_wa"], lw["gla_ba"], lw["gla_gn"], B, S)
    o_c = _mlstm(z["c_qk"], z["c_v"], z["c_o"], z["c_if"], lw["conv_w"], lw["conv_b"], B, S)
    qn = lw["nsa_qk_norm"]
    qt, ks, kw, vsl, vwd, gt = _nsa_prep(z["b_q"], z["b_ks"], z["b_kw"], z["b_vs"], z["b_vw"], z["b_g"], cos, sin,
                                         _lane_gain(qn[0]), _lane_gain(qn[2]), _lane_gain(qn[3]), gavg, B, S)
    n_blk = S // CMP_STRIDE
    xk = z["b_kc"].reshape(B, n_blk, CMP_STRIDE * LANE)
    xv = z["b_vc"].reshape(B, n_blk, CMP_STRIDE * LANE)
    kc, vct = _nsa_cmp(xk, xv, lw["cmp_pe"], lw["cmp_wt"], lw["cmp_wb"], lw["cmp_w2"], cos_c, sin_c,
                       _lane_gain(qn[1]), gavg)
    part, sel, selm = _nsa_select(qt, kc, vct, kw, vwd, gt, B, S)
    o_b = _nsa_selected(qt, ks, vsl, selm, sel, gt, part, B, S)
    x = _merge(x, o_a, o_b, o_c, z["gates"], lw["w_branch"], lw["w_out"])
    return _ffn(x, lw["ffn2_norm"], lw["ffn2_wg"], lw["ffn2_wu"], lw["ffn2_wd"])


def kernel(x, ffn1_norm, ffn1_w_gate, ffn1_w_up, ffn1_w_down, mix_norm, w_in, b_in, gla_w_alpha, gla_b_alpha, gla_out_norm, nsa_qk_norm, nsa_cmp_pos, nsa_cmp_w1, nsa_cmp_w2, mlstm_conv_w, mlstm_conv_b, w_branch, w_out, ffn2_norm, ffn2_w_gate, ffn2_w_up, ffn2_w_down):
    B, S, D = x.shape
    w_in_p, b_in_p = _pack_in_proj(w_in, b_in)
    cmp_wt, cmp_wb, cmp_w2 = _cmp_weights(nsa_cmp_w1, nsa_cmp_w2)
    layers = {
        "ffn1_norm": ffn1_norm[:, None, :], "ffn1_wg": ffn1_w_gate.astype(BF16), "ffn1_wu": ffn1_w_up.astype(BF16),
        "ffn1_wd": ffn1_w_down.astype(BF16),
        "mix_norm": mix_norm[:, None, :], "w_in": w_in_p, "b_in": b_in_p,
        "gla_wa": jnp.pad(gla_w_alpha, ((0, 0), (0, LANE - GLA_RANK), (0, 0))), "gla_ba": gla_b_alpha[:, None, :],
        "gla_gn": gla_out_norm[:, None, :],
        "nsa_qk_norm": nsa_qk_norm, "cmp_pe": _cmp_pos_rows(nsa_cmp_pos), "cmp_wt": cmp_wt, "cmp_wb": cmp_wb,
        "cmp_w2": cmp_w2,
        "conv_w": mlstm_conv_w, "conv_b": mlstm_conv_b[:, None, :],
        "w_branch": w_branch.astype(BF16), "w_out": w_out.astype(BF16),
        "ffn2_norm": ffn2_norm[:, None, :], "ffn2_wg": ffn2_w_gate.astype(BF16), "ffn2_wu": ffn2_w_up.astype(BF16),
        "ffn2_wd": ffn2_w_down.astype(BF16),
    }
    n_blk = S // CMP_STRIDE
    cos, sin = _rope_tables(jnp.arange(S))
    cos_c, sin_c = _rope_tables(jnp.arange(n_blk) * CMP_STRIDE + CMP_LEN - 1)
    gavg = jnp.asarray(np.kron(np.eye(LANE // NSA_D), np.full((NSA_D, NSA_D), 1.0 / NSA_D)), BF16)
    consts = (cos, sin, cos_c, sin_c, gavg)

    out = x.reshape(B * S, D)
    for l in range(w_in.shape[0]):
        out = _layer(out, {name: w[l] for name, w in layers.items()}, consts, B, S)
    return out.reshape(B, S, D)
```

```python
import functools

import numpy as np
import jax
import jax.numpy as jnp
from jax import lax
from jax.experimental import pallas as pl
from jax.experimental.pallas import tpu as pltpu

F32 = jnp.float32
BF16 = jnp.bfloat16

RMS_EPS = 1e-6
ROPE_THETA = 10000.0

GLA_H, GLA_DK, GLA_DV, GLA_RANK, GLA_GATE_NORM = 4, 64, 128, 16, 16.0
GLA_SUB = 16
NSA_H, NSA_G, NSA_D = 8, 2, 64
CMP_LEN, CMP_STRIDE, CMP_HIDDEN = 32, 16, 256
SLC_LEN, SLC_TOPK, WIN = 64, 16, 512
FORCED_SCORE = 1e4
MLSTM_H, MLSTM_DK, MLSTM_DV, CONV_W = 4, 64, 128, 4
MLSTM_CHUNK = 128
MIX_W = 512
LANE = 128
MXU_N = 256
NEG = -1e30
LOG2E = 1.4426950408889634
SEL_KC = 512
SEL_TQ = 512
PS_PAD = 8
CMP_VARIANTS = 4
VT_ROWS = 80

_SEGS = (
    ("a_q", 0, 256, 256), ("a_k", 256, 256, 256), ("a_v", 512, 512, 512), ("a_r", 1024, 512, 512),
    ("b_q", 1552, 512, 512), ("c_qk", 2856, 512, 512), ("c_v", 3368, 512, 512), ("c_o", 3880, 512, 512),
    ("gates", 4400, 3072, 3072),
    ("a_lr", 1536, 16, 128), ("b_kc", 2064, 128, 128), ("b_vc", 2192, 128, 128), ("b_ks", 2320, 128, 128),
    ("b_vs", 2448, 128, 128), ("b_kw", 2576, 128, 128), ("b_vw", 2704, 128, 128), ("b_g", 2832, 24, 128),
    ("c_if", 4392, 8, 128),
)
_N_PACK = sum(s[3] for s in _SEGS)
_BF16_SEGS = ("gates", "c_v", "b_vs", "b_vw")


def _dot(a, b):
    return jnp.dot(a, b, preferred_element_type=F32)


def _dot_nt(a, b):
    return lax.dot_general(a, b, (((1,), (1,)), ((), ())), preferred_element_type=F32)


def _split2(a):
    hi = a.astype(BF16)
    lo = (a - hi.astype(F32)).astype(BF16)
    return hi, lo


def _dot_l2(a, b):
    hi, lo = _split2(a)
    return _dot(hi, b) + _dot(lo, b)


def _dot_r2(a, b):
    hi, lo = _split2(b)
    return _dot(a, hi) + _dot(a, lo)


def _log_sigmoid(x):
    return jnp.minimum(x, 0.0) - jnp.log(1.0 + jnp.exp(-jnp.abs(x)))


def _sigmoid(x):
    return 1.0 / (1.0 + jnp.exp(-x))


def _silu(x):
    return x * _sigmoid(x)


def _iota(shape, dim):
    return lax.broadcasted_iota(jnp.int32, shape, dim)


def _const_spec(shape):
    nd = len(shape)
    return pl.BlockSpec(shape, lambda *_: (0,) * nd, pipeline_mode=pl.Buffered(1))


def _params(sem, vmem_mb=56):
    return pltpu.CompilerParams(dimension_semantics=sem, vmem_limit_bytes=vmem_mb * 1024 * 1024)


def _ffn_body(x_ref, g_ref, wg_ref, wu_ref, wd_ref, o_ref, *, bounds):
    x = x_ref[...]
    ms = jnp.mean(x * x, axis=-1, keepdims=True)
    h = (x * lax.rsqrt(ms + RMS_EPS) * g_ref[...]).astype(BF16)
    acc = jnp.zeros(x.shape, F32)
    for lo, hi in zip(bounds[:-1], bounds[1:]):
        a = _dot(h, wg_ref[:, lo:hi])
        u = _dot(h, wu_ref[:, lo:hi])
        t = (_silu(a) * u).astype(BF16)
        acc = acc + _dot(t, wd_ref[lo:hi, :])
    o_ref[...] = x + 0.5 * acc


def _ffn(x, g, wg, wu, wd, tm=512):
    T, D = x.shape
    F = wg.shape[1]
    bounds = (0, -(-F // (2 * MXU_N)) * MXU_N, F)
    return pl.pallas_call(
        functools.partial(_ffn_body, bounds=bounds),
        out_shape=jax.ShapeDtypeStruct((T, D), F32),
        grid=(T // tm,),
        in_specs=[pl.BlockSpec((tm, D), lambda i: (i, 0)), _const_spec((1, D)),
                  _const_spec((D, F)), _const_spec((D, F)), _const_spec((F, D))],
        out_specs=pl.BlockSpec((tm, D), lambda i: (i, 0)),
        compiler_params=_params(("parallel",)),
        name="ffn",
    )(x, g, wg, wu, wd)


def _proj_body(x_ref, g_ref, w_ref, b_ref, *o_refs):
    x = x_ref[...]
    ms = jnp.mean(x * x, axis=-1, keepdims=True)
    h = (x * lax.rsqrt(ms + RMS_EPS) * g_ref[...]).astype(BF16)
    off, i = 0, 0
    while i < len(o_refs):
        group, w = [], 0
        while i < len(o_refs) and (not group or w % MXU_N):
            group.append(o_refs[i])
            w += o_refs[i].shape[1]
            i += 1
        z = _dot(h, w_ref[:, off:off + w]) + b_ref[:, off:off + w]
        c = 0
        for o_ref in group:
            o_ref[...] = z[:, c:c + o_ref.shape[1]].astype(o_ref.dtype)
            c += o_ref.shape[1]
        off += w


def _proj(x, g, w, b, tm=256):
    T, D = x.shape
    return pl.pallas_call(
        _proj_body,
        out_shape=[jax.ShapeDtypeStruct((T, s[3]), BF16 if s[0] in _BF16_SEGS else F32) for s in _SEGS],
        grid=(T // tm,),
        in_specs=[pl.BlockSpec((tm, D), lambda i: (i, 0)), _const_spec((1, D)),
                  _const_spec((D, _N_PACK)), _const_spec((1, _N_PACK))],
        out_specs=[pl.BlockSpec((tm, s[3]), lambda i: (i, 0)) for s in _SEGS],
        compiler_params=_params(("parallel",)),
        name="in_proj",
    )(x, g, w, b)


def _gla_body(q_ref, k_ref, v_ref, r_ref, lr_ref, wa_ref, ba_ref, gn_ref, tri_ref, bones_ref, eh_ref,
              o_ref, st_ref, qs_ref, c_ref, tot_ref, kst_ref, vt4_ref, oi_ref):
    nB, Lc = q_ref.shape[0], q_ref.shape[1]
    n_sub = Lc // GLA_SUB
    HK = GLA_H * GLA_DK

    @pl.when(pl.program_id(0) == 0)
    def _():
        st_ref[...] = jnp.zeros(st_ref.shape, F32)

    lane_h = _iota((1, HK), 1) >> (GLA_DK.bit_length() - 1)
    row_i = _iota((GLA_SUB, 1), 0)
    col_r = _iota((1, GLA_H * Lc), 1) & (Lc - 1)
    wa_hi, wa_lo = _split2(wa_ref[...])

    for b in range(nB):
        lr_hi, lr_lo = _split2(lr_ref[b])
        u = _dot(lr_hi, wa_hi) + _dot(lr_hi, wa_lo) + _dot(lr_lo, wa_hi) + ba_ref[...]
        g = _log_sigmoid(u) * (1.0 / GLA_GATE_NORM)
        c = _dot_r2(tri_ref[...], g)
        tot = _dot_r2(bones_ref[...], g)
        qs_ref[b] = q_ref[b] * (GLA_DK ** -0.5)
        c_ref[b] = c
        tot_ref[b] = tot
        kt = k_ref[b] * jnp.exp(tot - c)
        vt = v_ref[b].T.astype(BF16)
        for h in range(GLA_H):
            kst_ref[b, h * Lc:(h + 1) * Lc, :] = jnp.where(lane_h == h, kt, 0.0).astype(BF16)
            vt4_ref[b, :, h * Lc:(h + 1) * Lc] = vt[h * GLA_DV:(h + 1) * GLA_DV, :]

    def sub(b, s):
        rows = slice(s * GLA_SUB, (s + 1) * GLA_SUB)
        qs = qs_ref[b, rows, :]
        cs = c_ref[b, rows, :]
        st = st_ref[b]
        qd = qs * jnp.exp(cs)
        q4 = jnp.concatenate([jnp.where(lane_h == h, qd, 0.0) for h in range(GLA_H)], axis=0).astype(BF16)
        inter4 = _dot_nt(q4, st.astype(BF16))
        inter = jnp.concatenate([inter4[h * GLA_SUB:(h + 1) * GLA_SUB, :] for h in range(GLA_H)], axis=1)
        xs = []
        for j in range(GLA_SUB):
            r = s * GLA_SUB + j
            x = qs * k_ref[b, r:r + 1, :] * jnp.exp(jnp.minimum(cs - c_ref[b, r:r + 1, :], 0.0))
            xs.append(jnp.where(row_i >= j, x, 0.0))
        r_all = _dot(jnp.concatenate(xs, axis=0).astype(BF16), eh_ref[...])
        intra = jnp.zeros((GLA_SUB, r_all.shape[1]), F32)
        for j in range(GLA_SUB):
            r = s * GLA_SUB + j
            intra = intra + r_all[j * GLA_SUB:(j + 1) * GLA_SUB, :] * v_ref[b, r:r + 1, :]
        oi_ref[b, rows, :] = inter + intra
        dec = jnp.exp(tot_ref[b, s * GLA_SUB:s * GLA_SUB + 1, :])
        in_sub = (col_r >= s * GLA_SUB) & (col_r < (s + 1) * GLA_SUB)
        vtm = jnp.where(in_sub, vt4_ref[b], jnp.zeros((), BF16))
        st_ref[b] = dec * st + _dot(vtm, kst_ref[b])

    for s in range(n_sub):
        for b in range(nB):
            sub(b, s)

    gn = gn_ref[...]
    for b in range(nB):
        for h in range(GLA_H):
            sl = slice(h * GLA_DV, (h + 1) * GLA_DV)
            o = oi_ref[b, :, sl]
            ms = jnp.mean(o * o, axis=-1, keepdims=True)
            o_ref[b, :, sl] = (o * lax.rsqrt(ms + RMS_EPS) * gn * _silu(r_ref[b, :, sl])).astype(o_ref.dtype)


def _gla_consts(Lc):
    r = np.arange(Lc)
    same = (r[:, None] // GLA_SUB) == (r[None, :] // GLA_SUB)
    tri = (same & (r[None, :] <= r[:, None])).astype(np.float32)
    bones = same.astype(np.float32)
    hk = np.arange(GLA_H * GLA_DK) // GLA_DK
    hv = np.arange(GLA_H * GLA_DV) // GLA_DV
    eh = (hk[:, None] == hv[None, :]).astype(np.float32)
    return jnp.asarray(tri, BF16), jnp.asarray(bones, BF16), jnp.asarray(eh, BF16)


def _gla(a_q, a_k, a_v, a_r, a_lr, wa, ba, gn, B, S, Lc=128):
    HK, HV = GLA_H * GLA_DK, GLA_H * GLA_DV
    nb = S // Lc
    assert Lc & (Lc - 1) == 0
    tri, bones, eh = _gla_consts(Lc)
    row = lambda w: pl.BlockSpec((B, Lc, w), lambda i: (0, i, 0))
    seq = lambda t: t.reshape(B, S, t.shape[-1])
    out = pl.pallas_call(
        _gla_body,
        out_shape=jax.ShapeDtypeStruct((B, S, HV), BF16),
        grid=(nb,),
        in_specs=[row(HK), row(HK), row(HV), row(HV), row(LANE),
                  _const_spec((LANE, HK)), _const_spec((1, HK)), _const_spec((1, GLA_DV)),
                  _const_spec((Lc, Lc)), _const_spec((Lc, Lc)), _const_spec((HK, HV))],
        out_specs=row(HV),
        scratch_shapes=[pltpu.VMEM((B, GLA_DV, HK), F32), pltpu.VMEM((B, Lc, HK), F32), pltpu.VMEM((B, Lc, HK), F32),
                        pltpu.VMEM((B, Lc, HK), F32), pltpu.VMEM((B, GLA_H * Lc, HK), BF16),
                        pltpu.VMEM((B, GLA_DV, GLA_H * Lc), BF16), pltpu.VMEM((B, Lc, HV), F32)],
        compiler_params=_params(("arbitrary",)),
        name="gla",
    )(seq(a_q), seq(a_k), seq(a_v), seq(a_r), seq(a_lr), wa, ba, gn, tri, bones, eh)
    return out.reshape(B * S, HV)


def _mlstm_body(qk_ref, v_ref, og_ref, if_ref, cw_ref, cb_ref, tri_ref, bdm_ref,
                o_ref, xx_ref, ct_ref, n_ref, m_ref, kw_ref, vt_ref, act_ref):
    Lc = v_ref.shape[0]
    L = MLSTM_CHUNK
    HK = MLSTM_H * MLSTM_DK
    tail = 8

    @pl.when(pl.program_id(1) == 0)
    def _():
        ct_ref[...] = jnp.zeros(ct_ref.shape, F32)
        n_ref[...] = jnp.zeros(n_ref.shape, F32)
        m_ref[...] = jnp.zeros(m_ref.shape, F32)
        xx_ref[0:tail, :] = jnp.zeros((tail, xx_ref.shape[1]), F32)

    xx_ref[tail:tail + Lc, :] = qk_ref[...]
    y = jnp.zeros((Lc, 2 * HK), F32) + cb_ref[...]
    for kk in range(CONV_W):
        y = y + cw_ref[kk:kk + 1, :] * xx_ref[pl.ds(tail - (CONV_W - 1) + kk, Lc), :]
    xx_ref[0:tail, :] = xx_ref[Lc:Lc + tail, :]
    act_ref[...] = _silu(y)

    gates = if_ref[...]
    logf = _log_sigmoid(gates)
    bcum = _dot_r2(tri_ref[...], logf)
    b_t = bcum.T
    q_t = act_ref[:, 0:HK].T
    qt_hi, qt_lo = _split2(q_t)
    kw_ref[...] = jnp.zeros(kw_ref.shape, BF16)
    vt_ref[...] = v_ref[...].astype(F32).T.astype(BF16)
    lane_hk = _iota((1, HK), 1) >> (MLSTM_DK.bit_length() - 1)
    col_l = _iota((1, Lc), 1)
    lane_w = _iota((1, LANE), 1)
    row_l = _iota((L, 1), 0)
    per_win = LANE // L

    for ci in range(Lc // L):
        p, c = divmod(ci, per_win)
        rows = slice(ci * L, (ci + 1) * L)
        win = slice(p * LANE, (p + 1) * LANE)
        causal_t = ((c * L + row_l) <= lane_w) & ((lane_w >> (L.bit_length() - 1)) == c)
        k_all = act_ref[rows, HK:2 * HK] * (MLSTM_DK ** -0.5)
        k_bf = k_all.astype(BF16)
        ct = ct_ref[...]
        ct_bf = ct.astype(BF16)
        n_row = n_ref[...]
        n_hi, n_lo = _split2(n_row)
        n8 = jnp.concatenate([jnp.where(lane_hk == h, part, jnp.zeros((), BF16))
                              for part in (n_hi, n_lo) for h in range(MLSTM_H)], axis=0)
        qn8 = _dot(n8, qt_hi[:, win]) + _dot(n8, qt_lo[:, win])
        wk_full = jnp.zeros((L, HK), F32)
        dec_row = jnp.zeros((1, HK), F32)
        for h in range(MLSTM_H):
            icol = gates[rows, h:h + 1]
            bcol = bcum[rows, MLSTM_H + h:MLSTM_H + h + 1]
            brow = b_t[MLSTM_H + h:MLSTM_H + h + 1, win]
            m_st = m_ref[0:1, h:h + 1]
            dmat_t = jnp.where(causal_t, brow + (icol - bcol), -jnp.inf)
            inter_log = brow + m_st
            m_row = jnp.maximum(inter_log, jnp.max(dmat_t, axis=0, keepdims=True))
            w_inter = jnp.exp(inter_log - m_row)
            head_rows = slice(h * MLSTM_DK, (h + 1) * MLSTM_DK)
            zero_q = jnp.zeros((MLSTM_DK, LANE), BF16)
            qh = jnp.concatenate([qt_hi[head_rows, win] if hh == h else zero_q for hh in range(MLSTM_H)], axis=0)
            s_qk = _dot(k_bf, qh) * jnp.exp(dmat_t - m_row)
            den = w_inter * (qn8[h:h + 1, :] + qn8[MLSTM_H + h:MLSTM_H + h + 1, :]) + jnp.sum(s_qk, axis=0, keepdims=True)
            s_bf = s_qk.astype(BF16)
            pads = [jnp.zeros((L, LANE), BF16)] * per_win
            pads[c] = s_bf
            dv = slice(h * MLSTM_DV, (h + 1) * MLSTM_DV)
            num_t = w_inter * _dot(ct_bf[dv, :], qt_hi[:, win]) + _dot(vt_ref[dv, win], jnp.concatenate(pads, axis=0))
            h_t = num_t / jnp.maximum(jnp.abs(den), jnp.exp(-m_row))
            o_ref[rows, dv] = (h_t.T[c * L:(c + 1) * L, :] * _sigmoid(og_ref[rows, dv])).astype(o_ref.dtype)
            m_new = m_row[:, c * L + L - 1:c * L + L]
            b_last = bcol[L - 1:L, :]
            w_k = jnp.exp(b_last - bcol + icol - m_new)
            decay = jnp.exp(b_last + m_st - m_new)
            head = lane_hk == h
            wk_full = wk_full + jnp.where(head, w_k, 0.0)
            dec_row = dec_row + jnp.where(head, decay, 0.0)
            m_ref[0:1, h:h + 1] = m_new
        kw = k_all * wk_full
        kw_ref[rows, :] = kw.astype(BF16)
        in_chunk = (col_l >= ci * L) & (col_l < (ci + 1) * L)
        vtm = jnp.where(in_chunk, vt_ref[...], jnp.zeros((), BF16))
        ct_ref[...] = dec_row * ct + _dot(vtm, kw_ref[...]) * bdm_ref[...]
        n_ref[...] = dec_row * n_row + jnp.sum(kw, axis=0, keepdims=True)


def _mlstm_consts(Lc):
    r = np.arange(Lc)
    same = (r[:, None] // MLSTM_CHUNK) == (r[None, :] // MLSTM_CHUNK)
    tri = (same & (r[None, :] <= r[:, None])).astype(np.float32)
    hk = np.arange(MLSTM_H * MLSTM_DK) // MLSTM_DK
    hv = np.arange(MLSTM_H * MLSTM_DV) // MLSTM_DV
    eh = (hk[:, None] == hv[None, :]).astype(np.float32)
    return jnp.asarray(tri, BF16), jnp.asarray(eh.T.copy(), F32)


def _mlstm(c_qk, c_v, c_o, c_if, cw, cb, B, S, Lc=256):
    HK, HV = MLSTM_H * MLSTM_DK, MLSTM_H * MLSTM_DV
    nb = S // Lc
    assert Lc % LANE == 0 and LANE % MLSTM_CHUNK == 0
    tri, bdm = _mlstm_consts(Lc)
    row = lambda w: pl.BlockSpec((Lc, w), lambda b, i: (b * nb + i, 0))
    return pl.pallas_call(
        _mlstm_body,
        out_shape=jax.ShapeDtypeStruct((B * S, HV), BF16),
        grid=(B, nb),
        in_specs=[row(2 * HK), row(HV), row(HV), row(LANE),
                  _const_spec((CONV_W, 2 * HK)), _const_spec((1, 2 * HK)),
                  _const_spec((Lc, Lc)), _const_spec((HV, HK))],
        out_specs=row(HV),
        scratch_shapes=[pltpu.VMEM((Lc + 8, 2 * HK), F32), pltpu.VMEM((HV, HK), F32), pltpu.VMEM((1, HK), F32),
                        pltpu.VMEM((8, LANE), F32), pltpu.VMEM((Lc, HK), BF16), pltpu.VMEM((HV, Lc), BF16),
                        pltpu.VMEM((Lc, 2 * HK), F32)],
        compiler_params=_params(("parallel", "arbitrary")),
        name="mlstm",
    )(c_qk, c_v, c_o, c_if, cw, cb, tri, bdm)


def _group_rms(x, gavg, gain):
    ms = _dot_l2(x * x, gavg)
    return x * lax.rsqrt(ms + RMS_EPS) * gain


def _rope_lanes(x, cos, sin_signed):
    half = NSA_D // 2
    first = (_iota((1, LANE), 1) & (NSA_D - 1)) < half
    swapped = jnp.where(first, pltpu.roll(x, LANE - half, 1), pltpu.roll(x, half, 1))
    return x * cos + swapped * sin_signed


def _store_vt_tiles(dst_ref, vt, width):
    ones = jnp.ones((VT_ROWS - NSA_D, width), BF16)
    for j in range(vt.shape[1] // width):
        for g in range(NSA_G):
            dst_ref[j, g, 0:NSA_D, :] = vt[g * NSA_D:(g + 1) * NSA_D, j * width:(j + 1) * width]
            dst_ref[j, g, NSA_D:VT_ROWS, :] = ones


def _nsa_prep_body(q_ref, ks_ref, kw_ref, vs_ref, vw_ref, g_ref, cos_ref, sin_ref, gq_ref, gs_ref, gw_ref, gavg_ref,
                   qt_ref, kso_ref, kwo_ref, vsl_ref, vwd_ref, gt_ref):
    cos, sin = cos_ref[...], sin_ref[...]
    gavg = gavg_ref[...]
    for cb in range(NSA_H * NSA_D // LANE):
        sl = slice(cb * LANE, (cb + 1) * LANE)
        qn = _rope_lanes(_group_rms(q_ref[:, sl], gavg, gq_ref[...]), cos, sin) * (NSA_D ** -0.5 * LOG2E)
        qt_ref[sl, :] = qn.T.astype(BF16)
    kso_ref[...] = _rope_lanes(_group_rms(ks_ref[...], gavg, gs_ref[...]), cos, sin).astype(BF16)
    kwo_ref[...] = _rope_lanes(_group_rms(kw_ref[...], gavg, gw_ref[...]), cos, sin).astype(BF16)
    vst = vs_ref[...].astype(F32).T.astype(BF16)
    _store_vt_tiles(vsl_ref, vst, SEL_KC)
    _store_vt_tiles(vwd_ref, vw_ref[...].astype(F32).T.astype(BF16), LANE)
    gt_ref[...] = _sigmoid(g_ref[...]).T


def _nsa_prep(b_q, b_ks, b_kw, b_vs, b_vw, b_g, cos, sin, gq, gs, gw, gavg, B, S, tm=SEL_KC):
    nb = S // tm
    row = lambda w: pl.BlockSpec((tm, w), lambda b, i: (b * nb + i, 0))
    tab = pl.BlockSpec((tm, LANE), lambda b, i: (i, 0))
    HD = NSA_H * NSA_D
    vt_shape = lambda width: jax.ShapeDtypeStruct((B, S // width, NSA_G, VT_ROWS, width), BF16)
    vt_spec = lambda width: pl.BlockSpec((None, tm // width, NSA_G, VT_ROWS, width), lambda b, i: (b, i, 0, 0, 0))
    return pl.pallas_call(
        _nsa_prep_body,
        out_shape=[jax.ShapeDtypeStruct((B, HD, S), BF16),
                   jax.ShapeDtypeStruct((B, S, LANE), BF16),
                   jax.ShapeDtypeStruct((B, S, LANE), BF16),
                   vt_shape(SEL_KC),
                   vt_shape(LANE),
                   jax.ShapeDtypeStruct((B, LANE, S), F32)],
        grid=(B, nb),
        in_specs=[row(HD), row(LANE), row(LANE), row(LANE), row(LANE), row(LANE), tab, tab,
                  _const_spec((1, LANE)), _const_spec((1, LANE)), _const_spec((1, LANE)), _const_spec((LANE, LANE))],
        out_specs=[pl.BlockSpec((None, HD, tm), lambda b, i: (b, 0, i)),
                   pl.BlockSpec((None, tm, LANE), lambda b, i: (b, i, 0)),
                   pl.BlockSpec((None, tm, LANE), lambda b, i: (b, i, 0)),
                   vt_spec(SEL_KC), vt_spec(LANE),
                   pl.BlockSpec((None, LANE, tm), lambda b, i: (b, 0, i))],
        compiler_params=_params(("parallel", "parallel")),
        name="nsa_prep",
    )(b_q, b_ks, b_kw, b_vs, b_vw, b_g, cos, sin, gq, gs, gw, gavg)


def _gelu_tanh(x):
    return 0.5 * x * (1.0 + jnp.tanh(0.7978845608028654 * (x + 0.044715 * x * x * x)))


def _nsa_cmp_body(xk_ref, xv_ref, pe_ref, wt_ref, wb_ref, w2_ref, cos_ref, sin_ref, gk_ref, gavg_ref,
                  kc_ref, vct_ref):
    n = xk_ref.shape[0]

    def compress(x, which):
        u = _dot((x + pe_ref[which, 0:1, :]).astype(BF16), wt_ref[which])
        v = _dot((x + pe_ref[which, 1:2, :]).astype(BF16), wb_ref[which])
        hid = u + pltpu.roll(v, n - 1, 0)
        return _dot(_gelu_tanh(hid).astype(BF16), w2_ref[which])

    ck = compress(xk_ref[...], 0)
    kc_ref[...] = _rope_lanes(_group_rms(ck, gavg_ref[...], gk_ref[...]), cos_ref[...], sin_ref[...]).astype(BF16)
    vct = compress(xv_ref[...], 1).T.astype(BF16)
    for g in range(NSA_G):
        vct_ref[g] = vct[g * NSA_D:(g + 1) * NSA_D, :]


def _nsa_cmp(xk, xv, pe, wt, wb, w2, cos, sin, gk, gavg):
    B, n, W = xk.shape
    return pl.pallas_call(
        _nsa_cmp_body,
        out_shape=[jax.ShapeDtypeStruct((B, n, LANE), BF16),
                   jax.ShapeDtypeStruct((B, NSA_G, NSA_D, n), BF16)],
        grid=(B,),
        in_specs=[pl.BlockSpec((None, n, W), lambda b: (b, 0, 0)), pl.BlockSpec((None, n, W), lambda b: (b, 0, 0)),
                  _const_spec(pe.shape), _const_spec(wt.shape), _const_spec(wb.shape), _const_spec(w2.shape),
                  _const_spec((n, LANE)), _const_spec((n, LANE)), _const_spec((1, LANE)), _const_spec((LANE, LANE))],
        out_specs=[pl.BlockSpec((None, n, LANE), lambda b: (b, 0, 0)),
                   pl.BlockSpec((None, NSA_G, NSA_D, n), lambda b: (b, 0, 0, 0))],
        compiler_params=_params(("parallel",)),
        name="nsa_compress",
    )(xk, xv, pe, wt, wb, w2, cos, sin, gk, gavg)


def _group_queries(qt_ref, g):
    HPG = NSA_H // NSA_G
    q4 = jnp.concatenate([qt_ref[(g * HPG + h) * NSA_D:(g * HPG + h + 1) * NSA_D, :] for h in range(HPG)], axis=1)
    parts = [jnp.zeros(q4.shape, BF16)] * NSA_G
    parts[g] = q4
    return jnp.concatenate(parts, axis=0)


def _bitonic_merge_desc(xs):
    xs = list(xs)
    j = len(xs) // 2
    while j >= 1:
        for a in range(len(xs)):
            b = a ^ j
            if b > a:
                xs[a], xs[b] = jnp.maximum(xs[a], xs[b]), jnp.minimum(xs[a], xs[b])
        j //= 2
    return xs


def _bitonic_sort_desc(xs):
    if len(xs) == 1:
        return list(xs)
    half = len(xs) // 2
    lo = _bitonic_sort_desc(xs[:half])
    hi = _bitonic_sort_desc(xs[half:])
    return _bitonic_merge_desc(lo + hi[::-1])


def _kth_largest(score, k):
    SUBLANES = 8
    tiles = [score[SUBLANES * v:SUBLANES * (v + 1), :] for v in range(score.shape[0] // SUBLANES)]
    size = max(k, 1 << (len(tiles) - 1).bit_length())
    tiles = tiles + [jnp.full(tiles[0].shape, NEG, F32)] * (size - len(tiles))

    def top_of_two(a, b):
        return _bitonic_merge_desc([jnp.maximum(a[r], b[k - 1 - r]) for r in range(k)])

    tops = [_bitonic_sort_desc(tiles[c:c + k]) for c in range(0, size, k)]
    while len(tops) > 1:
        tops = [top_of_two(tops[c], tops[c + 1]) for c in range(0, len(tops), 2)]
    top = tops[0]
    shift = SUBLANES // 2
    while shift >= 1:
        top = top_of_two(top, [pltpu.roll(t, shift, 0) for t in top])
        shift //= 2
    return top[k - 1][0:1, :]


def _nsa_select_dispatch(*refs, sel_tile, n_variants):
    n_cmp = refs[1].shape[0]
    need = (pl.program_id(1) + 1) * LANE // CMP_STRIDE
    variant = (need - 1) * n_variants // n_cmp
    for k in range(n_variants):
        pl.when(variant == k)(functools.partial(_nsa_select_body, *refs, sel_tile=sel_tile,
                                                n_rows=n_cmp * (k + 1) // n_variants))


def _nsa_select_body(qt_ref, kc_ref, vct_ref, kw_ref, vwd_ref, gt_ref, tril_ref, part_ref, sel_ref, selm_ref, ps_buf,
                     *, sel_tile, n_rows):
    TQ = LANE
    HPG = NSA_H // NSA_G
    W = HPG * TQ
    n_cmp = kc_ref.shape[0]
    n_sel = sel_ref.shape[1]
    i = pl.program_id(1)
    s0 = i * TQ
    t_row = s0 + _iota((1, TQ), 1)
    n_win = WIN // LANE + 1
    j0 = jnp.maximum(i - (n_win - 1), 0)
    w_start = pl.multiple_of(j0 * LANE, LANE)
    own_first = (s0 // sel_tile) * (sel_tile // SLC_LEN)
    ps_buf[:, 0:PS_PAD, :] = jnp.zeros((NSA_G, PS_PAD, TQ), F32)

    for g in range(NSA_G):
        qpad = _group_queries(qt_ref, g)

        cend = _iota((n_rows, 1), 0) * CMP_STRIDE + (CMP_LEN - 1)
        bias_c = jnp.where(cend <= t_row, 0.0, NEG)
        sc = _dot(kc_ref[0:n_rows, :], qpad) + jnp.concatenate([bias_c] * HPG, axis=1)
        m = jnp.max(sc, axis=0, keepdims=True)
        m = jnp.where(m > 0.5 * NEG, m, 0.0)
        p = jnp.exp2(sc - m)
        p = p * (1.0 / jnp.maximum(jnp.sum(p, axis=0, keepdims=True), 1e-30))
        o_c = _dot(vct_ref[g, :, 0:n_rows], p.astype(BF16))
        psum = p[:, 0:TQ]
        for h in range(1, HPG):
            psum = psum + p[:, h * TQ:(h + 1) * TQ]
        ps_buf[g, PS_PAD:PS_PAD + n_rows, :] = psum

        per = SLC_LEN // CMP_STRIDE
        n_blk = n_rows // per
        imp = ps_buf[g, pl.ds(PS_PAD - 1, n_blk, stride=per), :]
        for jj in range(per):
            imp = imp + ps_buf[g, pl.ds(PS_PAD + jj, n_blk, stride=per), :]
        blk = _iota((n_blk, 1), 0)
        cur = t_row >> (SLC_LEN.bit_length() - 1)
        valid = blk * SLC_LEN <= t_row
        forced = (blk == 0) | (blk == cur) | (blk == cur - 1)
        score0 = jnp.where(valid, jnp.where(forced, FORCED_SCORE, imp), NEG)
        kth = _kth_largest(score0, min(SLC_TOPK, n_sel))
        above = score0 > kth
        tied = jnp.where(score0 == kth, 1.0, 0.0)
        need = min(SLC_TOPK, n_sel) - jnp.sum(jnp.where(above, 1.0, 0.0), axis=0, keepdims=True)
        tied_all = jnp.concatenate([tied, jnp.zeros((n_sel - n_blk, TQ), F32)], axis=0) if n_blk < n_sel else tied
        rank = _dot(tril_ref[0:n_blk, :], tied_all.astype(BF16))
        chosen = (above | ((tied > 0.5) & (rank <= need))) & valid
        sel_ref[g, 0:n_blk, :] = jnp.where(chosen, 0.0, NEG)
        selm_ref[g, 0:n_blk, :] = jnp.where(chosen & (blk < own_first), 0.0, NEG)
        if n_blk < n_sel:
            sel_ref[g, n_blk:n_sel, :] = jnp.full((n_sel - n_blk, TQ), NEG, F32)
            selm_ref[g, n_blk:n_sel, :] = jnp.full((n_sel - n_blk, TQ), NEG, F32)

        kpos = w_start + _iota((n_win * LANE, 1), 0)
        bias_w = jnp.where((kpos <= t_row) & (kpos > t_row - WIN), 0.0, NEG)
        sw = _dot(kw_ref[pl.ds(w_start, n_win * LANE), :], qpad) + jnp.concatenate([bias_w] * HPG, axis=1)
        pw = jnp.exp2(sw - jnp.max(sw, axis=0, keepdims=True)).astype(BF16)
        acc_w = jnp.zeros((VT_ROWS, W), F32)
        for r in range(n_win):
            acc_w = acc_w + _dot(vwd_ref[j0 + r, g], pw[r * LANE:(r + 1) * LANE, :])
        o_w = acc_w[0:NSA_D, :] * (1.0 / acc_w[NSA_D:NSA_D + 1, :])

        for hp in range(HPG // 2):
            tiles = []
            for h in (2 * hp, 2 * hp + 1):
                gr = (g * HPG + h) * 3
                cs = slice(h * TQ, (h + 1) * TQ)
                tiles.append(gt_ref[gr:gr + 1, :] * o_c[:, cs] + gt_ref[gr + 2:gr + 3, :] * o_w[:, cs])
            col = (g * HPG + 2 * hp) * NSA_D
            part_ref[:, col:col + 2 * NSA_D] = jnp.concatenate(tiles, axis=0).T


def _nsa_select(qt, kc, vct, kw, vwd, gt, B, S):
    HD = NSA_H * NSA_D
    TQ = LANE
    n_cmp = kc.shape[1]
    n_sel = S // SLC_LEN
    assert S >= WIN + TQ and CMP_LEN == 2 * CMP_STRIDE
    whole = lambda shape: pl.BlockSpec((None,) + shape, lambda b, i: (b,) + (0,) * len(shape),
                                       pipeline_mode=pl.Buffered(1))
    mask_shape = jax.ShapeDtypeStruct((B, NSA_G, n_sel, S), F32)
    mask_spec = pl.BlockSpec((None, NSA_G, n_sel, TQ), lambda b, i: (b, 0, 0, i))
    tril = jnp.asarray(np.tril(np.ones((n_sel, n_sel), np.float32)), BF16)
    return pl.pallas_call(
        functools.partial(_nsa_select_dispatch, sel_tile=SEL_TQ, n_variants=CMP_VARIANTS),
        out_shape=[jax.ShapeDtypeStruct((B * S, HD), F32), mask_shape, mask_shape],
        grid=(B, S // TQ),
        in_specs=[pl.BlockSpec((None, HD, TQ), lambda b, i: (b, 0, i)),
                  whole((n_cmp, LANE)), whole(vct.shape[1:]), whole((S, LANE)), whole(vwd.shape[1:]),
                  pl.BlockSpec((None, LANE, TQ), lambda b, i: (b, 0, i)), _const_spec((n_sel, n_sel))],
        out_specs=[pl.BlockSpec((TQ, HD), lambda b, i: (b * (S // TQ) + i, 0)), mask_spec, mask_spec],
        scratch_shapes=[pltpu.VMEM((NSA_G, n_cmp + PS_PAD, TQ), F32)],
        compiler_params=_params(("parallel", "parallel")),
        name="nsa_select",
    )(qt, kc, vct, kw, vwd, gt, tril)


def _nsa_selected_body(qt_ref, ks_ref, ksd_ref, vsl_ref, selm_ref, seld_ref, gt_ref, part_ref, oneh_ref, o_ref,
                       s_a, s_b, p_a, p_b, al_a, al_b, m_ref, acc_ref):
    TQ = qt_ref.shape[1]
    HPG = NSA_H // NSA_G
    W = HPG * TQ
    n_kc = ks_ref.shape[0] // SEL_KC
    bps = SEL_KC // SLC_LEN
    i = pl.program_id(1)
    n_main = i * (TQ // SEL_KC)
    n_pairs = (n_main + 1) // 2
    diag_ok = _iota((TQ, 1), 0) <= _iota((1, TQ), 1)
    pad_rows = jnp.zeros((LANE - 2 * bps, W), BF16)

    def q_aug(qpad, brows):
        brows = jnp.concatenate([jnp.concatenate([brows] * HPG, axis=1), jnp.zeros((bps, W), F32)], axis=0)
        return jnp.concatenate([qpad, brows.astype(BF16), pad_rows], axis=0)

    heads = [slice(h * TQ, (h + 1) * TQ) for h in range(HPG)]

    groups = range(NSA_G)

    def softmax(g, s_buf, p_buf, al_ref, cs):
        sb = s_buf[g, :, cs]
        m_i = m_ref[g, :, cs]
        m_new = jnp.maximum(m_i, jnp.max(sb, axis=0, keepdims=True).astype(F32))
        p_buf[g, :, cs] = jnp.exp2(sb - m_new.astype(BF16))
        al_ref[g, :, cs] = jnp.exp2(m_i - m_new)
        m_ref[g, :, cs] = m_new

    def apply_values(g, p_buf, al_ref, vt, cs):
        acc_ref[g, :, cs] = al_ref[g, :, cs] * acc_ref[g, :, cs] + _dot(vt, p_buf[g, :, cs])

    qpads = [_group_queries(qt_ref, g) for g in groups]
    m_ref[...] = jnp.full(m_ref.shape, NEG, F32)
    acc_ref[...] = jnp.zeros(acc_ref.shape, F32)
    p_b[...] = jnp.zeros(p_b.shape, BF16)
    al_b[...] = jnp.ones(al_b.shape, F32)

    def step_operands(c):
        k0 = pl.multiple_of(c * SEL_KC, SEL_KC)
        rows = pl.ds(pl.multiple_of(c * bps, bps), bps)
        k_aug = jnp.concatenate([ks_ref[pl.ds(k0, SEL_KC), :], oneh_ref[...]], axis=1)
        return k_aug, [q_aug(qpads[g], selm_ref[g, rows, :]) for g in groups]

    def scores(ops, g, s_buf, cs):
        s_buf[g, :, cs] = _dot(ops[0], ops[1][g][:, cs]).astype(BF16)

    def stages(ops, c_prev, s_new, s_cur, p_cur, al_cur, p_prev, al_prev):
        for g in groups:
            vt = vsl_ref[c_prev, g]
            for cs in heads:
                scores(ops, g, s_new, cs)
                softmax(g, s_cur, p_cur, al_cur, cs)
                apply_values(g, p_prev, al_prev, vt, cs)

    def pair(cc, carry):
        c0 = 2 * cc
        stages(step_operands(c0 + 1), jnp.maximum(c0 - 1, 0), s_b, s_a, p_a, al_a, p_b, al_b)
        stages(step_operands(jnp.minimum(c0 + 2, n_kc - 1)), c0, s_a, s_b, p_b, al_b, p_a, al_a)
        return carry

    ops = step_operands(0)
    for g in groups:
        for cs in heads:
            scores(ops, g, s_a, cs)
    lax.fori_loop(0, n_pairs, pair, 0)
    for g in groups:
        vt = vsl_ref[jnp.maximum(2 * n_pairs - 1, 0), g]
        for cs in heads:
            apply_values(g, p_b, al_b, vt, cs)

    for d in range(TQ // SEL_KC):
        k_aug = jnp.concatenate([ksd_ref[d * SEL_KC:(d + 1) * SEL_KC, :], oneh_ref[...]], axis=1)
        for g in groups:
            qa = q_aug(qpads[g], seld_ref[g, d * bps:(d + 1) * bps, :])
            vt = vsl_ref[i * (TQ // SEL_KC) + d, g]
            for cs in heads:
                sd = _dot(k_aug, qa[:, cs])
                s_a[g, :, cs] = jnp.where(diag_ok[d * SEL_KC:(d + 1) * SEL_KC, :], sd, NEG).astype(BF16)
                softmax(g, s_a, p_a, al_a, cs)
                apply_values(g, p_a, al_a, vt, cs)

    for g in groups:
        acc_s = acc_ref[g]
        o_s = acc_s[0:NSA_D, :] * (1.0 / acc_s[NSA_D:NSA_D + 1, :])

        for hp in range(HPG // 2):
            tiles = []
            for h in (2 * hp, 2 * hp + 1):
                gr = (g * HPG + h) * 3 + 1
                tiles.append(gt_ref[gr:gr + 1, :] * o_s[:, h * TQ:(h + 1) * TQ])
            cols = slice((g * HPG + 2 * hp) * NSA_D, (g * HPG + 2 * hp + 2) * NSA_D)
            o_ref[:, cols] = (part_ref[:, cols] + jnp.concatenate(tiles, axis=0).T).astype(o_ref.dtype)


def _nsa_selected(qt, ks, vsl, selm, sel, gt, part, B, S):
    HD = NSA_H * NSA_D
    TQ = SEL_TQ
    W = NSA_H // NSA_G * TQ
    n_sel = S // SLC_LEN
    assert S % (2 * SEL_KC) == 0 and TQ % SEL_KC == 0
    whole = lambda shape: pl.BlockSpec((None,) + shape, lambda b, i: (b,) + (0,) * len(shape),
                                       pipeline_mode=pl.Buffered(1))
    oneh = jnp.asarray(np.arange(SEL_KC)[:, None] // SLC_LEN == np.arange(LANE)[None, :], BF16)
    return pl.pallas_call(
        _nsa_selected_body,
        out_shape=jax.ShapeDtypeStruct((B * S, HD), BF16),
        grid=(B, S // TQ),
        in_specs=[pl.BlockSpec((None, HD, TQ), lambda b, i: (b, 0, i)),
                  whole((S, LANE)), pl.BlockSpec((None, TQ, LANE), lambda b, i: (b, i, 0)),
                  whole(vsl.shape[1:]),
                  pl.BlockSpec((None, NSA_G, n_sel, TQ), lambda b, i: (b, 0, 0, i)),
                  pl.BlockSpec((None, NSA_G, TQ // SLC_LEN, TQ), lambda b, i: (b, 0, i, i)),
                  pl.BlockSpec((None, LANE, TQ), lambda b, i: (b, 0, i)),
                  pl.BlockSpec((TQ, HD), lambda b, i: (b * (S // TQ) + i, 0)),
                  _const_spec((SEL_KC, LANE))],
        out_specs=pl.BlockSpec((TQ, HD), lambda b, i: (b * (S // TQ) + i, 0)),
        scratch_shapes=[pltpu.VMEM((NSA_G, SEL_KC, W), BF16)] * 4 + [pltpu.VMEM((NSA_G, 1, W), F32)] * 3
                       + [pltpu.VMEM((NSA_G, VT_ROWS, W), F32)],
        compiler_params=_params(("parallel", "arbitrary")),
        name="nsa_selected",
    )(qt, ks, ks, vsl, selm, sel, gt, part, oneh)


def _merge_body(x_ref, oa_ref, ob_ref, oc_ref, gates_ref, wbr_ref, wo_ref, o_ref):
    D = x_ref.shape[1]
    y = jnp.zeros(x_ref.shape, F32)
    for j, br in enumerate((oa_ref, ob_ref, oc_ref)):
        y = y + _sigmoid(gates_ref[:, j * D:(j + 1) * D].astype(F32)) * _dot(br[...].astype(BF16), wbr_ref[j])
    o_ref[...] = x_ref[...] + _dot(y.astype(BF16), wo_ref[...])


def _merge(x, o_a, o_b, o_c, gates, wbr, wo, tm=512):
    T, D = x.shape
    row = lambda w: pl.BlockSpec((tm, w), lambda i: (i, 0))
    return pl.pallas_call(
        _merge_body,
        out_shape=jax.ShapeDtypeStruct((T, D), F32),
        grid=(T // tm,),
        in_specs=[row(D), row(MIX_W), row(MIX_W), row(MIX_W), row(3 * D), _const_spec(wbr.shape), _const_spec(wo.shape)],
        out_specs=row(D),
        compiler_params=_params(("parallel",)),
        name="merge_out",
    )(x, o_a, o_b, o_c, gates, wbr, wo)


def _pack_in_proj(w_in, b_in):
    ws, bs = [], []
    for _, off, w, wp in _SEGS:
        ws.append(jnp.pad(w_in[..., off:off + w], ((0, 0), (0, 0), (0, wp - w))))
        bs.append(jnp.pad(b_in[..., off:off + w], ((0, 0), (0, wp - w))))
    return jnp.concatenate(ws, axis=-1).astype(BF16), jnp.concatenate(bs, axis=-1)[:, None, :]


def _rope_tables(pos):
    half = NSA_D // 2
    freqs = ROPE_THETA ** (-jnp.arange(half, dtype=F32) / half)
    ang = pos.astype(F32)[:, None] * freqs[None, :]
    cos, sin = jnp.cos(ang), jnp.sin(ang)
    reps = LANE // NSA_D
    return jnp.tile(jnp.concatenate([cos, cos], axis=1), (1, reps)), jnp.tile(jnp.concatenate([-sin, sin], axis=1), (1, reps))


def _cmp_weights(w1, w2):
    L = w1.shape[0]
    w1r = w1.reshape(L, 2, 2, CMP_STRIDE, NSA_D, CMP_HIDDEN)
    eye_g = jnp.eye(NSA_G, dtype=w1.dtype)
    ex = jnp.einsum('lwstdh,gk->lwstgdkh', w1r, eye_g)
    ex = ex.reshape(L, 2, 2, CMP_STRIDE * NSA_G * NSA_D, NSA_G * CMP_HIDDEN)
    w2x = jnp.einsum('lwhd,gk->lwghkd', w2, eye_g).reshape(L, 2, NSA_G * CMP_HIDDEN, NSA_G * NSA_D)
    return ex[:, :, 0].astype(BF16), ex[:, :, 1].astype(BF16), w2x.astype(BF16)


def _cmp_pos_rows(pe):
    L = pe.shape[0]
    r = pe.reshape(L, 2, 2, CMP_STRIDE, 1, NSA_D)
    return jnp.broadcast_to(r, (L, 2, 2, CMP_STRIDE, NSA_G, NSA_D)).reshape(L, 2, 2, CMP_STRIDE * NSA_G * NSA_D)


def _lane_gain(g):
    return jnp.tile(g, LANE // g.shape[-1])[None, :]


def _layer(x, lw, consts, B, S):
    cos, sin, cos_c, sin_c, gavg = consts
    T = B * S
    x = _ffn(x, lw["ffn1_norm"], lw["ffn1_wg"], lw["ffn1_wu"], lw["ffn1_wd"])
    z = dict(zip([s[0] for s in _SEGS], _proj(x, lw["mix_norm"], lw["w_in"], lw["b_in"])))
    o_a = _gla(z["a_q"], z["a_k"], z["a_v"], z["a_r"], z["a_lr"], lw["gla_wa"], lw["gla_ba"], lw["gla_gn"], B, S)
    o_c = _mlstm(z["c_qk"], z["c_v"], z["c_o"], z["c_if"], lw["conv_w"], lw["conv_b"], B, S)
    qn = lw["nsa_qk_norm"]
    qt, ks, kw, vsl, vwd, gt = _nsa_prep(z["b_q"], z["b_ks"], z["b_kw"], z["b_vs"], z["b_vw"], z["b_g"], cos, sin,
                                         _lane_gain(qn[0]), _lane_gain(qn[2]), _lane_gain(qn[3]), gavg, B, S)
    n_blk = S // CMP_STRIDE
    xk = z["b_kc"].reshape(B, n_blk, CMP_STRIDE * LANE)
    xv = z["b_vc"].reshape(B, n_blk, CMP_STRIDE * LANE)
    kc, vct = _nsa_cmp(xk, xv, lw["cmp_pe"], lw["cmp_wt"], lw["cmp_wb"], lw["cmp_w2"], cos_c, sin_c,
                       _lane_gain(qn[1]), gavg)
    part, sel, selm = _nsa_select(qt, kc, vct, kw, vwd, gt, B, S)
    o_b = _nsa_selected(qt, ks, vsl, selm, sel, gt, part, B, S)
    x = _merge(x, o_a, o_b, o_c, z["gates"], lw["w_branch"], lw["w_out"])
    return _ffn(x, lw["ffn2_norm"], lw["ffn2_wg"], lw["ffn2_wu"], lw["ffn2_wd"])


def kernel(x, ffn1_norm, ffn1_w_gate, ffn1_w_up, ffn1_w_down, mix_norm, w_in, b_in, gla_w_alpha, gla_b_alpha, gla_out_norm, nsa_qk_norm, nsa_cmp_pos, nsa_cmp_w1, nsa_cmp_w2, mlstm_conv_w, mlstm_conv_b, w_branch, w_out, ffn2_norm, ffn2_w_gate, ffn2_w_up, ffn2_w_down):
    B, S, D = x.shape
    w_in_p, b_in_p = _pack_in_proj(w_in, b_in)
    cmp_wt, cmp_wb, cmp_w2 = _cmp_weights(nsa_cmp_w1, nsa_cmp_w2)
    layers = {
        "ffn1_norm": ffn1_norm[:, None, :], "ffn1_wg": ffn1_w_gate.astype(BF16), "ffn1_wu": ffn1_w_up.astype(BF16),
        "ffn1_wd": ffn1_w_down.astype(BF16),
        "mix_norm": mix_norm[:, None, :], "w_in": w_in_p, "b_in": b_in_p,
        "gla_wa": jnp.pad(gla_w_alpha, ((0, 0), (0, LANE - GLA_RANK), (0, 0))), "gla_ba": gla_b_alpha[:, None, :],
        "gla_gn": gla_out_norm[:, None, :],
        "nsa_qk_norm": nsa_qk_norm, "cmp_pe": _cmp_pos_rows(nsa_cmp_pos), "cmp_wt": cmp_wt, "cmp_wb": cmp_wb,
        "cmp_w2": cmp_w2,
        "conv_w": mlstm_conv_w, "conv_b": mlstm_conv_b[:, None, :],
        "w_branch": w_branch.astype(BF16), "w_out": w_out.astype(BF16),
        "ffn2_norm": ffn2_norm[:, None, :], "ffn2_wg": ffn2_w_gate.astype(BF16), "ffn2_wu": ffn2_w_up.astype(BF16),
        "ffn2_wd": ffn2_w_down.astype(BF16),
    }
    n_blk = S // CMP_STRIDE
    cos, sin = _rope_tables(jnp.arange(S))
    cos_c, sin_c = _rope_tables(jnp.arange(n_blk) * CMP_STRIDE + CMP_LEN - 1)
    gavg = jnp.asarray(np.kron(np.eye(LANE // NSA_D), np.full((NSA_D, NSA_D), 1.0 / NSA_D)), BF16)
    consts = (cos, sin, cos_c, sin_c, gavg)

    out = x.reshape(B * S, D)
    for l in range(w_in.shape[0]):
        out = _layer(out, {name: w[l] for name, w in layers.items()}, consts, B, S)
    return out.reshape(B, S, D)
```

```python
import functools

import numpy as np
import jax
import jax.numpy as jnp
from jax import lax
from jax.experimental import pallas as pl
from jax.experimental.pallas import tpu as pltpu

F32 = jnp.float32
BF16 = jnp.bfloat16

RMS_EPS = 1e-6
ROPE_THETA = 10000.0

GLA_H, GLA_DK, GLA_DV, GLA_RANK, GLA_GATE_NORM = 4, 64, 128, 16, 16.0
GLA_SUB = 16
NSA_H, NSA_G, NSA_D = 8, 2, 64
CMP_LEN, CMP_STRIDE, CMP_HIDDEN = 32, 16, 256
SLC_LEN, SLC_TOPK, WIN = 64, 16, 512
FORCED_SCORE = 1e4
MLSTM_H, MLSTM_DK, MLSTM_DV, CONV_W = 4, 64, 128, 4
MLSTM_CHUNK = 128
MIX_W = 512
LANE = 128
MXU_N = 256
NEG = -1e30
LOG2E = 1.4426950408889634
SEL_KC = 512
SEL_TQ = 512
PS_PAD = 8
CMP_VARIANTS = 4
VT_ROWS = 80

_SEGS = (
    ("a_q", 0, 256, 256), ("a_k", 256, 256, 256), ("a_v", 512, 512, 512), ("a_r", 1024, 512, 512),
    ("b_q", 1552, 512, 512), ("c_qk", 2856, 512, 512), ("c_v", 3368, 512, 512), ("c_o", 3880, 512, 512),
    ("gates", 4400, 3072, 3072),
    ("a_lr", 1536, 16, 128), ("b_kc", 2064, 128, 128), ("b_vc", 2192, 128, 128), ("b_ks", 2320, 128, 128),
    ("b_vs", 2448, 128, 128), ("b_kw", 2576, 128, 128), ("b_vw", 2704, 128, 128), ("b_g", 2832, 24, 128),
    ("c_if", 4392, 8, 128),
)
_N_PACK = sum(s[3] for s in _SEGS)
_BF16_SEGS = ("gates", "c_v", "b_vs", "b_vw")


def _dot(a, b):
    return jnp.dot(a, b, preferred_element_type=F32)


def _dot_nt(a, b):
    return lax.dot_general(a, b, (((1,), (1,)), ((), ())), preferred_element_type=F32)


def _split2(a):
    hi = a.astype(BF16)
    lo = (a - hi.astype(F32)).astype(BF16)
    return hi, lo


def _dot_l2(a, b):
    hi, lo = _split2(a)
    return _dot(hi, b) + _dot(lo, b)


def _dot_r2(a, b):
    hi, lo = _split2(b)
    return _dot(a, hi) + _dot(a, lo)


def _log_sigmoid(x):
    return jnp.minimum(x, 0.0) - jnp.log(1.0 + jnp.exp(-jnp.abs(x)))


def _sigmoid(x):
    return 1.0 / (1.0 + jnp.exp(-x))


def _silu(x):
    return x * _sigmoid(x)


def _iota(shape, dim):
    return lax.broadcasted_iota(jnp.int32, shape, dim)


def _const_spec(shape):
    nd = len(shape)
    return pl.BlockSpec(shape, lambda *_: (0,) * nd, pipeline_mode=pl.Buffered(1))


def _params(sem, vmem_mb=56):
    return pltpu.CompilerParams(dimension_semantics=sem, vmem_limit_bytes=vmem_mb * 1024 * 1024)


def _ffn_body(x_ref, g_ref, wg_ref, wu_ref, wd_ref, o_ref, *, bounds):
    x = x_ref[...]
    ms = jnp.mean(x * x, axis=-1, keepdims=True)
    h = (x * lax.rsqrt(ms + RMS_EPS) * g_ref[...]).astype(BF16)
    acc = jnp.zeros(x.shape, F32)
    for lo, hi in zip(bounds[:-1], bounds[1:]):
        a = _dot(h, wg_ref[:, lo:hi])
        u = _dot(h, wu_ref[:, lo:hi])
        t = (_silu(a) * u).astype(BF16)
        acc = acc + _dot(t, wd_ref[lo:hi, :])
    o_ref[...] = x + 0.5 * acc


def _ffn(x, g, wg, wu, wd, tm=512):
    T, D = x.shape
    F = wg.shape[1]
    bounds = (0, -(-F // (2 * MXU_N)) * MXU_N, F)
    return pl.pallas_call(
        functools.partial(_ffn_body, bounds=bounds),
        out_shape=jax.ShapeDtypeStruct((T, D), F32),
        grid=(T // tm,),
        in_specs=[pl.BlockSpec((tm, D), lambda i: (i, 0)), _const_spec((1, D)),
                  _const_spec((D, F)), _const_spec((D, F)), _const_spec((F, D))],
        out_specs=pl.BlockSpec((tm, D), lambda i: (i, 0)),
        compiler_params=_params(("parallel",)),
        name="ffn",
    )(x, g, wg, wu, wd)


def _proj_body(x_ref, g_ref, w_ref, b_ref, *o_refs):
    x = x_ref[...]
    ms = jnp.mean(x * x, axis=-1, keepdims=True)
    h = (x * lax.rsqrt(ms + RMS_EPS) * g_ref[...]).astype(BF16)
    off, i = 0, 0
    while i < len(o_refs):
        group, w = [], 0
        while i < len(o_refs) and (not group or w % MXU_N):
            group.append(o_refs[i])
            w += o_refs[i].shape[1]
            i += 1
        z = _dot(h, w_ref[:, off:off + w]) + b_ref[:, off:off + w]
        c = 0
        for o_ref in group:
            o_ref[...] = z[:, c:c + o_ref.shape[1]].astype(o_ref.dtype)
            c += o_ref.shape[1]
        off += w


def _proj(x, g, w, b, tm=512):
    T, D = x.shape
    return pl.pallas_call(
        _proj_body,
        out_shape=[jax.ShapeDtypeStruct((T, s[3]), BF16 if s[0] in _BF16_SEGS else F32) for s in _SEGS],
        grid=(T // tm,),
        in_specs=[pl.BlockSpec((tm, D), lambda i: (i, 0)), _const_spec((1, D)),
                  _const_spec((D, _N_PACK)), _const_spec((1, _N_PACK))],
        out_specs=[pl.BlockSpec((tm, s[3]), lambda i: (i, 0)) for s in _SEGS],
        compiler_params=_params(("parallel",)),
        name="in_proj",
    )(x, g, w, b)


def _gla_body(q_ref, k_ref, v_ref, r_ref, lr_ref, wa_ref, ba_ref, gn_ref, tri_ref, bones_ref, eh_ref,
              o_ref, st_ref, qs_ref, c_ref, tot_ref, kst_ref, vt4_ref, oi_ref):
    nB, Lc = q_ref.shape[0], q_ref.shape[1]
    n_sub = Lc // GLA_SUB
    HK = GLA_H * GLA_DK

    @pl.when(pl.program_id(0) == 0)
    def _():
        st_ref[...] = jnp.zeros(st_ref.shape, F32)

    lane_h = _iota((1, HK), 1) >> (GLA_DK.bit_length() - 1)
    row_i = _iota((GLA_SUB, 1), 0)
    col_r = _iota((1, GLA_H * Lc), 1) & (Lc - 1)
    wa_hi, wa_lo = _split2(wa_ref[...])

    for b in range(nB):
        lr_hi, lr_lo = _split2(lr_ref[b])
        u = _dot(lr_hi, wa_hi) + _dot(lr_hi, wa_lo) + _dot(lr_lo, wa_hi) + ba_ref[...]
        g = _log_sigmoid(u) * (1.0 / GLA_GATE_NORM)
        c = _dot_r2(tri_ref[...], g)
        tot = _dot_r2(bones_ref[...], g)
        qs_ref[b] = q_ref[b] * (GLA_DK ** -0.5)
        c_ref[b] = c
        tot_ref[b] = tot
        kt = k_ref[b] * jnp.exp(tot - c)
        vt = v_ref[b].T.astype(BF16)
        for h in range(GLA_H):
            kst_ref[b, h * Lc:(h + 1) * Lc, :] = jnp.where(lane_h == h, kt, 0.0).astype(BF16)
            vt4_ref[b, :, h * Lc:(h + 1) * Lc] = vt[h * GLA_DV:(h + 1) * GLA_DV, :]

    def sub(b, s):
        rows = slice(s * GLA_SUB, (s + 1) * GLA_SUB)
        qs = qs_ref[b, rows, :]
        cs = c_ref[b, rows, :]
        st = st_ref[b]
        qd = qs * jnp.exp(cs)
        q4 = jnp.concatenate([jnp.where(lane_h == h, qd, 0.0) for h in range(GLA_H)], axis=0).astype(BF16)
        inter4 = _dot_nt(q4, st.astype(BF16))
        inter = jnp.concatenate([inter4[h * GLA_SUB:(h + 1) * GLA_SUB, :] for h in range(GLA_H)], axis=1)
        xs = []
        for j in range(GLA_SUB):
            r = s * GLA_SUB + j
            x = qs * k_ref[b, r:r + 1, :] * jnp.exp(jnp.minimum(cs - c_ref[b, r:r + 1, :], 0.0))
            xs.append(jnp.where(row_i >= j, x, 0.0))
        r_all = _dot(jnp.concatenate(xs, axis=0).astype(BF16), eh_ref[...])
        intra = jnp.zeros((GLA_SUB, r_all.shape[1]), F32)
        for j in range(GLA_SUB):
            r = s * GLA_SUB + j
            intra = intra + r_all[j * GLA_SUB:(j + 1) * GLA_SUB, :] * v_ref[b, r:r + 1, :]
        oi_ref[b, rows, :] = inter + intra
        dec = jnp.exp(tot_ref[b, s * GLA_SUB:s * GLA_SUB + 1, :])
        in_sub = (col_r >= s * GLA_SUB) & (col_r < (s + 1) * GLA_SUB)
        vtm = jnp.where(in_sub, vt4_ref[b], jnp.zeros((), BF16))
        st_ref[b] = dec * st + _dot(vtm, kst_ref[b])

    for s in range(n_sub):
        for b in range(nB):
            sub(b, s)

    gn = gn_ref[...]
    for b in range(nB):
        for h in range(GLA_H):
            sl = slice(h * GLA_DV, (h + 1) * GLA_DV)
            o = oi_ref[b, :, sl]
            ms = jnp.mean(o * o, axis=-1, keepdims=True)
            o_ref[b, :, sl] = (o * lax.rsqrt(ms + RMS_EPS) * gn * _silu(r_ref[b, :, sl])).astype(o_ref.dtype)


def _gla_consts(Lc):
    r = np.arange(Lc)
    same = (r[:, None] // GLA_SUB) == (r[None, :] // GLA_SUB)
    tri = (same & (r[None, :] <= r[:, None])).astype(np.float32)
    bones = same.astype(np.float32)
    hk = np.arange(GLA_H * GLA_DK) // GLA_DK
    hv = np.arange(GLA_H * GLA_DV) // GLA_DV
    eh = (hk[:, None] == hv[None, :]).astype(np.float32)
    return jnp.asarray(tri, BF16), jnp.asarray(bones, BF16), jnp.asarray(eh, BF16)


def _gla(a_q, a_k, a_v, a_r, a_lr, wa, ba, gn, B, S, Lc=128):
    HK, HV = GLA_H * GLA_DK, GLA_H * GLA_DV
    nb = S // Lc
    assert Lc & (Lc - 1) == 0
    tri, bones, eh = _gla_consts(Lc)
    row = lambda w: pl.BlockSpec((B, Lc, w), lambda i: (0, i, 0))
    seq = lambda t: t.reshape(B, S, t.shape[-1])
    out = pl.pallas_call(
        _gla_body,
        out_shape=jax.ShapeDtypeStruct((B, S, HV), BF16),
        grid=(nb,),
        in_specs=[row(HK), row(HK), row(HV), row(HV), row(LANE),
                  _const_spec((LANE, HK)), _const_spec((1, HK)), _const_spec((1, GLA_DV)),
                  _const_spec((Lc, Lc)), _const_spec((Lc, Lc)), _const_spec((HK, HV))],
        out_specs=row(HV),
        scratch_shapes=[pltpu.VMEM((B, GLA_DV, HK), F32), pltpu.VMEM((B, Lc, HK), F32), pltpu.VMEM((B, Lc, HK), F32),
                        pltpu.VMEM((B, Lc, HK), F32), pltpu.VMEM((B, GLA_H * Lc, HK), BF16),
                        pltpu.VMEM((B, GLA_DV, GLA_H * Lc), BF16), pltpu.VMEM((B, Lc, HV), F32)],
        compiler_params=_params(("arbitrary",)),
        name="gla",
    )(seq(a_q), seq(a_k), seq(a_v), seq(a_r), seq(a_lr), wa, ba, gn, tri, bones, eh)
    return out.reshape(B * S, HV)


def _mlstm_body(qk_ref, v_ref, og_ref, if_ref, cw_ref, cb_ref, tri_ref, bdm_ref,
                o_ref, xx_ref, ct_ref, n_ref, m_ref, kw_ref, vt_ref, act_ref):
    Lc = v_ref.shape[0]
    L = MLSTM_CHUNK
    HK = MLSTM_H * MLSTM_DK
    tail = 8

    @pl.when(pl.program_id(1) == 0)
    def _():
        ct_ref[...] = jnp.zeros(ct_ref.shape, F32)
        n_ref[...] = jnp.zeros(n_ref.shape, F32)
        m_ref[...] = jnp.zeros(m_ref.shape, F32)
        xx_ref[0:tail, :] = jnp.zeros((tail, xx_ref.shape[1]), F32)

    xx_ref[tail:tail + Lc, :] = qk_ref[...]
    y = jnp.zeros((Lc, 2 * HK), F32) + cb_ref[...]
    for kk in range(CONV_W):
        y = y + cw_ref[kk:kk + 1, :] * xx_ref[pl.ds(tail - (CONV_W - 1) + kk, Lc), :]
    xx_ref[0:tail, :] = xx_ref[Lc:Lc + tail, :]
    act_ref[...] = _silu(y)

    gates = if_ref[...]
    logf = _log_sigmoid(gates)
    bcum = _dot_r2(tri_ref[...], logf)
    b_t = bcum.T
    q_t = act_ref[:, 0:HK].T
    qt_hi, qt_lo = _split2(q_t)
    kw_ref[...] = jnp.zeros(kw_ref.shape, BF16)
    vt_ref[...] = v_ref[...].astype(F32).T.astype(BF16)
    lane_hk = _iota((1, HK), 1) >> (MLSTM_DK.bit_length() - 1)
    col_l = _iota((1, Lc), 1)
    lane_w = _iota((1, LANE), 1)
    row_l = _iota((L, 1), 0)
    per_win = LANE // L

    for ci in range(Lc // L):
        p, c = divmod(ci, per_win)
        rows = slice(ci * L, (ci + 1) * L)
        win = slice(p * LANE, (p + 1) * LANE)
        causal_t = ((c * L + row_l) <= lane_w) & ((lane_w >> (L.bit_length() - 1)) == c)
        k_all = act_ref[rows, HK:2 * HK] * (MLSTM_DK ** -0.5)
        k_bf = k_all.astype(BF16)
        ct = ct_ref[...]
        ct_bf = ct.astype(BF16)
        n_row = n_ref[...]
        n_hi, n_lo = _split2(n_row)
        n8 = jnp.concatenate([jnp.where(lane_hk == h, part, jnp.zeros((), BF16))
                              for part in (n_hi, n_lo) for h in range(MLSTM_H)], axis=0)
        qn8 = _dot(n8, qt_hi[:, win]) + _dot(n8, qt_lo[:, win])
        wk_full = jnp.zeros((L, HK), F32)
        dec_row = jnp.zeros((1, HK), F32)
        for h in range(MLSTM_H):
            icol = gates[rows, h:h + 1]
            bcol = bcum[rows, MLSTM_H + h:MLSTM_H + h + 1]
            brow = b_t[MLSTM_H + h:MLSTM_H + h + 1, win]
            m_st = m_ref[0:1, h:h + 1]
            dmat_t = jnp.where(causal_t, brow + (icol - bcol), -jnp.inf)
            inter_log = brow + m_st
            m_row = jnp.maximum(inter_log, jnp.max(dmat_t, axis=0, keepdims=True))
            w_inter = jnp.exp(inter_log - m_row)
            head_rows = slice(h * MLSTM_DK, (h + 1) * MLSTM_DK)
            zero_q = jnp.zeros((MLSTM_DK, LANE), BF16)
            qh = jnp.concatenate([qt_hi[head_rows, win] if hh == h else zero_q for hh in range(MLSTM_H)], axis=0)
            s_qk = _dot(k_bf, qh) * jnp.exp(dmat_t - m_row)
            den = w_inter * (qn8[h:h + 1, :] + qn8[MLSTM_H + h:MLSTM_H + h + 1, :]) + jnp.sum(s_qk, axis=0, keepdims=True)
            s_bf = s_qk.astype(BF16)
            pads = [jnp.zeros((L, LANE), BF16)] * per_win
            pads[c] = s_bf
            dv = slice(h * MLSTM_DV, (h + 1) * MLSTM_DV)
            num_t = w_inter * _dot(ct_bf[dv, :], qt_hi[:, win]) + _dot(vt_ref[dv, win], jnp.concatenate(pads, axis=0))
            h_t = num_t / jnp.maximum(jnp.abs(den), jnp.exp(-m_row))
            o_ref[rows, dv] = (h_t.T[c * L:(c + 1) * L, :] * _sigmoid(og_ref[rows, dv])).astype(o_ref.dtype)
            m_new = m_row[:, c * L + L - 1:c * L + L]
            b_last = bcol[L - 1:L, :]
            w_k = jnp.exp(b_last - bcol + icol - m_new)
            decay = jnp.exp(b_last + m_st - m_new)
            head = lane_hk == h
            wk_full = wk_full + jnp.where(head, w_k, 0.0)
            dec_row = dec_row + jnp.where(head, decay, 0.0)
            m_ref[0:1, h:h + 1] = m_new
        kw = k_all * wk_full
        kw_ref[rows, :] = kw.astype(BF16)
        in_chunk = (col_l >= ci * L) & (col_l < (ci + 1) * L)
        vtm = jnp.where(in_chunk, vt_ref[...], jnp.zeros((), BF16))
        ct_ref[...] = dec_row * ct + _dot(vtm, kw_ref[...]) * bdm_ref[...]
        n_ref[...] = dec_row * n_row + jnp.sum(kw, axis=0, keepdims=True)


def _mlstm_consts(Lc):
    r = np.arange(Lc)
    same = (r[:, None] // MLSTM_CHUNK) == (r[None, :] // MLSTM_CHUNK)
    tri = (same & (r[None, :] <= r[:, None])).astype(np.float32)
    hk = np.arange(MLSTM_H * MLSTM_DK) // MLSTM_DK
    hv = np.arange(MLSTM_H * MLSTM_DV) // MLSTM_DV
    eh = (hk[:, None] == hv[None, :]).astype(np.float32)
    return jnp.asarray(tri, BF16), jnp.asarray(eh.T.copy(), F32)


def _mlstm(c_qk, c_v, c_o, c_if, cw, cb, B, S, Lc=256):
    HK, HV = MLSTM_H * MLSTM_DK, MLSTM_H * MLSTM_DV
    nb = S // Lc
    assert Lc % LANE == 0 and LANE % MLSTM_CHUNK == 0
    tri, bdm = _mlstm_consts(Lc)
    row = lambda w: pl.BlockSpec((Lc, w), lambda b, i: (b * nb + i, 0))
    return pl.pallas_call(
        _mlstm_body,
        out_shape=jax.ShapeDtypeStruct((B * S, HV), BF16),
        grid=(B, nb),
        in_specs=[row(2 * HK), row(HV), row(HV), row(LANE),
                  _const_spec((CONV_W, 2 * HK)), _const_spec((1, 2 * HK)),
                  _const_spec((Lc, Lc)), _const_spec((HV, HK))],
        out_specs=row(HV),
        scratch_shapes=[pltpu.VMEM((Lc + 8, 2 * HK), F32), pltpu.VMEM((HV, HK), F32), pltpu.VMEM((1, HK), F32),
                        pltpu.VMEM((8, LANE), F32), pltpu.VMEM((Lc, HK), BF16), pltpu.VMEM((HV, Lc), BF16),
                        pltpu.VMEM((Lc, 2 * HK), F32)],
        compiler_params=_params(("parallel", "arbitrary")),
        name="mlstm",
    )(c_qk, c_v, c_o, c_if, cw, cb, tri, bdm)


def _group_rms(x, gavg, gain):
    ms = _dot_l2(x * x, gavg)
    return x * lax.rsqrt(ms + RMS_EPS) * gain


def _rope_lanes(x, cos, sin_signed):
    half = NSA_D // 2
    first = (_iota((1, LANE), 1) & (NSA_D - 1)) < half
    swapped = jnp.where(first, pltpu.roll(x, LANE - half, 1), pltpu.roll(x, half, 1))
    return x * cos + swapped * sin_signed


def _store_vt_tiles(dst_ref, vt, width):
    ones = jnp.ones((VT_ROWS - NSA_D, width), BF16)
    for j in range(vt.shape[1] // width):
        for g in range(NSA_G):
            dst_ref[j, g, 0:NSA_D, :] = vt[g * NSA_D:(g + 1) * NSA_D, j * width:(j + 1) * width]
            dst_ref[j, g, NSA_D:VT_ROWS, :] = ones


def _nsa_prep_body(q_ref, ks_ref, kw_ref, vs_ref, vw_ref, g_ref, cos_ref, sin_ref, gq_ref, gs_ref, gw_ref, gavg_ref,
                   qt_ref, kso_ref, kwo_ref, vsl_ref, vwd_ref, gt_ref):
    cos, sin = cos_ref[...], sin_ref[...]
    gavg = gavg_ref[...]
    for cb in range(NSA_H * NSA_D // LANE):
        sl = slice(cb * LANE, (cb + 1) * LANE)
        qn = _rope_lanes(_group_rms(q_ref[:, sl], gavg, gq_ref[...]), cos, sin) * (NSA_D ** -0.5 * LOG2E)
        qt_ref[sl, :] = qn.T.astype(BF16)
    kso_ref[...] = _rope_lanes(_group_rms(ks_ref[...], gavg, gs_ref[...]), cos, sin).astype(BF16)
    kwo_ref[...] = _rope_lanes(_group_rms(kw_ref[...], gavg, gw_ref[...]), cos, sin).astype(BF16)
    vst = vs_ref[...].astype(F32).T.astype(BF16)
    _store_vt_tiles(vsl_ref, vst, SEL_KC)
    _store_vt_tiles(vwd_ref, vw_ref[...].astype(F32).T.astype(BF16), LANE)
    gt_ref[...] = _sigmoid(g_ref[...]).T


def _nsa_prep(b_q, b_ks, b_kw, b_vs, b_vw, b_g, cos, sin, gq, gs, gw, gavg, B, S, tm=SEL_KC):
    nb = S // tm
    row = lambda w: pl.BlockSpec((tm, w), lambda b, i: (b * nb + i, 0))
    tab = pl.BlockSpec((tm, LANE), lambda b, i: (i, 0))
    HD = NSA_H * NSA_D
    vt_shape = lambda width: jax.ShapeDtypeStruct((B, S // width, NSA_G, VT_ROWS, width), BF16)
    vt_spec = lambda width: pl.BlockSpec((None, tm // width, NSA_G, VT_ROWS, width), lambda b, i: (b, i, 0, 0, 0))
    return pl.pallas_call(
        _nsa_prep_body,
        out_shape=[jax.ShapeDtypeStruct((B, HD, S), BF16),
                   jax.ShapeDtypeStruct((B, S, LANE), BF16),
                   jax.ShapeDtypeStruct((B, S, LANE), BF16),
                   vt_shape(SEL_KC),
                   vt_shape(LANE),
                   jax.ShapeDtypeStruct((B, LANE, S), F32)],
        grid=(B, nb),
        in_specs=[row(HD), row(LANE), row(LANE), row(LANE), row(LANE), row(LANE), tab, tab,
                  _const_spec((1, LANE)), _const_spec((1, LANE)), _const_spec((1, LANE)), _const_spec((LANE, LANE))],
        out_specs=[pl.BlockSpec((None, HD, tm), lambda b, i: (b, 0, i)),
                   pl.BlockSpec((None, tm, LANE), lambda b, i: (b, i, 0)),
                   pl.BlockSpec((None, tm, LANE), lambda b, i: (b, i, 0)),
                   vt_spec(SEL_KC), vt_spec(LANE),
                   pl.BlockSpec((None, LANE, tm), lambda b, i: (b, 0, i))],
        compiler_params=_params(("parallel", "parallel")),
        name="nsa_prep",
    )(b_q, b_ks, b_kw, b_vs, b_vw, b_g, cos, sin, gq, gs, gw, gavg)


def _gelu_tanh(x):
    return 0.5 * x * (1.0 + jnp.tanh(0.7978845608028654 * (x + 0.044715 * x * x * x)))


def _nsa_cmp_body(xk_ref, xv_ref, pe_ref, wt_ref, wb_ref, w2_ref, cos_ref, sin_ref, gk_ref, gavg_ref,
                  kc_ref, vct_ref):
    n = xk_ref.shape[0]

    def compress(x, which):
        u = _dot((x + pe_ref[which, 0:1, :]).astype(BF16), wt_ref[which])
        v = _dot((x + pe_ref[which, 1:2, :]).astype(BF16), wb_ref[which])
        hid = u + pltpu.roll(v, n - 1, 0)
        return _dot(_gelu_tanh(hid).astype(BF16), w2_ref[which])

    ck = compress(xk_ref[...], 0)
    kc_ref[...] = _rope_lanes(_group_rms(ck, gavg_ref[...], gk_ref[...]), cos_ref[...], sin_ref[...]).astype(BF16)
    vct = compress(xv_ref[...], 1).T.astype(BF16)
    for g in range(NSA_G):
        vct_ref[g] = vct[g * NSA_D:(g + 1) * NSA_D, :]


def _nsa_cmp(xk, xv, pe, wt, wb, w2, cos, sin, gk, gavg):
    B, n, W = xk.shape
    return pl.pallas_call(
        _nsa_cmp_body,
        out_shape=[jax.ShapeDtypeStruct((B, n, LANE), BF16),
                   jax.ShapeDtypeStruct((B, NSA_G, NSA_D, n), BF16)],
        grid=(B,),
        in_specs=[pl.BlockSpec((None, n, W), lambda b: (b, 0, 0)), pl.BlockSpec((None, n, W), lambda b: (b, 0, 0)),
                  _const_spec(pe.shape), _const_spec(wt.shape), _const_spec(wb.shape), _const_spec(w2.shape),
                  _const_spec((n, LANE)), _const_spec((n, LANE)), _const_spec((1, LANE)), _const_spec((LANE, LANE))],
        out_specs=[pl.BlockSpec((None, n, LANE), lambda b: (b, 0, 0)),
                   pl.BlockSpec((None, NSA_G, NSA_D, n), lambda b: (b, 0, 0, 0))],
        compiler_params=_params(("parallel",)),
        name="nsa_compress",
    )(xk, xv, pe, wt, wb, w2, cos, sin, gk, gavg)


def _group_queries(qt_ref, g):
    HPG = NSA_H // NSA_G
    q4 = jnp.concatenate([qt_ref[(g * HPG + h) * NSA_D:(g * HPG + h + 1) * NSA_D, :] for h in range(HPG)], axis=1)
    parts = [jnp.zeros(q4.shape, BF16)] * NSA_G
    parts[g] = q4
    return jnp.concatenate(parts, axis=0)


def _bitonic_merge_desc(xs):
    xs = list(xs)
    j = len(xs) // 2
    while j >= 1:
        for a in range(len(xs)):
            b = a ^ j
            if b > a:
                xs[a], xs[b] = jnp.maximum(xs[a], xs[b]), jnp.minimum(xs[a], xs[b])
        j //= 2
    return xs


def _bitonic_sort_desc(xs):
    if len(xs) == 1:
        return list(xs)
    half = len(xs) // 2
    lo = _bitonic_sort_desc(xs[:half])
    hi = _bitonic_sort_desc(xs[half:])
    return _bitonic_merge_desc(lo + hi[::-1])


def _kth_largest(score, k):
    SUBLANES = 8
    tiles = [score[SUBLANES * v:SUBLANES * (v + 1), :] for v in range(score.shape[0] // SUBLANES)]
    size = max(k, 1 << (len(tiles) - 1).bit_length())
    tiles = tiles + [jnp.full(tiles[0].shape, NEG, F32)] * (size - len(tiles))

    def top_of_two(a, b):
        return _bitonic_merge_desc([jnp.maximum(a[r], b[k - 1 - r]) for r in range(k)])

    tops = [_bitonic_sort_desc(tiles[c:c + k]) for c in range(0, size, k)]
    while len(tops) > 1:
        tops = [top_of_two(tops[c], tops[c + 1]) for c in range(0, len(tops), 2)]
    top = tops[0]
    shift = SUBLANES // 2
    while shift >= 1:
        top = top_of_two(top, [pltpu.roll(t, shift, 0) for t in top])
        shift //= 2
    return top[k - 1][0:1, :]


def _nsa_select_dispatch(*refs, sel_tile, n_variants):
    n_cmp = refs[1].shape[0]
    need = (pl.program_id(1) + 1) * LANE // CMP_STRIDE
    variant = (need - 1) * n_variants // n_cmp
    for k in range(n_variants):
        pl.when(variant == k)(functools.partial(_nsa_select_body, *refs, sel_tile=sel_tile,
                                                n_rows=n_cmp * (k + 1) // n_variants))


def _nsa_select_body(qt_ref, kc_ref, vct_ref, kw_ref, vwd_ref, gt_ref, tril_ref, part_ref, sel_ref, selm_ref, ps_buf,
                     *, sel_tile, n_rows):
    TQ = LANE
    HPG = NSA_H // NSA_G
    W = HPG * TQ
    n_cmp = kc_ref.shape[0]
    n_sel = sel_ref.shape[1]
    i = pl.program_id(1)
    s0 = i * TQ
    t_row = s0 + _iota((1, TQ), 1)
    n_win = WIN // LANE + 1
    j0 = jnp.maximum(i - (n_win - 1), 0)
    w_start = pl.multiple_of(j0 * LANE, LANE)
    own_first = (s0 // sel_tile) * (sel_tile // SLC_LEN)
    ps_buf[:, 0:PS_PAD, :] = jnp.zeros((NSA_G, PS_PAD, TQ), F32)

    for g in range(NSA_G):
        qpad = _group_queries(qt_ref, g)

        cend = _iota((n_rows, 1), 0) * CMP_STRIDE + (CMP_LEN - 1)
        bias_c = jnp.where(cend <= t_row, 0.0, NEG)
        sc = _dot(kc_ref[0:n_rows, :], qpad) + jnp.concatenate([bias_c] * HPG, axis=1)
        m = jnp.max(sc, axis=0, keepdims=True)
        m = jnp.where(m > 0.5 * NEG, m, 0.0)
        p = jnp.exp2(sc - m)
        p = p * (1.0 / jnp.maximum(jnp.sum(p, axis=0, keepdims=True), 1e-30))
        o_c = _dot(vct_ref[g, :, 0:n_rows], p.astype(BF16))
        psum = p[:, 0:TQ]
        for h in range(1, HPG):
            psum = psum + p[:, h * TQ:(h + 1) * TQ]
        ps_buf[g, PS_PAD:PS_PAD + n_rows, :] = psum

        per = SLC_LEN // CMP_STRIDE
        n_blk = n_rows // per
        imp = ps_buf[g, pl.ds(PS_PAD - 1, n_blk, stride=per), :]
        for jj in range(per):
            imp = imp + ps_buf[g, pl.ds(PS_PAD + jj, n_blk, stride=per), :]
        blk = _iota((n_blk, 1), 0)
        cur = t_row >> (SLC_LEN.bit_length() - 1)
        valid = blk * SLC_LEN <= t_row
        forced = (blk == 0) | (blk == cur) | (blk == cur - 1)
        score0 = jnp.where(valid, jnp.where(forced, FORCED_SCORE, imp), NEG)
        kth = _kth_largest(score0, min(SLC_TOPK, n_sel))
        above = score0 > kth
        tied = jnp.where(score0 == kth, 1.0, 0.0)
        need = min(SLC_TOPK, n_sel) - jnp.sum(jnp.where(above, 1.0, 0.0), axis=0, keepdims=True)
        tied_all = jnp.concatenate([tied, jnp.zeros((n_sel - n_blk, TQ), F32)], axis=0) if n_blk < n_sel else tied
        rank = _dot(tril_ref[0:n_blk, :], tied_all.astype(BF16))
        chosen = (above | ((tied > 0.5) & (rank <= need))) & valid
        sel_ref[g, 0:n_blk, :] = jnp.where(chosen, 0.0, NEG)
        selm_ref[g, 0:n_blk, :] = jnp.where(chosen & (blk < own_first), 0.0, NEG)
        if n_blk < n_sel:
            sel_ref[g, n_blk:n_sel, :] = jnp.full((n_sel - n_blk, TQ), NEG, F32)
            selm_ref[g, n_blk:n_sel, :] = jnp.full((n_sel - n_blk, TQ), NEG, F32)

        kpos = w_start + _iota((n_win * LANE, 1), 0)
        bias_w = jnp.where((kpos <= t_row) & (kpos > t_row - WIN), 0.0, NEG)
        sw = _dot(kw_ref[pl.ds(w_start, n_win * LANE), :], qpad) + jnp.concatenate([bias_w] * HPG, axis=1)
        pw = jnp.exp2(sw - jnp.max(sw, axis=0, keepdims=True)).astype(BF16)
        acc_w = jnp.zeros((VT_ROWS, W), F32)
        for r in range(n_win):
            acc_w = acc_w + _dot(vwd_ref[j0 + r, g], pw[r * LANE:(r + 1) * LANE, :])
        o_w = acc_w[0:NSA_D, :] * (1.0 / acc_w[NSA_D:NSA_D + 1, :])

        for hp in range(HPG // 2):
            tiles = []
            for h in (2 * hp, 2 * hp + 1):
                gr = (g * HPG + h) * 3
                cs = slice(h * TQ, (h + 1) * TQ)
                tiles.append(gt_ref[gr:gr + 1, :] * o_c[:, cs] + gt_ref[gr + 2:gr + 3, :] * o_w[:, cs])
            col = (g * HPG + 2 * hp) * NSA_D
            part_ref[:, col:col + 2 * NSA_D] = jnp.concatenate(tiles, axis=0).T


def _nsa_select(qt, kc, vct, kw, vwd, gt, B, S):
    HD = NSA_H * NSA_D
    TQ = LANE
    n_cmp = kc.shape[1]
    n_sel = S // SLC_LEN
    assert S >= WIN + TQ and CMP_LEN == 2 * CMP_STRIDE
    whole = lambda shape: pl.BlockSpec((None,) + shape, lambda b, i: (b,) + (0,) * len(shape),
                                       pipeline_mode=pl.Buffered(1))
    mask_shape = jax.ShapeDtypeStruct((B, NSA_G, n_sel, S), F32)
    mask_spec = pl.BlockSpec((None, NSA_G, n_sel, TQ), lambda b, i: (b, 0, 0, i))
    tril = jnp.asarray(np.tril(np.ones((n_sel, n_sel), np.float32)), BF16)
    return pl.pallas_call(
        functools.partial(_nsa_select_dispatch, sel_tile=SEL_TQ, n_variants=CMP_VARIANTS),
        out_shape=[jax.ShapeDtypeStruct((B * S, HD), F32), mask_shape, mask_shape],
        grid=(B, S // TQ),
        in_specs=[pl.BlockSpec((None, HD, TQ), lambda b, i: (b, 0, i)),
                  whole((n_cmp, LANE)), whole(vct.shape[1:]), whole((S, LANE)), whole(vwd.shape[1:]),
                  pl.BlockSpec((None, LANE, TQ), lambda b, i: (b, 0, i)), _const_spec((n_sel, n_sel))],
        out_specs=[pl.BlockSpec((TQ, HD), lambda b, i: (b * (S // TQ) + i, 0)), mask_spec, mask_spec],
        scratch_shapes=[pltpu.VMEM((NSA_G, n_cmp + PS_PAD, TQ), F32)],
        compiler_params=_params(("parallel", "parallel")),
        name="nsa_select",
    )(qt, kc, vct, kw, vwd, gt, tril)


def _nsa_selected_body(qt_ref, ks_ref, ksd_ref, vsl_ref, selm_ref, seld_ref, gt_ref, part_ref, oneh_ref, o_ref,
                       s_a, s_b, p_a, p_b, al_a, al_b, m_ref, acc_ref):
    TQ = qt_ref.shape[1]
    HPG = NSA_H // NSA_G
    W = HPG * TQ
    n_kc = ks_ref.shape[0] // SEL_KC
    bps = SEL_KC // SLC_LEN
    i = pl.program_id(1)
    n_main = i * (TQ // SEL_KC)
    n_pairs = (n_main + 1) // 2
    diag_ok = _iota((TQ, 1), 0) <= _iota((1, TQ), 1)
    pad_rows = jnp.zeros((LANE - 2 * bps, W), BF16)

    def q_aug(qpad, brows):
        brows = jnp.concatenate([jnp.concatenate([brows] * HPG, axis=1), jnp.zeros((bps, W), F32)], axis=0)
        return jnp.concatenate([qpad, brows.astype(BF16), pad_rows], axis=0)

    heads = [slice(h * TQ, (h + 1) * TQ) for h in range(HPG)]

    groups = range(NSA_G)

    def softmax(g, s_buf, p_buf, al_ref, cs):
        sb = s_buf[g, :, cs]
        m_i = m_ref[g, :, cs]
        m_new = jnp.maximum(m_i, jnp.max(sb, axis=0, keepdims=True).astype(F32))
        p_buf[g, :, cs] = jnp.exp2(sb - m_new.astype(BF16))
        al_ref[g, :, cs] = jnp.exp2(m_i - m_new)
        m_ref[g, :, cs] = m_new

    def apply_values(g, p_buf, al_ref, vt, cs):
        acc_ref[g, :, cs] = al_ref[g, :, cs] * acc_ref[g, :, cs] + _dot(vt, p_buf[g, :, cs])

    qpads = [_group_queries(qt_ref, g) for g in groups]
    m_ref[...] = jnp.full(m_ref.shape, NEG, F32)
    acc_ref[...] = jnp.zeros(acc_ref.shape, F32)
    p_b[...] = jnp.zeros(p_b.shape, BF16)
    al_b[...] = jnp.ones(al_b.shape, F32)

    def step_operands(c):
        k0 = pl.multiple_of(c * SEL_KC, SEL_KC)
        rows = pl.ds(pl.multiple_of(c * bps, bps), bps)
        k_aug = jnp.concatenate([ks_ref[pl.ds(k0, SEL_KC), :], oneh_ref[...]], axis=1)
        return k_aug, [q_aug(qpads[g], selm_ref[g, rows, :]) for g in groups]

    def scores(ops, g, s_buf, cs):
        s_buf[g, :, cs] = _dot(ops[0], ops[1][g][:, cs]).astype(BF16)

    def stages(ops, c_prev, s_new, s_cur, p_cur, al_cur, p_prev, al_prev):
        for g in groups:
            vt = vsl_ref[c_prev, g]
            for cs in heads:
                scores(ops, g, s_new, cs)
                softmax(g, s_cur, p_cur, al_cur, cs)
                apply_values(g, p_prev, al_prev, vt, cs)

    def pair(cc, carry):
        c0 = 2 * cc
        stages(step_operands(c0 + 1), jnp.maximum(c0 - 1, 0), s_b, s_a, p_a, al_a, p_b, al_b)
        stages(step_operands(jnp.minimum(c0 + 2, n_kc - 1)), c0, s_a, s_b, p_b, al_b, p_a, al_a)
        return carry

    ops = step_operands(0)
    for g in groups:
        for cs in heads:
            scores(ops, g, s_a, cs)
    lax.fori_loop(0, n_pairs, pair, 0)
    for g in groups:
        vt = vsl_ref[jnp.maximum(2 * n_pairs - 1, 0), g]
        for cs in heads:
            apply_values(g, p_b, al_b, vt, cs)

    for d in range(TQ // SEL_KC):
        k_aug = jnp.concatenate([ksd_ref[d * SEL_KC:(d + 1) * SEL_KC, :], oneh_ref[...]], axis=1)
        for g in groups:
            qa = q_aug(qpads[g], seld_ref[g, d * bps:(d + 1) * bps, :])
            vt = vsl_ref[i * (TQ // SEL_KC) + d, g]
            for cs in heads:
                sd = _dot(k_aug, qa[:, cs])
                s_a[g, :, cs] = jnp.where(diag_ok[d * SEL_KC:(d + 1) * SEL_KC, :], sd, NEG).astype(BF16)
                softmax(g, s_a, p_a, al_a, cs)
                apply_values(g, p_a, al_a, vt, cs)

    for g in groups:
        acc_s = acc_ref[g]
        o_s = acc_s[0:NSA_D, :] * (1.0 / acc_s[NSA_D:NSA_D + 1, :])

        for hp in range(HPG // 2):
            tiles = []
            for h in (2 * hp, 2 * hp + 1):
                gr = (g * HPG + h) * 3 + 1
                tiles.append(gt_ref[gr:gr + 1, :] * o_s[:, h * TQ:(h + 1) * TQ])
            cols = slice((g * HPG + 2 * hp) * NSA_D, (g * HPG + 2 * hp + 2) * NSA_D)
            o_ref[:, cols] = (part_ref[:, cols] + jnp.concatenate(tiles, axis=0).T).astype(o_ref.dtype)


def _nsa_selected(qt, ks, vsl, selm, sel, gt, part, B, S):
    HD = NSA_H * NSA_D
    TQ = SEL_TQ
    W = NSA_H // NSA_G * TQ
    n_sel = S // SLC_LEN
    assert S % (2 * SEL_KC) == 0 and TQ % SEL_KC == 0
    whole = lambda shape: pl.BlockSpec((None,) + shape, lambda b, i: (b,) + (0,) * len(shape),
                                       pipeline_mode=pl.Buffered(1))
    oneh = jnp.asarray(np.arange(SEL_KC)[:, None] // SLC_LEN == np.arange(LANE)[None, :], BF16)
    return pl.pallas_call(
        _nsa_selected_body,
        out_shape=jax.ShapeDtypeStruct((B * S, HD), BF16),
        grid=(B, S // TQ),
        in_specs=[pl.BlockSpec((None, HD, TQ), lambda b, i: (b, 0, i)),
                  whole((S, LANE)), pl.BlockSpec((None, TQ, LANE), lambda b, i: (b, i, 0)),
                  whole(vsl.shape[1:]),
                  pl.BlockSpec((None, NSA_G, n_sel, TQ), lambda b, i: (b, 0, 0, i)),
                  pl.BlockSpec((None, NSA_G, TQ // SLC_LEN, TQ), lambda b, i: (b, 0, i, i)),
                  pl.BlockSpec((None, LANE, TQ), lambda b, i: (b, 0, i)),
                  pl.BlockSpec((TQ, HD), lambda b, i: (b * (S // TQ) + i, 0)),
                  _const_spec((SEL_KC, LANE))],
        out_specs=pl.BlockSpec((TQ, HD), lambda b, i: (b * (S // TQ) + i, 0)),
        scratch_shapes=[pltpu.VMEM((NSA_G, SEL_KC, W), BF16)] * 4 + [pltpu.VMEM((NSA_G, 1, W), F32)] * 3
                       + [pltpu.VMEM((NSA_G, VT_ROWS, W), F32)],
        compiler_params=_params(("parallel", "arbitrary")),
        name="nsa_selected",
    )(qt, ks, ks, vsl, selm, sel, gt, part, oneh)


def _merge_body(x_ref, oa_ref, ob_ref, oc_ref, gates_ref, wbr_ref, wo_ref, o_ref):
    D = x_ref.shape[1]
    y = jnp.zeros(x_ref.shape, F32)
    for j, br in enumerate((oa_ref, ob_ref, oc_ref)):
        y = y + _sigmoid(gates_ref[:, j * D:(j + 1) * D].astype(F32)) * _dot(br[...].astype(BF16), wbr_ref[j])
    o_ref[...] = x_ref[...] + _dot(y.astype(BF16), wo_ref[...])


def _merge(x, o_a, o_b, o_c, gates, wbr, wo, tm=1024):
    T, D = x.shape
    row = lambda w: pl.BlockSpec((tm, w), lambda i: (i, 0))
    return pl.pallas_call(
        _merge_body,
        out_shape=jax.ShapeDtypeStruct((T, D), F32),
        grid=(T // tm,),
        in_specs=[row(D), row(MIX_W), row(MIX_W), row(MIX_W), row(3 * D), _const_spec(wbr.shape), _const_spec(wo.shape)],
        out_specs=row(D),
        compiler_params=_params(("parallel",)),
        name="merge_out",
    )(x, o_a, o_b, o_c, gates, wbr, wo)


def _pack_in_proj(w_in, b_in):
    ws, bs = [], []
    for _, off, w, wp in _SEGS:
        ws.append(jnp.pad(w_in[..., off:off + w], ((0, 0), (0, 0), (0, wp - w))))
        bs.append(jnp.pad(b_in[..., off:off + w], ((0, 0), (0, wp - w))))
    return jnp.concatenate(ws, axis=-1).astype(BF16), jnp.concatenate(bs, axis=-1)[:, None, :]


def _rope_tables(pos):
    half = NSA_D // 2
    freqs = ROPE_THETA ** (-jnp.arange(half, dtype=F32) / half)
    ang = pos.astype(F32)[:, None] * freqs[None, :]
    cos, sin = jnp.cos(ang), jnp.sin(ang)
    reps = LANE // NSA_D
    return jnp.tile(jnp.concatenate([cos, cos], axis=1), (1, reps)), jnp.tile(jnp.concatenate([-sin, sin], axis=1), (1, reps))


def _cmp_weights(w1, w2):
    L = w1.shape[0]
    w1r = w1.reshape(L, 2, 2, CMP_STRIDE, NSA_D, CMP_HIDDEN)
    eye_g = jnp.eye(NSA_G, dtype=w1.dtype)
    ex = jnp.einsum('lwstdh,gk->lwstgdkh', w1r, eye_g)
    ex = ex.reshape(L, 2, 2, CMP_STRIDE * NSA_G * NSA_D, NSA_G * CMP_HIDDEN)
    w2x = jnp.einsum('lwhd,gk->lwghkd', w2, eye_g).reshape(L, 2, NSA_G * CMP_HIDDEN, NSA_G * NSA_D)
    return ex[:, :, 0].astype(BF16), ex[:, :, 1].astype(BF16), w2x.astype(BF16)


def _cmp_pos_rows(pe):
    L = pe.shape[0]
    r = pe.reshape(L, 2, 2, CMP_STRIDE, 1, NSA_D)
    return jnp.broadcast_to(r, (L, 2, 2, CMP_STRIDE, NSA_G, NSA_D)).reshape(L, 2, 2, CMP_STRIDE * NSA_G * NSA_D)


def _lane_gain(g):
    return jnp.tile(g, LANE // g.shape[-1])[None, :]


def _layer(x, lw, consts, B, S):
    cos, sin, cos_c, sin_c, gavg = consts
    T = B * S
    x = _ffn(x, lw["ffn1_norm"], lw["ffn1_wg"], lw["ffn1_wu"], lw["ffn1_wd"])
    z = dict(zip([s[0] for s in _SEGS], _proj(x, lw["mix_norm"], lw["w_in"], lw["b_in"])))
    o_a = _gla(z["a_q"], z["a_k"], z["a_v"], z["a_r"], z["a_lr"], lw["gla_wa"], lw["gla_ba"], lw["gla_gn"], B, S)
    o_c = _mlstm(z["c_qk"], z["c_v"], z["c_o"], z["c_if"], lw["conv_w"], lw["conv_b"], B, S)
    qn = lw["nsa_qk_norm"]
    qt, ks, kw, vsl, vwd, gt = _nsa_prep(z["b_q"], z["b_ks"], z["b_kw"], z["b_vs"], z["b_vw"], z["b_g"], cos, sin,
                                         _lane_gain(qn[0]), _lane_gain(qn[2]), _lane_gain(qn[3]), gavg, B, S)
    n_blk = S // CMP_STRIDE
    xk = z["b_kc"].reshape(B, n_blk, CMP_STRIDE * LANE)
    xv = z["b_vc"].reshape(B, n_blk, CMP_STRIDE * LANE)
    kc, vct = _nsa_cmp(xk, xv, lw["cmp_pe"], lw["cmp_wt"], lw["cmp_wb"], lw["cmp_w2"], cos_c, sin_c,
                       _lane_gain(qn[1]), gavg)
    part, sel, selm = _nsa_select(qt, kc, vct, kw, vwd, gt, B, S)
    o_b = _nsa_selected(qt, ks, vsl, selm, sel, gt, part, B, S)
    x = _merge(x, o_a, o_b, o_c, z["gates"], lw["w_branch"], lw["w_out"])
    return _ffn(x, lw["ffn2_norm"], lw["ffn2_wg"], lw["ffn2_wu"], lw["ffn2_wd"])


def kernel(x, ffn1_norm, ffn1_w_gate, ffn1_w_up, ffn1_w_down, mix_norm, w_in, b_in, gla_w_alpha, gla_b_alpha, gla_out_norm, nsa_qk_norm, nsa_cmp_pos, nsa_cmp_w1, nsa_cmp_w2, mlstm_conv_w, mlstm_conv_b, w_branch, w_out, ffn2_norm, ffn2_w_gate, ffn2_w_up, ffn2_w_down):
    B, S, D = x.shape
    w_in_p, b_in_p = _pack_in_proj(w_in, b_in)
    cmp_wt, cmp_wb, cmp_w2 = _cmp_weights(nsa_cmp_w1, nsa_cmp_w2)
    layers = {
        "ffn1_norm": ffn1_norm[:, None, :], "ffn1_wg": ffn1_w_gate.astype(BF16), "ffn1_wu": ffn1_w_up.astype(BF16),
        "ffn1_wd": ffn1_w_down.astype(BF16),
        "mix_norm": mix_norm[:, None, :], "w_in": w_in_p, "b_in": b_in_p,
        "gla_wa": jnp.pad(gla_w_alpha, ((0, 0), (0, LANE - GLA_RANK), (0, 0))), "gla_ba": gla_b_alpha[:, None, :],
        "gla_gn": gla_out_norm[:, None, :],
        "nsa_qk_norm": nsa_qk_norm, "cmp_pe": _cmp_pos_rows(nsa_cmp_pos), "cmp_wt": cmp_wt, "cmp_wb": cmp_wb,
        "cmp_w2": cmp_w2,
        "conv_w": mlstm_conv_w, "conv_b": mlstm_conv_b[:, None, :],
        "w_branch": w_branch.astype(BF16), "w_out": w_out.astype(BF16),
        "ffn2_norm": ffn2_norm[:, None, :], "ffn2_wg": ffn2_w_gate.astype(BF16), "ffn2_wu": ffn2_w_up.astype(BF16),
        "ffn2_wd": ffn2_w_down.astype(BF16),
    }
    n_blk = S // CMP_STRIDE
    cos, sin = _rope_tables(jnp.arange(S))
    cos_c, sin_c = _rope_tables(jnp.arange(n_blk) * CMP_STRIDE + CMP_LEN - 1)
    gavg = jnp.asarray(np.kron(np.eye(LANE // NSA_D), np.full((NSA_D, NSA_D), 1.0 / NSA_D)), BF16)
    consts = (cos, sin, cos_c, sin_c, gavg)

    out = x.reshape(B * S, D)
    for l in range(w_in.shape[0]):
        out = _layer(out, {name: w[l] for name, w in layers.items()}, consts, B, S)
    return out.reshape(B, S, D)
```

```python
import functools

import numpy as np
import jax
import jax.numpy as jnp
from jax import lax
from jax.experimental import pallas as pl
from jax.experimental.pallas import tpu as pltpu

F32 = jnp.float32
BF16 = jnp.bfloat16

RMS_EPS = 1e-6
ROPE_THETA = 10000.0

GLA_H, GLA_DK, GLA_DV, GLA_RANK, GLA_GATE_NORM = 4, 64, 128, 16, 16.0
GLA_SUB = 16
NSA_H, NSA_G, NSA_D = 8, 2, 64
CMP_LEN, CMP_STRIDE, CMP_HIDDEN = 32, 16, 256
SLC_LEN, SLC_TOPK, WIN = 64, 16, 512
FORCED_SCORE = 1e4
MLSTM_H, MLSTM_DK, MLSTM_DV, CONV_W = 4, 64, 128, 4
MLSTM_CHUNK = 128
MIX_W = 512
LANE = 128
MXU_N = 256
NEG = -1e30
LOG2E = 1.4426950408889634
SEL_KC = 512
SEL_TQ = 512
PS_PAD = 8
CMP_VARIANTS = 4
VT_ROWS = 80

_SEGS = (
    ("a_q", 0, 256, 256), ("a_k", 256, 256, 256), ("a_v", 512, 512, 512), ("a_r", 1024, 512, 512),
    ("b_q", 1552, 512, 512), ("c_qk", 2856, 512, 512), ("c_v", 3368, 512, 512), ("c_o", 3880, 512, 512),
    ("gates", 4400, 3072, 3072),
    ("a_lr", 1536, 16, 128), ("b_kc", 2064, 128, 128), ("b_vc", 2192, 128, 128), ("b_ks", 2320, 128, 128),
    ("b_vs", 2448, 128, 128), ("b_kw", 2576, 128, 128), ("b_vw", 2704, 128, 128), ("b_g", 2832, 24, 128),
    ("c_if", 4392, 8, 128),
)
_N_PACK = sum(s[3] for s in _SEGS)
_BF16_SEGS = ("gates", "c_v", "b_vs", "b_vw")


def _dot(a, b):
    return jnp.dot(a, b, preferred_element_type=F32)


def _dot_nt(a, b):
    return lax.dot_general(a, b, (((1,), (1,)), ((), ())), preferred_element_type=F32)


def _split2(a):
    hi = a.astype(BF16)
    lo = (a - hi.astype(F32)).astype(BF16)
    return hi, lo


def _dot_l2(a, b):
    hi, lo = _split2(a)
    return _dot(hi, b) + _dot(lo, b)


def _dot_r2(a, b):
    hi, lo = _split2(b)
    return _dot(a, hi) + _dot(a, lo)


def _log_sigmoid(x):
    return jnp.minimum(x, 0.0) - jnp.log(1.0 + jnp.exp(-jnp.abs(x)))


def _sigmoid(x):
    return 1.0 / (1.0 + jnp.exp(-x))


def _silu(x):
    return x * _sigmoid(x)


def _iota(shape, dim):
    return lax.broadcasted_iota(jnp.int32, shape, dim)


def _const_spec(shape):
    nd = len(shape)
    return pl.BlockSpec(shape, lambda *_: (0,) * nd, pipeline_mode=pl.Buffered(1))


def _params(sem, vmem_mb=56):
    return pltpu.CompilerParams(dimension_semantics=sem, vmem_limit_bytes=vmem_mb * 1024 * 1024)


def _ffn_body(x_ref, g_ref, wg_ref, wu_ref, wd_ref, o_ref, *, bounds):
    x = x_ref[...]
    ms = jnp.mean(x * x, axis=-1, keepdims=True)
    h = (x * lax.rsqrt(ms + RMS_EPS) * g_ref[...]).astype(BF16)
    acc = jnp.zeros(x.shape, F32)
    for lo, hi in zip(bounds[:-1], bounds[1:]):
        a = _dot(h, wg_ref[:, lo:hi])
        u = _dot(h, wu_ref[:, lo:hi])
        t = (_silu(a) * u).astype(BF16)
        acc = acc + _dot(t, wd_ref[lo:hi, :])
    o_ref[...] = x + 0.5 * acc


def _ffn(x, g, wg, wu, wd, tm=512):
    T, D = x.shape
    F = wg.shape[1]
    bounds = (0, -(-F // (2 * MXU_N)) * MXU_N, F)
    return pl.pallas_call(
        functools.partial(_ffn_body, bounds=bounds),
        out_shape=jax.ShapeDtypeStruct((T, D), F32),
        grid=(T // tm,),
        in_specs=[pl.BlockSpec((tm, D), lambda i: (i, 0)), _const_spec((1, D)),
                  _const_spec((D, F)), _const_spec((D, F)), _const_spec((F, D))],
        out_specs=pl.BlockSpec((tm, D), lambda i: (i, 0)),
        compiler_params=_params(("parallel",)),
        name="ffn",
    )(x, g, wg, wu, wd)


def _proj_body(x_ref, g_ref, w_ref, b_ref, *o_refs):
    x = x_ref[...]
    ms = jnp.mean(x * x, axis=-1, keepdims=True)
    h = (x * lax.rsqrt(ms + RMS_EPS) * g_ref[...]).astype(BF16)
    off, i = 0, 0
    while i < len(o_refs):
        group, w = [], 0
        while i < len(o_refs) and (not group or w % MXU_N):
            group.append(o_refs[i])
            w += o_refs[i].shape[1]
            i += 1
        z = _dot(h, w_ref[:, off:off + w]) + b_ref[:, off:off + w]
        c = 0
        for o_ref in group:
            o_ref[...] = z[:, c:c + o_ref.shape[1]].astype(o_ref.dtype)
            c += o_ref.shape[1]
        off += w


def _proj(x, g, w, b, tm=512):
    T, D = x.shape
    return pl.pallas_call(
        _proj_body,
        out_shape=[jax.ShapeDtypeStruct((T, s[3]), BF16 if s[0] in _BF16_SEGS else F32) for s in _SEGS],
        grid=(T // tm,),
        in_specs=[pl.BlockSpec((tm, D), lambda i: (i, 0)), _const_spec((1, D)),
                  _const_spec((D, _N_PACK)), _const_spec((1, _N_PACK))],
        out_specs=[pl.BlockSpec((tm, s[3]), lambda i: (i, 0)) for s in _SEGS],
        compiler_params=_params(("parallel",)),
        name="in_proj",
    )(x, g, w, b)


def _gla_body(q_ref, k_ref, v_ref, r_ref, lr_ref, wa_ref, ba_ref, gn_ref, tri_ref, bones_ref, eh_ref,
              o_ref, st_ref, qs_ref, c_ref, tot_ref, kst_ref, vt4_ref, oi_ref):
    nB, Lc = q_ref.shape[0], q_ref.shape[1]
    n_sub = Lc // GLA_SUB
    HK = GLA_H * GLA_DK

    @pl.when(pl.program_id(0) == 0)
    def _():
        st_ref[...] = jnp.zeros(st_ref.shape, F32)

    lane_h = _iota((1, HK), 1) >> (GLA_DK.bit_length() - 1)
    row_i = _iota((GLA_SUB, 1), 0)
    col_r = _iota((1, GLA_H * Lc), 1) & (Lc - 1)
    wa_hi, wa_lo = _split2(wa_ref[...])

    for b in range(nB):
        lr_hi, lr_lo = _split2(lr_ref[b])
        u = _dot(lr_hi, wa_hi) + _dot(lr_hi, wa_lo) + _dot(lr_lo, wa_hi) + ba_ref[...]
        g = _log_sigmoid(u) * (1.0 / GLA_GATE_NORM)
        c = _dot_r2(tri_ref[...], g)
        tot = _dot_r2(bones_ref[...], g)
        qs_ref[b] = q_ref[b] * (GLA_DK ** -0.5)
        c_ref[b] = c
        tot_ref[b] = tot
        kt = k_ref[b] * jnp.exp(tot - c)
        vt = v_ref[b].T.astype(BF16)
        for h in range(GLA_H):
            kst_ref[b, h * Lc:(h + 1) * Lc, :] = jnp.where(lane_h == h, kt, 0.0).astype(BF16)
            vt4_ref[b, :, h * Lc:(h + 1) * Lc] = vt[h * GLA_DV:(h + 1) * GLA_DV, :]

    def sub(b, s):
        rows = slice(s * GLA_SUB, (s + 1) * GLA_SUB)
        qs = qs_ref[b, rows, :]
        cs = c_ref[b, rows, :]
        st = st_ref[b]
        qd = qs * jnp.exp(cs)
        q4 = jnp.concatenate([jnp.where(lane_h == h, qd, 0.0) for h in range(GLA_H)], axis=0).astype(BF16)
        inter4 = _dot_nt(q4, st.astype(BF16))
        inter = jnp.concatenate([inter4[h * GLA_SUB:(h + 1) * GLA_SUB, :] for h in range(GLA_H)], axis=1)
        xs = []
        for j in range(GLA_SUB):
            r = s * GLA_SUB + j
            x = qs * k_ref[b, r:r + 1, :] * jnp.exp(jnp.minimum(cs - c_ref[b, r:r + 1, :], 0.0))
            xs.append(jnp.where(row_i >= j, x, 0.0))
        r_all = _dot(jnp.concatenate(xs, axis=0).astype(BF16), eh_ref[...])
        intra = jnp.zeros((GLA_SUB, r_all.shape[1]), F32)
        for j in range(GLA_SUB):
            r = s * GLA_SUB + j
            intra = intra + r_all[j * GLA_SUB:(j + 1) * GLA_SUB, :] * v_ref[b, r:r + 1, :]
        oi_ref[b, rows, :] = inter + intra
        dec = jnp.exp(tot_ref[b, s * GLA_SUB:s * GLA_SUB + 1, :])
        in_sub = (col_r >= s * GLA_SUB) & (col_r < (s + 1) * GLA_SUB)
        vtm = jnp.where(in_sub, vt4_ref[b], jnp.zeros((), BF16))
        st_ref[b] = dec * st + _dot(vtm, kst_ref[b])

    for s in range(n_sub):
        for b in range(nB):
            sub(b, s)

    gn = gn_ref[...]
    for b in range(nB):
        for h in range(GLA_H):
            sl = slice(h * GLA_DV, (h + 1) * GLA_DV)
            o = oi_ref[b, :, sl]
            ms = jnp.mean(o * o, axis=-1, keepdims=True)
            o_ref[b, :, sl] = (o * lax.rsqrt(ms + RMS_EPS) * gn * _silu(r_ref[b, :, sl])).astype(o_ref.dtype)


def _gla_consts(Lc):
    r = np.arange(Lc)
    same = (r[:, None] // GLA_SUB) == (r[None, :] // GLA_SUB)
    tri = (same & (r[None, :] <= r[:, None])).astype(np.float32)
    bones = same.astype(np.float32)
    hk = np.arange(GLA_H * GLA_DK) // GLA_DK
    hv = np.arange(GLA_H * GLA_DV) // GLA_DV
    eh = (hk[:, None] == hv[None, :]).astype(np.float32)
    return jnp.asarray(tri, BF16), jnp.asarray(bones, BF16), jnp.asarray(eh, BF16)


def _gla(a_q, a_k, a_v, a_r, a_lr, wa, ba, gn, B, S, Lc=128):
    HK, HV = GLA_H * GLA_DK, GLA_H * GLA_DV
    nb = S // Lc
    assert Lc & (Lc - 1) == 0
    tri, bones, eh = _gla_consts(Lc)
    row = lambda w: pl.BlockSpec((B, Lc, w), lambda i: (0, i, 0))
    seq = lambda t: t.reshape(B, S, t.shape[-1])
    out = pl.pallas_call(
        _gla_body,
        out_shape=jax.ShapeDtypeStruct((B, S, HV), BF16),
        grid=(nb,),
        in_specs=[row(HK), row(HK), row(HV), row(HV), row(LANE),
                  _const_spec((LANE, HK)), _const_spec((1, HK)), _const_spec((1, GLA_DV)),
                  _const_spec((Lc, Lc)), _const_spec((Lc, Lc)), _const_spec((HK, HV))],
        out_specs=row(HV),
        scratch_shapes=[pltpu.VMEM((B, GLA_DV, HK), F32), pltpu.VMEM((B, Lc, HK), F32), pltpu.VMEM((B, Lc, HK), F32),
                        pltpu.VMEM((B, Lc, HK), F32), pltpu.VMEM((B, GLA_H * Lc, HK), BF16),
                        pltpu.VMEM((B, GLA_DV, GLA_H * Lc), BF16), pltpu.VMEM((B, Lc, HV), F32)],
        compiler_params=_params(("arbitrary",)),
        name="gla",
    )(seq(a_q), seq(a_k), seq(a_v), seq(a_r), seq(a_lr), wa, ba, gn, tri, bones, eh)
    return out.reshape(B * S, HV)


def _mlstm_body(qk_ref, v_ref, og_ref, if_ref, cw_ref, cb_ref, tri_ref, bdm_ref,
                o_ref, xx_ref, ct_ref, n_ref, m_ref, kw_ref, vt_ref, act_ref):
    Lc = v_ref.shape[0]
    L = MLSTM_CHUNK
    HK = MLSTM_H * MLSTM_DK
    tail = 8

    @pl.when(pl.program_id(1) == 0)
    def _():
        ct_ref[...] = jnp.zeros(ct_ref.shape, F32)
        n_ref[...] = jnp.zeros(n_ref.shape, F32)
        m_ref[...] = jnp.zeros(m_ref.shape, F32)
        xx_ref[0:tail, :] = jnp.zeros((tail, xx_ref.shape[1]), F32)

    xx_ref[tail:tail + Lc, :] = qk_ref[...]
    y = jnp.zeros((Lc, 2 * HK), F32) + cb_ref[...]
    for kk in range(CONV_W):
        y = y + cw_ref[kk:kk + 1, :] * xx_ref[pl.ds(tail - (CONV_W - 1) + kk, Lc), :]
    xx_ref[0:tail, :] = xx_ref[Lc:Lc + tail, :]
    act_ref[...] = _silu(y)

    gates = if_ref[...]
    logf = _log_sigmoid(gates)
    bcum = _dot_r2(tri_ref[...], logf)
    b_t = bcum.T
    q_t = act_ref[:, 0:HK].T
    qt_hi, qt_lo = _split2(q_t)
    kw_ref[...] = jnp.zeros(kw_ref.shape, BF16)
    vt_ref[...] = v_ref[...].astype(F32).T.astype(BF16)
    lane_hk = _iota((1, HK), 1) >> (MLSTM_DK.bit_length() - 1)
    col_l = _iota((1, Lc), 1)
    lane_w = _iota((1, LANE), 1)
    row_l = _iota((L, 1), 0)
    per_win = LANE // L

    for ci in range(Lc // L):
        p, c = divmod(ci, per_win)
        rows = slice(ci * L, (ci + 1) * L)
        win = slice(p * LANE, (p + 1) * LANE)
        causal_t = ((c * L + row_l) <= lane_w) & ((lane_w >> (L.bit_length() - 1)) == c)
        k_all = act_ref[rows, HK:2 * HK] * (MLSTM_DK ** -0.5)
        k_bf = k_all.astype(BF16)
        ct = ct_ref[...]
        ct_bf = ct.astype(BF16)
        n_row = n_ref[...]
        n_hi, n_lo = _split2(n_row)
        n8 = jnp.concatenate([jnp.where(lane_hk == h, part, jnp.zeros((), BF16))
                              for part in (n_hi, n_lo) for h in range(MLSTM_H)], axis=0)
        qn8 = _dot(n8, qt_hi[:, win]) + _dot(n8, qt_lo[:, win])
        wk_full = jnp.zeros((L, HK), F32)
        dec_row = jnp.zeros((1, HK), F32)
        for h in range(MLSTM_H):
            icol = gates[rows, h:h + 1]
            bcol = bcum[rows, MLSTM_H + h:MLSTM_H + h + 1]
            brow = b_t[MLSTM_H + h:MLSTM_H + h + 1, win]
            m_st = m_ref[0:1, h:h + 1]
            dmat_t = jnp.where(causal_t, brow + (icol - bcol), -jnp.inf)
            inter_log = brow + m_st
            m_row = jnp.maximum(inter_log, jnp.max(dmat_t, axis=0, keepdims=True))
            w_inter = jnp.exp(inter_log - m_row)
            head_rows = slice(h * MLSTM_DK, (h + 1) * MLSTM_DK)
            zero_q = jnp.zeros((MLSTM_DK, LANE), BF16)
            qh = jnp.concatenate([qt_hi[head_rows, win] if hh == h else zero_q for hh in range(MLSTM_H)], axis=0)
            s_qk = _dot(k_bf, qh) * jnp.exp(dmat_t - m_row)
            den = w_inter * (qn8[h:h + 1, :] + qn8[MLSTM_H + h:MLSTM_H + h + 1, :]) + jnp.sum(s_qk, axis=0, keepdims=True)
            s_bf = s_qk.astype(BF16)
            pads = [jnp.zeros((L, LANE), BF16)] * per_win
            pads[c] = s_bf
            dv = slice(h * MLSTM_DV, (h + 1) * MLSTM_DV)
            num_t = w_inter * _dot(ct_bf[dv, :], qt_hi[:, win]) + _dot(vt_ref[dv, win], jnp.concatenate(pads, axis=0))
            h_t = num_t / jnp.maximum(jnp.abs(den), jnp.exp(-m_row))
            o_ref[rows, dv] = (h_t.T[c * L:(c + 1) * L, :] * _sigmoid(og_ref[rows, dv])).astype(o_ref.dtype)
            m_new = m_row[:, c * L + L - 1:c * L + L]
            b_last = bcol[L - 1:L, :]
            w_k = jnp.exp(b_last - bcol + icol - m_new)
            decay = jnp.exp(b_last + m_st - m_new)
            head = lane_hk == h
            wk_full = wk_full + jnp.where(head, w_k, 0.0)
            dec_row = dec_row + jnp.where(head, decay, 0.0)
            m_ref[0:1, h:h + 1] = m_new
        kw = k_all * wk_full
        kw_ref[rows, :] = kw.astype(BF16)
        in_chunk = (col_l >= ci * L) & (col_l < (ci + 1) * L)
        vtm = jnp.where(in_chunk, vt_ref[...], jnp.zeros((), BF16))
        ct_ref[...] = dec_row * ct + _dot(vtm, kw_ref[...]) * bdm_ref[...]
        n_ref[...] = dec_row * n_row + jnp.sum(kw, axis=0, keepdims=True)


def _mlstm_consts(Lc):
    r = np.arange(Lc)
    same = (r[:, None] // MLSTM_CHUNK) == (r[None, :] // MLSTM_CHUNK)
    tri = (same & (r[None, :] <= r[:, None])).astype(np.float32)
    hk = np.arange(MLSTM_H * MLSTM_DK) // MLSTM_DK
    hv = np.arange(MLSTM_H * MLSTM_DV) // MLSTM_DV
    eh = (hk[:, None] == hv[None, :]).astype(np.float32)
    return jnp.asarray(tri, BF16), jnp.asarray(eh.T.copy(), F32)


def _mlstm(c_qk, c_v, c_o, c_if, cw, cb, B, S, Lc=256):
    HK, HV = MLSTM_H * MLSTM_DK, MLSTM_H * MLSTM_DV
    nb = S // Lc
    assert Lc % LANE == 0 and LANE % MLSTM_CHUNK == 0
    tri, bdm = _mlstm_consts(Lc)
    row = lambda w: pl.BlockSpec((Lc, w), lambda b, i: (b * nb + i, 0))
    return pl.pallas_call(
        _mlstm_body,
        out_shape=jax.ShapeDtypeStruct((B * S, HV), BF16),
        grid=(B, nb),
        in_specs=[row(2 * HK), row(HV), row(HV), row(LANE),
                  _const_spec((CONV_W, 2 * HK)), _const_spec((1, 2 * HK)),
                  _const_spec((Lc, Lc)), _const_spec((HV, HK))],
        out_specs=row(HV),
        scratch_shapes=[pltpu.VMEM((Lc + 8, 2 * HK), F32), pltpu.VMEM((HV, HK), F32), pltpu.VMEM((1, HK), F32),
                        pltpu.VMEM((8, LANE), F32), pltpu.VMEM((Lc, HK), BF16), pltpu.VMEM((HV, Lc), BF16),
                        pltpu.VMEM((Lc, 2 * HK), F32)],
        compiler_params=_params(("parallel", "arbitrary")),
        name="mlstm",
    )(c_qk, c_v, c_o, c_if, cw, cb, tri, bdm)


def _group_rms(x, gavg, gain):
    ms = _dot_l2(x * x, gavg)
    return x * lax.rsqrt(ms + RMS_EPS) * gain


def _rope_lanes(x, cos, sin_signed):
    half = NSA_D // 2
    first = (_iota((1, LANE), 1) & (NSA_D - 1)) < half
    swapped = jnp.where(first, pltpu.roll(x, LANE - half, 1), pltpu.roll(x, half, 1))
    return x * cos + swapped * sin_signed


def _store_vt_tiles(dst_ref, vt, width):
    ones = jnp.ones((VT_ROWS - NSA_D, width), BF16)
    for j in range(vt.shape[1] // width):
        for g in range(NSA_G):
            dst_ref[j, g, 0:NSA_D, :] = vt[g * NSA_D:(g + 1) * NSA_D, j * width:(j + 1) * width]
            dst_ref[j, g, NSA_D:VT_ROWS, :] = ones


def _nsa_prep_body(q_ref, ks_ref, kw_ref, vs_ref, vw_ref, g_ref, cos_ref, sin_ref, gq_ref, gs_ref, gw_ref, gavg_ref,
                   qt_ref, kso_ref, kwo_ref, vsl_ref, vwd_ref, gt_ref):
    cos, sin = cos_ref[...], sin_ref[...]
    gavg = gavg_ref[...]
    for cb in range(NSA_H * NSA_D // LANE):
        sl = slice(cb * LANE, (cb + 1) * LANE)
        qn = _rope_lanes(_group_rms(q_ref[:, sl], gavg, gq_ref[...]), cos, sin) * (NSA_D ** -0.5 * LOG2E)
        qt_ref[sl, :] = qn.T.astype(BF16)
    kso_ref[...] = _rope_lanes(_group_rms(ks_ref[...], gavg, gs_ref[...]), cos, sin).astype(BF16)
    kwo_ref[...] = _rope_lanes(_group_rms(kw_ref[...], gavg, gw_ref[...]), cos, sin).astype(BF16)
    vst = vs_ref[...].astype(F32).T.astype(BF16)
    _store_vt_tiles(vsl_ref, vst, SEL_KC)
    _store_vt_tiles(vwd_ref, vw_ref[...].astype(F32).T.astype(BF16), LANE)
    gt_ref[...] = _sigmoid(g_ref[...]).T


def _nsa_prep(b_q, b_ks, b_kw, b_vs, b_vw, b_g, cos, sin, gq, gs, gw, gavg, B, S, tm=SEL_KC):
    nb = S // tm
    row = lambda w: pl.BlockSpec((tm, w), lambda b, i: (b * nb + i, 0))
    tab = pl.BlockSpec((tm, LANE), lambda b, i: (i, 0))
    HD = NSA_H * NSA_D
    vt_shape = lambda width: jax.ShapeDtypeStruct((B, S // width, NSA_G, VT_ROWS, width), BF16)
    vt_spec = lambda width: pl.BlockSpec((None, tm // width, NSA_G, VT_ROWS, width), lambda b, i: (b, i, 0, 0, 0))
    return pl.pallas_call(
        _nsa_prep_body,
        out_shape=[jax.ShapeDtypeStruct((B, HD, S), BF16),
                   jax.ShapeDtypeStruct((B, S, LANE), BF16),
                   jax.ShapeDtypeStruct((B, S, LANE), BF16),
                   vt_shape(SEL_KC),
                   vt_shape(LANE),
                   jax.ShapeDtypeStruct((B, LANE, S), F32)],
        grid=(B, nb),
        in_specs=[row(HD), row(LANE), row(LANE), row(LANE), row(LANE), row(LANE), tab, tab,
                  _const_spec((1, LANE)), _const_spec((1, LANE)), _const_spec((1, LANE)), _const_spec((LANE, LANE))],
        out_specs=[pl.BlockSpec((None, HD, tm), lambda b, i: (b, 0, i)),
                   pl.BlockSpec((None, tm, LANE), lambda b, i: (b, i, 0)),
                   pl.BlockSpec((None, tm, LANE), lambda b, i: (b, i, 0)),
                   vt_spec(SEL_KC), vt_spec(LANE),
                   pl.BlockSpec((None, LANE, tm), lambda b, i: (b, 0, i))],
        compiler_params=_params(("parallel", "parallel")),
        name="nsa_prep",
    )(b_q, b_ks, b_kw, b_vs, b_vw, b_g, cos, sin, gq, gs, gw, gavg)


def _gelu_tanh(x):
    return 0.5 * x * (1.0 + jnp.tanh(0.7978845608028654 * (x + 0.044715 * x * x * x)))


def _nsa_cmp_body(xk_ref, xv_ref, pe_ref, wt_ref, wb_ref, w2_ref, cos_ref, sin_ref, gk_ref, gavg_ref,
                  kc_ref, vct_ref):
    n = xk_ref.shape[0]

    def compress(x, which):
        u = _dot((x + pe_ref[which, 0:1, :]).astype(BF16), wt_ref[which])
        v = _dot((x + pe_ref[which, 1:2, :]).astype(BF16), wb_ref[which])
        hid = u + pltpu.roll(v, n - 1, 0)
        return _dot(_gelu_tanh(hid).astype(BF16), w2_ref[which])

    ck = compress(xk_ref[...], 0)
    kc_ref[...] = _rope_lanes(_group_rms(ck, gavg_ref[...], gk_ref[...]), cos_ref[...], sin_ref[...]).astype(BF16)
    vct = compress(xv_ref[...], 1).T.astype(BF16)
    for g in range(NSA_G):
        vct_ref[g] = vct[g * NSA_D:(g + 1) * NSA_D, :]


def _nsa_cmp(xk, xv, pe, wt, wb, w2, cos, sin, gk, gavg):
    B, n, W = xk.shape
    return pl.pallas_call(
        _nsa_cmp_body,
        out_shape=[jax.ShapeDtypeStruct((B, n, LANE), BF16),
                   jax.ShapeDtypeStruct((B, NSA_G, NSA_D, n), BF16)],
        grid=(B,),
        in_specs=[pl.BlockSpec((None, n, W), lambda b: (b, 0, 0)), pl.BlockSpec((None, n, W), lambda b: (b, 0, 0)),
                  _const_spec(pe.shape), _const_spec(wt.shape), _const_spec(wb.shape), _const_spec(w2.shape),
                  _const_spec((n, LANE)), _const_spec((n, LANE)), _const_spec((1, LANE)), _const_spec((LANE, LANE))],
        out_specs=[pl.BlockSpec((None, n, LANE), lambda b: (b, 0, 0)),
                   pl.BlockSpec((None, NSA_G, NSA_D, n), lambda b: (b, 0, 0, 0))],
        compiler_params=_params(("parallel",)),
        name="nsa_compress",
    )(xk, xv, pe, wt, wb, w2, cos, sin, gk, gavg)


def _group_queries(qt_ref, g):
    HPG = NSA_H // NSA_G
    q4 = jnp.concatenate([qt_ref[(g * HPG + h) * NSA_D:(g * HPG + h + 1) * NSA_D, :] for h in range(HPG)], axis=1)
    parts = [jnp.zeros(q4.shape, BF16)] * NSA_G
    parts[g] = q4
    return jnp.concatenate(parts, axis=0)


def _bitonic_merge_desc(xs):
    xs = list(xs)
    j = len(xs) // 2
    while j >= 1:
        for a in range(len(xs)):
            b = a ^ j
            if b > a:
                xs[a], xs[b] = jnp.maximum(xs[a], xs[b]), jnp.minimum(xs[a], xs[b])
        j //= 2
    return xs


def _bitonic_sort_desc(xs):
    if len(xs) == 1:
        return list(xs)
    half = len(xs) // 2
    lo = _bitonic_sort_desc(xs[:half])
    hi = _bitonic_sort_desc(xs[half:])
    return _bitonic_merge_desc(lo + hi[::-1])


def _kth_largest(score, k):
    SUBLANES = 8
    tiles = [score[SUBLANES * v:SUBLANES * (v + 1), :] for v in range(score.shape[0] // SUBLANES)]
    size = max(k, 1 << (len(tiles) - 1).bit_length())
    tiles = tiles + [jnp.full(tiles[0].shape, NEG, F32)] * (size - len(tiles))

    def top_of_two(a, b):
        return _bitonic_merge_desc([jnp.maximum(a[r], b[k - 1 - r]) for r in range(k)])

    tops = [_bitonic_sort_desc(tiles[c:c + k]) for c in range(0, size, k)]
    while len(tops) > 1:
        tops = [top_of_two(tops[c], tops[c + 1]) for c in range(0, len(tops), 2)]
    top = tops[0]
    shift = SUBLANES // 2
    while shift >= 1:
        top = top_of_two(top, [pltpu.roll(t, shift, 0) for t in top])
        shift //= 2
    return top[k - 1][0:1, :]


def _nsa_select_dispatch(*refs, sel_tile, n_variants):
    n_cmp = refs[1].shape[0]
    need = (pl.program_id(1) + 1) * LANE // CMP_STRIDE
    variant = (need - 1) * n_variants // n_cmp
    for k in range(n_variants):
        pl.when(variant == k)(functools.partial(_nsa_select_body, *refs, sel_tile=sel_tile,
                                                n_rows=n_cmp * (k + 1) // n_variants))


def _nsa_select_body(qt_ref, kc_ref, vct_ref, kw_ref, vwd_ref, gt_ref, tril_ref, part_ref, sel_ref, selm_ref, ps_buf,
                     *, sel_tile, n_rows):
    TQ = LANE
    HPG = NSA_H // NSA_G
    W = HPG * TQ
    n_cmp = kc_ref.shape[0]
    n_sel = sel_ref.shape[1]
    i = pl.program_id(1)
    s0 = i * TQ
    t_row = s0 + _iota((1, TQ), 1)
    n_win = WIN // LANE + 1
    j0 = jnp.maximum(i - (n_win - 1), 0)
    w_start = pl.multiple_of(j0 * LANE, LANE)
    own_first = (s0 // sel_tile) * (sel_tile // SLC_LEN)
    ps_buf[:, 0:PS_PAD, :] = jnp.zeros((NSA_G, PS_PAD, TQ), F32)

    for g in range(NSA_G):
        qpad = _group_queries(qt_ref, g)

        cend = _iota((n_rows, 1), 0) * CMP_STRIDE + (CMP_LEN - 1)
        bias_c = jnp.where(cend <= t_row, 0.0, NEG)
        sc = _dot(kc_ref[0:n_rows, :], qpad) + jnp.concatenate([bias_c] * HPG, axis=1)
        m = jnp.max(sc, axis=0, keepdims=True)
        m = jnp.where(m > 0.5 * NEG, m, 0.0)
        p = jnp.exp2(sc - m)
        p = p * (1.0 / jnp.maximum(jnp.sum(p, axis=0, keepdims=True), 1e-30))
        o_c = _dot(vct_ref[g, :, 0:n_rows], p.astype(BF16))
        psum = p[:, 0:TQ]
        for h in range(1, HPG):
            psum = psum + p[:, h * TQ:(h + 1) * TQ]
        ps_buf[g, PS_PAD:PS_PAD + n_rows, :] = psum

        per = SLC_LEN // CMP_STRIDE
        n_blk = n_rows // per
        imp = ps_buf[g, pl.ds(PS_PAD - 1, n_blk, stride=per), :]
        for jj in range(per):
            imp = imp + ps_buf[g, pl.ds(PS_PAD + jj, n_blk, stride=per), :]
        blk = _iota((n_blk, 1), 0)
        cur = t_row >> (SLC_LEN.bit_length() - 1)
        valid = blk * SLC_LEN <= t_row
        forced = (blk == 0) | (blk == cur) | (blk == cur - 1)
        score0 = jnp.where(valid, jnp.where(forced, FORCED_SCORE, imp), NEG)
        kth = _kth_largest(score0, min(SLC_TOPK, n_sel))
        above = score0 > kth
        tied = jnp.where(score0 == kth, 1.0, 0.0)
        need = min(SLC_TOPK, n_sel) - jnp.sum(jnp.where(above, 1.0, 0.0), axis=0, keepdims=True)
        tied_all = jnp.concatenate([tied, jnp.zeros((n_sel - n_blk, TQ), F32)], axis=0) if n_blk < n_sel else tied
        rank = _dot(tril_ref[0:n_blk, :], tied_all.astype(BF16))
        chosen = (above | ((tied > 0.5) & (rank <= need))) & valid
        sel_ref[g, 0:n_blk, :] = jnp.where(chosen, 0.0, NEG)
        selm_ref[g, 0:n_blk, :] = jnp.where(chosen & (blk < own_first), 0.0, NEG)
        if n_blk < n_sel:
            sel_ref[g, n_blk:n_sel, :] = jnp.full((n_sel - n_blk, TQ), NEG, F32)
            selm_ref[g, n_blk:n_sel, :] = jnp.full((n_sel - n_blk, TQ), NEG, F32)

        kpos = w_start + _iota((n_win * LANE, 1), 0)
        bias_w = jnp.where((kpos <= t_row) & (kpos > t_row - WIN), 0.0, NEG)
        sw = _dot(kw_ref[pl.ds(w_start, n_win * LANE), :], qpad) + jnp.concatenate([bias_w] * HPG, axis=1)
        pw = jnp.exp2(sw - jnp.max(sw, axis=0, keepdims=True)).astype(BF16)
        acc_w = jnp.zeros((VT_ROWS, W), F32)
        for r in range(n_win):
            acc_w = acc_w + _dot(vwd_ref[j0 + r, g], pw[r * LANE:(r + 1) * LANE, :])
        o_w = acc_w[0:NSA_D, :] * (1.0 / acc_w[NSA_D:NSA_D + 1, :])

        for hp in range(HPG // 2):
            tiles = []
            for h in (2 * hp, 2 * hp + 1):
                gr = (g * HPG + h) * 3
                cs = slice(h * TQ, (h + 1) * TQ)
                tiles.append(gt_ref[gr:gr + 1, :] * o_c[:, cs] + gt_ref[gr + 2:gr + 3, :] * o_w[:, cs])
            col = (g * HPG + 2 * hp) * NSA_D
            part_ref[:, col:col + 2 * NSA_D] = jnp.concatenate(tiles, axis=0).T


def _nsa_select(qt, kc, vct, kw, vwd, gt, B, S):
    HD = NSA_H * NSA_D
    TQ = LANE
    n_cmp = kc.shape[1]
    n_sel = S // SLC_LEN
    assert S >= WIN + TQ and CMP_LEN == 2 * CMP_STRIDE
    whole = lambda shape: pl.BlockSpec((None,) + shape, lambda b, i: (b,) + (0,) * len(shape),
                                       pipeline_mode=pl.Buffered(1))
    mask_shape = jax.ShapeDtypeStruct((B, NSA_G, n_sel, S), F32)
    mask_spec = pl.BlockSpec((None, NSA_G, n_sel, TQ), lambda b, i: (b, 0, 0, i))
    tril = jnp.asarray(np.tril(np.ones((n_sel, n_sel), np.float32)), BF16)
    return pl.pallas_call(
        functools.partial(_nsa_select_dispatch, sel_tile=SEL_TQ, n_variants=CMP_VARIANTS),
        out_shape=[jax.ShapeDtypeStruct((B * S, HD), F32), mask_shape, mask_shape],
        grid=(B, S // TQ),
        in_specs=[pl.BlockSpec((None, HD, TQ), lambda b, i: (b, 0, i)),
                  whole((n_cmp, LANE)), whole(vct.shape[1:]), whole((S, LANE)), whole(vwd.shape[1:]),
                  pl.BlockSpec((None, LANE, TQ), lambda b, i: (b, 0, i)), _const_spec((n_sel, n_sel))],
        out_specs=[pl.BlockSpec((TQ, HD), lambda b, i: (b * (S // TQ) + i, 0)), mask_spec, mask_spec],
        scratch_shapes=[pltpu.VMEM((NSA_G, n_cmp + PS_PAD, TQ), F32)],
        compiler_params=_params(("parallel", "parallel")),
        name="nsa_select",
    )(qt, kc, vct, kw, vwd, gt, tril)


def _nsa_selected_body(qt_ref, ks_ref, ksd_ref, vsl_ref, selm_ref, seld_ref, gt_ref, part_ref, oneh_ref, o_ref,
                       s_a, s_b, p_a, p_b, al_a, al_b, m_ref, acc_ref):
    TQ = qt_ref.shape[1]
    HPG = NSA_H // NSA_G
    W = HPG * TQ
    n_kc = ks_ref.shape[0] // SEL_KC
    bps = SEL_KC // SLC_LEN
    i = pl.program_id(1)
    n_main = i * (TQ // SEL_KC)
    n_pairs = (n_main + 1) // 2
    diag_ok = _iota((TQ, 1), 0) <= _iota((1, TQ), 1)
    pad_rows = jnp.zeros((LANE - 2 * bps, W), BF16)

    def q_aug(qpad, brows):
        brows = jnp.concatenate([jnp.concatenate([brows] * HPG, axis=1), jnp.zeros((bps, W), F32)], axis=0)
        return jnp.concatenate([qpad, brows.astype(BF16), pad_rows], axis=0)

    heads = [slice(h * TQ, (h + 1) * TQ) for h in range(HPG)]

    groups = range(NSA_G)

    def softmax(g, s_buf, p_buf, al_ref, cs):
        sb = s_buf[g, :, cs]
        m_i = m_ref[g, :, cs]
        m_new = jnp.maximum(m_i, jnp.max(sb, axis=0, keepdims=True).astype(F32))
        p_buf[g, :, cs] = jnp.exp2(sb - m_new.astype(BF16))
        al_ref[g, :, cs] = jnp.exp2(m_i - m_new)
        m_ref[g, :, cs] = m_new

    def apply_values(g, p_buf, al_ref, vt, cs):
        acc_ref[g, :, cs] = al_ref[g, :, cs] * acc_ref[g, :, cs] + _dot(vt, p_buf[g, :, cs])

    qpads = [_group_queries(qt_ref, g) for g in groups]
    m_ref[...] = jnp.full(m_ref.shape, NEG, F32)
    acc_ref[...] = jnp.zeros(acc_ref.shape, F32)
    p_b[...] = jnp.zeros(p_b.shape, BF16)
    al_b[...] = jnp.ones(al_b.shape, F32)

    def step_operands(c):
        k0 = pl.multiple_of(c * SEL_KC, SEL_KC)
        rows = pl.ds(pl.multiple_of(c * bps, bps), bps)
        k_aug = jnp.concatenate([ks_ref[pl.ds(k0, SEL_KC), :], oneh_ref[...]], axis=1)
        return k_aug, [q_aug(qpads[g], selm_ref[g, rows, :]) for g in groups]

    def scores(ops, g, s_buf, cs):
        s_buf[g, :, cs] = _dot(ops[0], ops[1][g][:, cs]).astype(BF16)

    def stages(ops, c_prev, s_new, s_cur, p_cur, al_cur, p_prev, al_prev):
        for g in groups:
            vt = vsl_ref[c_prev, g]
            for cs in heads:
                scores(ops, g, s_new, cs)
                softmax(g, s_cur, p_cur, al_cur, cs)
                apply_values(g, p_prev, al_prev, vt, cs)

    def pair(cc, carry):
        c0 = 2 * cc
        stages(step_operands(c0 + 1), jnp.maximum(c0 - 1, 0), s_b, s_a, p_a, al_a, p_b, al_b)
        stages(step_operands(jnp.minimum(c0 + 2, n_kc - 1)), c0, s_a, s_b, p_b, al_b, p_a, al_a)
        return carry

    ops = step_operands(0)
    for g in groups:
        for cs in heads:
            scores(ops, g, s_a, cs)
    lax.fori_loop(0, n_pairs, pair, 0)
    for g in groups:
        vt = vsl_ref[jnp.maximum(2 * n_pairs - 1, 0), g]
        for cs in heads:
            apply_values(g, p_b, al_b, vt, cs)

    for d in range(TQ // SEL_KC):
        k_aug = jnp.concatenate([ksd_ref[d * SEL_KC:(d + 1) * SEL_KC, :], oneh_ref[...]], axis=1)
        for g in groups:
            qa = q_aug(qpads[g], seld_ref[g, d * bps:(d + 1) * bps, :])
            vt = vsl_ref[i * (TQ // SEL_KC) + d, g]
            for cs in heads:
                sd = _dot(k_aug, qa[:, cs])
                s_a[g, :, cs] = jnp.where(diag_ok[d * SEL_KC:(d + 1) * SEL_KC, :], sd, NEG).astype(BF16)
                softmax(g, s_a, p_a, al_a, cs)
                apply_values(g, p_a, al_a, vt, cs)

    for g in groups:
        acc_s = acc_ref[g]
        o_s = acc_s[0:NSA_D, :] * (1.0 / acc_s[NSA_D:NSA_D + 1, :])

        for hp in range(HPG // 2):
            tiles = []
            for h in (2 * hp, 2 * hp + 1):
                gr = (g * HPG + h) * 3 + 1
                tiles.append(gt_ref[gr:gr + 1, :] * o_s[:, h * TQ:(h + 1) * TQ])
            cols = slice((g * HPG + 2 * hp) * NSA_D, (g * HPG + 2 * hp + 2) * NSA_D)
            o_ref[:, cols] = (part_ref[:, cols] + jnp.concatenate(tiles, axis=0).T).astype(o_ref.dtype)


def _nsa_selected(qt, ks, vsl, selm, sel, gt, part, B, S):
    HD = NSA_H * NSA_D
    TQ = SEL_TQ
    W = NSA_H // NSA_G * TQ
    n_sel = S // SLC_LEN
    assert S % (2 * SEL_KC) == 0 and TQ % SEL_KC == 0
    whole = lambda shape: pl.BlockSpec((None,) + shape, lambda b, i: (b,) + (0,) * len(shape),
                                       pipeline_mode=pl.Buffered(1))
    oneh = jnp.asarray(np.arange(SEL_KC)[:, None] // SLC_LEN == np.arange(LANE)[None, :], BF16)
    return pl.pallas_call(
        _nsa_selected_body,
        out_shape=jax.ShapeDtypeStruct((B * S, HD), BF16),
        grid=(B, S // TQ),
        in_specs=[pl.BlockSpec((None, HD, TQ), lambda b, i: (b, 0, i)),
                  whole((S, LANE)), pl.BlockSpec((None, TQ, LANE), lambda b, i: (b, i, 0)),
                  whole(vsl.shape[1:]),
                  pl.BlockSpec((None, NSA_G, n_sel, TQ), lambda b, i: (b, 0, 0, i)),
                  pl.BlockSpec((None, NSA_G, TQ // SLC_LEN, TQ), lambda b, i: (b, 0, i, i)),
                  pl.BlockSpec((None, LANE, TQ), lambda b, i: (b, 0, i)),
                  pl.BlockSpec((TQ, HD), lambda b, i: (b * (S // TQ) + i, 0)),
                  _const_spec((SEL_KC, LANE))],
        out_specs=pl.BlockSpec((TQ, HD), lambda b, i: (b * (S // TQ) + i, 0)),
        scratch_shapes=[pltpu.VMEM((NSA_G, SEL_KC, W), BF16)] * 4 + [pltpu.VMEM((NSA_G, 1, W), F32)] * 3
                       + [pltpu.VMEM((NSA_G, VT_ROWS, W), F32)],
        compiler_params=_params(("parallel", "arbitrary")),
        name="nsa_selected",
    )(qt, ks, ks, vsl, selm, sel, gt, part, oneh)


def _merge_ffn_body(x_ref, oa_ref, ob_ref, oc_ref, gates_ref, wbr_ref, wo_ref, g_ref, wg_ref, wu_ref, wd_ref, o_ref,
                    *, bounds):
    D = x_ref.shape[1]
    y = jnp.zeros(x_ref.shape, F32)
    for j, br in enumerate((oa_ref, ob_ref, oc_ref)):
        y = y + _sigmoid(gates_ref[:, j * D:(j + 1) * D].astype(F32)) * _dot(br[...].astype(BF16), wbr_ref[j])
    x = x_ref[...] + _dot(y.astype(BF16), wo_ref[...])
    ms = jnp.mean(x * x, axis=-1, keepdims=True)
    h = (x * lax.rsqrt(ms + RMS_EPS) * g_ref[...]).astype(BF16)
    acc = jnp.zeros(x.shape, F32)
    for lo, hi in zip(bounds[:-1], bounds[1:]):
        t = (_silu(_dot(h, wg_ref[:, lo:hi])) * _dot(h, wu_ref[:, lo:hi])).astype(BF16)
        acc = acc + _dot(t, wd_ref[lo:hi, :])
    o_ref[...] = x + 0.5 * acc


def _merge_ffn(x, o_a, o_b, o_c, gates, wbr, wo, g, wg, wu, wd, tm=256):
    T, D = x.shape
    F = wg.shape[1]
    bounds = (0, -(-F // (2 * MXU_N)) * MXU_N, F)
    row = lambda w: pl.BlockSpec((tm, w), lambda i: (i, 0))
    return pl.pallas_call(
        functools.partial(_merge_ffn_body, bounds=bounds),
        out_shape=jax.ShapeDtypeStruct((T, D), F32),
        grid=(T // tm,),
        in_specs=[row(D), row(MIX_W), row(MIX_W), row(MIX_W), row(3 * D), _const_spec(wbr.shape), _const_spec(wo.shape),
                  _const_spec((1, D)), _const_spec((D, F)), _const_spec((D, F)), _const_spec((F, D))],
        out_specs=row(D),
        compiler_params=_params(("parallel",)),
        name="merge_ffn",
    )(x, o_a, o_b, o_c, gates, wbr, wo, g, wg, wu, wd)


def _pack_in_proj(w_in, b_in):
    ws, bs = [], []
    for _, off, w, wp in _SEGS:
        ws.append(jnp.pad(w_in[..., off:off + w], ((0, 0), (0, 0), (0, wp - w))))
        bs.append(jnp.pad(b_in[..., off:off + w], ((0, 0), (0, wp - w))))
    return jnp.concatenate(ws, axis=-1).astype(BF16), jnp.concatenate(bs, axis=-1)[:, None, :]


def _rope_tables(pos):
    half = NSA_D // 2
    freqs = ROPE_THETA ** (-jnp.arange(half, dtype=F32) / half)
    ang = pos.astype(F32)[:, None] * freqs[None, :]
    cos, sin = jnp.cos(ang), jnp.sin(ang)
    reps = LANE // NSA_D
    return jnp.tile(jnp.concatenate([cos, cos], axis=1), (1, reps)), jnp.tile(jnp.concatenate([-sin, sin], axis=1), (1, reps))


def _cmp_weights(w1, w2):
    L = w1.shape[0]
    w1r = w1.reshape(L, 2, 2, CMP_STRIDE, NSA_D, CMP_HIDDEN)
    eye_g = jnp.eye(NSA_G, dtype=w1.dtype)
    ex = jnp.einsum('lwstdh,gk->lwstgdkh', w1r, eye_g)
    ex = ex.reshape(L, 2, 2, CMP_STRIDE * NSA_G * NSA_D, NSA_G * CMP_HIDDEN)
    w2x = jnp.einsum('lwhd,gk->lwghkd', w2, eye_g).reshape(L, 2, NSA_G * CMP_HIDDEN, NSA_G * NSA_D)
    return ex[:, :, 0].astype(BF16), ex[:, :, 1].astype(BF16), w2x.astype(BF16)


def _cmp_pos_rows(pe):
    L = pe.shape[0]
    r = pe.reshape(L, 2, 2, CMP_STRIDE, 1, NSA_D)
    return jnp.broadcast_to(r, (L, 2, 2, CMP_STRIDE, NSA_G, NSA_D)).reshape(L, 2, 2, CMP_STRIDE * NSA_G * NSA_D)


def _lane_gain(g):
    return jnp.tile(g, LANE // g.shape[-1])[None, :]


def _layer(x, lw, consts, B, S):
    cos, sin, cos_c, sin_c, gavg = consts
    T = B * S
    x = _ffn(x, lw["ffn1_norm"], lw["ffn1_wg"], lw["ffn1_wu"], lw["ffn1_wd"])
    z = dict(zip([s[0] for s in _SEGS], _proj(x, lw["mix_norm"], lw["w_in"], lw["b_in"])))
    o_a = _gla(z["a_q"], z["a_k"], z["a_v"], z["a_r"], z["a_lr"], lw["gla_wa"], lw["gla_ba"], lw["gla_gn"], B, S)
    o_c = _mlstm(z["c_qk"], z["c_v"], z["c_o"], z["c_if"], lw["conv_w"], lw["conv_b"], B, S)
    qn = lw["nsa_qk_norm"]
    qt, ks, kw, vsl, vwd, gt = _nsa_prep(z["b_q"], z["b_ks"], z["b_kw"], z["b_vs"], z["b_vw"], z["b_g"], cos, sin,
                                         _lane_gain(qn[0]), _lane_gain(qn[2]), _lane_gain(qn[3]), gavg, B, S)
    n_blk = S // CMP_STRIDE
    xk = z["b_kc"].reshape(B, n_blk, CMP_STRIDE * LANE)
    xv = z["b_vc"].reshape(B, n_blk, CMP_STRIDE * LANE)
    kc, vct = _nsa_cmp(xk, xv, lw["cmp_pe"], lw["cmp_wt"], lw["cmp_wb"], lw["cmp_w2"], cos_c, sin_c,
                       _lane_gain(qn[1]), gavg)
    part, sel, selm = _nsa_select(qt, kc, vct, kw, vwd, gt, B, S)
    o_b = _nsa_selected(qt, ks, vsl, selm, sel, gt, part, B, S)
    return _merge_ffn(x, o_a, o_b, o_c, z["gates"], lw["w_branch"], lw["w_out"],
                      lw["ffn2_norm"], lw["ffn2_wg"], lw["ffn2_wu"], lw["ffn2_wd"])


def kernel(x, ffn1_norm, ffn1_w_gate, ffn1_w_up, ffn1_w_down, mix_norm, w_in, b_in, gla_w_alpha, gla_b_alpha, gla_out_norm, nsa_qk_norm, nsa_cmp_pos, nsa_cmp_w1, nsa_cmp_w2, mlstm_conv_w, mlstm_conv_b, w_branch, w_out, ffn2_norm, ffn2_w_gate, ffn2_w_up, ffn2_w_down):
    B, S, D = x.shape
    w_in_p, b_in_p = _pack_in_proj(w_in, b_in)
    cmp_wt, cmp_wb, cmp_w2 = _cmp_weights(nsa_cmp_w1, nsa_cmp_w2)
    layers = {
        "ffn1_norm": ffn1_norm[:, None, :], "ffn1_wg": ffn1_w_gate.astype(BF16), "ffn1_wu": ffn1_w_up.astype(BF16),
        "ffn1_wd": ffn1_w_down.astype(BF16),
        "mix_norm": mix_norm[:, None, :], "w_in": w_in_p, "b_in": b_in_p,
        "gla_wa": jnp.pad(gla_w_alpha, ((0, 0), (0, LANE - GLA_RANK), (0, 0))), "gla_ba": gla_b_alpha[:, None, :],
        "gla_gn": gla_out_norm[:, None, :],
        "nsa_qk_norm": nsa_qk_norm, "cmp_pe": _cmp_pos_rows(nsa_cmp_pos), "cmp_wt": cmp_wt, "cmp_wb": cmp_wb,
        "cmp_w2": cmp_w2,
        "conv_w": mlstm_conv_w, "conv_b": mlstm_conv_b[:, None, :],
        "w_branch": w_branch.astype(BF16), "w_out": w_out.astype(BF16),
        "ffn2_norm": ffn2_norm[:, None, :], "ffn2_wg": ffn2_w_gate.astype(BF16), "ffn2_wu": ffn2_w_up.astype(BF16),
        "ffn2_wd": ffn2_w_down.astype(BF16),
    }
    n_blk = S // CMP_STRIDE
    cos, sin = _rope_tables(jnp.arange(S))
    cos_c, sin_c = _rope_tables(jnp.arange(n_blk) * CMP_STRIDE + CMP_LEN - 1)
    gavg = jnp.asarray(np.kron(np.eye(LANE // NSA_D), np.full((NSA_D, NSA_D), 1.0 / NSA_D)), BF16)
    consts = (cos, sin, cos_c, sin_c, gavg)

    out = x.reshape(B * S, D)
    for l in range(w_in.shape[0]):
        out = _layer(out, {name: w[l] for name, w in layers.items()}, consts, B, S)
    return out.reshape(B, S, D)
```
